```python
import math
import jax, jax.numpy as jnp
from jax import lax
import numpy as np

D_MODEL = 1024
BATCH = 4
SEQ = 4096
DEPTH = 2

GRID_W = 64
CTX_LEN = 256
EPS = 1e-6
F32 = jnp.float32
HEAD_DIM = 64
ROPE_BASE = 10000.0

WA_HEADS = 4
WA_KV_HEADS = 2
WA_WINDOW = 128
WA_BLOCK = 128
NA_HEADS = 4
NA_MAX_KH = 8
NA_KW = 16
NA_QBW = 16
NA_KBW = NA_QBW + NA_KW
SSM_HEADS = 8
SSM_HEAD_DIM = 64
SSM_INNER = SSM_HEADS * SSM_HEAD_DIM
SSM_GROUPS = 2
SSM_STATE = 128
SSM_CONV = 7
SSM_CHUNK = 128
D_FF = ((8 * D_MODEL + 3 * 256 - 1) // (3 * 256)) * 256

QA_COLS = WA_HEADS * HEAD_DIM
QB_COLS = NA_HEADS * HEAD_DIM
Z_COLS = SSM_INNER
Q_SIDE = QA_COLS + QB_COLS + Z_COLS
KA_COLS = WA_KV_HEADS * HEAD_DIM
KB_COLS = NA_HEADS * HEAD_DIM
XBC_COLS = SSM_INNER + 2 * SSM_GROUPS * SSM_STATE
DT_COLS = 2 * SSM_HEADS
IN_COLS = Q_SIDE + 2 * KA_COLS + 2 * KB_COLS + XBC_COLS + DT_COLS
MIX_WIDTH = QA_COLS + QB_COLS + SSM_INNER

kernel_name = 'hymba_style_window_natten_ssd_prefix_dit'


def rms_norm(x, g):
    xf = x.astype(F32)
    y = xf * lax.rsqrt(jnp.mean(xf * xf, axis=-1, keepdims=True) + EPS)
    return (y * g.astype(F32)).astype(x.dtype)


def modulate(h, shift, scale):
    return h * (1 + scale) + shift


def split_cols(p, sizes):
    out, off = [], 0
    for s in sizes:
        out.append(p[..., off:off + s])
        off += s
    return out


def rope_2d(x, rows, cols):
    d = x.shape[-1]
    half = d // 2
    quarter = half // 2
    inv_freq = ROPE_BASE ** (-jnp.arange(quarter, dtype=F32) / quarter)
    xf = x.astype(F32)

    def rot(xp, pos):
        ang = pos.astype(F32)[:, None] * inv_freq[None, :]
        cos = jnp.cos(ang)[None, :, None, :]
        sin = jnp.sin(ang)[None, :, None, :]
        x1, x2 = xp[..., :quarter], xp[..., quarter:]
        return jnp.concatenate([x1 * cos - x2 * sin, x2 * cos + x1 * sin], axis=-1)

    return jnp.concatenate([rot(xf[..., :half], rows), rot(xf[..., half:], cols)], axis=-1).astype(x.dtype)


def window_attention(q, k, v, k_ctx, v_ctx, sink):
    b, L, H, d = q.shape
    G = k.shape[2]
    rep = H // G
    blk = WA_BLOCK
    nb = L // blk
    Lc = k_ctx.shape[1]
    scale = d ** -0.5
    qb = q.reshape(b, nb, blk, G, rep, d)

    def band(t):
        tp = jnp.pad(t, ((0, 0), (blk, blk), (0, 0), (0, 0))).reshape(b, nb + 2, blk, G, d)
        return jnp.concatenate([tp[:, 0:nb], tp[:, 1:nb + 1], tp[:, 2:nb + 2]], axis=2)

    kb, vb = band(k), band(v)
    qpos = jnp.arange(nb)[:, None] * blk + jnp.arange(blk)[None, :]
    kpos = (jnp.arange(nb)[:, None] - 1) * blk + jnp.arange(3 * blk)[None, :]
    mask = ((jnp.abs(qpos[:, :, None] - kpos[:, None, :]) <= WA_WINDOW)
            & (kpos[:, None, :] >= 0) & (kpos[:, None, :] < L))
    s_loc = jnp.einsum('bnqgrd,bnkgd->bngrqk', qb, kb).astype(F32) * scale
    s_loc = jnp.where(mask[None, :, None, None], s_loc, -jnp.inf)
    s_ctx = jnp.einsum('bnqgrd,bcgd->bngrqc', qb, k_ctx).astype(F32) * scale
    s_sink = jnp.broadcast_to(sink.astype(F32).reshape(G, rep)[None, None, :, :, None, None],
                              s_loc.shape[:-1] + (1,))
    p = jax.nn.softmax(jnp.concatenate([s_loc, s_ctx, s_sink], axis=-1), axis=-1).astype(v.dtype)
    nk = 3 * blk
    o = (jnp.einsum('bngrqk,bnkgd->bnqgrd', p[..., :nk], vb)
         + jnp.einsum('bngrqc,bcgd->bnqgrd', p[..., nk:nk + Lc], v_ctx))
    return o.reshape(b, L, H * d)


def context_attention(q, k, v, sink):
    b, Lc, H, d = q.shape
    G = k.shape[2]
    rep = H // G
    qg = q.reshape(b, Lc, G, rep, d)
    s = jnp.einsum('bqgrd,bkgd->bgrqk', qg, k).astype(F32) * d ** -0.5
    if sink is not None:
        s_sink = jnp.broadcast_to(sink.astype(F32).reshape(G, rep)[None, :, :, None, None], s.shape[:-1] + (1,))
        s = jnp.concatenate([s, s_sink], axis=-1)
    p = jax.nn.softmax(s, axis=-1)[..., :Lc].astype(v.dtype)
    o = jnp.einsum('bgrqk,bkgd->bqgrd', p, v)
    return o.reshape(b, Lc, H * d)


def neighbourhood_attention(q, k, v, k_ctx, v_ctx, rpb, grid_rows):
    b, L, H, d = q.shape
    kh = min(NA_MAX_KH, grid_rows)
    ncb = GRID_W // NA_QBW
    scale = d ** -0.5
    r = jnp.arange(grid_rows)
    row_idx = jnp.clip(r - kh // 2, 0, grid_rows - kh)[:, None] + jnp.arange(kh)[None, :]
    cb = jnp.arange(ncb)
    col_idx = jnp.clip(cb * NA_QBW - NA_KW // 2, 0, GRID_W - NA_KBW)[:, None] + jnp.arange(NA_KBW)[None, :]
    qcol = cb[:, None] * NA_QBW + jnp.arange(NA_QBW)[None, :]
    cstart = jnp.clip(qcol - NA_KW // 2, 0, GRID_W - NA_KW)
    cmask = (col_idx[:, None, :] >= cstart[:, :, None]) & (col_idx[:, None, :] < cstart[:, :, None] + NA_KW)
    dy = row_idx - r[:, None] + (NA_MAX_KH - 1)
    dx = jnp.clip(col_idx[:, None, :] - qcol[:, :, None], -(NA_KW - 1), NA_KW - 1) + (NA_KW - 1)
    bias = rpb.astype(F32)[:, dy[:, None, None, :, None], dx[None, :, :, None, :]]
    bias = jnp.moveaxis(bias, 0, 2)
    qg = q.reshape(b, grid_rows, ncb, NA_QBW, H, d)
    kg = k.reshape(b, grid_rows, GRID_W, H, d)
    vg = v.reshape(b, grid_rows, GRID_W, H, d)
    ri = row_idx[:, None, :, None]
    ci = col_idx[None, :, None, :]
    kwin = kg[:, ri, ci]
    vwin = vg[:, ri, ci]
    s = jnp.einsum('brcqhd,brcyxhd->brchqyx', qg, kwin).astype(F32) * scale + bias[None]
    s = jnp.where(cmask[None, None, :, None, :, None, :], s, -jnp.inf)
    nloc = kh * NA_KBW
    s = s.reshape(b, grid_rows, ncb, H, NA_QBW, nloc)
    s_ctx = jnp.einsum('brcqhd,bkhd->brchqk', qg, k_ctx).astype(F32) * scale
    p = jax.nn.softmax(jnp.concatenate([s, s_ctx], axis=-1), axis=-1).astype(v.dtype)
    p_loc = p[..., :nloc].reshape(b, grid_rows, ncb, H, NA_QBW, kh, NA_KBW)
    o = (jnp.einsum('brchqyx,brcyxhd->brcqhd', p_loc, vwin)
         + jnp.einsum('brchqk,bkhd->brcqhd', p[..., nloc:], v_ctx))
    return o.reshape(b, L, H * d)


def depthwise_conv(x, w, bias):
    k = w.shape[0]
    y = lax.conv_general_dilated(x, w[:, None, :].astype(x.dtype), window_strides=(1,),
                                 padding=[(k // 2, k // 2)], dimension_numbers=('NWC', 'WIO', 'NWC'),
                                 feature_group_count=x.shape[-1])
    return y + bias


def ssm_prepare(xbc_raw, dt_raw, conv_w, conv_b, dt_bias):
    xbc = jax.nn.silu(depthwise_conv(xbc_raw, conv_w, conv_b))
    xs, bm, cm = split_cols(xbc, (SSM_INNER, SSM_GROUPS * SSM_STATE, SSM_GROUPS * SSM_STATE))
    b, L = xs.shape[:2]
    rep = SSM_HEADS // SSM_GROUPS
    xs = xs.reshape(b, L, SSM_HEADS, SSM_HEAD_DIM)
    bm = jnp.repeat(bm.reshape(b, L, SSM_GROUPS, SSM_STATE), rep, axis=2)
    cm = jnp.repeat(cm.reshape(b, L, SSM_GROUPS, SSM_STATE), rep, axis=2)
    dt = jax.nn.softplus(dt_raw.astype(F32) + dt_bias.astype(F32).reshape(2 * SSM_HEADS))
    return xs, bm, cm, dt.reshape(b, L, 2, SSM_HEADS)


def ssd_scan(x, dt, A, Bm, Cm, h0, with_output):
    b, L, H, P = x.shape
    N = Bm.shape[-1]
    Q = SSM_CHUNK
    nc = L // Q
    xc = x.astype(F32).reshape(b, nc, Q, H, P)
    dtc = dt.astype(F32).reshape(b, nc, Q, H)
    bc = Bm.astype(F32).reshape(b, nc, Q, H, N)
    cc = Cm.astype(F32).reshape(b, nc, Q, H, N)
    acum = jnp.cumsum(dtc * A, axis=2)
    decay_to_end = jnp.exp(acum[:, :, -1:, :] - acum)
    states = jnp.einsum('bcjhn,bcjh,bcjhp->bchpn', bc, decay_to_end * dtc, xc)
    chunk_decay = jnp.exp(acum[:, :, -1, :])

    def step(h, inp):
        st, dec = inp
        return h * dec[:, :, None, None] + st, h

    h_final, h_enter = lax.scan(step, h0, (jnp.moveaxis(states, 1, 0), jnp.moveaxis(chunk_decay, 1, 0)))
    if not with_output:
        return h_final
    h_enter = jnp.moveaxis(h_enter, 0, 1)
    seg = acum[:, :, :, None, :] - acum[:, :, None, :, :]
    lower = jnp.tril(jnp.ones((Q, Q), dtype=bool))
    decay_ij = jnp.exp(jnp.where(lower[None, None, :, :, None], seg, -jnp.inf))
    w = jnp.einsum('bcihn,bcjhn->bcijh', cc, bc) * decay_ij * dtc[:, :, None, :, :]
    y = (jnp.einsum('bcijh,bcjhp->bcihp', w, xc)
         + jnp.einsum('bcihn,bchpn->bcihp', cc, h_enter) * jnp.exp(acum)[..., None])
    return y.reshape(b, L, H, P), h_final


def ssd_bidir(xs, bm, cm, dt, A, h0_f, h0_b, with_output):
    rev = lambda t: jnp.flip(t, axis=1)
    fwd = ssd_scan(xs, dt[:, :, 0], A[0], bm, cm, h0_f, with_output)
    bwd = ssd_scan(rev(xs), rev(dt[:, :, 1]), A[1], rev(bm), rev(cm), h0_b, with_output)
    if not with_output:
        return fwd, bwd
    (y_f, h_f), (y_b, h_b) = fwd, bwd
    return y_f + rev(y_b), h_f, h_b


def ssm_output(y, xs, z, d_skip, g):
    b, L = y.shape[:2]
    y = y + d_skip.astype(F32)[:, None] * xs.astype(F32)
    y = y.reshape(b, L, SSM_INNER) * jax.nn.silu(z.astype(F32))
    return rms_norm(y, g).astype(z.dtype)


def swiglu(h, w_in, w_out):
    gate, up = jnp.split(h @ w_in, 2, axis=-1)
    return (jax.nn.silu(gate) * up) @ w_out


def hybrid_layer(xl, xc, sc, scc, rows, cols, grid_rows, w_mod, b_mod, g_mix, w_in, wa_sink, na_rpb,
                 conv_w, conv_b, dt_bias, a_log, d_skip, ssm_g, w_out, g_ffn, w_ffn_in, w_ffn_out, ctx_out):
    D = D_MODEL
    b, L, _ = xl.shape
    Lc = xc.shape[1]
    hd = HEAD_DIM
    mod_l = sc @ w_mod + b_mod
    sh1, sc1, gt1, sh2, sc2, gt2 = [m[:, None, :] for m in jnp.split(mod_l, 6, axis=-1)]
    n_ctx_mod = 6 if ctx_out else 2
    mods_c = jnp.split(scc @ w_mod[:, :n_ctx_mod * D] + b_mod[:n_ctx_mod * D], n_ctx_mod)

    hl = modulate(rms_norm(xl, g_mix), sh1, sc1)
    hc = modulate(rms_norm(xc, g_mix), mods_c[0], mods_c[1])
    kv_sizes = (KA_COLS, KA_COLS, KB_COLS, KB_COLS, XBC_COLS, DT_COLS)
    qa, qb, z, ka, va, kb, vb, xbc, dtr = split_cols(hl @ w_in, (QA_COLS, QB_COLS, Z_COLS) + kv_sizes)
    if ctx_out:
        qa_c, qb_c, z_c, ka_c, va_c, kb_c, vb_c, xbc_c, dtr_c = split_cols(hc @ w_in, (QA_COLS, QB_COLS, Z_COLS) + kv_sizes)
    else:
        ka_c, va_c, kb_c, vb_c, xbc_c, dtr_c = split_cols(hc @ w_in[:, Q_SIDE:], kv_sizes)

    ka_c = ka_c.reshape(b, Lc, WA_KV_HEADS, hd)
    va_c = va_c.reshape(b, Lc, WA_KV_HEADS, hd)
    o_a = window_attention(rope_2d(qa.reshape(b, L, WA_HEADS, hd), rows, cols),
                           rope_2d(ka.reshape(b, L, WA_KV_HEADS, hd), rows, cols),
                           va.reshape(b, L, WA_KV_HEADS, hd), ka_c, va_c, wa_sink)
    kb_c = kb_c.reshape(b, Lc, NA_HEADS, hd)
    vb_c = vb_c.reshape(b, Lc, NA_HEADS, hd)
    o_b = neighbourhood_attention(qb.reshape(b, L, NA_HEADS, hd), kb.reshape(b, L, NA_HEADS, hd),
                                  vb.reshape(b, L, NA_HEADS, hd), kb_c, vb_c, na_rpb, grid_rows)
    A = -jnp.exp(a_log.astype(F32))
    xs_c, bm_c, cm_c, dt_c = ssm_prepare(xbc_c, dtr_c, conv_w, conv_b, dt_bias)
    h0 = jnp.zeros((b, SSM_HEADS, SSM_HEAD_DIM, SSM_STATE), F32)
    if ctx_out:
        y_c, h_f, h_b = ssd_bidir(xs_c, bm_c, cm_c, dt_c, A, h0, h0, True)
    else:
        h_f, h_b = ssd_bidir(xs_c, bm_c, cm_c, dt_c, A, h0, h0, False)
    xs, bm, cm, dt = ssm_prepare(xbc, dtr, conv_w, conv_b, dt_bias)
    y_l, _, _ = ssd_bidir(xs, bm, cm, dt, A, h_f, h_b, True)
    o_c = ssm_output(y_l, xs, z, d_skip, ssm_g)

    mix = jnp.concatenate([o_a, o_b, o_c.astype(o_a.dtype)], axis=-1) @ w_out
    xl = xl + gt1 * mix
    xl = xl + gt2 * swiglu(modulate(rms_norm(xl, g_ffn), sh2, sc2), w_ffn_in, w_ffn_out)
    if not ctx_out:
        return xl, None

    o_ac = context_attention(qa_c.reshape(b, Lc, WA_HEADS, hd), ka_c, va_c, wa_sink)
    o_bc = context_attention(qb_c.reshape(b, Lc, NA_HEADS, hd), kb_c, vb_c, None)
    o_cc = ssm_output(y_c, xs_c, z_c, d_skip, ssm_g)
    mix_c = jnp.concatenate([o_ac, o_bc, o_cc.astype(o_ac.dtype)], axis=-1) @ w_out
    xc = xc + mods_c[2] * mix_c
    xc = xc + mods_c[5] * swiglu(modulate(rms_norm(xc, g_ffn), mods_c[3], mods_c[4]), w_ffn_in, w_ffn_out)
    return xl, xc


def setup_inputs(seed: int = 0) -> dict:
    key = jax.random.key(seed)
    ks = jax.random.split(key, 24)
    nrm = jax.random.normal
    D = D_MODEL
    dt0 = jnp.exp(jax.random.uniform(ks[12], (DEPTH, 2, SSM_HEADS), minval=math.log(1e-3), maxval=math.log(0.1)))
    return {
        'x': nrm(ks[0], (BATCH, SEQ, D), F32),
        'c': nrm(ks[1], (BATCH, D), F32),
        'ctx': nrm(ks[2], (BATCH, CTX_LEN, D), F32),
        'c_ctx': nrm(ks[3], (D,), F32),
        'w_mod': nrm(ks[4], (DEPTH, D, 6 * D), F32) * (0.5 * D ** -0.5),
        'b_mod': nrm(ks[5], (DEPTH, 6 * D), F32) * 0.01,
        'g_mix': 1.0 + 0.05 * nrm(ks[6], (DEPTH, D), F32),
        'w_in': nrm(ks[7], (DEPTH, D, IN_COLS), F32) * D ** -0.5,
        'wa_sink': nrm(ks[8], (DEPTH, WA_HEADS), F32) * 0.5,
        'na_rpb': nrm(ks[9], (DEPTH, NA_HEADS, 2 * NA_MAX_KH - 1, 2 * NA_KW - 1), F32) * 0.1,
        'ssm_conv_w': nrm(ks[10], (DEPTH, SSM_CONV, XBC_COLS), F32) * SSM_CONV ** -0.5,
        'ssm_conv_b': nrm(ks[11], (DEPTH, XBC_COLS), F32) * 0.01,
        'ssm_dt_bias': dt0 + jnp.log(-jnp.expm1(-dt0)),
        'ssm_a_log': jnp.log(jax.random.uniform(ks[13], (DEPTH, 2, SSM_HEADS), minval=1.0, maxval=16.0)),
        'ssm_d': 1.0 + 0.1 * nrm(ks[14], (DEPTH, SSM_HEADS), F32),
        'ssm_norm_g': 1.0 + 0.05 * nrm(ks[15], (DEPTH, SSM_INNER), F32),
        'w_out': nrm(ks[16], (DEPTH, MIX_WIDTH, D), F32) * MIX_WIDTH ** -0.5,
        'g_ffn': 1.0 + 0.05 * nrm(ks[17], (DEPTH, D), F32),
        'w_ffn_in': nrm(ks[18], (DEPTH, D, 2 * D_FF), F32) * D ** -0.5,
        'w_ffn_out': nrm(ks[19], (DEPTH, D_FF, D), F32) * D_FF ** -0.5,
        'g_final': 1.0 + 0.05 * nrm(ks[20], (D,), F32),
    }


def reference(x, c, ctx, c_ctx, w_mod, b_mod, g_mix, w_in, wa_sink, na_rpb, ssm_conv_w, ssm_conv_b,
              ssm_dt_bias, ssm_a_log, ssm_d, ssm_norm_g, w_out, g_ffn, w_ffn_in, w_ffn_out, g_final):
    L = x.shape[1]
    grid_rows = L // GRID_W
    t = jnp.arange(L)
    rows, cols = t // GRID_W, t % GRID_W
    sc = jax.nn.silu(c)
    scc = jax.nn.silu(c_ctx)
    xl, xc = x, ctx
    for i in range(DEPTH):
        xl, xc = hybrid_layer(xl, xc, sc, scc, rows, cols, grid_rows, w_mod[i], b_mod[i], g_mix[i], w_in[i],
                              wa_sink[i], na_rpb[i], ssm_conv_w[i], ssm_conv_b[i], ssm_dt_bias[i], ssm_a_log[i],
                              ssm_d[i], ssm_norm_g[i], w_out[i], g_ffn[i], w_ffn_in[i], w_ffn_out[i],
                              ctx_out=(i < DEPTH - 1))
    return rms_norm(xl, g_final)
```

```python
import functools
import math

import numpy as np
import jax
import jax.numpy as jnp
from jax import lax
from jax.experimental import pallas as pl
from jax.experimental.pallas import tpu as pltpu

F32 = jnp.float32
BF16 = jnp.bfloat16

D_MODEL = 1024
BATCH = 4
SEQ = 4096
DEPTH = 2
GRID_W = 64
GRID_ROWS = SEQ // GRID_W
CTX_LEN = 256
EPS = 1e-6
HEAD_DIM = 64
ROPE_BASE = 10000.0
WA_HEADS = 4
WA_KV_HEADS = 2
WA_WINDOW = 128
WA_BLOCK = 128
NA_HEADS = 4
NA_KH = 8
NA_KW = 16
SSM_HEADS = 8
SSM_HEAD_DIM = 64
SSM_INNER = SSM_HEADS * SSM_HEAD_DIM
SSM_GROUPS = 2
SSM_STATE = 128
SSM_CONV = 7
SSM_CHUNK = 128
D_FF = 2816
XBC_COLS = SSM_INNER + 2 * SSM_GROUPS * SSM_STATE
DT_COLS = 2 * SSM_HEADS

LANES = 128
SUBLANES = 8
VMEM_LIMIT = 56 * 1024 * 1024

A_COLS = 512
B_COLS = 768
XZ_COLS = 1536
DT_PAD = LANES
PROJ_COLS = A_COLS + B_COLS + XZ_COLS + DT_PAD

ATT_SCALE = HEAD_DIM ** -0.5
NEG_INF = float("-inf")
NT_DIMS = (((1,), (1,)), ((), ()))


def _silu(v):
    return v / (1.0 + jnp.exp(-v))


def _bdot(a, b):
    return jnp.dot(a.astype(BF16), b.astype(BF16), preferred_element_type=F32)


def _bdot_nt(a, b):
    return lax.dot_general(a.astype(BF16), b.astype(BF16), NT_DIMS, preferred_element_type=F32)


def _cparams(sem):
    return pltpu.CompilerParams(dimension_semantics=sem, vmem_limit_bytes=VMEM_LIMIT)


MOD_TN = 1536


def _mod_kernel(c_ref, w_ref, b_ref, o_ref):
    s = _silu(c_ref[...])
    o_ref[0] = _bdot(s, w_ref[0]) + b_ref[0]


def _modulation(cin, w_mod, b_mod):
    n = 6 * D_MODEL
    return pl.pallas_call(
        _mod_kernel,
        out_shape=jax.ShapeDtypeStruct((DEPTH, SUBLANES, n), F32),
        grid=(DEPTH, n // MOD_TN),
        in_specs=[
            pl.BlockSpec((SUBLANES, D_MODEL), lambda l, j: (0, 0)),
            pl.BlockSpec((1, D_MODEL, MOD_TN), lambda l, j: (l, 0, j)),
            pl.BlockSpec((1, 1, MOD_TN), lambda l, j: (l, 0, j)),
        ],
        out_specs=pl.BlockSpec((1, SUBLANES, MOD_TN), lambda l, j: (l, 0, j)),
        compiler_params=_cparams(("parallel", "parallel")),
        name="modulation",
    )(cin, w_mod, b_mod.reshape(DEPTH, 1, n))


def _norm_mod(x, g, shift, scale):
    var = jnp.mean(x * x, axis=-1, keepdims=True)
    h = x * lax.rsqrt(var + EPS) * g
    return h * (1.0 + scale) + shift


def _inproj_kernel(x_ref, mod_ref, g_ref, w_ref, *rest, rope):
    if rope:
        cos_ref, sa_ref, sb_ref, oa_ref, ob_ref, oxz_ref, odt_ref = rest
    else:
        oa_ref, ob_ref, oxz_ref, odt_ref = rest
    h = _norm_mod(x_ref[0], g_ref[...], mod_ref[0, 0:1, :], mod_ref[0, 1:2, :]).astype(BF16)
    a = jnp.dot(h, w_ref[:, 0:A_COLS], preferred_element_type=F32)
    if rope:
        cos, sa, sb = cos_ref[...], sa_ref[...], sb_ref[...]
        for j in range(3):
            v = a[:, j * LANES:(j + 1) * LANES]
            up = pltpu.roll(v, LANES - 16, axis=1)
            dn = pltpu.roll(v, 16, axis=1)
            oa_ref[0, :, j * LANES:(j + 1) * LANES] = v * cos + up * sa + dn * sb
        oa_ref[0, :, 3 * LANES:] = a[:, 3 * LANES:]
    else:
        oa_ref[0] = a
    ob_ref[0] = jnp.dot(h, w_ref[:, A_COLS:A_COLS + B_COLS], preferred_element_type=F32)
    oxz_ref[0] = jnp.dot(h, w_ref[:, A_COLS + B_COLS:A_COLS + B_COLS + XZ_COLS],
                         preferred_element_type=F32)
    odt_ref[0] = jnp.dot(h, w_ref[:, A_COLS + B_COLS + XZ_COLS:], preferred_element_type=F32)


def _inproj(x, mods, g, w, rope_tabs, tm):
    b, t, _ = x.shape
    rope = rope_tabs is not None
    per_batch = mods.shape[0] > 1
    in_specs = [
        pl.BlockSpec((1, tm, D_MODEL), lambda i, j: (i, j, 0)),
        pl.BlockSpec((1, SUBLANES, D_MODEL), (lambda i, j: (i, 0, 0)) if per_batch else (lambda i, j: (0, 0, 0))),
        pl.BlockSpec((1, D_MODEL), lambda i, j: (0, 0)),
        pl.BlockSpec((D_MODEL, PROJ_COLS), lambda i, j: (0, 0)),
    ]
    args = [x, mods, g, w]
    if rope:
        in_specs += [pl.BlockSpec((tm, LANES), lambda i, j: (j, 0))] * 3
        args += list(rope_tabs)
    widths = (A_COLS, B_COLS, XZ_COLS, DT_PAD)
    return pl.pallas_call(
        functools.partial(_inproj_kernel, rope=rope),
        out_shape=[jax.ShapeDtypeStruct((b, t, n), F32) for n in widths],
        grid=(b, t // tm),
        in_specs=in_specs,
        out_specs=[pl.BlockSpec((1, tm, n), lambda i, j: (i, j, 0)) for n in widths],
        compiler_params=_cparams(("parallel", "parallel")),
        name="inproj_rope" if rope else "inproj",
    )(*args)


WA_KEYS = 3 * WA_BLOCK


def _attn_a_kernel(sink_ref, q_ref, k_ref, v_ref, kc_ref, vc_ref, o_ref):
    n = pl.program_id(1)
    ks = pl.multiple_of(jnp.clip((n - 1) * WA_BLOCK, 0, SEQ - WA_KEYS), WA_BLOCK)
    q = q_ref[0]
    kall = jnp.concatenate([k_ref[0, pl.ds(ks, WA_KEYS), :], kc_ref[0]], axis=0).astype(BF16)
    vall = jnp.concatenate([v_ref[0, pl.ds(ks, WA_KEYS), :], vc_ref[0]], axis=0).astype(BF16)
    nk = WA_KEYS + CTX_LEN
    rows = lax.broadcasted_iota(jnp.int32, (2 * WA_BLOCK, nk), 0)
    cols = lax.broadcasted_iota(jnp.int32, (2 * WA_BLOCK, nk), 1)
    qpos = n * WA_BLOCK + jnp.where(rows >= WA_BLOCK, rows - WA_BLOCK, rows)
    valid = (cols >= WA_KEYS) | (jnp.abs(qpos - (ks + cols)) <= WA_WINDOW)
    row1 = lax.broadcasted_iota(jnp.int32, (2 * WA_BLOCK, 1), 0)
    for g in range(WA_KV_HEADS):
        h0, h1 = 2 * g, 2 * g + 1
        q2 = jnp.concatenate([q[:, h0 * HEAD_DIM:(h0 + 1) * HEAD_DIM],
                              q[:, h1 * HEAD_DIM:(h1 + 1) * HEAD_DIM]], axis=0)
        s = _bdot_nt(q2, kall[:, g * HEAD_DIM:(g + 1) * HEAD_DIM]) * ATT_SCALE
        s = jnp.where(valid, s, NEG_INF)
        sink = jnp.where(row1 < WA_BLOCK, sink_ref[h0], sink_ref[h1])
        m = jnp.maximum(jnp.max(s, axis=1, keepdims=True), sink)
        p = jnp.exp(s - m)
        den = jnp.sum(p, axis=1, keepdims=True) + jnp.exp(sink - m)
        o = _bdot(p, vall[:, g * HEAD_DIM:(g + 1) * HEAD_DIM]) / den
        o_ref[0, :, h0 * HEAD_DIM:(h0 + 1) * HEAD_DIM] = o[:WA_BLOCK]
        o_ref[0, :, h1 * HEAD_DIM:(h1 + 1) * HEAD_DIM] = o[WA_BLOCK:]


def _attn_a(sink, qkv, qkv_c):
    b = qkv.shape[0]
    return pl.pallas_call(
        _attn_a_kernel,
        out_shape=jax.ShapeDtypeStruct((b, SEQ, WA_HEADS * HEAD_DIM), F32),
        grid=(b, SEQ // WA_BLOCK),
        in_specs=[
            pl.BlockSpec(memory_space=pltpu.SMEM),
            pl.BlockSpec((1, WA_BLOCK, 2 * LANES), lambda i, j: (i, j, 0)),
            pl.BlockSpec((1, SEQ, LANES), lambda i, j: (i, 0, 2)),
            pl.BlockSpec((1, SEQ, LANES), lambda i, j: (i, 0, 3)),
            pl.BlockSpec((1, CTX_LEN, LANES), lambda i, j: (i, 0, 2)),
            pl.BlockSpec((1, CTX_LEN, LANES), lambda i, j: (i, 0, 3)),
        ],
        out_specs=pl.BlockSpec((1, WA_BLOCK, 2 * LANES), lambda i, j: (i, j, 0)),
        compiler_params=_cparams(("parallel", "arbitrary")),
        name="attn_window",
    )(sink, qkv, qkv, qkv, qkv_c, qkv_c)


NB_ROWS_PER_STEP = 2
NB_LOC = NA_KH * GRID_W
NB_KEYS = NB_LOC + CTX_LEN


def _attn_b_kernel(q_ref, k_ref, v_ref, kc_ref, vc_ref, t_ref, o_ref):
    i = pl.program_id(1)
    kc = kc_ref[0].astype(BF16)
    vc = vc_ref[0].astype(BF16)
    for rr in range(NB_ROWS_PER_STEP):
        r = i * NB_ROWS_PER_STEP + rr
        rs = jnp.clip(r - NA_KH // 2, 0, GRID_ROWS - NA_KH)
        dl = r - rs
        st = pl.multiple_of(rs * GRID_W, GRID_W)
        kw = k_ref[0, pl.ds(st, NB_LOC), :].astype(BF16)
        vw = v_ref[0, pl.ds(st, NB_LOC), :].astype(BF16)
        q = q_ref[0, rr * GRID_W:(rr + 1) * GRID_W, :]
        for h in range(NA_HEADS):
            hs = slice(h * HEAD_DIM, (h + 1) * HEAD_DIM)
            kk = jnp.concatenate([kw[:, hs], kc[:, hs]], axis=0)
            vv = jnp.concatenate([vw[:, hs], vc[:, hs]], axis=0)
            s = _bdot_nt(q[:, hs], kk) * ATT_SCALE + t_ref[dl, h]
            m = jnp.max(s, axis=1, keepdims=True)
            p = jnp.exp(s - m)
            den = jnp.sum(p, axis=1, keepdims=True)
            o_ref[0, rr * GRID_W:(rr + 1) * GRID_W, hs] = _bdot(p, vv) / den


def _attn_b(qkv, qkv_c, table):
    b = qkv.shape[0]
    tq = NB_ROWS_PER_STEP * GRID_W
    w = NA_HEADS * HEAD_DIM
    return pl.pallas_call(
        _attn_b_kernel,
        out_shape=jax.ShapeDtypeStruct((b, SEQ, w), F32),
        grid=(b, SEQ // tq),
        in_specs=[
            pl.BlockSpec((1, tq, w), lambda i, j: (i, j, 0)),
            pl.BlockSpec((1, SEQ, w), lambda i, j: (i, 0, 1)),
            pl.BlockSpec((1, SEQ, w), lambda i, j: (i, 0, 2)),
            pl.BlockSpec((1, CTX_LEN, w), lambda i, j: (i, 0, 1)),
            pl.BlockSpec((1, CTX_LEN, w), lambda i, j: (i, 0, 2)),
            pl.BlockSpec((NA_KH, NA_HEADS, GRID_W, NB_KEYS), lambda i, j: (0, 0, 0, 0)),
        ],
        out_specs=pl.BlockSpec((1, tq, w), lambda i, j: (i, j, 0)),
        compiler_params=_cparams(("parallel", "arbitrary")),
        name="attn_neighbourhood",
    )(qkv, qkv, qkv, qkv_c, qkv_c, table)


def _na_bias_table(rpb):
    dl = np.arange(NA_KH)[:, None, None, None]
    y = np.arange(NA_KH)[None, :, None, None]
    qc = np.arange(GRID_W)[None, None, :, None]
    x = np.arange(GRID_W)[None, None, None, :]
    dy = np.broadcast_to(y - dl + NA_KH - 1, (NA_KH, NA_KH, GRID_W, GRID_W))
    dx = np.broadcast_to(np.clip(x - qc, -(NA_KW - 1), NA_KW - 1) + NA_KW - 1, dy.shape)
    cstart = np.clip(qc - NA_KW // 2, 0, GRID_W - NA_KW)
    inside = np.broadcast_to((x >= cstart) & (x < cstart + NA_KW), dy.shape)
    bias = rpb.astype(F32)[:, dy, dx]
    bias = jnp.where(inside[None], bias, NEG_INF)
    bias = jnp.transpose(bias, (1, 0, 3, 2, 4)).reshape(NA_KH, NA_HEADS, GRID_W, NB_LOC)
    return jnp.concatenate([bias, jnp.zeros((NA_KH, NA_HEADS, GRID_W, CTX_LEN), F32)], axis=-1)


def _ctx_attn_kernel(sink_ref, a_ref, b_ref, oa_ref, ob_ref):
    a = a_ref[0]
    row1 = lax.broadcasted_iota(jnp.int32, (2 * CTX_LEN, 1), 0)
    for g in range(WA_KV_HEADS):
        h0, h1 = 2 * g, 2 * g + 1
        q2 = jnp.concatenate([a[:, h0 * HEAD_DIM:(h0 + 1) * HEAD_DIM],
                              a[:, h1 * HEAD_DIM:(h1 + 1) * HEAD_DIM]], axis=0)
        k = a[:, 2 * LANES + g * HEAD_DIM:2 * LANES + (g + 1) * HEAD_DIM]
        v = a[:, 3 * LANES + g * HEAD_DIM:3 * LANES + (g + 1) * HEAD_DIM]
        s = _bdot_nt(q2, k) * ATT_SCALE
        sink = jnp.where(row1 < CTX_LEN, sink_ref[h0], sink_ref[h1])
        m = jnp.maximum(jnp.max(s, axis=1, keepdims=True), sink)
        p = jnp.exp(s - m)
        den = jnp.sum(p, axis=1, keepdims=True) + jnp.exp(sink - m)
        o = _bdot(p, v) / den
        oa_ref[0, :, h0 * HEAD_DIM:(h0 + 1) * HEAD_DIM] = o[:CTX_LEN]
        oa_ref[0, :, h1 * HEAD_DIM:(h1 + 1) * HEAD_DIM] = o[CTX_LEN:]
    bq = b_ref[0]
    w = NA_HEADS * HEAD_DIM
    for h in range(NA_HEADS):
        hs = slice(h * HEAD_DIM, (h + 1) * HEAD_DIM)
        s = _bdot_nt(bq[:, hs], bq[:, w + h * HEAD_DIM:w + (h + 1) * HEAD_DIM]) * ATT_SCALE
        m = jnp.max(s, axis=1, keepdims=True)
        p = jnp.exp(s - m)
        den = jnp.sum(p, axis=1, keepdims=True)
        ob_ref[0, :, hs] = _bdot(p, bq[:, 2 * w + h * HEAD_DIM:2 * w + (h + 1) * HEAD_DIM]) / den


def _ctx_attn(sink, qkv_a_c, qkv_b_c):
    b = qkv_a_c.shape[0]
    w = 2 * LANES
    return pl.pallas_call(
        _ctx_attn_kernel,
        out_shape=[jax.ShapeDtypeStruct((b, CTX_LEN, w), F32)] * 2,
        grid=(b,),
        in_specs=[
            pl.BlockSpec(memory_space=pltpu.SMEM),
            pl.BlockSpec((1, CTX_LEN, A_COLS), lambda i: (i, 0, 0)),
            pl.BlockSpec((1, CTX_LEN, B_COLS), lambda i: (i, 0, 0)),
        ],
        out_specs=[pl.BlockSpec((1, CTX_LEN, w), lambda i: (i, 0, 0))] * 2,
        compiler_params=_cparams(("parallel",)),
        name="attn_context",
    )(sink, qkv_a_c, qkv_b_c)


CONV_HALO = SUBLANES


def _conv_kernel(prev_ref, cur_ref, next_ref, w_ref, b_ref, o_ref, ext_ref, *, tl, nt):
    j = pl.program_id(1)
    ext_ref[0:CONV_HALO, :] = jnp.where(j > 0, prev_ref[0], 0.0)
    ext_ref[CONV_HALO:CONV_HALO + tl, :] = cur_ref[0]
    ext_ref[CONV_HALO + tl:, :] = jnp.where(j < nt - 1, next_ref[0], 0.0)
    acc = jnp.zeros((tl, XBC_COLS), F32) + b_ref[...]
    base = CONV_HALO - SSM_CONV // 2
    for k in range(SSM_CONV):
        acc = acc + w_ref[k:k + 1, :] * ext_ref[base + k:base + k + tl, :]
    o_ref[0] = _silu(acc)


def _conv_silu(xz, conv_w, conv_b, tl):
    b, t, _ = xz.shape
    nt = t // tl
    hb = tl // CONV_HALO
    last = t // CONV_HALO - 1
    return pl.pallas_call(
        functools.partial(_conv_kernel, tl=tl, nt=nt),
        out_shape=jax.ShapeDtypeStruct((b, t, XBC_COLS), F32),
        grid=(b, nt),
        in_specs=[
            pl.BlockSpec((1, CONV_HALO, XBC_COLS), lambda i, j: (i, jnp.maximum(j * hb - 1, 0), 0)),
            pl.BlockSpec((1, tl, XBC_COLS), lambda i, j: (i, j, 0)),
            pl.BlockSpec((1, CONV_HALO, XBC_COLS), lambda i, j: (i, jnp.minimum((j + 1) * hb, last), 0)),
            pl.BlockSpec((SUBLANES, XBC_COLS), lambda i, j: (0, 0)),
            pl.BlockSpec((1, XBC_COLS), lambda i, j: (0, 0)),
        ],
        out_specs=pl.BlockSpec((1, tl, XBC_COLS), lambda i, j: (i, j, 0)),
        scratch_shapes=[pltpu.VMEM((tl + 2 * CONV_HALO, XBC_COLS), F32)],
        compiler_params=_cparams(("parallel", "parallel")),
        name="ssm_conv",
    )(xz, xz, xz, conv_w, conv_b)


Q = SSM_CHUNK
GS = SSM_GROUPS * SSM_STATE
HPG = SSM_HEADS // SSM_GROUPS


def _softplus(v):
    return jnp.maximum(v, 0.0) + jnp.log1p(jnp.exp(-jnp.abs(v)))


def _cumsum_mat(tri, a):
    a1 = a.astype(BF16)
    r1 = a - a1.astype(F32)
    a2 = r1.astype(BF16)
    a3 = (r1 - a2.astype(F32)).astype(BF16)
    return (jnp.dot(tri, a1, preferred_element_type=F32) + jnp.dot(tri, a2, preferred_element_type=F32)
            + jnp.dot(tri, a3, preferred_element_type=F32))


def _ssd_kernel(xm_ref, dtm_ref, xb_ref, dtb_ref, bias_ref, alog_ref, h0f_ref, h0b_ref,
                y1_ref, y2_ref, hf_out_ref, hb_out_ref, hf_ref, hb_ref, *, nc):
    i = pl.program_id(1)

    @pl.when(i == 0)
    def _():
        hf_ref[...] = h0f_ref[0]
        hb_ref[...] = h0b_ref[0]

    ii = lax.broadcasted_iota(jnp.int32, (Q, Q), 0)
    jj = lax.broadcasted_iota(jnp.int32, (Q, Q), 1)
    lower = ii >= jj
    upper = ii <= jj
    tril = jnp.where(lower, 1.0, 0.0).astype(BF16)
    triu = jnp.where(upper, 1.0, 0.0).astype(BF16)
    a_row = -jnp.exp(alog_ref[...])

    xbc = xm_ref[0]
    dt = _softplus(dtm_ref[0] + bias_ref[...])
    a = dt * a_row
    cumf = _cumsum_mat(tril, a)
    cumb = _cumsum_mat(triu, a)
    cumf_t, cumb_t, dt_t = cumf.T, cumb.T, dt.T
    for g in range(SSM_GROUPS):
        bg = xbc[:, SSM_INNER + g * SSM_STATE:SSM_INNER + (g + 1) * SSM_STATE]
        cg = xbc[:, SSM_INNER + GS + g * SSM_STATE:SSM_INNER + GS + (g + 1) * SSM_STATE]
        bg_t = bg.T
        gmat = _bdot(cg, bg_t)
        cg16 = cg.astype(BF16)
        for hh in range(HPG):
            h = g * HPG + hh
            hb_ = SSM_HEADS + h
            xh = xbc[:, h * SSM_HEAD_DIM:(h + 1) * SSM_HEAD_DIM].astype(BF16)
            cf, rf, dtf = cumf[:, h:h + 1], cumf_t[h:h + 1, :], dt_t[h:h + 1, :]
            cb, rb, dtb = cumb[:, hb_:hb_ + 1], cumb_t[hb_:hb_ + 1, :], dt_t[hb_:hb_ + 1, :]
            lf = jnp.exp(jnp.where(lower, cf - rf, NEG_INF))
            lb = jnp.exp(jnp.where(upper, cb - rb, NEG_INF))
            wmat = gmat * (lf * dtf + lb * dtb)
            y = _bdot(wmat, xh)
            hprev = hf_ref[h]
            y = y + jnp.exp(cf) * jnp.dot(cg16, hprev.astype(BF16), preferred_element_type=F32)
            y1_ref[0, :, h * SSM_HEAD_DIM:(h + 1) * SSM_HEAD_DIM] = y
            tot = cumf_t[h:h + 1, Q - 1:Q]
            wrow = jnp.exp(tot - rf) * dtf
            hf_ref[h] = hprev * jnp.exp(tot) + _bdot(bg_t * wrow, xh)

    xbc = xb_ref[0]
    dt = _softplus(dtb_ref[0] + bias_ref[...])
    cumb = _cumsum_mat(triu, dt * a_row)
    cumb_t, dt_t = cumb.T, dt.T
    for g in range(SSM_GROUPS):
        bg_t = xbc[:, SSM_INNER + g * SSM_STATE:SSM_INNER + (g + 1) * SSM_STATE].T
        cg16 = xbc[:, SSM_INNER + GS + g * SSM_STATE:SSM_INNER + GS + (g + 1) * SSM_STATE].astype(BF16)
        for hh in range(HPG):
            h = g * HPG + hh
            hb_ = SSM_HEADS + h
            xh = xbc[:, h * SSM_HEAD_DIM:(h + 1) * SSM_HEAD_DIM].astype(BF16)
            cb, rb, dtb = cumb[:, hb_:hb_ + 1], cumb_t[hb_:hb_ + 1, :], dt_t[hb_:hb_ + 1, :]
            hprev = hb_ref[h]
            y2_ref[0, :, h * SSM_HEAD_DIM:(h + 1) * SSM_HEAD_DIM] = (
                jnp.exp(cb) * jnp.dot(cg16, hprev.astype(BF16), preferred_element_type=F32))
            tot = cumb_t[hb_:hb_ + 1, 0:1]
            wrow = jnp.exp(tot - rb) * dtb
            hb_ref[h] = hprev * jnp.exp(tot) + _bdot(bg_t * wrow, xh)

    @pl.when(i == nc - 1)
    def _():
        hf_out_ref[0] = hf_ref[...]
        hb_out_ref[0] = hb_ref[...]


def _ssd(xbc, dt_raw, dt_bias, a_log, h0f, h0b):
    b, t, _ = xbc.shape
    nc = t // Q
    st_shape = (b, SSM_HEADS, SSM_STATE, SSM_HEAD_DIM)
    st_spec = pl.BlockSpec((1, SSM_HEADS, SSM_STATE, SSM_HEAD_DIM), lambda i, j: (i, 0, 0, 0))
    fwd = lambda i, j: (i, j, 0)
    bwd = lambda i, j: (i, nc - 1 - j, 0)
    return pl.pallas_call(
        functools.partial(_ssd_kernel, nc=nc),
        out_shape=[jax.ShapeDtypeStruct((b, t, SSM_INNER), F32)] * 2
        + [jax.ShapeDtypeStruct(st_shape, F32)] * 2,
        grid=(b, nc),
        in_specs=[
            pl.BlockSpec((1, Q, XBC_COLS), fwd),
            pl.BlockSpec((1, Q, DT_PAD), fwd),
            pl.BlockSpec((1, Q, XBC_COLS), bwd),
            pl.BlockSpec((1, Q, DT_PAD), bwd),
            pl.BlockSpec((1, DT_PAD), lambda i, j: (0, 0)),
            pl.BlockSpec((1, DT_PAD), lambda i, j: (0, 0)),
            st_spec, st_spec,
        ],
        out_specs=[pl.BlockSpec((1, Q, SSM_INNER), fwd), pl.BlockSpec((1, Q, SSM_INNER), bwd),
                   st_spec, st_spec],
        scratch_shapes=[pltpu.VMEM((SSM_HEADS, SSM_STATE, SSM_HEAD_DIM), F32)] * 2,
        compiler_params=_cparams(("parallel", "arbitrary")),
        name="ssd_scan",
    )(xbc, dt_raw, xbc, dt_raw, dt_bias, a_log, h0f, h0b)


def _outproj_kernel(x_ref, mod_ref, oa_ref, ob_ref, y1_ref, y2_ref, xs_ref, z_ref, d_ref, g_ref,
                    w_ref, o_ref):
    y = y1_ref[0] + y2_ref[0] + d_ref[...] * xs_ref[0]
    y = y * _silu(z_ref[0])
    var = jnp.mean(y * y, axis=-1, keepdims=True)
    oc = y * lax.rsqrt(var + EPS) * g_ref[...]
    wa = 2 * LANES
    mix = (jnp.dot(oa_ref[0].astype(BF16), w_ref[0:wa, :], preferred_element_type=F32)
           + jnp.dot(ob_ref[0].astype(BF16), w_ref[wa:2 * wa, :], preferred_element_type=F32)
           + jnp.dot(oc.astype(BF16), w_ref[2 * wa:, :], preferred_element_type=F32))
    o_ref[0] = x_ref[0] + mod_ref[0, 2:3, :] * mix


def _outproj(x, mods, o_a, o_b, y1, y2, xbc, xz, d_exp, ssm_g, w, tm):
    b, t, _ = x.shape
    per_batch = mods.shape[0] > 1
    row = lambda n: pl.BlockSpec((1, tm, n), lambda i, j: (i, j, 0))
    return pl.pallas_call(
        _outproj_kernel,
        out_shape=jax.ShapeDtypeStruct((b, t, D_MODEL), F32),
        grid=(b, t // tm),
        in_specs=[
            row(D_MODEL),
            pl.BlockSpec((1, SUBLANES, D_MODEL), (lambda i, j: (i, 0, 0)) if per_batch else (lambda i, j: (0, 0, 0))),
            row(2 * LANES), row(2 * LANES), row(SSM_INNER), row(SSM_INNER),
            pl.BlockSpec((1, tm, SSM_INNER), lambda i, j: (i, j, 0)),
            pl.BlockSpec((1, tm, SSM_INNER), lambda i, j: (i, j, 2)),
            pl.BlockSpec((1, SSM_INNER), lambda i, j: (0, 0)),
            pl.BlockSpec((1, SSM_INNER), lambda i, j: (0, 0)),
            pl.BlockSpec((D_MODEL, D_MODEL), lambda i, j: (0, 0)),
        ],
        out_specs=row(D_MODEL),
        compiler_params=_cparams(("parallel", "parallel")),
        name="outproj",
    )(x, mods, o_a, o_b, y1, y2, xbc, xz, d_exp, ssm_g, w)


FFN_NF = 2
FFN_TF = D_FF // FFN_NF


def _ffn_kernel(x_ref, mod_ref, g_ref, wg_ref, wu_ref, wo_ref, gfin_ref, o_ref, h_ref, acc_ref, *, final):
    f = pl.program_id(2)

    @pl.when(f == 0)
    def _():
        h_ref[...] = _norm_mod(x_ref[0], g_ref[...], mod_ref[0, 3:4, :], mod_ref[0, 4:5, :]).astype(BF16)
        acc_ref[...] = jnp.zeros_like(acc_ref)

    h = h_ref[...]
    gate = jnp.dot(h, wg_ref[...], preferred_element_type=F32)
    up = jnp.dot(h, wu_ref[...], preferred_element_type=F32)
    act = (_silu(gate) * up).astype(BF16)
    acc_ref[...] += jnp.dot(act, wo_ref[...], preferred_element_type=F32)

    @pl.when(f == FFN_NF - 1)
    def _():
        out = x_ref[0] + mod_ref[0, 5:6, :] * acc_ref[...]
        if final:
            var = jnp.mean(out * out, axis=-1, keepdims=True)
            out = out * lax.rsqrt(var + EPS) * gfin_ref[...]
        o_ref[0] = out


def _ffn(x, mods, g, w_in, w_out, g_final, tm, final):
    b, t, _ = x.shape
    per_batch = mods.shape[0] > 1
    return pl.pallas_call(
        functools.partial(_ffn_kernel, final=final),
        out_shape=jax.ShapeDtypeStruct((b, t, D_MODEL), F32),
        grid=(b, t // tm, FFN_NF),
        in_specs=[
            pl.BlockSpec((1, tm, D_MODEL), lambda i, j, f: (i, j, 0)),
            pl.BlockSpec((1, SUBLANES, D_MODEL),
                         (lambda i, j, f: (i, 0, 0)) if per_batch else (lambda i, j, f: (0, 0, 0))),
            pl.BlockSpec((1, D_MODEL), lambda i, j, f: (0, 0)),
            pl.BlockSpec((D_MODEL, FFN_TF), lambda i, j, f: (0, f)),
            pl.BlockSpec((D_MODEL, FFN_TF), lambda i, j, f: (0, FFN_NF + f)),
            pl.BlockSpec((FFN_TF, D_MODEL), lambda i, j, f: (f, 0)),
            pl.BlockSpec((1, D_MODEL), lambda i, j, f: (0, 0)),
        ],
        out_specs=pl.BlockSpec((1, tm, D_MODEL), lambda i, j, f: (i, j, 0)),
        scratch_shapes=[pltpu.VMEM((tm, D_MODEL), BF16), pltpu.VMEM((tm, D_MODEL), F32)],
        compiler_params=_cparams(("parallel", "parallel", "arbitrary")),
        name="ffn_final" if final else "ffn",
    )(x, mods, g, w_in, w_in, w_out, g_final)


def _rope_tables():
    t = np.arange(SEQ)
    pos = np.stack([t // GRID_W, t % GRID_W], axis=1).astype(np.float32)
    quarter = HEAD_DIM // 4
    inv_freq = jnp.asarray(ROPE_BASE, F32) ** (-jnp.arange(quarter, dtype=F32) / quarter)
    lane = np.arange(LANES) % HEAD_DIM
    half = lane // (HEAD_DIM // 2)
    idx = lane % (HEAD_DIM // 2)
    ang = jnp.asarray(pos)[:, half] * inv_freq[idx % quarter][None, :]
    cos, sin = jnp.cos(ang), jnp.sin(ang)
    first = jnp.asarray(idx < quarter)[None, :]
    return cos, jnp.where(first, -sin, 0.0), jnp.where(first, 0.0, sin)


def _proj_weight(w_in):
    qa, qb, z, ka, va, kb, vb, xbc, dtc = jnp.split(
        w_in, np.cumsum([256, 256, 512, 128, 128, 256, 256, XBC_COLS])[:8].tolist(), axis=1)
    pad = jnp.zeros((D_MODEL, DT_PAD - DT_COLS), w_in.dtype)
    return jnp.concatenate([qa, ka, va, qb, kb, vb, xbc, z, dtc, pad], axis=1).astype(BF16)


def _pad_lanes(v, n=LANES):
    v = v.reshape(1, -1)
    return jnp.pad(v, ((0, 0), (0, n - v.shape[1])))


def kernel(x, c, ctx, c_ctx, w_mod, b_mod, g_mix, w_in, wa_sink, na_rpb, ssm_conv_w, ssm_conv_b,
           ssm_dt_bias, ssm_a_log, ssm_d, ssm_norm_g, w_out, g_ffn, w_ffn_in, w_ffn_out, g_final):
    cin = jnp.concatenate([c, c_ctx[None, :], jnp.zeros((SUBLANES - BATCH - 1, D_MODEL), F32)], axis=0)
    mod_all = _modulation(cin, w_mod, b_mod)
    rope_tabs = _rope_tables()
    zeros_state = jnp.zeros((BATCH, SSM_HEADS, SSM_STATE, SSM_HEAD_DIM), F32)
    gfin = g_final.reshape(1, D_MODEL)

    xl, xc = x, ctx
    for l in range(DEPTH):
        last = l == DEPTH - 1
        m6 = mod_all[l].reshape(SUBLANES, 6, D_MODEL)
        mods_l = jnp.pad(m6[:BATCH], ((0, 0), (0, 2), (0, 0)))
        mods_c = jnp.pad(m6[BATCH:BATCH + 1], ((0, 0), (0, 2), (0, 0)))
        w_proj = _proj_weight(w_in[l])
        g1 = g_mix[l].reshape(1, D_MODEL)
        g2 = g_ffn[l].reshape(1, D_MODEL)
        conv_w = jnp.pad(ssm_conv_w[l], ((0, SUBLANES - SSM_CONV), (0, 0)))
        conv_b = ssm_conv_b[l].reshape(1, XBC_COLS)
        dt_bias = _pad_lanes(ssm_dt_bias[l])
        a_log = _pad_lanes(ssm_a_log[l])
        d_exp = jnp.repeat(ssm_d[l], SSM_HEAD_DIM).reshape(1, SSM_INNER)
        sg = ssm_norm_g[l].reshape(1, SSM_INNER)
        wo = w_out[l].astype(BF16)
        wfi = w_ffn_in[l].astype(BF16)
        wfo = w_ffn_out[l].astype(BF16)

        a_c, b_c, xz_c, dt_c = _inproj(xc, mods_c, g1, w_proj, None, CTX_LEN)
        a_l, b_l, xz_l, dt_l = _inproj(xl, mods_l, g1, w_proj, rope_tabs, 512)

        o_a = _attn_a(wa_sink[l], a_l, a_c)
        o_b = _attn_b(b_l, b_c, _na_bias_table(na_rpb[l]))

        xbc_c = _conv_silu(xz_c, conv_w, conv_b, CTX_LEN)
        xbc_l = _conv_silu(xz_l, conv_w, conv_b, 512)
        y1_c, y2_c, h_f, h_b = _ssd(xbc_c, dt_c, dt_bias, a_log, zeros_state, zeros_state)
        y1_l, y2_l, _, _ = _ssd(xbc_l, dt_l, dt_bias, a_log, h_f, h_b)

        xl = _outproj(xl, mods_l, o_a, o_b, y1_l, y2_l, xbc_l, xz_l, d_exp, sg, wo, 512)
        xl = _ffn(xl, mods_l, g2, wfi, wfo, gfin, 512, last)
        if not last:
            o_ac, o_bc = _ctx_attn(wa_sink[l], a_c, b_c)
            xc = _outproj(xc, mods_c, o_ac, o_bc, y1_c, y2_c, xbc_c, xz_c, d_exp, sg, wo, CTX_LEN)
            xc = _ffn(xc, mods_c, g2, wfi, wfo, gfin, CTX_LEN, False)
    return xl
```

```python
import functools
import math

import numpy as np
import jax
import jax.numpy as jnp
from jax import lax
from jax.experimental import pallas as pl
from jax.experimental.pallas import tpu as pltpu

F32 = jnp.float32
BF16 = jnp.bfloat16

D_MODEL = 1024
BATCH = 4
SEQ = 4096
DEPTH = 2
GRID_W = 64
GRID_ROWS = SEQ // GRID_W
CTX_LEN = 256
EPS = 1e-6
HEAD_DIM = 64
ROPE_BASE = 10000.0
WA_HEADS = 4
WA_KV_HEADS = 2
WA_WINDOW = 128
WA_BLOCK = 128
NA_HEADS = 4
NA_KH = 8
NA_KW = 16
SSM_HEADS = 8
SSM_HEAD_DIM = 64
SSM_INNER = SSM_HEADS * SSM_HEAD_DIM
SSM_GROUPS = 2
SSM_STATE = 128
SSM_CONV = 7
SSM_CHUNK = 128
D_FF = 2816
XBC_COLS = SSM_INNER + 2 * SSM_GROUPS * SSM_STATE
DT_COLS = 2 * SSM_HEADS

LANES = 128
SUBLANES = 8
VMEM_LIMIT = 56 * 1024 * 1024

A_COLS = 512
B_COLS = 768
XZ_COLS = 1536
DT_PAD = LANES
PROJ_COLS = A_COLS + B_COLS + XZ_COLS + DT_PAD

ATT_SCALE = HEAD_DIM ** -0.5
NEG_INF = float("-inf")
NT_DIMS = (((1,), (1,)), ((), ()))


def _silu(v):
    return v / (1.0 + jnp.exp(-v))


def _bdot(a, b):
    return jnp.dot(a.astype(BF16), b.astype(BF16), preferred_element_type=F32)


def _bdot_nt(a, b):
    return lax.dot_general(a.astype(BF16), b.astype(BF16), NT_DIMS, preferred_element_type=F32)


def _cparams(sem):
    return pltpu.CompilerParams(dimension_semantics=sem, vmem_limit_bytes=VMEM_LIMIT)


MOD_TN = 1536


def _mod_kernel(c_ref, w_ref, b_ref, o_ref):
    s = _silu(c_ref[...])
    o_ref[0] = _bdot(s, w_ref[0]) + b_ref[0]


def _modulation(cin, w_mod, b_mod):
    n = 6 * D_MODEL
    return pl.pallas_call(
        _mod_kernel,
        out_shape=jax.ShapeDtypeStruct((DEPTH, SUBLANES, n), F32),
        grid=(DEPTH, n // MOD_TN),
        in_specs=[
            pl.BlockSpec((SUBLANES, D_MODEL), lambda l, j: (0, 0)),
            pl.BlockSpec((1, D_MODEL, MOD_TN), lambda l, j: (l, 0, j)),
            pl.BlockSpec((1, 1, MOD_TN), lambda l, j: (l, 0, j)),
        ],
        out_specs=pl.BlockSpec((1, SUBLANES, MOD_TN), lambda l, j: (l, 0, j)),
        compiler_params=_cparams(("parallel", "parallel")),
        name="modulation",
    )(cin, w_mod, b_mod.reshape(DEPTH, 1, n))


def _norm_mod(x, g, shift, scale):
    var = jnp.mean(x * x, axis=-1, keepdims=True)
    h = x * lax.rsqrt(var + EPS) * g
    return h * (1.0 + scale) + shift


W_QA, W_QB, W_Z, W_KVA, W_KVB, W_XBC, W_DT = 0, 256, 512, 1024, 1280, 1792, 2816
IN_COLS = W_DT + DT_COLS


def _inproj_kernel(x_ref, mod_ref, g_ref, w_ref, wdt_ref, *rest, rope):
    if rope:
        cos_ref, sa_ref, sb_ref, oa_ref, ob_ref, oxz_ref, odt_ref = rest
    else:
        oa_ref, ob_ref, oxz_ref, odt_ref = rest
    h = _norm_mod(x_ref[0], g_ref[...], mod_ref[0, 0:1, :], mod_ref[0, 1:2, :]).astype(BF16)

    def proj(lo, hi):
        return jnp.dot(h, w_ref[:, lo:hi], preferred_element_type=F32)

    qa = proj(W_QA, W_QB)
    kva = proj(W_KVA, W_KVB)
    if rope:
        cos, sa, sb = cos_ref[...], sa_ref[...], sb_ref[...]

        def rot(v):
            up = pltpu.roll(v, LANES - 16, axis=1)
            dn = pltpu.roll(v, 16, axis=1)
            return v * cos + up * sa + dn * sb

        oa_ref[0, :, 0:LANES] = rot(qa[:, 0:LANES])
        oa_ref[0, :, LANES:2 * LANES] = rot(qa[:, LANES:])
        oa_ref[0, :, 2 * LANES:3 * LANES] = rot(kva[:, 0:LANES])
        oa_ref[0, :, 3 * LANES:] = kva[:, LANES:]
    else:
        oa_ref[0, :, 0:2 * LANES] = qa
        oa_ref[0, :, 2 * LANES:] = kva
    ob_ref[0, :, 0:2 * LANES] = proj(W_QB, W_Z)
    ob_ref[0, :, 2 * LANES:] = proj(W_KVB, W_XBC)
    oxz_ref[0, :, 0:XBC_COLS] = proj(W_XBC, W_DT)
    oxz_ref[0, :, XBC_COLS:] = proj(W_Z, W_KVA)
    odt_ref[0] = jnp.dot(h, wdt_ref[...], preferred_element_type=F32)


def _inproj(x, mods, g, w, wdt, rope_tabs, tm):
    b, t, _ = x.shape
    rope = rope_tabs is not None
    per_batch = mods.shape[0] > 1
    in_specs = [
        pl.BlockSpec((1, tm, D_MODEL), lambda i, j: (i, j, 0)),
        pl.BlockSpec((1, SUBLANES, D_MODEL), (lambda i, j: (i, 0, 0)) if per_batch else (lambda i, j: (0, 0, 0))),
        pl.BlockSpec((1, D_MODEL), lambda i, j: (0, 0)),
        pl.BlockSpec((D_MODEL, IN_COLS), lambda i, j: (0, 0)),
        pl.BlockSpec((D_MODEL, DT_PAD), lambda i, j: (0, 0)),
    ]
    args = [x, mods, g, w, wdt]
    if rope:
        in_specs += [pl.BlockSpec((tm, LANES), lambda i, j: (j, 0))] * 3
        args += list(rope_tabs)
    widths = (A_COLS, B_COLS, XZ_COLS, DT_PAD)
    return pl.pallas_call(
        functools.partial(_inproj_kernel, rope=rope),
        out_shape=[jax.ShapeDtypeStruct((b, t, n), F32) for n in widths],
        grid=(b, t // tm),
        in_specs=in_specs,
        out_specs=[pl.BlockSpec((1, tm, n), lambda i, j: (i, j, 0)) for n in widths],
        compiler_params=_cparams(("parallel", "parallel")),
        name="inproj_rope" if rope else "inproj",
    )(*args)


WA_KEYS = 3 * WA_BLOCK


def _attn_a_kernel(sink_ref, q_ref, k_ref, v_ref, kc_ref, vc_ref, o_ref):
    n = pl.program_id(1)
    ks = pl.multiple_of(jnp.clip((n - 1) * WA_BLOCK, 0, SEQ - WA_KEYS), WA_BLOCK)
    q = q_ref[0]
    kall = jnp.concatenate([k_ref[0, pl.ds(ks, WA_KEYS), :], kc_ref[0]], axis=0).astype(BF16)
    vall = jnp.concatenate([v_ref[0, pl.ds(ks, WA_KEYS), :], vc_ref[0]], axis=0).astype(BF16)
    nk = WA_KEYS + CTX_LEN
    rows = lax.broadcasted_iota(jnp.int32, (2 * WA_BLOCK, nk), 0)
    cols = lax.broadcasted_iota(jnp.int32, (2 * WA_BLOCK, nk), 1)
    qpos = n * WA_BLOCK + jnp.where(rows >= WA_BLOCK, rows - WA_BLOCK, rows)
    valid = (cols >= WA_KEYS) | (jnp.abs(qpos - (ks + cols)) <= WA_WINDOW)
    row1 = lax.broadcasted_iota(jnp.int32, (2 * WA_BLOCK, 1), 0)
    for g in range(WA_KV_HEADS):
        h0, h1 = 2 * g, 2 * g + 1
        q2 = jnp.concatenate([q[:, h0 * HEAD_DIM:(h0 + 1) * HEAD_DIM],
                              q[:, h1 * HEAD_DIM:(h1 + 1) * HEAD_DIM]], axis=0)
        s = _bdot_nt(q2, kall[:, g * HEAD_DIM:(g + 1) * HEAD_DIM]) * ATT_SCALE
        s = jnp.where(valid, s, NEG_INF)
        sink = jnp.where(row1 < WA_BLOCK, sink_ref[h0], sink_ref[h1])
        m = jnp.maximum(jnp.max(s, axis=1, keepdims=True), sink)
        p = jnp.exp(s - m)
        den = jnp.sum(p, axis=1, keepdims=True) + jnp.exp(sink - m)
        o = _bdot(p, vall[:, g * HEAD_DIM:(g + 1) * HEAD_DIM]) / den
        o_ref[0, :, h0 * HEAD_DIM:(h0 + 1) * HEAD_DIM] = o[:WA_BLOCK]
        o_ref[0, :, h1 * HEAD_DIM:(h1 + 1) * HEAD_DIM] = o[WA_BLOCK:]


def _attn_a(sink, qkv, qkv_c):
    b = qkv.shape[0]
    return pl.pallas_call(
        _attn_a_kernel,
        out_shape=jax.ShapeDtypeStruct((b, SEQ, WA_HEADS * HEAD_DIM), F32),
        grid=(b, SEQ // WA_BLOCK),
        in_specs=[
            pl.BlockSpec(memory_space=pltpu.SMEM),
            pl.BlockSpec((1, WA_BLOCK, 2 * LANES), lambda i, j: (i, j, 0)),
            pl.BlockSpec((1, SEQ, LANES), lambda i, j: (i, 0, 2)),
            pl.BlockSpec((1, SEQ, LANES), lambda i, j: (i, 0, 3)),
            pl.BlockSpec((1, CTX_LEN, LANES), lambda i, j: (i, 0, 2)),
            pl.BlockSpec((1, CTX_LEN, LANES), lambda i, j: (i, 0, 3)),
        ],
        out_specs=pl.BlockSpec((1, WA_BLOCK, 2 * LANES), lambda i, j: (i, j, 0)),
        compiler_params=_cparams(("parallel", "arbitrary")),
        name="attn_window",
    )(sink, qkv, qkv, qkv, qkv_c, qkv_c)


NB_ROWS_PER_STEP = 2
NB_LOC = NA_KH * GRID_W
NB_DY_PAIRS = 2 * NA_KH - 2


def _attn_b_kernel(q_ref, k_ref, v_ref, kc_ref, vc_ref, t_ref, o_ref):
    i = pl.program_id(1)
    kc = kc_ref[0].astype(BF16)
    vc = vc_ref[0].astype(BF16)
    for rr in range(NB_ROWS_PER_STEP):
        r = i * NB_ROWS_PER_STEP + rr
        rs = jnp.clip(r - NA_KH // 2, 0, GRID_ROWS - NA_KH)
        dl = r - rs
        st = pl.multiple_of(rs * GRID_W, GRID_W)
        kw = k_ref[0, pl.ds(st, NB_LOC), :].astype(BF16)
        vw = v_ref[0, pl.ds(st, NB_LOC), :].astype(BF16)
        q = q_ref[0, rr * GRID_W:(rr + 1) * GRID_W, :]
        for h in range(NA_HEADS):
            hs = slice(h * HEAD_DIM, (h + 1) * HEAD_DIM)
            qh = q[:, hs].astype(BF16)
            bias = jnp.concatenate([t_ref[h, 2 * k - dl + NA_KH - 1] for k in range(NA_KH // 2)], axis=1)
            s_loc = _bdot_nt(qh, kw[:, hs]) * ATT_SCALE + bias
            s_ctx = _bdot_nt(qh, kc[:, hs]) * ATT_SCALE
            m = jnp.maximum(jnp.max(s_loc, axis=1, keepdims=True), jnp.max(s_ctx, axis=1, keepdims=True))
            p_loc = jnp.exp(s_loc - m)
            p_ctx = jnp.exp(s_ctx - m)
            den = jnp.sum(p_loc, axis=1, keepdims=True) + jnp.sum(p_ctx, axis=1, keepdims=True)
            o = _bdot(p_loc, vw[:, hs]) + _bdot(p_ctx, vc[:, hs])
            o_ref[0, rr * GRID_W:(rr + 1) * GRID_W, hs] = o / den


def _attn_b(qkv, qkv_c, table):
    b = qkv.shape[0]
    tq = NB_ROWS_PER_STEP * GRID_W
    w = NA_HEADS * HEAD_DIM
    return pl.pallas_call(
        _attn_b_kernel,
        out_shape=jax.ShapeDtypeStruct((b, SEQ, w), F32),
        grid=(b, SEQ // tq),
        in_specs=[
            pl.BlockSpec((1, tq, w), lambda i, j: (i, j, 0)),
            pl.BlockSpec((1, SEQ, w), lambda i, j: (i, 0, 1)),
            pl.BlockSpec((1, SEQ, w), lambda i, j: (i, 0, 2)),
            pl.BlockSpec((1, CTX_LEN, w), lambda i, j: (i, 0, 1)),
            pl.BlockSpec((1, CTX_LEN, w), lambda i, j: (i, 0, 2)),
            pl.BlockSpec((NA_HEADS, NB_DY_PAIRS, GRID_W, LANES), lambda i, j: (0, 0, 0, 0)),
        ],
        out_specs=pl.BlockSpec((1, tq, w), lambda i, j: (i, j, 0)),
        compiler_params=_cparams(("parallel", "arbitrary")),
        name="attn_neighbourhood",
    )(qkv, qkv, qkv, qkv_c, qkv_c, table)


def _split3(a):
    a1 = a.astype(BF16)
    r1 = a - a1.astype(F32)
    a2 = r1.astype(BF16)
    a3 = (r1 - a2.astype(F32)).astype(BF16)
    return a1, a2, a3


def _bias_kernel(r_ref, oh_ref, o_ref):
    oh = oh_ref[...]
    o_ref[...] = sum(jnp.dot(t, oh, preferred_element_type=F32) for t in _split3(r_ref[...]))


def _na_bias_tables(rpb):
    ndy, ndx = 2 * NA_KH - 1, 2 * NA_KW - 1
    qc = np.arange(GRID_W)[:, None]
    x = np.arange(GRID_W)[None, :]
    dx = np.clip(x - qc, -(NA_KW - 1), NA_KW - 1) + NA_KW - 1
    onehot = (np.arange(LANES)[:, None, None] == dx[None]).reshape(LANES, GRID_W * GRID_W)
    cstart = np.clip(qc - NA_KW // 2, 0, GRID_W - NA_KW)
    inside = (x >= cstart) & (x < cstart + NA_KW)
    rows = DEPTH * NA_HEADS * ndy
    r = jnp.pad(rpb.astype(F32).reshape(rows, ndx), ((0, LANES - rows), (0, LANES - ndx)))
    m = pl.pallas_call(
        _bias_kernel,
        out_shape=jax.ShapeDtypeStruct((LANES, GRID_W * GRID_W), F32),
        name="na_bias_expand",
    )(r, jnp.asarray(onehot, BF16))
    m = m[:rows].reshape(DEPTH, NA_HEADS, ndy, GRID_W, GRID_W)
    m = jnp.where(jnp.asarray(inside), m, NEG_INF)
    return jnp.concatenate([m[:, :, :ndy - 1], m[:, :, 1:]], axis=-1)


def _ctx_attn_kernel(sink_ref, a_ref, b_ref, oa_ref, ob_ref):
    a = a_ref[0]
    row1 = lax.broadcasted_iota(jnp.int32, (2 * CTX_LEN, 1), 0)
    for g in range(WA_KV_HEADS):
        h0, h1 = 2 * g, 2 * g + 1
        q2 = jnp.concatenate([a[:, h0 * HEAD_DIM:(h0 + 1) * HEAD_DIM],
                              a[:, h1 * HEAD_DIM:(h1 + 1) * HEAD_DIM]], axis=0)
        k = a[:, 2 * LANES + g * HEAD_DIM:2 * LANES + (g + 1) * HEAD_DIM]
        v = a[:, 3 * LANES + g * HEAD_DIM:3 * LANES + (g + 1) * HEAD_DIM]
        s = _bdot_nt(q2, k) * ATT_SCALE
        sink = jnp.where(row1 < CTX_LEN, sink_ref[h0], sink_ref[h1])
        m = jnp.maximum(jnp.max(s, axis=1, keepdims=True), sink)
        p = jnp.exp(s - m)
        den = jnp.sum(p, axis=1, keepdims=True) + jnp.exp(sink - m)
        o = _bdot(p, v) / den
        oa_ref[0, :, h0 * HEAD_DIM:(h0 + 1) * HEAD_DIM] = o[:CTX_LEN]
        oa_ref[0, :, h1 * HEAD_DIM:(h1 + 1) * HEAD_DIM] = o[CTX_LEN:]
    bq = b_ref[0]
    w = NA_HEADS * HEAD_DIM
    for h in range(NA_HEADS):
        hs = slice(h * HEAD_DIM, (h + 1) * HEAD_DIM)
        s = _bdot_nt(bq[:, hs], bq[:, w + h * HEAD_DIM:w + (h + 1) * HEAD_DIM]) * ATT_SCALE
        m = jnp.max(s, axis=1, keepdims=True)
        p = jnp.exp(s - m)
        den = jnp.sum(p, axis=1, keepdims=True)
        ob_ref[0, :, hs] = _bdot(p, bq[:, 2 * w + h * HEAD_DIM:2 * w + (h + 1) * HEAD_DIM]) / den


def _ctx_attn(sink, qkv_a_c, qkv_b_c):
    b = qkv_a_c.shape[0]
    w = 2 * LANES
    return pl.pallas_call(
        _ctx_attn_kernel,
        out_shape=[jax.ShapeDtypeStruct((b, CTX_LEN, w), F32)] * 2,
        grid=(b,),
        in_specs=[
            pl.BlockSpec(memory_space=pltpu.SMEM),
            pl.BlockSpec((1, CTX_LEN, A_COLS), lambda i: (i, 0, 0)),
            pl.BlockSpec((1, CTX_LEN, B_COLS), lambda i: (i, 0, 0)),
        ],
        out_specs=[pl.BlockSpec((1, CTX_LEN, w), lambda i: (i, 0, 0))] * 2,
        compiler_params=_cparams(("parallel",)),
        name="attn_context",
    )(sink, qkv_a_c, qkv_b_c)


CONV_HALO = SUBLANES


def _conv_kernel(prev_ref, cur_ref, next_ref, w_ref, b_ref, o_ref, ext_ref, *, tl, nt):
    j = pl.program_id(1)
    ext_ref[0:CONV_HALO, :] = jnp.where(j > 0, prev_ref[0], 0.0)
    ext_ref[CONV_HALO:CONV_HALO + tl, :] = cur_ref[0]
    ext_ref[CONV_HALO + tl:, :] = jnp.where(j < nt - 1, next_ref[0], 0.0)
    acc = jnp.zeros((tl, XBC_COLS), F32) + b_ref[...]
    base = CONV_HALO - SSM_CONV // 2
    for k in range(SSM_CONV):
        acc = acc + w_ref[k:k + 1, :] * ext_ref[base + k:base + k + tl, :]
    o_ref[0] = _silu(acc)


def _conv_silu(xz, conv_w, conv_b, tl):
    b, t, _ = xz.shape
    nt = t // tl
    hb = tl // CONV_HALO
    last = t // CONV_HALO - 1
    return pl.pallas_call(
        functools.partial(_conv_kernel, tl=tl, nt=nt),
        out_shape=jax.ShapeDtypeStruct((b, t, XBC_COLS), F32),
        grid=(b, nt),
        in_specs=[
            pl.BlockSpec((1, CONV_HALO, XBC_COLS), lambda i, j: (i, jnp.maximum(j * hb - 1, 0), 0)),
            pl.BlockSpec((1, tl, XBC_COLS), lambda i, j: (i, j, 0)),
            pl.BlockSpec((1, CONV_HALO, XBC_COLS), lambda i, j: (i, jnp.minimum((j + 1) * hb, last), 0)),
            pl.BlockSpec((SUBLANES, XBC_COLS), lambda i, j: (0, 0)),
            pl.BlockSpec((1, XBC_COLS), lambda i, j: (0, 0)),
        ],
        out_specs=pl.BlockSpec((1, tl, XBC_COLS), lambda i, j: (i, j, 0)),
        scratch_shapes=[pltpu.VMEM((tl + 2 * CONV_HALO, XBC_COLS), F32)],
        compiler_params=_cparams(("parallel", "parallel")),
        name="ssm_conv",
    )(xz, xz, xz, conv_w, conv_b)


Q = SSM_CHUNK
GS = SSM_GROUPS * SSM_STATE
HPG = SSM_HEADS // SSM_GROUPS


def _softplus(v):
    return jnp.maximum(v, 0.0) + jnp.log1p(jnp.exp(-jnp.abs(v)))


def _cumsum_mat(tri, a):
    return sum(jnp.dot(tri, t, preferred_element_type=F32) for t in _split3(a))


def _ssd_kernel(xm_ref, dtm_ref, xb_ref, dtb_ref, bias_ref, alog_ref, h0f_ref, h0b_ref,
                y1_ref, y2_ref, hf_out_ref, hb_out_ref, hf_ref, hb_ref, *, nc):
    i = pl.program_id(1)

    @pl.when(i == 0)
    def _():
        hf_ref[...] = h0f_ref[0]
        hb_ref[...] = h0b_ref[0]

    ii = lax.broadcasted_iota(jnp.int32, (Q, Q), 0)
    jj = lax.broadcasted_iota(jnp.int32, (Q, Q), 1)
    lower = ii >= jj
    upper = ii <= jj
    tril = jnp.where(lower, 1.0, 0.0).astype(BF16)
    triu = jnp.where(upper, 1.0, 0.0).astype(BF16)
    a_row = -jnp.exp(alog_ref[...])

    xbc = xm_ref[0]
    dt = _softplus(dtm_ref[0] + bias_ref[...])
    a = dt * a_row
    cumf = _cumsum_mat(tril, a)
    cumb = _cumsum_mat(triu, a)
    cumf_t, cumb_t, dt_t = cumf.T, cumb.T, dt.T
    for g in range(SSM_GROUPS):
        bg = xbc[:, SSM_INNER + g * SSM_STATE:SSM_INNER + (g + 1) * SSM_STATE]
        cg = xbc[:, SSM_INNER + GS + g * SSM_STATE:SSM_INNER + GS + (g + 1) * SSM_STATE]
        bg_t = bg.T
        gmat = _bdot(cg, bg_t)
        cg16 = cg.astype(BF16)
        for hh in range(HPG):
            h = g * HPG + hh
            hb_ = SSM_HEADS + h
            xh = xbc[:, h * SSM_HEAD_DIM:(h + 1) * SSM_HEAD_DIM].astype(BF16)
            cf, rf, dtf = cumf[:, h:h + 1], cumf_t[h:h + 1, :], dt_t[h:h + 1, :]
            cb, rb, dtb = cumb[:, hb_:hb_ + 1], cumb_t[hb_:hb_ + 1, :], dt_t[hb_:hb_ + 1, :]
            lf = jnp.exp(jnp.where(lower, cf - rf, NEG_INF))
            lb = jnp.exp(jnp.where(upper, cb - rb, NEG_INF))
            wmat = gmat * (lf * dtf + lb * dtb)
            y = _bdot(wmat, xh)
            hprev = hf_ref[h]
            y = y + jnp.exp(cf) * jnp.dot(cg16, hprev.astype(BF16), preferred_element_type=F32)
            y1_ref[0, :, h * SSM_HEAD_DIM:(h + 1) * SSM_HEAD_DIM] = y
            tot = cumf_t[h:h + 1, Q - 1:Q]
            wrow = jnp.exp(tot - rf) * dtf
            hf_ref[h] = hprev * jnp.exp(tot) + _bdot(bg_t * wrow, xh)

    xbc = xb_ref[0]
    dt = _softplus(dtb_ref[0] + bias_ref[...])
    cumb = _cumsum_mat(triu, dt * a_row)
    cumb_t, dt_t = cumb.T, dt.T
    for g in range(SSM_GROUPS):
        bg_t = xbc[:, SSM_INNER + g * SSM_STATE:SSM_INNER + (g + 1) * SSM_STATE].T
        cg16 = xbc[:, SSM_INNER + GS + g * SSM_STATE:SSM_INNER + GS + (g + 1) * SSM_STATE].astype(BF16)
        for hh in range(HPG):
            h = g * HPG + hh
            hb_ = SSM_HEADS + h
            xh = xbc[:, h * SSM_HEAD_DIM:(h + 1) * SSM_HEAD_DIM].astype(BF16)
            cb, rb, dtb = cumb[:, hb_:hb_ + 1], cumb_t[hb_:hb_ + 1, :], dt_t[hb_:hb_ + 1, :]
            hprev = hb_ref[h]
            y2_ref[0, :, h * SSM_HEAD_DIM:(h + 1) * SSM_HEAD_DIM] = (
                jnp.exp(cb) * jnp.dot(cg16, hprev.astype(BF16), preferred_element_type=F32))
            tot = cumb_t[hb_:hb_ + 1, 0:1]
            wrow = jnp.exp(tot - rb) * dtb
            hb_ref[h] = hprev * jnp.exp(tot) + _bdot(bg_t * wrow, xh)

    @pl.when(i == nc - 1)
    def _():
        hf_out_ref[0] = hf_ref[...]
        hb_out_ref[0] = hb_ref[...]


def _ssd(xbc, dt_raw, dt_bias, a_log, h0f, h0b):
    b, t, _ = xbc.shape
    nc = t // Q
    st_shape = (b, SSM_HEADS, SSM_STATE, SSM_HEAD_DIM)
    st_spec = pl.BlockSpec((1, SSM_HEADS, SSM_STATE, SSM_HEAD_DIM), lambda i, j: (i, 0, 0, 0))
    fwd = lambda i, j: (i, j, 0)
    bwd = lambda i, j: (i, nc - 1 - j, 0)
    return pl.pallas_call(
        functools.partial(_ssd_kernel, nc=nc),
        out_shape=[jax.ShapeDtypeStruct((b, t, SSM_INNER), F32)] * 2
        + [jax.ShapeDtypeStruct(st_shape, F32)] * 2,
        grid=(b, nc),
        in_specs=[
            pl.BlockSpec((1, Q, XBC_COLS), fwd),
            pl.BlockSpec((1, Q, DT_PAD), fwd),
            pl.BlockSpec((1, Q, XBC_COLS), bwd),
            pl.BlockSpec((1, Q, DT_PAD), bwd),
            pl.BlockSpec((1, DT_PAD), lambda i, j: (0, 0)),
            pl.BlockSpec((1, DT_PAD), lambda i, j: (0, 0)),
            st_spec, st_spec,
        ],
        out_specs=[pl.BlockSpec((1, Q, SSM_INNER), fwd), pl.BlockSpec((1, Q, SSM_INNER), bwd),
                   st_spec, st_spec],
        scratch_shapes=[pltpu.VMEM((SSM_HEADS, SSM_STATE, SSM_HEAD_DIM), F32)] * 2,
        compiler_params=_cparams(("parallel", "arbitrary")),
        name="ssd_scan",
    )(xbc, dt_raw, xbc, dt_raw, dt_bias, a_log, h0f, h0b)


def _outproj_kernel(x_ref, mod_ref, oa_ref, ob_ref, y1_ref, y2_ref, xs_ref, z_ref, d_ref, g_ref,
                    w_ref, o_ref):
    y = y1_ref[0] + y2_ref[0] + d_ref[...] * xs_ref[0]
    y = y * _silu(z_ref[0])
    var = jnp.mean(y * y, axis=-1, keepdims=True)
    oc = y * lax.rsqrt(var + EPS) * g_ref[...]
    wa = 2 * LANES
    mix = (jnp.dot(oa_ref[0].astype(BF16), w_ref[0:wa, :], preferred_element_type=F32)
           + jnp.dot(ob_ref[0].astype(BF16), w_ref[wa:2 * wa, :], preferred_element_type=F32)
           + jnp.dot(oc.astype(BF16), w_ref[2 * wa:, :], preferred_element_type=F32))
    o_ref[0] = x_ref[0] + mod_ref[0, 2:3, :] * mix


def _outproj(x, mods, o_a, o_b, y1, y2, xbc, xz, d_exp, ssm_g, w, tm):
    b, t, _ = x.shape
    per_batch = mods.shape[0] > 1
    row = lambda n: pl.BlockSpec((1, tm, n), lambda i, j: (i, j, 0))
    return pl.pallas_call(
        _outproj_kernel,
        out_shape=jax.ShapeDtypeStruct((b, t, D_MODEL), F32),
        grid=(b, t // tm),
        in_specs=[
            row(D_MODEL),
            pl.BlockSpec((1, SUBLANES, D_MODEL), (lambda i, j: (i, 0, 0)) if per_batch else (lambda i, j: (0, 0, 0))),
            row(2 * LANES), row(2 * LANES), row(SSM_INNER), row(SSM_INNER),
            pl.BlockSpec((1, tm, SSM_INNER), lambda i, j: (i, j, 0)),
            pl.BlockSpec((1, tm, SSM_INNER), lambda i, j: (i, j, 2)),
            pl.BlockSpec((1, SSM_INNER), lambda i, j: (0, 0)),
            pl.BlockSpec((1, SSM_INNER), lambda i, j: (0, 0)),
            pl.BlockSpec((D_MODEL, D_MODEL), lambda i, j: (0, 0)),
        ],
        out_specs=row(D_MODEL),
        compiler_params=_cparams(("parallel", "parallel")),
        name="outproj",
    )(x, mods, o_a, o_b, y1, y2, xbc, xz, d_exp, ssm_g, w)


FFN_NF = 2
FFN_TF = D_FF // FFN_NF


def _ffn_kernel(x_ref, mod_ref, g_ref, wg_ref, wu_ref, wo_ref, gfin_ref, o_ref, h_ref, acc_ref, *, final):
    f = pl.program_id(2)

    @pl.when(f == 0)
    def _():
        h_ref[...] = _norm_mod(x_ref[0], g_ref[...], mod_ref[0, 3:4, :], mod_ref[0, 4:5, :]).astype(BF16)
        acc_ref[...] = jnp.zeros_like(acc_ref)

    h = h_ref[...]
    gate = jnp.dot(h, wg_ref[...], preferred_element_type=F32)
    up = jnp.dot(h, wu_ref[...], preferred_element_type=F32)
    act = (_silu(gate) * up).astype(BF16)
    acc_ref[...] += jnp.dot(act, wo_ref[...], preferred_element_type=F32)

    @pl.when(f == FFN_NF - 1)
    def _():
        out = x_ref[0] + mod_ref[0, 5:6, :] * acc_ref[...]
        if final:
            var = jnp.mean(out * out, axis=-1, keepdims=True)
            out = out * lax.rsqrt(var + EPS) * gfin_ref[...]
        o_ref[0] = out


def _ffn(x, mods, g, w_in, w_out, g_final, tm, final):
    b, t, _ = x.shape
    per_batch = mods.shape[0] > 1
    return pl.pallas_call(
        functools.partial(_ffn_kernel, final=final),
        out_shape=jax.ShapeDtypeStruct((b, t, D_MODEL), F32),
        grid=(b, t // tm, FFN_NF),
        in_specs=[
            pl.BlockSpec((1, tm, D_MODEL), lambda i, j, f: (i, j, 0)),
            pl.BlockSpec((1, SUBLANES, D_MODEL),
                         (lambda i, j, f: (i, 0, 0)) if per_batch else (lambda i, j, f: (0, 0, 0))),
            pl.BlockSpec((1, D_MODEL), lambda i, j, f: (0, 0)),
            pl.BlockSpec((D_MODEL, FFN_TF), lambda i, j, f: (0, f)),
            pl.BlockSpec((D_MODEL, FFN_TF), lambda i, j, f: (0, FFN_NF + f)),
            pl.BlockSpec((FFN_TF, D_MODEL), lambda i, j, f: (f, 0)),
            pl.BlockSpec((1, D_MODEL), lambda i, j, f: (0, 0)),
        ],
        out_specs=pl.BlockSpec((1, tm, D_MODEL), lambda i, j, f: (i, j, 0)),
        scratch_shapes=[pltpu.VMEM((tm, D_MODEL), BF16), pltpu.VMEM((tm, D_MODEL), F32)],
        compiler_params=_cparams(("parallel", "parallel", "arbitrary")),
        name="ffn_final" if final else "ffn",
    )(x, mods, g, w_in, w_in, w_out, g_final)


def _rope_tables():
    t = np.arange(SEQ)
    pos = np.stack([t // GRID_W, t % GRID_W], axis=1).astype(np.float32)
    quarter = HEAD_DIM // 4
    inv_freq = jnp.asarray(ROPE_BASE, F32) ** (-jnp.arange(quarter, dtype=F32) / quarter)
    lane = np.arange(LANES) % HEAD_DIM
    half = lane // (HEAD_DIM // 2)
    idx = lane % (HEAD_DIM // 2)
    ang = jnp.asarray(pos)[:, half] * inv_freq[idx % quarter][None, :]
    cos, sin = jnp.cos(ang), jnp.sin(ang)
    first = jnp.asarray(idx < quarter)[None, :]
    return cos, jnp.where(first, -sin, 0.0), jnp.where(first, 0.0, sin)


def _pad_lanes(v, n=LANES):
    v = v.reshape(1, -1)
    return jnp.pad(v, ((0, 0), (0, n - v.shape[1])))


def kernel(x, c, ctx, c_ctx, w_mod, b_mod, g_mix, w_in, wa_sink, na_rpb, ssm_conv_w, ssm_conv_b,
           ssm_dt_bias, ssm_a_log, ssm_d, ssm_norm_g, w_out, g_ffn, w_ffn_in, w_ffn_out, g_final):
    cin = jnp.concatenate([c, c_ctx[None, :], jnp.zeros((SUBLANES - BATCH - 1, D_MODEL), F32)], axis=0)
    mod_all = _modulation(cin, w_mod, b_mod)
    rope_tabs = _rope_tables()
    bias_tabs = _na_bias_tables(na_rpb)
    zeros_state = jnp.zeros((BATCH, SSM_HEADS, SSM_STATE, SSM_HEAD_DIM), F32)
    gfin = g_final.reshape(1, D_MODEL)

    xl, xc = x, ctx
    for l in range(DEPTH):
        last = l == DEPTH - 1
        m6 = mod_all[l].reshape(SUBLANES, 6, D_MODEL)
        mods_l = jnp.pad(m6[:BATCH], ((0, 0), (0, 2), (0, 0)))
        mods_c = jnp.pad(m6[BATCH:BATCH + 1], ((0, 0), (0, 2), (0, 0)))
        w_proj = w_in[l].astype(BF16)
        w_dt = jnp.pad(w_in[l][:, W_DT:], ((0, 0), (0, DT_PAD - DT_COLS))).astype(BF16)
        g1 = g_mix[l].reshape(1, D_MODEL)
        g2 = g_ffn[l].reshape(1, D_MODEL)
        conv_w = jnp.pad(ssm_conv_w[l], ((0, SUBLANES - SSM_CONV), (0, 0)))
        conv_b = ssm_conv_b[l].reshape(1, XBC_COLS)
        dt_bias = _pad_lanes(ssm_dt_bias[l])
        a_log = _pad_lanes(ssm_a_log[l])
        d_exp = jnp.repeat(ssm_d[l], SSM_HEAD_DIM).reshape(1, SSM_INNER)
        sg = ssm_norm_g[l].reshape(1, SSM_INNER)
        wo = w_out[l].astype(BF16)
        wfi = w_ffn_in[l].astype(BF16)
        wfo = w_ffn_out[l].astype(BF16)

        a_c, b_c, xz_c, dt_c = _inproj(xc, mods_c, g1, w_proj, w_dt, None, CTX_LEN)
        a_l, b_l, xz_l, dt_l = _inproj(xl, mods_l, g1, w_proj, w_dt, rope_tabs, 512)

        o_a = _attn_a(wa_sink[l], a_l, a_c)
        o_b = _attn_b(b_l, b_c, bias_tabs[l])

        xbc_c = _conv_silu(xz_c, conv_w, conv_b, CTX_LEN)
        xbc_l = _conv_silu(xz_l, conv_w, conv_b, 512)
        y1_c, y2_c, h_f, h_b = _ssd(xbc_c, dt_c, dt_bias, a_log, zeros_state, zeros_state)
        y1_l, y2_l, _, _ = _ssd(xbc_l, dt_l, dt_bias, a_log, h_f, h_b)

        xl = _outproj(xl, mods_l, o_a, o_b, y1_l, y2_l, xbc_l, xz_l, d_exp, sg, wo, 512)
        xl = _ffn(xl, mods_l, g2, wfi, wfo, gfin, 512, last)
        if not last:
            o_ac, o_bc = _ctx_attn(wa_sink[l], a_c, b_c)
            xc = _outproj(xc, mods_c, o_ac, o_bc, y1_c, y2_c, xbc_c, xz_c, d_exp, sg, wo, CTX_LEN)
            xc = _ffn(xc, mods_c, g2, wfi, wfo, gfin, CTX_LEN, False)
    return xl
```

```python
import functools
import math

import numpy as np
import jax
import jax.numpy as jnp
from jax import lax
from jax.experimental import pallas as pl
from jax.experimental.pallas import tpu as pltpu

F32 = jnp.float32
BF16 = jnp.bfloat16

D_MODEL = 1024
BATCH = 4
SEQ = 4096
DEPTH = 2
GRID_W = 64
GRID_ROWS = SEQ // GRID_W
CTX_LEN = 256
EPS = 1e-6
HEAD_DIM = 64
ROPE_BASE = 10000.0
WA_HEADS = 4
WA_KV_HEADS = 2
WA_WINDOW = 128
WA_BLOCK = 128
NA_HEADS = 4
NA_KH = 8
NA_KW = 16
SSM_HEADS = 8
SSM_HEAD_DIM = 64
SSM_INNER = SSM_HEADS * SSM_HEAD_DIM
SSM_GROUPS = 2
SSM_STATE = 128
SSM_CONV = 7
SSM_CHUNK = 128
D_FF = 2816
XBC_COLS = SSM_INNER + 2 * SSM_GROUPS * SSM_STATE
DT_COLS = 2 * SSM_HEADS

LANES = 128
SUBLANES = 8
VMEM_LIMIT = 56 * 1024 * 1024

A_COLS = 512
B_COLS = 768
XZ_COLS = 1536
DT_PAD = LANES
PROJ_COLS = A_COLS + B_COLS + XZ_COLS + DT_PAD

ATT_SCALE = HEAD_DIM ** -0.5
NEG_INF = float("-inf")
NT_DIMS = (((1,), (1,)), ((), ()))


def _silu(v):
    return v / (1.0 + jnp.exp(-v))


def _bdot(a, b):
    return jnp.dot(a.astype(BF16), b.astype(BF16), preferred_element_type=F32)


def _bdot_nt(a, b):
    return lax.dot_general(a.astype(BF16), b.astype(BF16), NT_DIMS, preferred_element_type=F32)


def _cparams(sem):
    return pltpu.CompilerParams(dimension_semantics=sem, vmem_limit_bytes=VMEM_LIMIT)


MOD_TN = 1536


def _mod_kernel(c_ref, w_ref, b_ref, o_ref):
    s = _silu(c_ref[...])
    o_ref[0] = _bdot(s, w_ref[0]) + b_ref[0]


def _modulation(cin, w_mod, b_mod):
    n = 6 * D_MODEL
    return pl.pallas_call(
        _mod_kernel,
        out_shape=jax.ShapeDtypeStruct((DEPTH, SUBLANES, n), F32),
        grid=(DEPTH, n // MOD_TN),
        in_specs=[
            pl.BlockSpec((SUBLANES, D_MODEL), lambda l, j: (0, 0)),
            pl.BlockSpec((1, D_MODEL, MOD_TN), lambda l, j: (l, 0, j)),
            pl.BlockSpec((1, 1, MOD_TN), lambda l, j: (l, 0, j)),
        ],
        out_specs=pl.BlockSpec((1, SUBLANES, MOD_TN), lambda l, j: (l, 0, j)),
        compiler_params=_cparams(("parallel", "parallel")),
        name="modulation",
    )(cin, w_mod, b_mod.reshape(DEPTH, 1, n))


def _norm_mod(x, g, shift, scale):
    var = jnp.mean(x * x, axis=-1, keepdims=True)
    h = x * lax.rsqrt(var + EPS) * g
    return h * (1.0 + scale) + shift


W_QA, W_QB, W_Z, W_KVA, W_KVB, W_XBC, W_DT = 0, 256, 512, 1024, 1280, 1792, 2816
IN_COLS = W_DT + DT_COLS


def _inproj_kernel(x_ref, mod_ref, g_ref, w_ref, wdt_ref, *rest, rope):
    if rope:
        cos_ref, sa_ref, sb_ref, oa_ref, ob_ref, oxz_ref, odt_ref = rest
    else:
        oa_ref, ob_ref, oxz_ref, odt_ref = rest
    h = _norm_mod(x_ref[0], g_ref[...], mod_ref[0, 0:1, :], mod_ref[0, 1:2, :]).astype(BF16)

    def proj(lo, hi):
        return jnp.dot(h, w_ref[:, lo:hi], preferred_element_type=F32)

    qa = proj(W_QA, W_QB)
    kva = proj(W_KVA, W_KVB)
    if rope:
        cos, sa, sb = cos_ref[...], sa_ref[...], sb_ref[...]

        def rot(v):
            up = pltpu.roll(v, LANES - 16, axis=1)
            dn = pltpu.roll(v, 16, axis=1)
            return v * cos + up * sa + dn * sb

        oa_ref[0, :, 0:LANES] = rot(qa[:, 0:LANES]).astype(BF16)
        oa_ref[0, :, LANES:2 * LANES] = rot(qa[:, LANES:]).astype(BF16)
        oa_ref[0, :, 2 * LANES:3 * LANES] = rot(kva[:, 0:LANES]).astype(BF16)
        oa_ref[0, :, 3 * LANES:] = kva[:, LANES:].astype(BF16)
    else:
        oa_ref[0, :, 0:2 * LANES] = qa.astype(BF16)
        oa_ref[0, :, 2 * LANES:] = kva.astype(BF16)
    ob_ref[0, :, 0:2 * LANES] = proj(W_QB, W_Z).astype(BF16)
    ob_ref[0, :, 2 * LANES:] = proj(W_KVB, W_XBC).astype(BF16)
    oxz_ref[0, :, 0:XBC_COLS] = proj(W_XBC, W_DT)
    oxz_ref[0, :, XBC_COLS:] = proj(W_Z, W_KVA)
    odt_ref[0] = jnp.dot(h, wdt_ref[...], preferred_element_type=F32)


def _inproj(x, mods, g, w, wdt, rope_tabs, tm):
    b, t, _ = x.shape
    rope = rope_tabs is not None
    per_batch = mods.shape[0] > 1
    in_specs = [
        pl.BlockSpec((1, tm, D_MODEL), lambda i, j: (i, j, 0)),
        pl.BlockSpec((1, SUBLANES, D_MODEL), (lambda i, j: (i, 0, 0)) if per_batch else (lambda i, j: (0, 0, 0))),
        pl.BlockSpec((1, D_MODEL), lambda i, j: (0, 0)),
        pl.BlockSpec((D_MODEL, IN_COLS), lambda i, j: (0, 0)),
        pl.BlockSpec((D_MODEL, DT_PAD), lambda i, j: (0, 0)),
    ]
    args = [x, mods, g, w, wdt]
    if rope:
        in_specs += [pl.BlockSpec((tm, LANES), lambda i, j: (j, 0))] * 3
        args += list(rope_tabs)
    widths = (A_COLS, B_COLS, XZ_COLS, DT_PAD)
    return pl.pallas_call(
        functools.partial(_inproj_kernel, rope=rope),
        out_shape=[jax.ShapeDtypeStruct((b, t, n), dt) for n, dt in zip(widths, (BF16, BF16, F32, F32))],
        grid=(b, t // tm),
        in_specs=in_specs,
        out_specs=[pl.BlockSpec((1, tm, n), lambda i, j: (i, j, 0)) for n in widths],
        compiler_params=_cparams(("parallel", "parallel")),
        name="inproj_rope" if rope else "inproj",
    )(*args)


WA_KEYS = 3 * WA_BLOCK


def _attn_a_kernel(sink_ref, q_ref, k_ref, v_ref, kc_ref, vc_ref, o_ref, s_ref, p_ref):
    n = pl.program_id(1)
    ks = pl.multiple_of(jnp.clip((n - 1) * WA_BLOCK, 0, SEQ - WA_KEYS), WA_BLOCK)
    q = q_ref[0] * ATT_SCALE
    kall = jnp.concatenate([k_ref[0, pl.ds(ks, WA_KEYS), :], kc_ref[0]], axis=0)
    vall = jnp.concatenate([v_ref[0, pl.ds(ks, WA_KEYS), :], vc_ref[0]], axis=0)
    nk = WA_KEYS + CTX_LEN
    nq = WA_HEADS * WA_BLOCK
    for h in range(WA_HEADS):
        g = h // (WA_HEADS // WA_KV_HEADS)
        s_ref[h * WA_BLOCK:(h + 1) * WA_BLOCK, :] = _bdot_nt(
            q[:, h * HEAD_DIM:(h + 1) * HEAD_DIM], kall[:, g * HEAD_DIM:(g + 1) * HEAD_DIM])
    rows = lax.broadcasted_iota(jnp.int32, (nq, nk), 0)
    cols = lax.broadcasted_iota(jnp.int32, (nq, nk), 1)
    qpos = n * WA_BLOCK + (rows & (WA_BLOCK - 1))
    valid = (cols >= WA_KEYS) | (jnp.abs(qpos - (ks + cols)) <= WA_WINDOW)
    s = jnp.where(valid, s_ref[...], NEG_INF)
    row1 = lax.broadcasted_iota(jnp.int32, (nq, 1), 0)
    sink = jnp.where(row1 < WA_BLOCK, sink_ref[0],
                     jnp.where(row1 < 2 * WA_BLOCK, sink_ref[1],
                               jnp.where(row1 < 3 * WA_BLOCK, sink_ref[2], sink_ref[3])))
    m = jnp.maximum(jnp.max(s, axis=1, keepdims=True), sink)
    p = jnp.exp(s - m)
    inv = 1.0 / (jnp.sum(p, axis=1, keepdims=True) + jnp.exp(sink - m))
    p_ref[...] = p.astype(BF16)
    for h in range(WA_HEADS):
        g = h // (WA_HEADS // WA_KV_HEADS)
        rs = slice(h * WA_BLOCK, (h + 1) * WA_BLOCK)
        o = jnp.dot(p_ref[rs, :], vall[:, g * HEAD_DIM:(g + 1) * HEAD_DIM], preferred_element_type=F32)
        o_ref[0, :, h * HEAD_DIM:(h + 1) * HEAD_DIM] = (o * inv[rs]).astype(o_ref.dtype)


def _attn_a(sink, qkv, qkv_c):
    b = qkv.shape[0]
    nk = WA_KEYS + CTX_LEN
    return pl.pallas_call(
        _attn_a_kernel,
        out_shape=jax.ShapeDtypeStruct((b, SEQ, WA_HEADS * HEAD_DIM), BF16),
        scratch_shapes=[pltpu.VMEM((WA_HEADS * WA_BLOCK, nk), F32), pltpu.VMEM((WA_HEADS * WA_BLOCK, nk), BF16)],
        grid=(b, SEQ // WA_BLOCK),
        in_specs=[
            pl.BlockSpec(memory_space=pltpu.SMEM),
            pl.BlockSpec((1, WA_BLOCK, 2 * LANES), lambda i, j: (i, j, 0)),
            pl.BlockSpec((1, SEQ, LANES), lambda i, j: (i, 0, 2)),
            pl.BlockSpec((1, SEQ, LANES), lambda i, j: (i, 0, 3)),
            pl.BlockSpec((1, CTX_LEN, LANES), lambda i, j: (i, 0, 2)),
            pl.BlockSpec((1, CTX_LEN, LANES), lambda i, j: (i, 0, 3)),
        ],
        out_specs=pl.BlockSpec((1, WA_BLOCK, 2 * LANES), lambda i, j: (i, j, 0)),
        compiler_params=_cparams(("parallel", "arbitrary")),
        name="attn_window",
    )(sink, qkv, qkv, qkv, qkv_c, qkv_c)


NB_ROWS_PER_STEP = 4
NB_LOC = NA_KH * GRID_W
NB_DY_PAIRS = 2 * NA_KH - 2


def _attn_b_kernel(q_ref, k_ref, v_ref, kc_ref, vc_ref, t_ref, o_ref, s_ref, p_ref):
    i = pl.program_id(1)
    kc = kc_ref[0]
    vc = vc_ref[0]
    units = [(rr, h) for rr in range(NB_ROWS_PER_STEP) for h in range(NA_HEADS)]
    starts, shifts = [], []
    for rr in range(NB_ROWS_PER_STEP):
        r = i * NB_ROWS_PER_STEP + rr
        rs = jnp.clip(r - NA_KH // 2, 0, GRID_ROWS - NA_KH)
        shifts.append(r - rs)
        starts.append(pl.multiple_of(rs * GRID_W, GRID_W))

    for u, (rr, h) in enumerate(units):
        if h == 0:
            kw = k_ref[0, pl.ds(starts[rr], NB_LOC), :]
            q = q_ref[0, rr * GRID_W:(rr + 1) * GRID_W, :] * ATT_SCALE
        hs = slice(h * HEAD_DIM, (h + 1) * HEAD_DIM)
        rows = slice(u * GRID_W, (u + 1) * GRID_W)
        bias = jnp.concatenate(
            [t_ref[h, 2 * k - shifts[rr] + NA_KH - 1] for k in range(NA_KH // 2)], axis=1)
        s_ref[rows, 0:NB_LOC] = _bdot_nt(q[:, hs], kw[:, hs]) + bias
        s_ref[rows, NB_LOC:] = _bdot_nt(q[:, hs], kc[:, hs])

    s = s_ref[...]
    p = jnp.exp(s - jnp.max(s, axis=1, keepdims=True))
    inv = 1.0 / jnp.sum(p, axis=1, keepdims=True)
    p_ref[...] = p.astype(BF16)

    for u, (rr, h) in enumerate(units):
        if h == 0:
            vw = v_ref[0, pl.ds(starts[rr], NB_LOC), :]
        hs = slice(h * HEAD_DIM, (h + 1) * HEAD_DIM)
        rows = slice(u * GRID_W, (u + 1) * GRID_W)
        o = (jnp.dot(p_ref[rows, 0:NB_LOC], vw[:, hs], preferred_element_type=F32)
             + jnp.dot(p_ref[rows, NB_LOC:], vc[:, hs], preferred_element_type=F32))
        o_ref[0, rr * GRID_W:(rr + 1) * GRID_W, hs] = (o * inv[rows]).astype(o_ref.dtype)


def _attn_b(qkv, qkv_c, table):
    b = qkv.shape[0]
    tq = NB_ROWS_PER_STEP * GRID_W
    w = NA_HEADS * HEAD_DIM
    nu = NB_ROWS_PER_STEP * NA_HEADS * GRID_W
    return pl.pallas_call(
        _attn_b_kernel,
        out_shape=jax.ShapeDtypeStruct((b, SEQ, w), BF16),
        scratch_shapes=[pltpu.VMEM((nu, NB_LOC + CTX_LEN), F32), pltpu.VMEM((nu, NB_LOC + CTX_LEN), BF16)],
        grid=(b, SEQ // tq),
        in_specs=[
            pl.BlockSpec((1, tq, w), lambda i, j: (i, j, 0)),
            pl.BlockSpec((1, SEQ, w), lambda i, j: (i, 0, 1)),
            pl.BlockSpec((1, SEQ, w), lambda i, j: (i, 0, 2)),
            pl.BlockSpec((1, CTX_LEN, w), lambda i, j: (i, 0, 1)),
            pl.BlockSpec((1, CTX_LEN, w), lambda i, j: (i, 0, 2)),
            pl.BlockSpec((NA_HEADS, NB_DY_PAIRS, GRID_W, LANES), lambda i, j: (0, 0, 0, 0)),
        ],
        out_specs=pl.BlockSpec((1, tq, w), lambda i, j: (i, j, 0)),
        compiler_params=_cparams(("parallel", "arbitrary")),
        name="attn_neighbourhood",
    )(qkv, qkv, qkv, qkv_c, qkv_c, table)


def _split3(a):
    a1 = a.astype(BF16)
    r1 = a - a1.astype(F32)
    a2 = r1.astype(BF16)
    a3 = (r1 - a2.astype(F32)).astype(BF16)
    return a1, a2, a3


def _bias_kernel(r_ref, oh_ref, o_ref):
    oh = oh_ref[...]
    o_ref[...] = sum(jnp.dot(t, oh, preferred_element_type=F32) for t in _split3(r_ref[...]))


def _na_bias_tables(rpb):
    ndy, ndx = 2 * NA_KH - 1, 2 * NA_KW - 1
    qc = np.arange(GRID_W)[:, None]
    x = np.arange(GRID_W)[None, :]
    dx = np.clip(x - qc, -(NA_KW - 1), NA_KW - 1) + NA_KW - 1
    onehot = (np.arange(LANES)[:, None, None] == dx[None]).reshape(LANES, GRID_W * GRID_W)
    cstart = np.clip(qc - NA_KW // 2, 0, GRID_W - NA_KW)
    inside = (x >= cstart) & (x < cstart + NA_KW)
    rows = DEPTH * NA_HEADS * ndy
    r = jnp.pad(rpb.astype(F32).reshape(rows, ndx), ((0, LANES - rows), (0, LANES - ndx)))
    m = pl.pallas_call(
        _bias_kernel,
        out_shape=jax.ShapeDtypeStruct((LANES, GRID_W * GRID_W), F32),
        name="na_bias_expand",
    )(r, jnp.asarray(onehot, BF16))
    m = m[:rows].reshape(DEPTH, NA_HEADS, ndy, GRID_W, GRID_W)
    m = jnp.where(jnp.asarray(inside), m, NEG_INF)
    return jnp.concatenate([m[:, :, :ndy - 1], m[:, :, 1:]], axis=-1)


def _ctx_attn_kernel(sink_ref, a_ref, b_ref, oa_ref, ob_ref):
    a = a_ref[0]
    row1 = lax.broadcasted_iota(jnp.int32, (2 * CTX_LEN, 1), 0)
    for g in range(WA_KV_HEADS):
        h0, h1 = 2 * g, 2 * g + 1
        q2 = jnp.concatenate([a[:, h0 * HEAD_DIM:(h0 + 1) * HEAD_DIM],
                              a[:, h1 * HEAD_DIM:(h1 + 1) * HEAD_DIM]], axis=0)
        k = a[:, 2 * LANES + g * HEAD_DIM:2 * LANES + (g + 1) * HEAD_DIM]
        v = a[:, 3 * LANES + g * HEAD_DIM:3 * LANES + (g + 1) * HEAD_DIM]
        s = _bdot_nt(q2, k) * ATT_SCALE
        sink = jnp.where(row1 < CTX_LEN, sink_ref[h0], sink_ref[h1])
        m = jnp.maximum(jnp.max(s, axis=1, keepdims=True), sink)
        p = jnp.exp(s - m)
        den = jnp.sum(p, axis=1, keepdims=True) + jnp.exp(sink - m)
        o = _bdot(p, v) / den
        oa_ref[0, :, h0 * HEAD_DIM:(h0 + 1) * HEAD_DIM] = o[:CTX_LEN]
        oa_ref[0, :, h1 * HEAD_DIM:(h1 + 1) * HEAD_DIM] = o[CTX_LEN:]
    bq = b_ref[0]
    w = NA_HEADS * HEAD_DIM
    for h in range(NA_HEADS):
        hs = slice(h * HEAD_DIM, (h + 1) * HEAD_DIM)
        s = _bdot_nt(bq[:, hs], bq[:, w + h * HEAD_DIM:w + (h + 1) * HEAD_DIM]) * ATT_SCALE
        m = jnp.max(s, axis=1, keepdims=True)
        p = jnp.exp(s - m)
        den = jnp.sum(p, axis=1, keepdims=True)
        ob_ref[0, :, hs] = _bdot(p, bq[:, 2 * w + h * HEAD_DIM:2 * w + (h + 1) * HEAD_DIM]) / den


def _ctx_attn(sink, qkv_a_c, qkv_b_c):
    b = qkv_a_c.shape[0]
    w = 2 * LANES
    return pl.pallas_call(
        _ctx_attn_kernel,
        out_shape=[jax.ShapeDtypeStruct((b, CTX_LEN, w), F32)] * 2,
        grid=(b,),
        in_specs=[
            pl.BlockSpec(memory_space=pltpu.SMEM),
            pl.BlockSpec((1, CTX_LEN, A_COLS), lambda i: (i, 0, 0)),
            pl.BlockSpec((1, CTX_LEN, B_COLS), lambda i: (i, 0, 0)),
        ],
        out_specs=[pl.BlockSpec((1, CTX_LEN, w), lambda i: (i, 0, 0))] * 2,
        compiler_params=_cparams(("parallel",)),
        name="attn_context",
    )(sink, qkv_a_c, qkv_b_c)


CONV_HALO = SUBLANES


def _conv_kernel(prev_ref, cur_ref, next_ref, w_ref, b_ref, o_ref, ext_ref, *, tl, nt):
    j = pl.program_id(1)
    ext_ref[0:CONV_HALO, :] = jnp.where(j > 0, prev_ref[0], 0.0)
    ext_ref[CONV_HALO:CONV_HALO + tl, :] = cur_ref[0]
    ext_ref[CONV_HALO + tl:, :] = jnp.where(j < nt - 1, next_ref[0], 0.0)
    acc = jnp.zeros((tl, XBC_COLS), F32) + b_ref[...]
    base = CONV_HALO - SSM_CONV // 2
    for k in range(SSM_CONV):
        acc = acc + w_ref[k:k + 1, :] * ext_ref[base + k:base + k + tl, :]
    o_ref[0] = _silu(acc)


def _conv_silu(xz, conv_w, conv_b, tl):
    b, t, _ = xz.shape
    nt = t // tl
    hb = tl // CONV_HALO
    last = t // CONV_HALO - 1
    return pl.pallas_call(
        functools.partial(_conv_kernel, tl=tl, nt=nt),
        out_shape=jax.ShapeDtypeStruct((b, t, XBC_COLS), F32),
        grid=(b, nt),
        in_specs=[
            pl.BlockSpec((1, CONV_HALO, XBC_COLS), lambda i, j: (i, jnp.maximum(j * hb - 1, 0), 0)),
            pl.BlockSpec((1, tl, XBC_COLS), lambda i, j: (i, j, 0)),
            pl.BlockSpec((1, CONV_HALO, XBC_COLS), lambda i, j: (i, jnp.minimum((j + 1) * hb, last), 0)),
            pl.BlockSpec((SUBLANES, XBC_COLS), lambda i, j: (0, 0)),
            pl.BlockSpec((1, XBC_COLS), lambda i, j: (0, 0)),
        ],
        out_specs=pl.BlockSpec((1, tl, XBC_COLS), lambda i, j: (i, j, 0)),
        scratch_shapes=[pltpu.VMEM((tl + 2 * CONV_HALO, XBC_COLS), F32)],
        compiler_params=_cparams(("parallel", "parallel")),
        name="ssm_conv",
    )(xz, xz, xz, conv_w, conv_b)


Q = SSM_CHUNK
GS = SSM_GROUPS * SSM_STATE
HPG = SSM_HEADS // SSM_GROUPS


def _softplus(v):
    return jnp.maximum(v, 0.0) + jnp.log1p(jnp.exp(-jnp.abs(v)))


def _cumsum_mat(tri, a):
    return sum(jnp.dot(tri, t, preferred_element_type=F32) for t in _split3(a))


def _ssd_kernel(xm_ref, dtm_ref, xb_ref, dtb_ref, bias_ref, alog_ref, h0f_ref, h0b_ref,
                y1_ref, y2_ref, hf_out_ref, hb_out_ref, hf_ref, hb_ref, *, nc):
    i = pl.program_id(1)

    @pl.when(i == 0)
    def _():
        hf_ref[...] = h0f_ref[0]
        hb_ref[...] = h0b_ref[0]

    ii = lax.broadcasted_iota(jnp.int32, (Q, Q), 0)
    jj = lax.broadcasted_iota(jnp.int32, (Q, Q), 1)
    lower = ii >= jj
    upper = ii <= jj
    tril = jnp.where(lower, 1.0, 0.0).astype(BF16)
    triu = jnp.where(upper, 1.0, 0.0).astype(BF16)
    a_row = -jnp.exp(alog_ref[...])

    xbc = xm_ref[0]
    dt = _softplus(dtm_ref[0] + bias_ref[...])
    a = dt * a_row
    cumf = _cumsum_mat(tril, a)
    cumb = _cumsum_mat(triu, a)
    cumf_t, cumb_t, dt_t = cumf.T, cumb.T, dt.T
    for g in range(SSM_GROUPS):
        bg = xbc[:, SSM_INNER + g * SSM_STATE:SSM_INNER + (g + 1) * SSM_STATE]
        cg = xbc[:, SSM_INNER + GS + g * SSM_STATE:SSM_INNER + GS + (g + 1) * SSM_STATE]
        bg_t = bg.T
        gmat = _bdot(cg, bg_t)
        cg16 = cg.astype(BF16)
        for hh in range(HPG):
            h = g * HPG + hh
            hb_ = SSM_HEADS + h
            xh = xbc[:, h * SSM_HEAD_DIM:(h + 1) * SSM_HEAD_DIM].astype(BF16)
            cf, rf, dtf = cumf[:, h:h + 1], cumf_t[h:h + 1, :], dt_t[h:h + 1, :]
            cb, rb, dtb = cumb[:, hb_:hb_ + 1], cumb_t[hb_:hb_ + 1, :], dt_t[hb_:hb_ + 1, :]
            lf = jnp.exp(jnp.where(lower, cf - rf, NEG_INF))
            lb = jnp.exp(jnp.where(upper, cb - rb, NEG_INF))
            wmat = gmat * (lf * dtf + lb * dtb)
            y = _bdot(wmat, xh)
            hprev = hf_ref[h]
            y = y + jnp.exp(cf) * jnp.dot(cg16, hprev.astype(BF16), preferred_element_type=F32)
            y1_ref[0, :, h * SSM_HEAD_DIM:(h + 1) * SSM_HEAD_DIM] = y
            tot = cumf_t[h:h + 1, Q - 1:Q]
            wrow = jnp.exp(tot - rf) * dtf
            hf_ref[h] = hprev * jnp.exp(tot) + _bdot(bg_t * wrow, xh)

    xbc = xb_ref[0]
    dt = _softplus(dtb_ref[0] + bias_ref[...])
    cumb = _cumsum_mat(triu, dt * a_row)
    cumb_t, dt_t = cumb.T, dt.T
    for g in range(SSM_GROUPS):
        bg_t = xbc[:, SSM_INNER + g * SSM_STATE:SSM_INNER + (g + 1) * SSM_STATE].T
        cg16 = xbc[:, SSM_INNER + GS + g * SSM_STATE:SSM_INNER + GS + (g + 1) * SSM_STATE].astype(BF16)
        for hh in range(HPG):
            h = g * HPG + hh
            hb_ = SSM_HEADS + h
            xh = xbc[:, h * SSM_HEAD_DIM:(h + 1) * SSM_HEAD_DIM].astype(BF16)
            cb, rb, dtb = cumb[:, hb_:hb_ + 1], cumb_t[hb_:hb_ + 1, :], dt_t[hb_:hb_ + 1, :]
            hprev = hb_ref[h]
            y2_ref[0, :, h * SSM_HEAD_DIM:(h + 1) * SSM_HEAD_DIM] = (
                jnp.exp(cb) * jnp.dot(cg16, hprev.astype(BF16), preferred_element_type=F32))
            tot = cumb_t[hb_:hb_ + 1, 0:1]
            wrow = jnp.exp(tot - rb) * dtb
            hb_ref[h] = hprev * jnp.exp(tot) + _bdot(bg_t * wrow, xh)

    @pl.when(i == nc - 1)
    def _():
        hf_out_ref[0] = hf_ref[...]
        hb_out_ref[0] = hb_ref[...]


def _ssd(xbc, dt_raw, dt_bias, a_log, h0f, h0b):
    b, t, _ = xbc.shape
    nc = t // Q
    st_shape = (b, SSM_HEADS, SSM_STATE, SSM_HEAD_DIM)
    st_spec = pl.BlockSpec((1, SSM_HEADS, SSM_STATE, SSM_HEAD_DIM), lambda i, j: (i, 0, 0, 0))
    fwd = lambda i, j: (i, j, 0)
    bwd = lambda i, j: (i, nc - 1 - j, 0)
    return pl.pallas_call(
        functools.partial(_ssd_kernel, nc=nc),
        out_shape=[jax.ShapeDtypeStruct((b, t, SSM_INNER), F32)] * 2
        + [jax.ShapeDtypeStruct(st_shape, F32)] * 2,
        grid=(b, nc),
        in_specs=[
            pl.BlockSpec((1, Q, XBC_COLS), fwd),
            pl.BlockSpec((1, Q, DT_PAD), fwd),
            pl.BlockSpec((1, Q, XBC_COLS), bwd),
            pl.BlockSpec((1, Q, DT_PAD), bwd),
            pl.BlockSpec((1, DT_PAD), lambda i, j: (0, 0)),
            pl.BlockSpec((1, DT_PAD), lambda i, j: (0, 0)),
            st_spec, st_spec,
        ],
        out_specs=[pl.BlockSpec((1, Q, SSM_INNER), fwd), pl.BlockSpec((1, Q, SSM_INNER), bwd),
                   st_spec, st_spec],
        scratch_shapes=[pltpu.VMEM((SSM_HEADS, SSM_STATE, SSM_HEAD_DIM), F32)] * 2,
        compiler_params=_cparams(("parallel", "arbitrary")),
        name="ssd_scan",
    )(xbc, dt_raw, xbc, dt_raw, dt_bias, a_log, h0f, h0b)


def _outproj_kernel(x_ref, mod_ref, oa_ref, ob_ref, y1_ref, y2_ref, xs_ref, z_ref, d_ref, g_ref,
                    w_ref, o_ref):
    y = y1_ref[0] + y2_ref[0] + d_ref[...] * xs_ref[0]
    y = y * _silu(z_ref[0])
    var = jnp.mean(y * y, axis=-1, keepdims=True)
    oc = y * lax.rsqrt(var + EPS) * g_ref[...]
    wa = 2 * LANES
    mix = (jnp.dot(oa_ref[0].astype(BF16), w_ref[0:wa, :], preferred_element_type=F32)
           + jnp.dot(ob_ref[0].astype(BF16), w_ref[wa:2 * wa, :], preferred_element_type=F32)
           + jnp.dot(oc.astype(BF16), w_ref[2 * wa:, :], preferred_element_type=F32))
    o_ref[0] = x_ref[0] + mod_ref[0, 2:3, :] * mix


def _outproj(x, mods, o_a, o_b, y1, y2, xbc, xz, d_exp, ssm_g, w, tm):
    b, t, _ = x.shape
    per_batch = mods.shape[0] > 1
    row = lambda n: pl.BlockSpec((1, tm, n), lambda i, j: (i, j, 0))
    return pl.pallas_call(
        _outproj_kernel,
        out_shape=jax.ShapeDtypeStruct((b, t, D_MODEL), F32),
        grid=(b, t // tm),
        in_specs=[
            row(D_MODEL),
            pl.BlockSpec((1, SUBLANES, D_MODEL), (lambda i, j: (i, 0, 0)) if per_batch else (lambda i, j: (0, 0, 0))),
            row(2 * LANES), row(2 * LANES), row(SSM_INNER), row(SSM_INNER),
            pl.BlockSpec((1, tm, SSM_INNER), lambda i, j: (i, j, 0)),
            pl.BlockSpec((1, tm, SSM_INNER), lambda i, j: (i, j, 2)),
            pl.BlockSpec((1, SSM_INNER), lambda i, j: (0, 0)),
            pl.BlockSpec((1, SSM_INNER), lambda i, j: (0, 0)),
            pl.BlockSpec((D_MODEL, D_MODEL), lambda i, j: (0, 0)),
        ],
        out_specs=row(D_MODEL),
        compiler_params=_cparams(("parallel", "parallel")),
        name="outproj",
    )(x, mods, o_a, o_b, y1, y2, xbc, xz, d_exp, ssm_g, w)


FFN_NF = 2
FFN_TF = D_FF // FFN_NF


def _ffn_kernel(x_ref, mod_ref, g_ref, wg_ref, wu_ref, wo_ref, gfin_ref, o_ref, h_ref, acc_ref, *, final):
    f = pl.program_id(2)

    @pl.when(f == 0)
    def _():
        h_ref[...] = _norm_mod(x_ref[0], g_ref[...], mod_ref[0, 3:4, :], mod_ref[0, 4:5, :]).astype(BF16)
        acc_ref[...] = jnp.zeros_like(acc_ref)

    h = h_ref[...]
    gate = jnp.dot(h, wg_ref[...], preferred_element_type=F32)
    up = jnp.dot(h, wu_ref[...], preferred_element_type=F32)
    act = (_silu(gate) * up).astype(BF16)
    acc_ref[...] += jnp.dot(act, wo_ref[...], preferred_element_type=F32)

    @pl.when(f == FFN_NF - 1)
    def _():
        out = x_ref[0] + mod_ref[0, 5:6, :] * acc_ref[...]
        if final:
            var = jnp.mean(out * out, axis=-1, keepdims=True)
            out = out * lax.rsqrt(var + EPS) * gfin_ref[...]
        o_ref[0] = out


def _ffn(x, mods, g, w_in, w_out, g_final, tm, final):
    b, t, _ = x.shape
    per_batch = mods.shape[0] > 1
    return pl.pallas_call(
        functools.partial(_ffn_kernel, final=final),
        out_shape=jax.ShapeDtypeStruct((b, t, D_MODEL), F32),
        grid=(b, t // tm, FFN_NF),
        in_specs=[
            pl.BlockSpec((1, tm, D_MODEL), lambda i, j, f: (i, j, 0)),
            pl.BlockSpec((1, SUBLANES, D_MODEL),
                         (lambda i, j, f: (i, 0, 0)) if per_batch else (lambda i, j, f: (0, 0, 0))),
            pl.BlockSpec((1, D_MODEL), lambda i, j, f: (0, 0)),
            pl.BlockSpec((D_MODEL, FFN_TF), lambda i, j, f: (0, f)),
            pl.BlockSpec((D_MODEL, FFN_TF), lambda i, j, f: (0, FFN_NF + f)),
            pl.BlockSpec((FFN_TF, D_MODEL), lambda i, j, f: (f, 0)),
            pl.BlockSpec((1, D_MODEL), lambda i, j, f: (0, 0)),
        ],
        out_specs=pl.BlockSpec((1, tm, D_MODEL), lambda i, j, f: (i, j, 0)),
        scratch_shapes=[pltpu.VMEM((tm, D_MODEL), BF16), pltpu.VMEM((tm, D_MODEL), F32)],
        compiler_params=_cparams(("parallel", "parallel", "arbitrary")),
        name="ffn_final" if final else "ffn",
    )(x, mods, g, w_in, w_in, w_out, g_final)


def _rope_tables():
    t = np.arange(SEQ)
    pos = np.stack([t // GRID_W, t % GRID_W], axis=1).astype(np.float32)
    quarter = HEAD_DIM // 4
    inv_freq = jnp.asarray(ROPE_BASE, F32) ** (-jnp.arange(quarter, dtype=F32) / quarter)
    lane = np.arange(LANES) % HEAD_DIM
    half = lane // (HEAD_DIM // 2)
    idx = lane % (HEAD_DIM // 2)
    ang = jnp.asarray(pos)[:, half] * inv_freq[idx % quarter][None, :]
    cos, sin = jnp.cos(ang), jnp.sin(ang)
    first = jnp.asarray(idx < quarter)[None, :]
    return cos, jnp.where(first, -sin, 0.0), jnp.where(first, 0.0, sin)


def _pad_lanes(v, n=LANES):
    v = v.reshape(1, -1)
    return jnp.pad(v, ((0, 0), (0, n - v.shape[1])))


def kernel(x, c, ctx, c_ctx, w_mod, b_mod, g_mix, w_in, wa_sink, na_rpb, ssm_conv_w, ssm_conv_b,
           ssm_dt_bias, ssm_a_log, ssm_d, ssm_norm_g, w_out, g_ffn, w_ffn_in, w_ffn_out, g_final):
    cin = jnp.concatenate([c, c_ctx[None, :], jnp.zeros((SUBLANES - BATCH - 1, D_MODEL), F32)], axis=0)
    mod_all = _modulation(cin, w_mod, b_mod)
    rope_tabs = _rope_tables()
    bias_tabs = _na_bias_tables(na_rpb)
    zeros_state = jnp.zeros((BATCH, SSM_HEADS, SSM_STATE, SSM_HEAD_DIM), F32)
    gfin = g_final.reshape(1, D_MODEL)

    xl, xc = x, ctx
    for l in range(DEPTH):
        last = l == DEPTH - 1
        m6 = mod_all[l].reshape(SUBLANES, 6, D_MODEL)
        mods_l = jnp.pad(m6[:BATCH], ((0, 0), (0, 2), (0, 0)))
        mods_c = jnp.pad(m6[BATCH:BATCH + 1], ((0, 0), (0, 2), (0, 0)))
        w_proj = w_in[l].astype(BF16)
        w_dt = jnp.pad(w_in[l][:, W_DT:], ((0, 0), (0, DT_PAD - DT_COLS))).astype(BF16)
        g1 = g_mix[l].reshape(1, D_MODEL)
        g2 = g_ffn[l].reshape(1, D_MODEL)
        conv_w = jnp.pad(ssm_conv_w[l], ((0, SUBLANES - SSM_CONV), (0, 0)))
        conv_b = ssm_conv_b[l].reshape(1, XBC_COLS)
        dt_bias = _pad_lanes(ssm_dt_bias[l])
        a_log = _pad_lanes(ssm_a_log[l])
        d_exp = jnp.repeat(ssm_d[l], SSM_HEAD_DIM).reshape(1, SSM_INNER)
        sg = ssm_norm_g[l].reshape(1, SSM_INNER)
        wo = w_out[l].astype(BF16)
        wfi = w_ffn_in[l].astype(BF16)
        wfo = w_ffn_out[l].astype(BF16)

        a_c, b_c, xz_c, dt_c = _inproj(xc, mods_c, g1, w_proj, w_dt, None, CTX_LEN)
        a_l, b_l, xz_l, dt_l = _inproj(xl, mods_l, g1, w_proj, w_dt, rope_tabs, 512)

        o_a = _attn_a(wa_sink[l], a_l, a_c)
        o_b = _attn_b(b_l, b_c, bias_tabs[l])

        xbc_c = _conv_silu(xz_c, conv_w, conv_b, CTX_LEN)
        xbc_l = _conv_silu(xz_l, conv_w, conv_b, 512)
        y1_c, y2_c, h_f, h_b = _ssd(xbc_c, dt_c, dt_bias, a_log, zeros_state, zeros_state)
        y1_l, y2_l, _, _ = _ssd(xbc_l, dt_l, dt_bias, a_log, h_f, h_b)

        xl = _outproj(xl, mods_l, o_a, o_b, y1_l, y2_l, xbc_l, xz_l, d_exp, sg, wo, 512)
        xl = _ffn(xl, mods_l, g2, wfi, wfo, gfin, 512, last)
        if not last:
            o_ac, o_bc = _ctx_attn(wa_sink[l], a_c, b_c)
            xc = _outproj(xc, mods_c, o_ac, o_bc, y1_c, y2_c, xbc_c, xz_c, d_exp, sg, wo, CTX_LEN)
            xc = _ffn(xc, mods_c, g2, wfi, wfo, gfin, CTX_LEN, False)
    return xl
```

```python
import functools
import math

import numpy as np
import jax
import jax.numpy as jnp
from jax import lax
from jax.experimental import pallas as pl
from jax.experimental.pallas import tpu as pltpu

F32 = jnp.float32
BF16 = jnp.bfloat16

D_MODEL = 1024
BATCH = 4
SEQ = 4096
DEPTH = 2
GRID_W = 64
GRID_ROWS = SEQ // GRID_W
CTX_LEN = 256
EPS = 1e-6
HEAD_DIM = 64
ROPE_BASE = 10000.0
WA_HEADS = 4
WA_KV_HEADS = 2
WA_WINDOW = 128
WA_BLOCK = 128
NA_HEADS = 4
NA_KH = 8
NA_KW = 16
SSM_HEADS = 8
SSM_HEAD_DIM = 64
SSM_INNER = SSM_HEADS * SSM_HEAD_DIM
SSM_GROUPS = 2
SSM_STATE = 128
SSM_CONV = 7
SSM_CHUNK = 128
D_FF = 2816
XBC_COLS = SSM_INNER + 2 * SSM_GROUPS * SSM_STATE
DT_COLS = 2 * SSM_HEADS

LANES = 128
SUBLANES = 8
VMEM_LIMIT = 56 * 1024 * 1024

A_COLS = 512
B_COLS = 768
XZ_COLS = 1536
DT_PAD = LANES
PROJ_COLS = A_COLS + B_COLS + XZ_COLS + DT_PAD

ATT_SCALE = HEAD_DIM ** -0.5
NEG_INF = float("-inf")
NT_DIMS = (((1,), (1,)), ((), ()))


def _silu(v):
    return v / (1.0 + jnp.exp(-v))


def _bdot(a, b):
    return jnp.dot(a.astype(BF16), b.astype(BF16), preferred_element_type=F32)


def _bdot_nt(a, b):
    return lax.dot_general(a.astype(BF16), b.astype(BF16), NT_DIMS, preferred_element_type=F32)


def _cparams(sem):
    return pltpu.CompilerParams(dimension_semantics=sem, vmem_limit_bytes=VMEM_LIMIT)


MOD_TN = 1536


def _mod_kernel(c_ref, w_ref, b_ref, o_ref):
    s = _silu(c_ref[...])
    o_ref[0] = _bdot(s, w_ref[0]) + b_ref[0]


def _modulation(cin, w_mod, b_mod):
    n = 6 * D_MODEL
    return pl.pallas_call(
        _mod_kernel,
        out_shape=jax.ShapeDtypeStruct((DEPTH, SUBLANES, n), F32),
        grid=(DEPTH, n // MOD_TN),
        in_specs=[
            pl.BlockSpec((SUBLANES, D_MODEL), lambda l, j: (0, 0)),
            pl.BlockSpec((1, D_MODEL, MOD_TN), lambda l, j: (l, 0, j)),
            pl.BlockSpec((1, 1, MOD_TN), lambda l, j: (l, 0, j)),
        ],
        out_specs=pl.BlockSpec((1, SUBLANES, MOD_TN), lambda l, j: (l, 0, j)),
        compiler_params=_cparams(("parallel", "parallel")),
        name="modulation",
    )(cin, w_mod, b_mod.reshape(DEPTH, 1, n))


def _norm_mod(x, g, shift, scale):
    var = jnp.mean(x * x, axis=-1, keepdims=True)
    h = x * lax.rsqrt(var + EPS) * g
    return h * (1.0 + scale) + shift


W_QA, W_QB, W_Z, W_KVA, W_KVB, W_XBC, W_DT = 0, 256, 512, 1024, 1280, 1792, 2816
IN_COLS = W_DT + DT_COLS


def _inproj_kernel(x_ref, mod_ref, g_ref, w_ref, wdt_ref, *rest, rope):
    if rope:
        cos_ref, sa_ref, sb_ref, oa_ref, ob_ref, oxz_ref, odt_ref = rest
    else:
        oa_ref, ob_ref, oxz_ref, odt_ref = rest
    h = _norm_mod(x_ref[0], g_ref[...], mod_ref[0, 0:1, :], mod_ref[0, 1:2, :]).astype(BF16)

    def proj(lo, hi):
        return jnp.dot(h, w_ref[:, lo:hi], preferred_element_type=F32)

    qa = proj(W_QA, W_QB)
    kva = proj(W_KVA, W_KVB)
    if rope:
        cos, sa, sb = cos_ref[...], sa_ref[...], sb_ref[...]

        def rot(v):
            up = pltpu.roll(v, LANES - 16, axis=1)
            dn = pltpu.roll(v, 16, axis=1)
            return v * cos + up * sa + dn * sb

        oa_ref[0, :, 0:LANES] = rot(qa[:, 0:LANES]).astype(BF16)
        oa_ref[0, :, LANES:2 * LANES] = rot(qa[:, LANES:]).astype(BF16)
        oa_ref[0, :, 2 * LANES:3 * LANES] = rot(kva[:, 0:LANES]).astype(BF16)
        oa_ref[0, :, 3 * LANES:] = kva[:, LANES:].astype(BF16)
    else:
        oa_ref[0, :, 0:2 * LANES] = qa.astype(BF16)
        oa_ref[0, :, 2 * LANES:] = kva.astype(BF16)
    ob_ref[0, :, 0:2 * LANES] = proj(W_QB, W_Z).astype(BF16)
    ob_ref[0, :, 2 * LANES:] = proj(W_KVB, W_XBC).astype(BF16)
    oxz_ref[0, :, 0:XBC_COLS] = proj(W_XBC, W_DT)
    oxz_ref[0, :, XBC_COLS:] = proj(W_Z, W_KVA)
    odt_ref[0] = jnp.dot(h, wdt_ref[...], preferred_element_type=F32)


def _inproj(x, mods, g, w, wdt, rope_tabs, tm):
    b, t, _ = x.shape
    rope = rope_tabs is not None
    per_batch = mods.shape[0] > 1
    in_specs = [
        pl.BlockSpec((1, tm, D_MODEL), lambda i, j: (i, j, 0)),
        pl.BlockSpec((1, SUBLANES, D_MODEL), (lambda i, j: (i, 0, 0)) if per_batch else (lambda i, j: (0, 0, 0))),
        pl.BlockSpec((1, D_MODEL), lambda i, j: (0, 0)),
        pl.BlockSpec((D_MODEL, IN_COLS), lambda i, j: (0, 0)),
        pl.BlockSpec((D_MODEL, DT_PAD), lambda i, j: (0, 0)),
    ]
    args = [x, mods, g, w, wdt]
    if rope:
        in_specs += [pl.BlockSpec((tm, LANES), lambda i, j: (j, 0))] * 3
        args += list(rope_tabs)
    widths = (A_COLS, B_COLS, XZ_COLS, DT_PAD)
    return pl.pallas_call(
        functools.partial(_inproj_kernel, rope=rope),
        out_shape=[jax.ShapeDtypeStruct((b, t, n), dt) for n, dt in zip(widths, (BF16, BF16, F32, F32))],
        grid=(b, t // tm),
        in_specs=in_specs,
        out_specs=[pl.BlockSpec((1, tm, n), lambda i, j: (i, j, 0)) for n in widths],
        compiler_params=_cparams(("parallel", "parallel")),
        name="inproj_rope" if rope else "inproj",
    )(*args)


WA_KEYS = 3 * WA_BLOCK


WA_BLOCKS_PER_STEP = 2


def _attn_a_kernel(sink_ref, q_ref, k_ref, v_ref, kc_ref, vc_ref, o_ref, s_ref, p_ref):
    step = pl.program_id(1)
    nk = WA_KEYS + CTX_LEN
    nq = WA_HEADS * WA_BLOCK
    kc, vc = kc_ref[0], vc_ref[0]
    starts = []
    for bb in range(WA_BLOCKS_PER_STEP):
        n = step * WA_BLOCKS_PER_STEP + bb
        starts.append(pl.multiple_of(jnp.clip((n - 1) * WA_BLOCK, 0, SEQ - WA_KEYS), WA_BLOCK))

    for bb in range(WA_BLOCKS_PER_STEP):
        q = q_ref[0, bb * WA_BLOCK:(bb + 1) * WA_BLOCK, :] * ATT_SCALE
        kall = jnp.concatenate([k_ref[0, pl.ds(starts[bb], WA_KEYS), :], kc], axis=0)
        for h in range(WA_HEADS):
            g = h // (WA_HEADS // WA_KV_HEADS)
            r0 = bb * nq + h * WA_BLOCK
            s_ref[r0:r0 + WA_BLOCK, :] = _bdot_nt(
                q[:, h * HEAD_DIM:(h + 1) * HEAD_DIM], kall[:, g * HEAD_DIM:(g + 1) * HEAD_DIM])

    nr = WA_BLOCKS_PER_STEP * nq
    rows = lax.broadcasted_iota(jnp.int32, (nr, nk), 0)
    cols = lax.broadcasted_iota(jnp.int32, (nr, nk), 1)
    row1 = lax.broadcasted_iota(jnp.int32, (nr, 1), 0)
    blk = row1 // nq
    qpos = (step * WA_BLOCKS_PER_STEP + blk) * WA_BLOCK + (rows & (WA_BLOCK - 1))
    kstart = starts[-1]
    for bb in range(WA_BLOCKS_PER_STEP - 2, -1, -1):
        kstart = jnp.where(blk == bb, starts[bb], kstart)
    valid = (cols >= WA_KEYS) | (jnp.abs(qpos - (kstart + cols)) <= WA_WINDOW)
    s = jnp.where(valid, s_ref[...], NEG_INF)
    head = (row1 // WA_BLOCK) % WA_HEADS
    sink = jnp.where(head == 0, sink_ref[0],
                     jnp.where(head == 1, sink_ref[1], jnp.where(head == 2, sink_ref[2], sink_ref[3])))
    m = jnp.maximum(jnp.max(s, axis=1, keepdims=True), sink)
    p = jnp.exp(s - m)
    inv = 1.0 / (jnp.sum(p, axis=1, keepdims=True) + jnp.exp(sink - m))
    p_ref[...] = p.astype(BF16)

    for bb in range(WA_BLOCKS_PER_STEP):
        vall = jnp.concatenate([v_ref[0, pl.ds(starts[bb], WA_KEYS), :], vc], axis=0)
        for h in range(WA_HEADS):
            g = h // (WA_HEADS // WA_KV_HEADS)
            rs = slice(bb * nq + h * WA_BLOCK, bb * nq + (h + 1) * WA_BLOCK)
            o = jnp.dot(p_ref[rs, :], vall[:, g * HEAD_DIM:(g + 1) * HEAD_DIM], preferred_element_type=F32)
            o_ref[0, bb * WA_BLOCK:(bb + 1) * WA_BLOCK, h * HEAD_DIM:(h + 1) * HEAD_DIM] = (
                o * inv[rs]).astype(o_ref.dtype)


def _attn_a(sink, qkv, qkv_c):
    b = qkv.shape[0]
    nk = WA_KEYS + CTX_LEN
    tq = WA_BLOCKS_PER_STEP * WA_BLOCK
    nr = WA_BLOCKS_PER_STEP * WA_HEADS * WA_BLOCK
    return pl.pallas_call(
        _attn_a_kernel,
        out_shape=jax.ShapeDtypeStruct((b, SEQ, WA_HEADS * HEAD_DIM), BF16),
        scratch_shapes=[pltpu.VMEM((nr, nk), F32), pltpu.VMEM((nr, nk), BF16)],
        grid=(b, SEQ // tq),
        in_specs=[
            pl.BlockSpec(memory_space=pltpu.SMEM),
            pl.BlockSpec((1, tq, 2 * LANES), lambda i, j: (i, j, 0)),
            pl.BlockSpec((1, SEQ, LANES), lambda i, j: (i, 0, 2)),
            pl.BlockSpec((1, SEQ, LANES), lambda i, j: (i, 0, 3)),
            pl.BlockSpec((1, CTX_LEN, LANES), lambda i, j: (i, 0, 2)),
            pl.BlockSpec((1, CTX_LEN, LANES), lambda i, j: (i, 0, 3)),
        ],
        out_specs=pl.BlockSpec((1, tq, 2 * LANES), lambda i, j: (i, j, 0)),
        compiler_params=_cparams(("parallel", "arbitrary")),
        name="attn_window",
    )(sink, qkv, qkv, qkv, qkv_c, qkv_c)


NB_ROWS_PER_STEP = 4
NB_LOC = NA_KH * GRID_W
NB_DY_PAIRS = 2 * NA_KH - 2


def _attn_b_kernel(q_ref, k_ref, v_ref, kc_ref, vc_ref, t_ref, o_ref, s_ref, p_ref):
    i = pl.program_id(1)
    kc = kc_ref[0]
    vc = vc_ref[0]
    units = [(rr, h) for rr in range(NB_ROWS_PER_STEP) for h in range(NA_HEADS)]
    starts, shifts = [], []
    for rr in range(NB_ROWS_PER_STEP):
        r = i * NB_ROWS_PER_STEP + rr
        rs = jnp.clip(r - NA_KH // 2, 0, GRID_ROWS - NA_KH)
        shifts.append(r - rs)
        starts.append(pl.multiple_of(rs * GRID_W, GRID_W))

    for u, (rr, h) in enumerate(units):
        if h == 0:
            kw = k_ref[0, pl.ds(starts[rr], NB_LOC), :]
            q = q_ref[0, rr * GRID_W:(rr + 1) * GRID_W, :] * ATT_SCALE
        hs = slice(h * HEAD_DIM, (h + 1) * HEAD_DIM)
        rows = slice(u * GRID_W, (u + 1) * GRID_W)
        bias = jnp.concatenate(
            [t_ref[h, 2 * k - shifts[rr] + NA_KH - 1] for k in range(NA_KH // 2)], axis=1)
        s_ref[rows, 0:NB_LOC] = _bdot_nt(q[:, hs], kw[:, hs]) + bias
        s_ref[rows, NB_LOC:] = _bdot_nt(q[:, hs], kc[:, hs])

    s = s_ref[...]
    p = jnp.exp(s - jnp.max(s, axis=1, keepdims=True))
    inv = 1.0 / jnp.sum(p, axis=1, keepdims=True)
    p_ref[...] = p.astype(BF16)

    for u, (rr, h) in enumerate(units):
        if h == 0:
            vw = v_ref[0, pl.ds(starts[rr], NB_LOC), :]
        hs = slice(h * HEAD_DIM, (h + 1) * HEAD_DIM)
        rows = slice(u * GRID_W, (u + 1) * GRID_W)
        o = (jnp.dot(p_ref[rows, 0:NB_LOC], vw[:, hs], preferred_element_type=F32)
             + jnp.dot(p_ref[rows, NB_LOC:], vc[:, hs], preferred_element_type=F32))
        o_ref[0, rr * GRID_W:(rr + 1) * GRID_W, hs] = (o * inv[rows]).astype(o_ref.dtype)


def _attn_b(qkv, qkv_c, table):
    b = qkv.shape[0]
    tq = NB_ROWS_PER_STEP * GRID_W
    w = NA_HEADS * HEAD_DIM
    nu = NB_ROWS_PER_STEP * NA_HEADS * GRID_W
    return pl.pallas_call(
        _attn_b_kernel,
        out_shape=jax.ShapeDtypeStruct((b, SEQ, w), BF16),
        scratch_shapes=[pltpu.VMEM((nu, NB_LOC + CTX_LEN), F32), pltpu.VMEM((nu, NB_LOC + CTX_LEN), BF16)],
        grid=(b, SEQ // tq),
        in_specs=[
            pl.BlockSpec((1, tq, w), lambda i, j: (i, j, 0)),
            pl.BlockSpec((1, SEQ, w), lambda i, j: (i, 0, 1)),
            pl.BlockSpec((1, SEQ, w), lambda i, j: (i, 0, 2)),
            pl.BlockSpec((1, CTX_LEN, w), lambda i, j: (i, 0, 1)),
            pl.BlockSpec((1, CTX_LEN, w), lambda i, j: (i, 0, 2)),
            pl.BlockSpec((NA_HEADS, NB_DY_PAIRS, GRID_W, LANES), lambda i, j: (0, 0, 0, 0)),
        ],
        out_specs=pl.BlockSpec((1, tq, w), lambda i, j: (i, j, 0)),
        compiler_params=_cparams(("parallel", "arbitrary")),
        name="attn_neighbourhood",
    )(qkv, qkv, qkv, qkv_c, qkv_c, table)


def _split3(a):
    a1 = a.astype(BF16)
    r1 = a - a1.astype(F32)
    a2 = r1.astype(BF16)
    a3 = (r1 - a2.astype(F32)).astype(BF16)
    return a1, a2, a3


def _bias_kernel(r_ref, oh_ref, o_ref):
    oh = oh_ref[...]
    o_ref[...] = sum(jnp.dot(t, oh, preferred_element_type=F32) for t in _split3(r_ref[...]))


def _na_bias_tables(rpb):
    ndy, ndx = 2 * NA_KH - 1, 2 * NA_KW - 1
    qc = np.arange(GRID_W)[:, None]
    x = np.arange(GRID_W)[None, :]
    dx = np.clip(x - qc, -(NA_KW - 1), NA_KW - 1) + NA_KW - 1
    onehot = (np.arange(LANES)[:, None, None] == dx[None]).reshape(LANES, GRID_W * GRID_W)
    cstart = np.clip(qc - NA_KW // 2, 0, GRID_W - NA_KW)
    inside = (x >= cstart) & (x < cstart + NA_KW)
    rows = DEPTH * NA_HEADS * ndy
    r = jnp.pad(rpb.astype(F32).reshape(rows, ndx), ((0, LANES - rows), (0, LANES - ndx)))
    m = pl.pallas_call(
        _bias_kernel,
        out_shape=jax.ShapeDtypeStruct((LANES, GRID_W * GRID_W), F32),
        name="na_bias_expand",
    )(r, jnp.asarray(onehot, BF16))
    m = m[:rows].reshape(DEPTH, NA_HEADS, ndy, GRID_W, GRID_W)
    m = jnp.where(jnp.asarray(inside), m, NEG_INF)
    return jnp.concatenate([m[:, :, :ndy - 1], m[:, :, 1:]], axis=-1)


def _ctx_attn_kernel(sink_ref, a_ref, b_ref, oa_ref, ob_ref):
    a = a_ref[0]
    row1 = lax.broadcasted_iota(jnp.int32, (2 * CTX_LEN, 1), 0)
    for g in range(WA_KV_HEADS):
        h0, h1 = 2 * g, 2 * g + 1
        q2 = jnp.concatenate([a[:, h0 * HEAD_DIM:(h0 + 1) * HEAD_DIM],
                              a[:, h1 * HEAD_DIM:(h1 + 1) * HEAD_DIM]], axis=0)
        k = a[:, 2 * LANES + g * HEAD_DIM:2 * LANES + (g + 1) * HEAD_DIM]
        v = a[:, 3 * LANES + g * HEAD_DIM:3 * LANES + (g + 1) * HEAD_DIM]
        s = _bdot_nt(q2, k) * ATT_SCALE
        sink = jnp.where(row1 < CTX_LEN, sink_ref[h0], sink_ref[h1])
        m = jnp.maximum(jnp.max(s, axis=1, keepdims=True), sink)
        p = jnp.exp(s - m)
        den = jnp.sum(p, axis=1, keepdims=True) + jnp.exp(sink - m)
        o = _bdot(p, v) / den
        oa_ref[0, :, h0 * HEAD_DIM:(h0 + 1) * HEAD_DIM] = o[:CTX_LEN]
        oa_ref[0, :, h1 * HEAD_DIM:(h1 + 1) * HEAD_DIM] = o[CTX_LEN:]
    bq = b_ref[0]
    w = NA_HEADS * HEAD_DIM
    for h in range(NA_HEADS):
        hs = slice(h * HEAD_DIM, (h + 1) * HEAD_DIM)
        s = _bdot_nt(bq[:, hs], bq[:, w + h * HEAD_DIM:w + (h + 1) * HEAD_DIM]) * ATT_SCALE
        m = jnp.max(s, axis=1, keepdims=True)
        p = jnp.exp(s - m)
        den = jnp.sum(p, axis=1, keepdims=True)
        ob_ref[0, :, hs] = _bdot(p, bq[:, 2 * w + h * HEAD_DIM:2 * w + (h + 1) * HEAD_DIM]) / den


def _ctx_attn(sink, qkv_a_c, qkv_b_c):
    b = qkv_a_c.shape[0]
    w = 2 * LANES
    return pl.pallas_call(
        _ctx_attn_kernel,
        out_shape=[jax.ShapeDtypeStruct((b, CTX_LEN, w), F32)] * 2,
        grid=(b,),
        in_specs=[
            pl.BlockSpec(memory_space=pltpu.SMEM),
            pl.BlockSpec((1, CTX_LEN, A_COLS), lambda i: (i, 0, 0)),
            pl.BlockSpec((1, CTX_LEN, B_COLS), lambda i: (i, 0, 0)),
        ],
        out_specs=[pl.BlockSpec((1, CTX_LEN, w), lambda i: (i, 0, 0))] * 2,
        compiler_params=_cparams(("parallel",)),
        name="attn_context",
    )(sink, qkv_a_c, qkv_b_c)


CONV_HALO = SUBLANES


def _conv_kernel(prev_ref, cur_ref, next_ref, w_ref, b_ref, o_ref, ext_ref, *, tl, nt):
    j = pl.program_id(1)
    ext_ref[0:CONV_HALO, :] = jnp.where(j > 0, prev_ref[0], 0.0)
    ext_ref[CONV_HALO:CONV_HALO + tl, :] = cur_ref[0]
    ext_ref[CONV_HALO + tl:, :] = jnp.where(j < nt - 1, next_ref[0], 0.0)
    acc = jnp.zeros((tl, XBC_COLS), F32) + b_ref[...]
    base = CONV_HALO - SSM_CONV // 2
    for k in range(SSM_CONV):
        acc = acc + w_ref[k:k + 1, :] * ext_ref[base + k:base + k + tl, :]
    o_ref[0] = _silu(acc)


def _conv_silu(xz, conv_w, conv_b, tl):
    b, t, _ = xz.shape
    nt = t // tl
    hb = tl // CONV_HALO
    last = t // CONV_HALO - 1
    return pl.pallas_call(
        functools.partial(_conv_kernel, tl=tl, nt=nt),
        out_shape=jax.ShapeDtypeStruct((b, t, XBC_COLS), F32),
        grid=(b, nt),
        in_specs=[
            pl.BlockSpec((1, CONV_HALO, XBC_COLS), lambda i, j: (i, jnp.maximum(j * hb - 1, 0), 0)),
            pl.BlockSpec((1, tl, XBC_COLS), lambda i, j: (i, j, 0)),
            pl.BlockSpec((1, CONV_HALO, XBC_COLS), lambda i, j: (i, jnp.minimum((j + 1) * hb, last), 0)),
            pl.BlockSpec((SUBLANES, XBC_COLS), lambda i, j: (0, 0)),
            pl.BlockSpec((1, XBC_COLS), lambda i, j: (0, 0)),
        ],
        out_specs=pl.BlockSpec((1, tl, XBC_COLS), lambda i, j: (i, j, 0)),
        scratch_shapes=[pltpu.VMEM((tl + 2 * CONV_HALO, XBC_COLS), F32)],
        compiler_params=_cparams(("parallel", "parallel")),
        name="ssm_conv",
    )(xz, xz, xz, conv_w, conv_b)


Q = SSM_CHUNK
GS = SSM_GROUPS * SSM_STATE
HPG = SSM_HEADS // SSM_GROUPS


def _softplus(v):
    return jnp.maximum(v, 0.0) + jnp.log1p(jnp.exp(-jnp.abs(v)))


def _cumsum_mat(tri, a):
    return sum(jnp.dot(tri, t, preferred_element_type=F32) for t in _split3(a))


def _ssd_kernel(xm_ref, dtm_ref, xb_ref, dtb_ref, bias_ref, alog_ref, h0f_ref, h0b_ref,
                y1_ref, y2_ref, hf_out_ref, hb_out_ref, hf_ref, hb_ref, *, nc):
    i = pl.program_id(1)

    @pl.when(i == 0)
    def _():
        hf_ref[...] = h0f_ref[0]
        hb_ref[...] = h0b_ref[0]

    ii = lax.broadcasted_iota(jnp.int32, (Q, Q), 0)
    jj = lax.broadcasted_iota(jnp.int32, (Q, Q), 1)
    lower = ii >= jj
    upper = ii <= jj
    tril = jnp.where(lower, 1.0, 0.0).astype(BF16)
    triu = jnp.where(upper, 1.0, 0.0).astype(BF16)
    a_row = -jnp.exp(alog_ref[...])

    xbc = xm_ref[0]
    dt = _softplus(dtm_ref[0] + bias_ref[...])
    a = dt * a_row
    cumf = _cumsum_mat(tril, a)
    cumb = _cumsum_mat(triu, a)
    cumf_t, cumb_t, dt_t = cumf.T, cumb.T, dt.T
    for g in range(SSM_GROUPS):
        bg = xbc[:, SSM_INNER + g * SSM_STATE:SSM_INNER + (g + 1) * SSM_STATE]
        cg = xbc[:, SSM_INNER + GS + g * SSM_STATE:SSM_INNER + GS + (g + 1) * SSM_STATE]
        bg_t = bg.T
        gmat = _bdot(cg, bg_t)
        cg16 = cg.astype(BF16)
        for hh in range(HPG):
            h = g * HPG + hh
            hb_ = SSM_HEADS + h
            xh = xbc[:, h * SSM_HEAD_DIM:(h + 1) * SSM_HEAD_DIM].astype(BF16)
            cf, rf, dtf = cumf[:, h:h + 1], cumf_t[h:h + 1, :], dt_t[h:h + 1, :]
            cb, rb, dtb = cumb[:, hb_:hb_ + 1], cumb_t[hb_:hb_ + 1, :], dt_t[hb_:hb_ + 1, :]
            lf = jnp.exp(jnp.where(lower, cf - rf, NEG_INF))
            lb = jnp.exp(jnp.where(upper, cb - rb, NEG_INF))
            wmat = gmat * (lf * dtf + lb * dtb)
            y = _bdot(wmat, xh)
            hprev = hf_ref[h]
            y = y + jnp.exp(cf) * jnp.dot(cg16, hprev.astype(BF16), preferred_element_type=F32)
            y1_ref[0, :, h * SSM_HEAD_DIM:(h + 1) * SSM_HEAD_DIM] = y
            tot = cumf_t[h:h + 1, Q - 1:Q]
            wrow = jnp.exp(tot - rf) * dtf
            hf_ref[h] = hprev * jnp.exp(tot) + _bdot(bg_t * wrow, xh)

    xbc = xb_ref[0]
    dt = _softplus(dtb_ref[0] + bias_ref[...])
    cumb = _cumsum_mat(triu, dt * a_row)
    cumb_t, dt_t = cumb.T, dt.T
    for g in range(SSM_GROUPS):
        bg_t = xbc[:, SSM_INNER + g * SSM_STATE:SSM_INNER + (g + 1) * SSM_STATE].T
        cg16 = xbc[:, SSM_INNER + GS + g * SSM_STATE:SSM_INNER + GS + (g + 1) * SSM_STATE].astype(BF16)
        for hh in range(HPG):
            h = g * HPG + hh
            hb_ = SSM_HEADS + h
            xh = xbc[:, h * SSM_HEAD_DIM:(h + 1) * SSM_HEAD_DIM].astype(BF16)
            cb, rb, dtb = cumb[:, hb_:hb_ + 1], cumb_t[hb_:hb_ + 1, :], dt_t[hb_:hb_ + 1, :]
            hprev = hb_ref[h]
            y2_ref[0, :, h * SSM_HEAD_DIM:(h + 1) * SSM_HEAD_DIM] = (
                jnp.exp(cb) * jnp.dot(cg16, hprev.astype(BF16), preferred_element_type=F32))
            tot = cumb_t[hb_:hb_ + 1, 0:1]
            wrow = jnp.exp(tot - rb) * dtb
            hb_ref[h] = hprev * jnp.exp(tot) + _bdot(bg_t * wrow, xh)

    @pl.when(i == nc - 1)
    def _():
        hf_out_ref[0] = hf_ref[...]
        hb_out_ref[0] = hb_ref[...]


def _ssd(xbc, dt_raw, dt_bias, a_log, h0f, h0b):
    b, t, _ = xbc.shape
    nc = t // Q
    st_shape = (b, SSM_HEADS, SSM_STATE, SSM_HEAD_DIM)
    st_spec = pl.BlockSpec((1, SSM_HEADS, SSM_STATE, SSM_HEAD_DIM), lambda i, j: (i, 0, 0, 0))
    fwd = lambda i, j: (i, j, 0)
    bwd = lambda i, j: (i, nc - 1 - j, 0)
    return pl.pallas_call(
        functools.partial(_ssd_kernel, nc=nc),
        out_shape=[jax.ShapeDtypeStruct((b, t, SSM_INNER), F32)] * 2
        + [jax.ShapeDtypeStruct(st_shape, F32)] * 2,
        grid=(b, nc),
        in_specs=[
            pl.BlockSpec((1, Q, XBC_COLS), fwd),
            pl.BlockSpec((1, Q, DT_PAD), fwd),
            pl.BlockSpec((1, Q, XBC_COLS), bwd),
            pl.BlockSpec((1, Q, DT_PAD), bwd),
            pl.BlockSpec((1, DT_PAD), lambda i, j: (0, 0)),
            pl.BlockSpec((1, DT_PAD), lambda i, j: (0, 0)),
            st_spec, st_spec,
        ],
        out_specs=[pl.BlockSpec((1, Q, SSM_INNER), fwd), pl.BlockSpec((1, Q, SSM_INNER), bwd),
                   st_spec, st_spec],
        scratch_shapes=[pltpu.VMEM((SSM_HEADS, SSM_STATE, SSM_HEAD_DIM), F32)] * 2,
        compiler_params=_cparams(("parallel", "arbitrary")),
        name="ssd_scan",
    )(xbc, dt_raw, xbc, dt_raw, dt_bias, a_log, h0f, h0b)


FFN_CHUNKS = 2
FFN_TF = D_FF // FFN_CHUNKS


def _mix_ffn_kernel(x_ref, mod_ref, oa_ref, ob_ref, y1_ref, y2_ref, xs_ref, z_ref, d_ref, sg_ref,
                    wo_ref, g_ref, wfi_ref, wfo_ref, gfin_ref, o_ref, *, final):
    y = y1_ref[0] + y2_ref[0] + d_ref[...] * xs_ref[0]
    y = y * _silu(z_ref[0])
    var = jnp.mean(y * y, axis=-1, keepdims=True)
    oc = (y * lax.rsqrt(var + EPS) * sg_ref[...]).astype(BF16)
    wa = 2 * LANES
    mix = (jnp.dot(oa_ref[0].astype(BF16), wo_ref[0:wa, :], preferred_element_type=F32)
           + jnp.dot(ob_ref[0].astype(BF16), wo_ref[wa:2 * wa, :], preferred_element_type=F32)
           + jnp.dot(oc, wo_ref[2 * wa:, :], preferred_element_type=F32))
    xn = x_ref[0] + mod_ref[0, 2:3, :] * mix
    h = _norm_mod(xn, g_ref[...], mod_ref[0, 3:4, :], mod_ref[0, 4:5, :]).astype(BF16)
    acc = None
    for f in range(FFN_CHUNKS):
        gate = jnp.dot(h, wfi_ref[:, f * FFN_TF:(f + 1) * FFN_TF], preferred_element_type=F32)
        up = jnp.dot(h, wfi_ref[:, D_FF + f * FFN_TF:D_FF + (f + 1) * FFN_TF], preferred_element_type=F32)
        act = (_silu(gate) * up).astype(BF16)
        part = jnp.dot(act, wfo_ref[f * FFN_TF:(f + 1) * FFN_TF, :], preferred_element_type=F32)
        acc = part if acc is None else acc + part
    out = xn + mod_ref[0, 5:6, :] * acc
    if final:
        var = jnp.mean(out * out, axis=-1, keepdims=True)
        out = out * lax.rsqrt(var + EPS) * gfin_ref[...]
    o_ref[0] = out


def _mix_ffn(x, mods, o_a, o_b, y1, y2, xbc, xz, d_exp, ssm_g, w_out, g_ffn, w_ffn_in, w_ffn_out,
             g_final, tm, final):
    b, t, _ = x.shape
    per_batch = mods.shape[0] > 1
    row = lambda n: pl.BlockSpec((1, tm, n), lambda i, j: (i, j, 0))
    const = lambda shape: pl.BlockSpec(shape, lambda i, j: (0, 0), pipeline_mode=pl.Buffered(1))
    return pl.pallas_call(
        functools.partial(_mix_ffn_kernel, final=final),
        out_shape=jax.ShapeDtypeStruct((b, t, D_MODEL), F32),
        grid=(b, t // tm),
        in_specs=[
            row(D_MODEL),
            pl.BlockSpec((1, SUBLANES, D_MODEL), (lambda i, j: (i, 0, 0)) if per_batch else (lambda i, j: (0, 0, 0))),
            row(2 * LANES), row(2 * LANES), row(SSM_INNER), row(SSM_INNER),
            pl.BlockSpec((1, tm, SSM_INNER), lambda i, j: (i, j, 0)),
            pl.BlockSpec((1, tm, SSM_INNER), lambda i, j: (i, j, 2)),
            const((1, SSM_INNER)), const((1, SSM_INNER)),
            const((D_MODEL, D_MODEL)),
            const((1, D_MODEL)),
            const((D_MODEL, 2 * D_FF)),
            const((D_FF, D_MODEL)),
            const((1, D_MODEL)),
        ],
        out_specs=row(D_MODEL),
        compiler_params=_cparams(("parallel", "parallel")),
        name="mix_ffn_final" if final else "mix_ffn",
    )(x, mods, o_a, o_b, y1, y2, xbc, xz, d_exp, ssm_g, w_out, g_ffn, w_ffn_in, w_ffn_out, g_final)


def _rope_tables():
    t = np.arange(SEQ)
    pos = np.stack([t // GRID_W, t % GRID_W], axis=1).astype(np.float32)
    quarter = HEAD_DIM // 4
    inv_freq = jnp.asarray(ROPE_BASE, F32) ** (-jnp.arange(quarter, dtype=F32) / quarter)
    lane = np.arange(LANES) % HEAD_DIM
    half = lane // (HEAD_DIM // 2)
    idx = lane % (HEAD_DIM // 2)
    ang = jnp.asarray(pos)[:, half] * inv_freq[idx % quarter][None, :]
    cos, sin = jnp.cos(ang), jnp.sin(ang)
    first = jnp.asarray(idx < quarter)[None, :]
    return cos, jnp.where(first, -sin, 0.0), jnp.where(first, 0.0, sin)


def _pad_lanes(v, n=LANES):
    v = v.reshape(1, -1)
    return jnp.pad(v, ((0, 0), (0, n - v.shape[1])))


def kernel(x, c, ctx, c_ctx, w_mod, b_mod, g_mix, w_in, wa_sink, na_rpb, ssm_conv_w, ssm_conv_b,
           ssm_dt_bias, ssm_a_log, ssm_d, ssm_norm_g, w_out, g_ffn, w_ffn_in, w_ffn_out, g_final):
    cin = jnp.concatenate([c, c_ctx[None, :], jnp.zeros((SUBLANES - BATCH - 1, D_MODEL), F32)], axis=0)
    mod_all = _modulation(cin, w_mod, b_mod)
    rope_tabs = _rope_tables()
    bias_tabs = _na_bias_tables(na_rpb)
    zeros_state = jnp.zeros((BATCH, SSM_HEADS, SSM_STATE, SSM_HEAD_DIM), F32)
    gfin = g_final.reshape(1, D_MODEL)

    xl, xc = x, ctx
    for l in range(DEPTH):
        last = l == DEPTH - 1
        m6 = mod_all[l].reshape(SUBLANES, 6, D_MODEL)
        mods_l = jnp.pad(m6[:BATCH], ((0, 0), (0, 2), (0, 0)))
        mods_c = jnp.pad(m6[BATCH:BATCH + 1], ((0, 0), (0, 2), (0, 0)))
        w_proj = w_in[l].astype(BF16)
        w_dt = jnp.pad(w_in[l][:, W_DT:], ((0, 0), (0, DT_PAD - DT_COLS))).astype(BF16)
        g1 = g_mix[l].reshape(1, D_MODEL)
        g2 = g_ffn[l].reshape(1, D_MODEL)
        conv_w = jnp.pad(ssm_conv_w[l], ((0, SUBLANES - SSM_CONV), (0, 0)))
        conv_b = ssm_conv_b[l].reshape(1, XBC_COLS)
        dt_bias = _pad_lanes(ssm_dt_bias[l])
        a_log = _pad_lanes(ssm_a_log[l])
        d_exp = jnp.repeat(ssm_d[l], SSM_HEAD_DIM).reshape(1, SSM_INNER)
        sg = ssm_norm_g[l].reshape(1, SSM_INNER)
        wo = w_out[l].astype(BF16)
        wfi = w_ffn_in[l].astype(BF16)
        wfo = w_ffn_out[l].astype(BF16)

        a_c, b_c, xz_c, dt_c = _inproj(xc, mods_c, g1, w_proj, w_dt, None, CTX_LEN)
        a_l, b_l, xz_l, dt_l = _inproj(xl, mods_l, g1, w_proj, w_dt, rope_tabs, 512)

        o_a = _attn_a(wa_sink[l], a_l, a_c)
        o_b = _attn_b(b_l, b_c, bias_tabs[l])

        xbc_c = _conv_silu(xz_c, conv_w, conv_b, CTX_LEN)
        xbc_l = _conv_silu(xz_l, conv_w, conv_b, 512)
        y1_c, y2_c, h_f, h_b = _ssd(xbc_c, dt_c, dt_bias, a_log, zeros_state, zeros_state)
        y1_l, y2_l, _, _ = _ssd(xbc_l, dt_l, dt_bias, a_log, h_f, h_b)

        xl = _mix_ffn(xl, mods_l, o_a, o_b, y1_l, y2_l, xbc_l, xz_l, d_exp, sg, wo, g2, wfi, wfo, gfin,
                      512, last)
        if not last:
            o_ac, o_bc = _ctx_attn(wa_sink[l], a_c, b_c)
            xc = _mix_ffn(xc, mods_c, o_ac, o_bc, y1_c, y2_c, xbc_c, xz_c, d_exp, sg, wo, g2, wfi, wfo,
                          gfin, CTX_LEN, False)
    return xl
```

```python
import functools
import math

import numpy as np
import jax
import jax.numpy as jnp
from jax import lax
from jax.experimental import pallas as pl
from jax.experimental.pallas import tpu as pltpu

F32 = jnp.float32
BF16 = jnp.bfloat16

D_MODEL = 1024
BATCH = 4
SEQ = 4096
DEPTH = 2
GRID_W = 64
GRID_ROWS = SEQ // GRID_W
CTX_LEN = 256
EPS = 1e-6
HEAD_DIM = 64
ROPE_BASE = 10000.0
WA_HEADS = 4
WA_KV_HEADS = 2
WA_WINDOW = 128
WA_BLOCK = 128
NA_HEADS = 4
NA_KH = 8
NA_KW = 16
SSM_HEADS = 8
SSM_HEAD_DIM = 64
SSM_INNER = SSM_HEADS * SSM_HEAD_DIM
SSM_GROUPS = 2
SSM_STATE = 128
SSM_CONV = 7
SSM_CHUNK = 128
D_FF = 2816
XBC_COLS = SSM_INNER + 2 * SSM_GROUPS * SSM_STATE
DT_COLS = 2 * SSM_HEADS

LANES = 128
SUBLANES = 8
VMEM_LIMIT = 56 * 1024 * 1024

A_COLS = 512
B_COLS = 768
XZ_COLS = 1536
DT_PAD = LANES
PROJ_COLS = A_COLS + B_COLS + XZ_COLS + DT_PAD

ATT_SCALE = HEAD_DIM ** -0.5
NEG_INF = float("-inf")
NT_DIMS = (((1,), (1,)), ((), ()))


def _silu(v):
    return v / (1.0 + jnp.exp(-v))


def _bdot(a, b):
    return jnp.dot(a.astype(BF16), b.astype(BF16), preferred_element_type=F32)


def _bdot_nt(a, b):
    return lax.dot_general(a.astype(BF16), b.astype(BF16), NT_DIMS, preferred_element_type=F32)


def _cparams(sem):
    return pltpu.CompilerParams(dimension_semantics=sem, vmem_limit_bytes=VMEM_LIMIT)


MOD_TN = 1536


def _mod_kernel(c_ref, w_ref, b_ref, o_ref):
    s = _silu(c_ref[...])
    o_ref[0] = _bdot(s, w_ref[0]) + b_ref[0]


def _modulation(cin, w_mod, b_mod):
    n = 6 * D_MODEL
    return pl.pallas_call(
        _mod_kernel,
        out_shape=jax.ShapeDtypeStruct((DEPTH, SUBLANES, n), F32),
        grid=(DEPTH, n // MOD_TN),
        in_specs=[
            pl.BlockSpec((SUBLANES, D_MODEL), lambda l, j: (0, 0)),
            pl.BlockSpec((1, D_MODEL, MOD_TN), lambda l, j: (l, 0, j)),
            pl.BlockSpec((1, 1, MOD_TN), lambda l, j: (l, 0, j)),
        ],
        out_specs=pl.BlockSpec((1, SUBLANES, MOD_TN), lambda l, j: (l, 0, j)),
        compiler_params=_cparams(("parallel", "parallel")),
        name="modulation",
    )(cin, w_mod, b_mod.reshape(DEPTH, 1, n))


def _norm_mod(x, g, shift, scale):
    var = jnp.mean(x * x, axis=-1, keepdims=True)
    h = x * lax.rsqrt(var + EPS) * g
    return h * (1.0 + scale) + shift


W_QA, W_QB, W_Z, W_KVA, W_KVB, W_XBC, W_DT = 0, 256, 512, 1024, 1280, 1792, 2816
IN_COLS = W_DT + DT_COLS


def _inproj_kernel(x_ref, mod_ref, g_ref, w_ref, wdt_ref, *rest, rope):
    if rope:
        cos_ref, sa_ref, sb_ref, oa_ref, ob_ref, oxz_ref, odt_ref = rest
    else:
        oa_ref, ob_ref, oxz_ref, odt_ref = rest
    h = _norm_mod(x_ref[0], g_ref[...], mod_ref[0, 0:1, :], mod_ref[0, 1:2, :]).astype(BF16)

    def proj(lo, hi):
        return jnp.dot(h, w_ref[0, :, lo:hi], preferred_element_type=F32)

    qa = proj(W_QA, W_QB)
    kva = proj(W_KVA, W_KVB)
    if rope:
        cos, sa, sb = cos_ref[...], sa_ref[...], sb_ref[...]

        def rot(v):
            up = pltpu.roll(v, LANES - 16, axis=1)
            dn = pltpu.roll(v, 16, axis=1)
            return v * cos + up * sa + dn * sb

        oa_ref[0, :, 0:LANES] = rot(qa[:, 0:LANES]).astype(BF16)
        oa_ref[0, :, LANES:2 * LANES] = rot(qa[:, LANES:]).astype(BF16)
        oa_ref[0, :, 2 * LANES:3 * LANES] = rot(kva[:, 0:LANES]).astype(BF16)
        oa_ref[0, :, 3 * LANES:] = kva[:, LANES:].astype(BF16)
    else:
        oa_ref[0, :, 0:2 * LANES] = qa.astype(BF16)
        oa_ref[0, :, 2 * LANES:] = kva.astype(BF16)
    ob_ref[0, :, 0:2 * LANES] = proj(W_QB, W_Z).astype(BF16)
    ob_ref[0, :, 2 * LANES:] = proj(W_KVB, W_XBC).astype(BF16)
    oxz_ref[0, :, 0:XBC_COLS] = proj(W_XBC, W_DT)
    oxz_ref[0, :, XBC_COLS:] = proj(W_Z, W_KVA)
    odt_ref[0] = jnp.dot(h, wdt_ref[...], preferred_element_type=F32)


def _inproj(x, mods, g, w, layer, wdt, rope_tabs, tm):
    b, t, _ = x.shape
    rope = rope_tabs is not None
    per_batch = mods.shape[0] > 1
    in_specs = [
        pl.BlockSpec((1, tm, D_MODEL), lambda i, j: (i, j, 0)),
        pl.BlockSpec((1, SUBLANES, D_MODEL), (lambda i, j: (i, 0, 0)) if per_batch else (lambda i, j: (0, 0, 0))),
        pl.BlockSpec((1, D_MODEL), lambda i, j: (0, 0)),
        pl.BlockSpec((1, D_MODEL, IN_COLS), lambda i, j: (layer, 0, 0)),
        pl.BlockSpec((D_MODEL, DT_PAD), lambda i, j: (0, 0)),
    ]
    args = [x, mods, g, w, wdt]
    if rope:
        in_specs += [pl.BlockSpec((tm, LANES), lambda i, j: (j, 0))] * 3
        args += list(rope_tabs)
    widths = (A_COLS, B_COLS, XZ_COLS, DT_PAD)
    return pl.pallas_call(
        functools.partial(_inproj_kernel, rope=rope),
        out_shape=[jax.ShapeDtypeStruct((b, t, n), dt) for n, dt in zip(widths, (BF16, BF16, F32, F32))],
        grid=(b, t // tm),
        in_specs=in_specs,
        out_specs=[pl.BlockSpec((1, tm, n), lambda i, j: (i, j, 0)) for n in widths],
        compiler_params=_cparams(("parallel", "parallel")),
        name="inproj_rope" if rope else "inproj",
    )(*args)


WA_KEYS = 3 * WA_BLOCK


WA_BLOCKS_PER_STEP = 2


def _attn_a_kernel(sink_ref, q_ref, k_ref, v_ref, kc_ref, vc_ref, o_ref, s_ref, p_ref):
    step = pl.program_id(1)
    nk = WA_KEYS + CTX_LEN
    nq = WA_HEADS * WA_BLOCK
    kc, vc = kc_ref[0], vc_ref[0]
    starts = []
    for bb in range(WA_BLOCKS_PER_STEP):
        n = step * WA_BLOCKS_PER_STEP + bb
        starts.append(pl.multiple_of(jnp.clip((n - 1) * WA_BLOCK, 0, SEQ - WA_KEYS), WA_BLOCK))

    for bb in range(WA_BLOCKS_PER_STEP):
        q = q_ref[0, bb * WA_BLOCK:(bb + 1) * WA_BLOCK, :] * ATT_SCALE
        kall = jnp.concatenate([k_ref[0, pl.ds(starts[bb], WA_KEYS), :], kc], axis=0)
        for h in range(WA_HEADS):
            g = h // (WA_HEADS // WA_KV_HEADS)
            r0 = bb * nq + h * WA_BLOCK
            s_ref[r0:r0 + WA_BLOCK, :] = _bdot_nt(
                q[:, h * HEAD_DIM:(h + 1) * HEAD_DIM], kall[:, g * HEAD_DIM:(g + 1) * HEAD_DIM])

    nr = WA_BLOCKS_PER_STEP * nq
    rows = lax.broadcasted_iota(jnp.int32, (nr, nk), 0)
    cols = lax.broadcasted_iota(jnp.int32, (nr, nk), 1)
    row1 = lax.broadcasted_iota(jnp.int32, (nr, 1), 0)
    blk = row1 // nq
    qpos = (step * WA_BLOCKS_PER_STEP + blk) * WA_BLOCK + (rows & (WA_BLOCK - 1))
    kstart = starts[-1]
    for bb in range(WA_BLOCKS_PER_STEP - 2, -1, -1):
        kstart = jnp.where(blk == bb, starts[bb], kstart)
    valid = (cols >= WA_KEYS) | (jnp.abs(qpos - (kstart + cols)) <= WA_WINDOW)
    s = jnp.where(valid, s_ref[...], NEG_INF)
    head = (row1 // WA_BLOCK) % WA_HEADS
    sink = jnp.where(head == 0, sink_ref[0],
                     jnp.where(head == 1, sink_ref[1], jnp.where(head == 2, sink_ref[2], sink_ref[3])))
    m = jnp.maximum(jnp.max(s, axis=1, keepdims=True), sink)
    p = jnp.exp(s - m)
    inv = 1.0 / (jnp.sum(p, axis=1, keepdims=True) + jnp.exp(sink - m))
    p_ref[...] = p.astype(BF16)

    for bb in range(WA_BLOCKS_PER_STEP):
        vall = jnp.concatenate([v_ref[0, pl.ds(starts[bb], WA_KEYS), :], vc], axis=0)
        for h in range(WA_HEADS):
            g = h // (WA_HEADS // WA_KV_HEADS)
            rs = slice(bb * nq + h * WA_BLOCK, bb * nq + (h + 1) * WA_BLOCK)
            o = jnp.dot(p_ref[rs, :], vall[:, g * HEAD_DIM:(g + 1) * HEAD_DIM], preferred_element_type=F32)
            o_ref[0, bb * WA_BLOCK:(bb + 1) * WA_BLOCK, h * HEAD_DIM:(h + 1) * HEAD_DIM] = (
                o * inv[rs]).astype(o_ref.dtype)


def _attn_a(sink, qkv, qkv_c):
    b = qkv.shape[0]
    nk = WA_KEYS + CTX_LEN
    tq = WA_BLOCKS_PER_STEP * WA_BLOCK
    nr = WA_BLOCKS_PER_STEP * WA_HEADS * WA_BLOCK
    return pl.pallas_call(
        _attn_a_kernel,
        out_shape=jax.ShapeDtypeStruct((b, SEQ, WA_HEADS * HEAD_DIM), BF16),
        scratch_shapes=[pltpu.VMEM((nr, nk), F32), pltpu.VMEM((nr, nk), BF16)],
        grid=(b, SEQ // tq),
        in_specs=[
            pl.BlockSpec(memory_space=pltpu.SMEM),
            pl.BlockSpec((1, tq, 2 * LANES), lambda i, j: (i, j, 0)),
            pl.BlockSpec((1, SEQ, LANES), lambda i, j: (i, 0, 2)),
            pl.BlockSpec((1, SEQ, LANES), lambda i, j: (i, 0, 3)),
            pl.BlockSpec((1, CTX_LEN, LANES), lambda i, j: (i, 0, 2)),
            pl.BlockSpec((1, CTX_LEN, LANES), lambda i, j: (i, 0, 3)),
        ],
        out_specs=pl.BlockSpec((1, tq, 2 * LANES), lambda i, j: (i, j, 0)),
        compiler_params=_cparams(("parallel", "arbitrary")),
        name="attn_window",
    )(sink, qkv, qkv, qkv, qkv_c, qkv_c)


NB_ROWS_PER_STEP = 4
NB_LOC = NA_KH * GRID_W
NB_DY_PAIRS = 2 * NA_KH - 2


def _attn_b_kernel(q_ref, k_ref, v_ref, kc_ref, vc_ref, t_ref, o_ref, s_ref, p_ref):
    i = pl.program_id(1)
    kc = kc_ref[0]
    vc = vc_ref[0]
    units = [(rr, h) for rr in range(NB_ROWS_PER_STEP) for h in range(NA_HEADS)]
    starts, shifts = [], []
    for rr in range(NB_ROWS_PER_STEP):
        r = i * NB_ROWS_PER_STEP + rr
        rs = jnp.clip(r - NA_KH // 2, 0, GRID_ROWS - NA_KH)
        shifts.append(r - rs)
        starts.append(pl.multiple_of(rs * GRID_W, GRID_W))

    for u, (rr, h) in enumerate(units):
        if h == 0:
            kw = k_ref[0, pl.ds(starts[rr], NB_LOC), :]
            q = q_ref[0, rr * GRID_W:(rr + 1) * GRID_W, :] * ATT_SCALE
        hs = slice(h * HEAD_DIM, (h + 1) * HEAD_DIM)
        rows = slice(u * GRID_W, (u + 1) * GRID_W)
        bias = jnp.concatenate(
            [t_ref[h, 2 * k - shifts[rr] + NA_KH - 1] for k in range(NA_KH // 2)], axis=1)
        s_ref[rows, 0:NB_LOC] = _bdot_nt(q[:, hs], kw[:, hs]) + bias
        s_ref[rows, NB_LOC:] = _bdot_nt(q[:, hs], kc[:, hs])

    s = s_ref[...]
    p = jnp.exp(s - jnp.max(s, axis=1, keepdims=True))
    inv = 1.0 / jnp.sum(p, axis=1, keepdims=True)
    p_ref[...] = p.astype(BF16)

    for u, (rr, h) in enumerate(units):
        if h == 0:
            vw = v_ref[0, pl.ds(starts[rr], NB_LOC), :]
        hs = slice(h * HEAD_DIM, (h + 1) * HEAD_DIM)
        rows = slice(u * GRID_W, (u + 1) * GRID_W)
        o = (jnp.dot(p_ref[rows, 0:NB_LOC], vw[:, hs], preferred_element_type=F32)
             + jnp.dot(p_ref[rows, NB_LOC:], vc[:, hs], preferred_element_type=F32))
        o_ref[0, rr * GRID_W:(rr + 1) * GRID_W, hs] = (o * inv[rows]).astype(o_ref.dtype)


def _attn_b(qkv, qkv_c, table):
    b = qkv.shape[0]
    tq = NB_ROWS_PER_STEP * GRID_W
    w = NA_HEADS * HEAD_DIM
    nu = NB_ROWS_PER_STEP * NA_HEADS * GRID_W
    return pl.pallas_call(
        _attn_b_kernel,
        out_shape=jax.ShapeDtypeStruct((b, SEQ, w), BF16),
        scratch_shapes=[pltpu.VMEM((nu, NB_LOC + CTX_LEN), F32), pltpu.VMEM((nu, NB_LOC + CTX_LEN), BF16)],
        grid=(b, SEQ // tq),
        in_specs=[
            pl.BlockSpec((1, tq, w), lambda i, j: (i, j, 0)),
            pl.BlockSpec((1, SEQ, w), lambda i, j: (i, 0, 1)),
            pl.BlockSpec((1, SEQ, w), lambda i, j: (i, 0, 2)),
            pl.BlockSpec((1, CTX_LEN, w), lambda i, j: (i, 0, 1)),
            pl.BlockSpec((1, CTX_LEN, w), lambda i, j: (i, 0, 2)),
            pl.BlockSpec((NA_HEADS, NB_DY_PAIRS, GRID_W, LANES), lambda i, j: (0, 0, 0, 0)),
        ],
        out_specs=pl.BlockSpec((1, tq, w), lambda i, j: (i, j, 0)),
        compiler_params=_cparams(("parallel", "arbitrary")),
        name="attn_neighbourhood",
    )(qkv, qkv, qkv, qkv_c, qkv_c, table)


def _split3(a):
    a1 = a.astype(BF16)
    r1 = a - a1.astype(F32)
    a2 = r1.astype(BF16)
    a3 = (r1 - a2.astype(F32)).astype(BF16)
    return a1, a2, a3


def _bias_kernel(r_ref, oh_ref, o_ref):
    oh = oh_ref[...]
    o_ref[...] = sum(jnp.dot(t, oh, preferred_element_type=F32) for t in _split3(r_ref[...]))


def _na_bias_tables(rpb):
    ndy, ndx = 2 * NA_KH - 1, 2 * NA_KW - 1
    qc = np.arange(GRID_W)[:, None]
    x = np.arange(GRID_W)[None, :]
    dx = np.clip(x - qc, -(NA_KW - 1), NA_KW - 1) + NA_KW - 1
    onehot = (np.arange(LANES)[:, None, None] == dx[None]).reshape(LANES, GRID_W * GRID_W)
    cstart = np.clip(qc - NA_KW // 2, 0, GRID_W - NA_KW)
    inside = (x >= cstart) & (x < cstart + NA_KW)
    rows = DEPTH * NA_HEADS * ndy
    r = jnp.pad(rpb.astype(F32).reshape(rows, ndx), ((0, LANES - rows), (0, LANES - ndx)))
    m = pl.pallas_call(
        _bias_kernel,
        out_shape=jax.ShapeDtypeStruct((LANES, GRID_W * GRID_W), F32),
        name="na_bias_expand",
    )(r, jnp.asarray(onehot, BF16))
    m = m[:rows].reshape(DEPTH, NA_HEADS, ndy, GRID_W, GRID_W)
    m = jnp.where(jnp.asarray(inside), m, NEG_INF)
    return jnp.concatenate([m[:, :, :ndy - 1], m[:, :, 1:]], axis=-1)


def _ctx_attn_kernel(sink_ref, a_ref, b_ref, oa_ref, ob_ref):
    a = a_ref[0]
    row1 = lax.broadcasted_iota(jnp.int32, (2 * CTX_LEN, 1), 0)
    for g in range(WA_KV_HEADS):
        h0, h1 = 2 * g, 2 * g + 1
        q2 = jnp.concatenate([a[:, h0 * HEAD_DIM:(h0 + 1) * HEAD_DIM],
                              a[:, h1 * HEAD_DIM:(h1 + 1) * HEAD_DIM]], axis=0)
        k = a[:, 2 * LANES + g * HEAD_DIM:2 * LANES + (g + 1) * HEAD_DIM]
        v = a[:, 3 * LANES + g * HEAD_DIM:3 * LANES + (g + 1) * HEAD_DIM]
        s = _bdot_nt(q2, k) * ATT_SCALE
        sink = jnp.where(row1 < CTX_LEN, sink_ref[h0], sink_ref[h1])
        m = jnp.maximum(jnp.max(s, axis=1, keepdims=True), sink)
        p = jnp.exp(s - m)
        den = jnp.sum(p, axis=1, keepdims=True) + jnp.exp(sink - m)
        o = _bdot(p, v) / den
        oa_ref[0, :, h0 * HEAD_DIM:(h0 + 1) * HEAD_DIM] = o[:CTX_LEN]
        oa_ref[0, :, h1 * HEAD_DIM:(h1 + 1) * HEAD_DIM] = o[CTX_LEN:]
    bq = b_ref[0]
    w = NA_HEADS * HEAD_DIM
    for h in range(NA_HEADS):
        hs = slice(h * HEAD_DIM, (h + 1) * HEAD_DIM)
        s = _bdot_nt(bq[:, hs], bq[:, w + h * HEAD_DIM:w + (h + 1) * HEAD_DIM]) * ATT_SCALE
        m = jnp.max(s, axis=1, keepdims=True)
        p = jnp.exp(s - m)
        den = jnp.sum(p, axis=1, keepdims=True)
        ob_ref[0, :, hs] = _bdot(p, bq[:, 2 * w + h * HEAD_DIM:2 * w + (h + 1) * HEAD_DIM]) / den


def _ctx_attn(sink, qkv_a_c, qkv_b_c):
    b = qkv_a_c.shape[0]
    w = 2 * LANES
    return pl.pallas_call(
        _ctx_attn_kernel,
        out_shape=[jax.ShapeDtypeStruct((b, CTX_LEN, w), F32)] * 2,
        grid=(b,),
        in_specs=[
            pl.BlockSpec(memory_space=pltpu.SMEM),
            pl.BlockSpec((1, CTX_LEN, A_COLS), lambda i: (i, 0, 0)),
            pl.BlockSpec((1, CTX_LEN, B_COLS), lambda i: (i, 0, 0)),
        ],
        out_specs=[pl.BlockSpec((1, CTX_LEN, w), lambda i: (i, 0, 0))] * 2,
        compiler_params=_cparams(("parallel",)),
        name="attn_context",
    )(sink, qkv_a_c, qkv_b_c)


CONV_HALO = SUBLANES


def _conv_kernel(prev_ref, cur_ref, next_ref, w_ref, b_ref, o_ref, ext_ref, *, tl, nt):
    j = pl.program_id(1)
    ext_ref[0:CONV_HALO, :] = jnp.where(j > 0, prev_ref[0], 0.0)
    ext_ref[CONV_HALO:CONV_HALO + tl, :] = cur_ref[0]
    ext_ref[CONV_HALO + tl:, :] = jnp.where(j < nt - 1, next_ref[0], 0.0)
    acc = jnp.zeros((tl, XBC_COLS), F32) + b_ref[...]
    base = CONV_HALO - SSM_CONV // 2
    for k in range(SSM_CONV):
        acc = acc + w_ref[k:k + 1, :] * ext_ref[base + k:base + k + tl, :]
    o_ref[0] = _silu(acc)


def _conv_silu(xz, conv_w, conv_b, tl):
    b, t, _ = xz.shape
    nt = t // tl
    hb = tl // CONV_HALO
    last = t // CONV_HALO - 1
    return pl.pallas_call(
        functools.partial(_conv_kernel, tl=tl, nt=nt),
        out_shape=jax.ShapeDtypeStruct((b, t, XBC_COLS), F32),
        grid=(b, nt),
        in_specs=[
            pl.BlockSpec((1, CONV_HALO, XBC_COLS), lambda i, j: (i, jnp.maximum(j * hb - 1, 0), 0)),
            pl.BlockSpec((1, tl, XBC_COLS), lambda i, j: (i, j, 0)),
            pl.BlockSpec((1, CONV_HALO, XBC_COLS), lambda i, j: (i, jnp.minimum((j + 1) * hb, last), 0)),
            pl.BlockSpec((SUBLANES, XBC_COLS), lambda i, j: (0, 0)),
            pl.BlockSpec((1, XBC_COLS), lambda i, j: (0, 0)),
        ],
        out_specs=pl.BlockSpec((1, tl, XBC_COLS), lambda i, j: (i, j, 0)),
        scratch_shapes=[pltpu.VMEM((tl + 2 * CONV_HALO, XBC_COLS), F32)],
        compiler_params=_cparams(("parallel", "parallel")),
        name="ssm_conv",
    )(xz, xz, xz, conv_w, conv_b)


Q = SSM_CHUNK
GS = SSM_GROUPS * SSM_STATE
HPG = SSM_HEADS // SSM_GROUPS
SSD_STATE_SHAPE = (SSM_GROUPS, SSM_STATE, HPG * SSM_HEAD_DIM)
SSD_BATCH_PER_STEP = 2


def _softplus(v):
    return jnp.maximum(v, 0.0) + jnp.log1p(jnp.exp(-jnp.abs(v)))


def _cumsum_mat(tri, a):
    r = jnp.dot(tri, jnp.concatenate(_split3(a), axis=1), preferred_element_type=F32)
    n = a.shape[1]
    return r[:, 0:n] + r[:, n:2 * n] + r[:, 2 * n:]


def _ssd_kernel(xm_ref, dtm_ref, xb_ref, dtb_ref, bias_ref, alog_ref, h0f_ref, h0b_ref,
                y1_ref, y2_ref, hf_out_ref, hb_out_ref, hf_ref, hb_ref, *, nc):
    i = pl.program_id(1)

    @pl.when(i == 0)
    def _():
        hf_ref[...] = h0f_ref[...]
        hb_ref[...] = h0b_ref[...]

    ii = lax.broadcasted_iota(jnp.int32, (Q, Q), 0)
    jj = lax.broadcasted_iota(jnp.int32, (Q, Q), 1)
    lower = ii >= jj
    diag = ii == jj
    tril = jnp.where(lower, 1.0, 0.0).astype(BF16)
    triu = jnp.where(ii <= jj, 1.0, 0.0).astype(BF16)
    first_half = lax.broadcasted_iota(jnp.int32, (Q, LANES), 1) < SSM_HEAD_DIM
    a_row = -jnp.exp(alog_ref[...])
    gw = HPG * SSM_HEAD_DIM

    def group_operands(xbc, g):
        bg = xbc[:, SSM_INNER + g * SSM_STATE:SSM_INNER + (g + 1) * SSM_STATE]
        cg = xbc[:, SSM_INNER + GS + g * SSM_STATE:SSM_INNER + GS + (g + 1) * SSM_STATE]
        return bg, cg.astype(BF16), xbc[:, g * gw:(g + 1) * gw]

    nbs = range(SSD_BATCH_PER_STEP)
    groups = range(SSM_GROUPS)
    bias = bias_ref[...]
    zero = jnp.zeros((), F32)


    xm = [xm_ref[bi] for bi in nbs]
    xb = [xb_ref[bi] for bi in nbs]
    dtm = [_softplus(dtm_ref[bi] + bias) for bi in nbs]
    dtb = [_softplus(dtb_ref[bi] + bias) for bi in nbs]
    cumf = [_cumsum_mat(tril, d * a_row) for d in dtm]
    cumr = [_cumsum_mat(triu, d * a_row) for d in dtm]
    cumb = [_cumsum_mat(triu, d * a_row) for d in dtb]

    opm = [[group_operands(xm[bi], g) for g in groups] for bi in nbs]
    opb = [[group_operands(xb[bi], g) for g in groups] for bi in nbs]
    gmat = [[_bdot_nt(opm[bi][g][1], opm[bi][g][0]) for g in groups] for bi in nbs]
    hf_prev = [[hf_ref[bi, g] for g in groups] for bi in nbs]
    hb_prev = [[hb_ref[bi, g] for g in groups] for bi in nbs]
    inter_f = [[jnp.dot(opm[bi][g][1], hf_prev[bi][g].astype(BF16), preferred_element_type=F32)
                for g in groups] for bi in nbs]
    inter_b = [[jnp.dot(opb[bi][g][1], hb_prev[bi][g].astype(BF16), preferred_element_type=F32)
                for g in groups] for bi in nbs]
    bt_m = [[opm[bi][g][0].T for g in groups] for bi in nbs]
    bt_b = [[opb[bi][g][0].T for g in groups] for bi in nbs]

    cumf_t = [c.T for c in cumf]
    cumr_t = [c.T for c in cumr]
    dtm_t = [d.T for d in dtm]
    cumb_t = [c.T for c in cumb]
    dtb_t = [d.T for d in dtb]

    heads = [(bi, h) for bi in nbs for h in range(SSM_HEADS)]
    hb_ = lambda h: SSM_HEADS + h
    cf = {k: jnp.broadcast_to(cumf[k[0]][:, k[1]:k[1] + 1], (Q, Q)) for k in heads}
    cr = {k: jnp.broadcast_to(cumr[k[0]][:, hb_(k[1]):hb_(k[1]) + 1], (Q, Q)) for k in heads}
    cb = {k: jnp.broadcast_to(cumb[k[0]][:, hb_(k[1]):hb_(k[1]) + 1], (Q, Q)) for k in heads}
    arg = {k: jnp.where(lower, cf[k] - cumf_t[k[0]][k[1]:k[1] + 1, :],
                        cr[k] - cumr_t[k[0]][hb_(k[1]):hb_(k[1]) + 1, :]) for k in heads}
    ex = {k: jnp.exp(arg[k]) for k in heads}
    ein_f = {k: jnp.exp(cf[k]) for k in heads}
    ein_b = {k: jnp.exp(cb[k]) for k in heads}
    wm = {}
    for bi, h in heads:
        dtf_row = dtm_t[bi][h:h + 1, :]
        dtb_row = dtm_t[bi][hb_(h):hb_(h) + 1, :]
        wm[bi, h] = (gmat[bi][h // HPG] * (ex[bi, h] * jnp.where(lower, dtf_row, dtb_row)
                                           + jnp.where(diag, dtb_row, zero))).astype(BF16)
    wst_f, wst_b = {}, {}
    for bi, h in heads:
        rf = cumf_t[bi][h:h + 1, :]
        rb = cumb_t[bi][hb_(h):hb_(h) + 1, :]
        w_f = jnp.exp(rf[:, Q - 1:Q] - rf) * dtm_t[bi][h:h + 1, :]
        w_b = jnp.exp(rb[:, 0:1] - rb) * dtb_t[bi][hb_(h):hb_(h) + 1, :]
        wst_f[bi, h] = (bt_m[bi][h // HPG] * w_f).astype(BF16)
        wst_b[bi, h] = (bt_b[bi][h // HPG] * w_b).astype(BF16)

    def pair_rhs(x, pr):
        xp = x[:, pr * LANES:(pr + 1) * LANES]
        return jnp.concatenate([jnp.where(first_half, xp, zero), jnp.where(first_half, zero, xp)],
                               axis=0).astype(BF16)

    def pair_dot(mats, bi, pr, rhs):
        lhs = jnp.concatenate([mats[bi, 2 * pr], mats[bi, 2 * pr + 1]], axis=1)
        return jnp.dot(lhs, rhs, preferred_element_type=F32)

    pairs = [(bi, pr) for bi in nbs for pr in range(SSM_HEADS // 2)]
    rhs_m = {k: pair_rhs(xm[k[0]], k[1]) for k in pairs}
    rhs_b = {k: pair_rhs(xb[k[0]], k[1]) for k in pairs}
    y_intra = {k: pair_dot(wm, k[0], k[1], rhs_m[k]) for k in pairs}
    st_f = {k: pair_dot(wst_f, k[0], k[1], rhs_m[k]) for k in pairs}
    st_b = {k: pair_dot(wst_b, k[0], k[1], rhs_b[k]) for k in pairs}

    ppg = HPG // 2
    for bi, pr in pairs:
        g, k = pr // ppg, pr % ppg
        ls = slice(k * LANES, (k + 1) * LANES)
        sin_f = jnp.where(first_half, ein_f[bi, 2 * pr], ein_f[bi, 2 * pr + 1])
        sin_b = jnp.where(first_half, ein_b[bi, 2 * pr], ein_b[bi, 2 * pr + 1])
        y1_ref[bi, :, pr * LANES:(pr + 1) * LANES] = y_intra[bi, pr] + sin_f * inter_f[bi][g][:, ls]
        y2_ref[bi, :, pr * LANES:(pr + 1) * LANES] = sin_b * inter_b[bi][g][:, ls]
        hf_ref[bi, g, :, ls] = hf_prev[bi][g][:, ls] * sin_f[Q - 1:Q, :] + st_f[bi, pr]
        hb_ref[bi, g, :, ls] = hb_prev[bi][g][:, ls] * sin_b[0:1, :] + st_b[bi, pr]

    @pl.when(i == nc - 1)
    def _():
        hf_out_ref[...] = hf_ref[...]
        hb_out_ref[...] = hb_ref[...]


def _ssd(xbc, dt_raw, dt_bias, a_log, h0f, h0b):
    b, t, _ = xbc.shape
    nc = t // Q
    nb = SSD_BATCH_PER_STEP
    st_shape = (b,) + SSD_STATE_SHAPE
    st_spec = pl.BlockSpec((nb,) + SSD_STATE_SHAPE, lambda i, j: (i, 0, 0, 0))
    fwd = lambda i, j: (i, j, 0)
    bwd = lambda i, j: (i, nc - 1 - j, 0)
    return pl.pallas_call(
        functools.partial(_ssd_kernel, nc=nc),
        out_shape=[jax.ShapeDtypeStruct((b, t, SSM_INNER), F32)] * 2
        + [jax.ShapeDtypeStruct(st_shape, F32)] * 2,
        grid=(b // nb, nc),
        in_specs=[
            pl.BlockSpec((nb, Q, XBC_COLS), fwd),
            pl.BlockSpec((nb, Q, DT_PAD), fwd),
            pl.BlockSpec((nb, Q, XBC_COLS), bwd),
            pl.BlockSpec((nb, Q, DT_PAD), bwd),
            pl.BlockSpec((1, DT_PAD), lambda i, j: (0, 0)),
            pl.BlockSpec((1, DT_PAD), lambda i, j: (0, 0)),
            st_spec, st_spec,
        ],
        out_specs=[pl.BlockSpec((nb, Q, SSM_INNER), fwd), pl.BlockSpec((nb, Q, SSM_INNER), bwd),
                   st_spec, st_spec],
        scratch_shapes=[pltpu.VMEM((nb,) + SSD_STATE_SHAPE, F32)] * 2,
        compiler_params=_cparams(("parallel", "arbitrary")),
        name="ssd_scan",
    )(xbc, dt_raw, xbc, dt_raw, dt_bias, a_log, h0f, h0b)


FFN_CHUNKS = 2
FFN_TF = D_FF // FFN_CHUNKS


def _mix_ffn_kernel(x_ref, mod_ref, oa_ref, ob_ref, y1_ref, y2_ref, xs_ref, z_ref, d_ref, sg_ref,
                    wo_ref, g_ref, wfi_ref, wfo_ref, gfin_ref, o_ref, *, final):
    y = y1_ref[0] + y2_ref[0] + d_ref[...] * xs_ref[0]
    y = y * _silu(z_ref[0])
    var = jnp.mean(y * y, axis=-1, keepdims=True)
    oc = (y * lax.rsqrt(var + EPS) * sg_ref[...]).astype(BF16)
    wa = 2 * LANES
    mix = (jnp.dot(oa_ref[0].astype(BF16), wo_ref[0, 0:wa, :], preferred_element_type=F32)
           + jnp.dot(ob_ref[0].astype(BF16), wo_ref[0, wa:2 * wa, :], preferred_element_type=F32)
           + jnp.dot(oc, wo_ref[0, 2 * wa:, :], preferred_element_type=F32))
    xn = x_ref[0] + mod_ref[0, 2:3, :] * mix
    h = _norm_mod(xn, g_ref[...], mod_ref[0, 3:4, :], mod_ref[0, 4:5, :]).astype(BF16)
    acc = None
    for f in range(FFN_CHUNKS):
        gate = jnp.dot(h, wfi_ref[0, :, f * FFN_TF:(f + 1) * FFN_TF], preferred_element_type=F32)
        up = jnp.dot(h, wfi_ref[0, :, D_FF + f * FFN_TF:D_FF + (f + 1) * FFN_TF],
                     preferred_element_type=F32)
        act = (_silu(gate) * up).astype(BF16)
        part = jnp.dot(act, wfo_ref[0, f * FFN_TF:(f + 1) * FFN_TF, :], preferred_element_type=F32)
        acc = part if acc is None else acc + part
    out = xn + mod_ref[0, 5:6, :] * acc
    if final:
        var = jnp.mean(out * out, axis=-1, keepdims=True)
        out = out * lax.rsqrt(var + EPS) * gfin_ref[...]
    o_ref[0] = out


def _mix_ffn(x, mods, o_a, o_b, y1, y2, xbc, xz, d_exp, ssm_g, w_out, g_ffn, w_ffn_in, w_ffn_out,
             g_final, layer, tm, final):
    b, t, _ = x.shape
    per_batch = mods.shape[0] > 1
    row = lambda n: pl.BlockSpec((1, tm, n), lambda i, j: (i, j, 0))
    const = lambda shape: pl.BlockSpec(shape, lambda i, j: (0, 0), pipeline_mode=pl.Buffered(1))
    weight = lambda r, c: pl.BlockSpec((1, r, c), lambda i, j: (layer, 0, 0), pipeline_mode=pl.Buffered(1))
    return pl.pallas_call(
        functools.partial(_mix_ffn_kernel, final=final),
        out_shape=jax.ShapeDtypeStruct((b, t, D_MODEL), F32),
        grid=(b, t // tm),
        in_specs=[
            row(D_MODEL),
            pl.BlockSpec((1, SUBLANES, D_MODEL), (lambda i, j: (i, 0, 0)) if per_batch else (lambda i, j: (0, 0, 0))),
            row(2 * LANES), row(2 * LANES), row(SSM_INNER), row(SSM_INNER),
            pl.BlockSpec((1, tm, SSM_INNER), lambda i, j: (i, j, 0)),
            pl.BlockSpec((1, tm, SSM_INNER), lambda i, j: (i, j, 2)),
            const((1, SSM_INNER)), const((1, SSM_INNER)),
            weight(D_MODEL, D_MODEL),
            const((1, D_MODEL)),
            weight(D_MODEL, 2 * D_FF),
            weight(D_FF, D_MODEL),
            const((1, D_MODEL)),
        ],
        out_specs=row(D_MODEL),
        compiler_params=_cparams(("parallel", "parallel")),
        name="mix_ffn_final" if final else "mix_ffn",
    )(x, mods, o_a, o_b, y1, y2, xbc, xz, d_exp, ssm_g, w_out, g_ffn, w_ffn_in, w_ffn_out, g_final)


def _rope_tables():
    t = np.arange(SEQ)
    pos = np.stack([t // GRID_W, t % GRID_W], axis=1).astype(np.float64)
    quarter = HEAD_DIM // 4
    inv_freq = ROPE_BASE ** (-np.arange(quarter, dtype=np.float64) / quarter)
    lane = np.arange(LANES) % HEAD_DIM
    half = lane // (HEAD_DIM // 2)
    idx = lane % (HEAD_DIM // 2)
    ang = pos[:, half] * inv_freq[idx % quarter][None, :]
    cos, sin = np.cos(ang), np.sin(ang)
    first = (idx < quarter)[None, :]
    tabs = (cos, np.where(first, -sin, 0.0), np.where(first, 0.0, sin))
    return tuple(jnp.asarray(v, F32) for v in tabs)


def _pad_lanes(v, n=LANES):
    v = v.reshape(1, -1)
    return jnp.pad(v, ((0, 0), (0, n - v.shape[1])))


def kernel(x, c, ctx, c_ctx, w_mod, b_mod, g_mix, w_in, wa_sink, na_rpb, ssm_conv_w, ssm_conv_b,
           ssm_dt_bias, ssm_a_log, ssm_d, ssm_norm_g, w_out, g_ffn, w_ffn_in, w_ffn_out, g_final):
    cin = jnp.concatenate([c, c_ctx[None, :], jnp.zeros((SUBLANES - BATCH - 1, D_MODEL), F32)], axis=0)
    mod_all = _modulation(cin, w_mod, b_mod)
    rope_tabs = _rope_tables()
    bias_tabs = _na_bias_tables(na_rpb)
    zeros_state = jnp.zeros((BATCH,) + SSD_STATE_SHAPE, F32)
    gfin = g_final.reshape(1, D_MODEL)
    w_proj = w_in.astype(BF16)
    wo = w_out.astype(BF16)
    wfi = w_ffn_in.astype(BF16)
    wfo = w_ffn_out.astype(BF16)

    xl, xc = x, ctx
    for l in range(DEPTH):
        last = l == DEPTH - 1
        m6 = mod_all[l].reshape(SUBLANES, 6, D_MODEL)
        mods_l = jnp.pad(m6[:BATCH], ((0, 0), (0, 2), (0, 0)))
        mods_c = jnp.pad(m6[BATCH:BATCH + 1], ((0, 0), (0, 2), (0, 0)))
        w_dt = jnp.pad(w_in[l][:, W_DT:], ((0, 0), (0, DT_PAD - DT_COLS))).astype(BF16)
        g1 = g_mix[l].reshape(1, D_MODEL)
        g2 = g_ffn[l].reshape(1, D_MODEL)
        conv_w = jnp.pad(ssm_conv_w[l], ((0, SUBLANES - SSM_CONV), (0, 0)))
        conv_b = ssm_conv_b[l].reshape(1, XBC_COLS)
        dt_bias = _pad_lanes(ssm_dt_bias[l])
        a_log = _pad_lanes(ssm_a_log[l])
        d_exp = jnp.repeat(ssm_d[l], SSM_HEAD_DIM).reshape(1, SSM_INNER)
        sg = ssm_norm_g[l].reshape(1, SSM_INNER)

        a_c, b_c, xz_c, dt_c = _inproj(xc, mods_c, g1, w_proj, l, w_dt, None, CTX_LEN)
        a_l, b_l, xz_l, dt_l = _inproj(xl, mods_l, g1, w_proj, l, w_dt, rope_tabs, 512)

        o_a = _attn_a(wa_sink[l], a_l, a_c)
        o_b = _attn_b(b_l, b_c, bias_tabs[l])

        xbc_c = _conv_silu(xz_c, conv_w, conv_b, CTX_LEN)
        xbc_l = _conv_silu(xz_l, conv_w, conv_b, 512)
        y1_c, y2_c, h_f, h_b = _ssd(xbc_c, dt_c, dt_bias, a_log, zeros_state, zeros_state)
        y1_l, y2_l, _, _ = _ssd(xbc_l, dt_l, dt_bias, a_log, h_f, h_b)

        xl = _mix_ffn(xl, mods_l, o_a, o_b, y1_l, y2_l, xbc_l, xz_l, d_exp, sg, wo, g2, wfi, wfo, gfin,
                      l, 512, last)
        if not last:
            o_ac, o_bc = _ctx_attn(wa_sink[l], a_c, b_c)
            xc = _mix_ffn(xc, mods_c, o_ac, o_bc, y1_c, y2_c, xbc_c, xz_c, d_exp, sg, wo, g2, wfi, wfo,
                          gfin, l, CTX_LEN, False)
    return xl
```

```python
import functools
import math

import numpy as np
import jax
import jax.numpy as jnp
from jax import lax
from jax.experimental import pallas as pl
from jax.experimental.pallas import tpu as pltpu

F32 = jnp.float32
BF16 = jnp.bfloat16

D_MODEL = 1024
BATCH = 4
SEQ = 4096
DEPTH = 2
GRID_W = 64
GRID_ROWS = SEQ // GRID_W
CTX_LEN = 256
EPS = 1e-6
HEAD_DIM = 64
ROPE_BASE = 10000.0
WA_HEADS = 4
WA_KV_HEADS = 2
WA_WINDOW = 128
WA_BLOCK = 128
NA_HEADS = 4
NA_KH = 8
NA_KW = 16
SSM_HEADS = 8
SSM_HEAD_DIM = 64
SSM_INNER = SSM_HEADS * SSM_HEAD_DIM
SSM_GROUPS = 2
SSM_STATE = 128
SSM_CONV = 7
SSM_CHUNK = 128
D_FF = 2816
XBC_COLS = SSM_INNER + 2 * SSM_GROUPS * SSM_STATE
DT_COLS = 2 * SSM_HEADS

LANES = 128
SUBLANES = 8
VMEM_LIMIT = 56 * 1024 * 1024

A_COLS = 512
B_COLS = 768
XZ_COLS = 1536
DT_PAD = LANES
PROJ_COLS = A_COLS + B_COLS + XZ_COLS + DT_PAD

ATT_SCALE = HEAD_DIM ** -0.5
NEG_INF = float("-inf")
NT_DIMS = (((1,), (1,)), ((), ()))


def _silu(v):
    return v / (1.0 + jnp.exp(-v))


def _bdot(a, b):
    return jnp.dot(a.astype(BF16), b.astype(BF16), preferred_element_type=F32)


def _bdot_nt(a, b):
    return lax.dot_general(a.astype(BF16), b.astype(BF16), NT_DIMS, preferred_element_type=F32)


def _cparams(sem):
    return pltpu.CompilerParams(dimension_semantics=sem, vmem_limit_bytes=VMEM_LIMIT)


MOD_TN = 1536


def _mod_kernel(c_ref, w_ref, b_ref, o_ref):
    s = _silu(c_ref[...])
    o_ref[0] = _bdot(s, w_ref[0]) + b_ref[0]


def _modulation(cin, w_mod, b_mod):
    n = 6 * D_MODEL
    return pl.pallas_call(
        _mod_kernel,
        out_shape=jax.ShapeDtypeStruct((DEPTH, SUBLANES, n), F32),
        grid=(DEPTH, n // MOD_TN),
        in_specs=[
            pl.BlockSpec((SUBLANES, D_MODEL), lambda l, j: (0, 0)),
            pl.BlockSpec((1, D_MODEL, MOD_TN), lambda l, j: (l, 0, j)),
            pl.BlockSpec((1, 1, MOD_TN), lambda l, j: (l, 0, j)),
        ],
        out_specs=pl.BlockSpec((1, SUBLANES, MOD_TN), lambda l, j: (l, 0, j)),
        compiler_params=_cparams(("parallel", "parallel")),
        name="modulation",
    )(cin, w_mod, b_mod.reshape(DEPTH, 1, n))


def _norm_mod(x, g, shift, scale):
    var = jnp.mean(x * x, axis=-1, keepdims=True)
    h = x * lax.rsqrt(var + EPS) * g
    return h * (1.0 + scale) + shift


W_QA, W_QB, W_Z, W_KVA, W_KVB, W_XBC, W_DT = 0, 256, 512, 1024, 1280, 1792, 2816
IN_COLS = W_DT + DT_COLS


def _inproj_kernel(x_ref, mod_ref, g_ref, w_ref, wdt_ref, *rest, rope):
    if rope:
        cos_ref, sa_ref, sb_ref, oa_ref, ob_ref, oxz_ref, odt_ref = rest
    else:
        oa_ref, ob_ref, oxz_ref, odt_ref = rest
    h = _norm_mod(x_ref[0], g_ref[...], mod_ref[0, 0:1, :], mod_ref[0, 1:2, :]).astype(BF16)

    def proj(lo, hi):
        return jnp.dot(h, w_ref[0, :, lo:hi], preferred_element_type=F32)

    qa = proj(W_QA, W_QB)
    kva = proj(W_KVA, W_KVB)
    if rope:
        cos, sa, sb = cos_ref[...], sa_ref[...], sb_ref[...]

        def rot(v):
            up = pltpu.roll(v, LANES - 16, axis=1)
            dn = pltpu.roll(v, 16, axis=1)
            return v * cos + up * sa + dn * sb

        oa_ref[0, :, 0:LANES] = rot(qa[:, 0:LANES]).astype(BF16)
        oa_ref[0, :, LANES:2 * LANES] = rot(qa[:, LANES:]).astype(BF16)
        oa_ref[0, :, 2 * LANES:3 * LANES] = rot(kva[:, 0:LANES]).astype(BF16)
        oa_ref[0, :, 3 * LANES:] = kva[:, LANES:].astype(BF16)
    else:
        oa_ref[0, :, 0:2 * LANES] = qa.astype(BF16)
        oa_ref[0, :, 2 * LANES:] = kva.astype(BF16)
    ob_ref[0, :, 0:2 * LANES] = proj(W_QB, W_Z).astype(BF16)
    ob_ref[0, :, 2 * LANES:] = proj(W_KVB, W_XBC).astype(BF16)
    oxz_ref[0, :, 0:XBC_COLS] = proj(W_XBC, W_DT)
    oxz_ref[0, :, XBC_COLS:] = proj(W_Z, W_KVA)
    odt_ref[0] = jnp.dot(h, wdt_ref[...], preferred_element_type=F32)


def _inproj(x, mods, g, w, layer, wdt, rope_tabs, tm):
    b, t, _ = x.shape
    rope = rope_tabs is not None
    per_batch = mods.shape[0] > 1
    in_specs = [
        pl.BlockSpec((1, tm, D_MODEL), lambda i, j: (i, j, 0)),
        pl.BlockSpec((1, SUBLANES, D_MODEL), (lambda i, j: (i, 0, 0)) if per_batch else (lambda i, j: (0, 0, 0))),
        pl.BlockSpec((1, D_MODEL), lambda i, j: (0, 0)),
        pl.BlockSpec((1, D_MODEL, IN_COLS), lambda i, j: (layer, 0, 0)),
        pl.BlockSpec((D_MODEL, DT_PAD), lambda i, j: (0, 0)),
    ]
    args = [x, mods, g, w, wdt]
    if rope:
        in_specs += [pl.BlockSpec((tm, LANES), lambda i, j: (j, 0))] * 3
        args += list(rope_tabs)
    widths = (A_COLS, B_COLS, XZ_COLS, DT_PAD)
    return pl.pallas_call(
        functools.partial(_inproj_kernel, rope=rope),
        out_shape=[jax.ShapeDtypeStruct((b, t, n), dt) for n, dt in zip(widths, (BF16, BF16, F32, F32))],
        grid=(b, t // tm),
        in_specs=in_specs,
        out_specs=[pl.BlockSpec((1, tm, n), lambda i, j: (i, j, 0)) for n in widths],
        compiler_params=_cparams(("parallel", "parallel")),
        name="inproj_rope" if rope else "inproj",
    )(*args)


WA_KEYS = 3 * WA_BLOCK


WA_BLOCKS_PER_STEP = 2


def _attn_a_kernel(sink_ref, q_ref, k_ref, v_ref, kc_ref, vc_ref, o_ref, s_ref, p_ref):
    step = pl.program_id(1)
    nk = WA_KEYS + CTX_LEN
    nq = WA_HEADS * WA_BLOCK
    kc, vc = kc_ref[0], vc_ref[0]
    starts = []
    for bb in range(WA_BLOCKS_PER_STEP):
        n = step * WA_BLOCKS_PER_STEP + bb
        starts.append(pl.multiple_of(jnp.clip((n - 1) * WA_BLOCK, 0, SEQ - WA_KEYS), WA_BLOCK))

    first_half = lax.broadcasted_iota(jnp.int32, (WA_BLOCK, LANES), 1) < HEAD_DIM
    swap = lambda v: pltpu.roll(v, HEAD_DIM, axis=1)

    for bb in range(WA_BLOCKS_PER_STEP):
        q = q_ref[0, bb * WA_BLOCK:(bb + 1) * WA_BLOCK, :].astype(F32) * ATT_SCALE
        q01, q23 = q[:, 0:LANES], q[:, LANES:]
        lhs = jnp.concatenate([
            jnp.where(first_half, q01, 0.0), jnp.where(first_half, swap(q01), 0.0),
            jnp.where(first_half, 0.0, swap(q23)), jnp.where(first_half, 0.0, q23)], axis=0)
        kall = jnp.concatenate([k_ref[0, pl.ds(starts[bb], WA_KEYS), :], kc], axis=0)
        s_ref[bb * nq:(bb + 1) * nq, :] = _bdot_nt(lhs, kall)

    nr = WA_BLOCKS_PER_STEP * nq
    rows = lax.broadcasted_iota(jnp.int32, (nr, nk), 0)
    cols = lax.broadcasted_iota(jnp.int32, (nr, nk), 1)
    row1 = lax.broadcasted_iota(jnp.int32, (nr, 1), 0)
    blk = row1 // nq
    qpos = (step * WA_BLOCKS_PER_STEP + blk) * WA_BLOCK + (rows & (WA_BLOCK - 1))
    kstart = starts[-1]
    for bb in range(WA_BLOCKS_PER_STEP - 2, -1, -1):
        kstart = jnp.where(blk == bb, starts[bb], kstart)
    valid = (cols >= WA_KEYS) | (jnp.abs(qpos - (kstart + cols)) <= WA_WINDOW)
    s = jnp.where(valid, s_ref[...], NEG_INF)
    head = (row1 // WA_BLOCK) % WA_HEADS
    sink = jnp.where(head == 0, sink_ref[0],
                     jnp.where(head == 1, sink_ref[1], jnp.where(head == 2, sink_ref[2], sink_ref[3])))
    m = jnp.maximum(jnp.max(s, axis=1, keepdims=True), sink)
    p = jnp.exp(s - m)
    inv = 1.0 / (jnp.sum(p, axis=1, keepdims=True) + jnp.exp(sink - m))
    p_ref[...] = p.astype(BF16)

    for bb in range(WA_BLOCKS_PER_STEP):
        vall = jnp.concatenate([v_ref[0, pl.ds(starts[bb], WA_KEYS), :], vc], axis=0)
        rs = slice(bb * nq, (bb + 1) * nq)
        o = jnp.dot(p_ref[rs, :], vall, preferred_element_type=F32) * inv[rs]
        o0, o1, o2, o3 = (o[h * WA_BLOCK:(h + 1) * WA_BLOCK] for h in range(WA_HEADS))
        qs = slice(bb * WA_BLOCK, (bb + 1) * WA_BLOCK)
        o_ref[0, qs, 0:LANES] = jnp.where(first_half, o0, swap(o1)).astype(o_ref.dtype)
        o_ref[0, qs, LANES:] = jnp.where(first_half, swap(o2), o3).astype(o_ref.dtype)


def _attn_a(sink, qkv, qkv_c):
    b = qkv.shape[0]
    nk = WA_KEYS + CTX_LEN
    tq = WA_BLOCKS_PER_STEP * WA_BLOCK
    nr = WA_BLOCKS_PER_STEP * WA_HEADS * WA_BLOCK
    return pl.pallas_call(
        _attn_a_kernel,
        out_shape=jax.ShapeDtypeStruct((b, SEQ, WA_HEADS * HEAD_DIM), BF16),
        scratch_shapes=[pltpu.VMEM((nr, nk), F32), pltpu.VMEM((nr, nk), BF16)],
        grid=(b, SEQ // tq),
        in_specs=[
            pl.BlockSpec(memory_space=pltpu.SMEM),
            pl.BlockSpec((1, tq, 2 * LANES), lambda i, j: (i, j, 0)),
            pl.BlockSpec((1, SEQ, LANES), lambda i, j: (i, 0, 2)),
            pl.BlockSpec((1, SEQ, LANES), lambda i, j: (i, 0, 3)),
            pl.BlockSpec((1, CTX_LEN, LANES), lambda i, j: (i, 0, 2)),
            pl.BlockSpec((1, CTX_LEN, LANES), lambda i, j: (i, 0, 3)),
        ],
        out_specs=pl.BlockSpec((1, tq, 2 * LANES), lambda i, j: (i, j, 0)),
        compiler_params=_cparams(("parallel", "arbitrary")),
        name="attn_window",
    )(sink, qkv, qkv, qkv, qkv_c, qkv_c)


NB_ROWS_PER_STEP = 4
NB_LOC = NA_KH * GRID_W
NB_DY_PAIRS = 2 * NA_KH - 2


def _attn_b_kernel(q_ref, k_ref, v_ref, kc_ref, vc_ref, t_ref, o_ref, s_ref, p_ref):
    i = pl.program_id(1)
    kc = kc_ref[0]
    vc = vc_ref[0]
    units = [(rr, pp) for rr in range(NB_ROWS_PER_STEP) for pp in range(NA_HEADS // 2)]
    starts, shifts = [], []
    for rr in range(NB_ROWS_PER_STEP):
        r = i * NB_ROWS_PER_STEP + rr
        rs = jnp.clip(r - NA_KH // 2, 0, GRID_ROWS - NA_KH)
        shifts.append(r - rs)
        starts.append(pl.multiple_of(rs * GRID_W, GRID_W))
    first_half = lax.broadcasted_iota(jnp.int32, (GRID_W, LANES), 1) < HEAD_DIM
    zero = jnp.zeros((), BF16)
    pw = 2 * GRID_W

    for u, (rr, pp) in enumerate(units):
        ls = slice(pp * LANES, (pp + 1) * LANES)
        if pp == 0:
            kw = k_ref[0, pl.ds(starts[rr], NB_LOC), :]
            q = q_ref[0, rr * GRID_W:(rr + 1) * GRID_W, :] * ATT_SCALE
        qs = q[:, ls]
        lhs = jnp.concatenate([jnp.where(first_half, qs, zero), jnp.where(first_half, zero, qs)], axis=0)
        bias = jnp.concatenate(
            [jnp.concatenate([t_ref[h, 2 * k - shifts[rr] + NA_KH - 1] for k in range(NA_KH // 2)], axis=1)
             for h in (2 * pp, 2 * pp + 1)], axis=0)
        s_ref[u * pw:(u + 1) * pw, 0:NB_LOC] = _bdot_nt(lhs, kw[:, ls]) + bias
        s_ref[u * pw:(u + 1) * pw, NB_LOC:] = _bdot_nt(lhs, kc[:, ls])

    s = s_ref[...]
    p = jnp.exp(s - jnp.max(s, axis=1, keepdims=True))
    inv = 1.0 / jnp.sum(p, axis=1, keepdims=True)
    p_ref[...] = p.astype(BF16)

    for u, (rr, pp) in enumerate(units):
        ls = slice(pp * LANES, (pp + 1) * LANES)
        if pp == 0:
            vw = v_ref[0, pl.ds(starts[rr], NB_LOC), :]
        rows = slice(u * pw, (u + 1) * pw)
        o = (jnp.dot(p_ref[rows, 0:NB_LOC], vw[:, ls], preferred_element_type=F32)
             + jnp.dot(p_ref[rows, NB_LOC:], vc[:, ls], preferred_element_type=F32)) * inv[rows]
        o_ref[0, rr * GRID_W:(rr + 1) * GRID_W, ls] = jnp.where(
            first_half, o[:GRID_W], o[GRID_W:]).astype(o_ref.dtype)


def _attn_b(qkv, qkv_c, table):
    b = qkv.shape[0]
    tq = NB_ROWS_PER_STEP * GRID_W
    w = NA_HEADS * HEAD_DIM
    nu = NB_ROWS_PER_STEP * NA_HEADS * GRID_W
    return pl.pallas_call(
        _attn_b_kernel,
        out_shape=jax.ShapeDtypeStruct((b, SEQ, w), BF16),
        scratch_shapes=[pltpu.VMEM((nu, NB_LOC + CTX_LEN), F32), pltpu.VMEM((nu, NB_LOC + CTX_LEN), BF16)],
        grid=(b, SEQ // tq),
        in_specs=[
            pl.BlockSpec((1, tq, w), lambda i, j: (i, j, 0)),
            pl.BlockSpec((1, SEQ, w), lambda i, j: (i, 0, 1)),
            pl.BlockSpec((1, SEQ, w), lambda i, j: (i, 0, 2)),
            pl.BlockSpec((1, CTX_LEN, w), lambda i, j: (i, 0, 1)),
            pl.BlockSpec((1, CTX_LEN, w), lambda i, j: (i, 0, 2)),
            pl.BlockSpec((NA_HEADS, NB_DY_PAIRS, GRID_W, LANES), lambda i, j: (0, 0, 0, 0)),
        ],
        out_specs=pl.BlockSpec((1, tq, w), lambda i, j: (i, j, 0)),
        compiler_params=_cparams(("parallel", "arbitrary")),
        name="attn_neighbourhood",
    )(qkv, qkv, qkv, qkv_c, qkv_c, table)


def _split3(a):
    a1 = a.astype(BF16)
    r1 = a - a1.astype(F32)
    a2 = r1.astype(BF16)
    a3 = (r1 - a2.astype(F32)).astype(BF16)
    return a1, a2, a3


def _bias_kernel(r_ref, oh_ref, o_ref):
    oh = oh_ref[...]
    o_ref[...] = sum(jnp.dot(t, oh, preferred_element_type=F32) for t in _split3(r_ref[...]))


def _na_bias_tables(rpb):
    ndy, ndx = 2 * NA_KH - 1, 2 * NA_KW - 1
    qc = np.arange(GRID_W)[:, None]
    x = np.arange(GRID_W)[None, :]
    dx = np.clip(x - qc, -(NA_KW - 1), NA_KW - 1) + NA_KW - 1
    onehot = (np.arange(LANES)[:, None, None] == dx[None]).reshape(LANES, GRID_W * GRID_W)
    cstart = np.clip(qc - NA_KW // 2, 0, GRID_W - NA_KW)
    inside = (x >= cstart) & (x < cstart + NA_KW)
    rows = DEPTH * NA_HEADS * ndy
    r = jnp.pad(rpb.astype(F32).reshape(rows, ndx), ((0, LANES - rows), (0, LANES - ndx)))
    m = pl.pallas_call(
        _bias_kernel,
        out_shape=jax.ShapeDtypeStruct((LANES, GRID_W * GRID_W), F32),
        name="na_bias_expand",
    )(r, jnp.asarray(onehot, BF16))
    m = m[:rows].reshape(DEPTH, NA_HEADS, ndy, GRID_W, GRID_W)
    m = jnp.where(jnp.asarray(inside), m, NEG_INF)
    return jnp.concatenate([m[:, :, :ndy - 1], m[:, :, 1:]], axis=-1)


def _ctx_attn_kernel(sink_ref, a_ref, b_ref, oa_ref, ob_ref):
    a = a_ref[0]
    row1 = lax.broadcasted_iota(jnp.int32, (2 * CTX_LEN, 1), 0)
    for g in range(WA_KV_HEADS):
        h0, h1 = 2 * g, 2 * g + 1
        q2 = jnp.concatenate([a[:, h0 * HEAD_DIM:(h0 + 1) * HEAD_DIM],
                              a[:, h1 * HEAD_DIM:(h1 + 1) * HEAD_DIM]], axis=0)
        k = a[:, 2 * LANES + g * HEAD_DIM:2 * LANES + (g + 1) * HEAD_DIM]
        v = a[:, 3 * LANES + g * HEAD_DIM:3 * LANES + (g + 1) * HEAD_DIM]
        s = _bdot_nt(q2, k) * ATT_SCALE
        sink = jnp.where(row1 < CTX_LEN, sink_ref[h0], sink_ref[h1])
        m = jnp.maximum(jnp.max(s, axis=1, keepdims=True), sink)
        p = jnp.exp(s - m)
        den = jnp.sum(p, axis=1, keepdims=True) + jnp.exp(sink - m)
        o = _bdot(p, v) / den
        oa_ref[0, :, h0 * HEAD_DIM:(h0 + 1) * HEAD_DIM] = o[:CTX_LEN]
        oa_ref[0, :, h1 * HEAD_DIM:(h1 + 1) * HEAD_DIM] = o[CTX_LEN:]
    bq = b_ref[0]
    w = NA_HEADS * HEAD_DIM
    for h in range(NA_HEADS):
        hs = slice(h * HEAD_DIM, (h + 1) * HEAD_DIM)
        s = _bdot_nt(bq[:, hs], bq[:, w + h * HEAD_DIM:w + (h + 1) * HEAD_DIM]) * ATT_SCALE
        m = jnp.max(s, axis=1, keepdims=True)
        p = jnp.exp(s - m)
        den = jnp.sum(p, axis=1, keepdims=True)
        ob_ref[0, :, hs] = _bdot(p, bq[:, 2 * w + h * HEAD_DIM:2 * w + (h + 1) * HEAD_DIM]) / den


def _ctx_attn(sink, qkv_a_c, qkv_b_c):
    b = qkv_a_c.shape[0]
    w = 2 * LANES
    return pl.pallas_call(
        _ctx_attn_kernel,
        out_shape=[jax.ShapeDtypeStruct((b, CTX_LEN, w), F32)] * 2,
        grid=(b,),
        in_specs=[
            pl.BlockSpec(memory_space=pltpu.SMEM),
            pl.BlockSpec((1, CTX_LEN, A_COLS), lambda i: (i, 0, 0)),
            pl.BlockSpec((1, CTX_LEN, B_COLS), lambda i: (i, 0, 0)),
        ],
        out_specs=[pl.BlockSpec((1, CTX_LEN, w), lambda i: (i, 0, 0))] * 2,
        compiler_params=_cparams(("parallel",)),
        name="attn_context",
    )(sink, qkv_a_c, qkv_b_c)


CONV_HALO = SUBLANES


def _conv_kernel(prev_ref, cur_ref, next_ref, w_ref, b_ref, o_ref, ext_ref, *, tl, nt):
    j = pl.program_id(1)
    ext_ref[0:CONV_HALO, :] = jnp.where(j > 0, prev_ref[0], 0.0)
    ext_ref[CONV_HALO:CONV_HALO + tl, :] = cur_ref[0]
    ext_ref[CONV_HALO + tl:, :] = jnp.where(j < nt - 1, next_ref[0], 0.0)
    acc = jnp.zeros((tl, XBC_COLS), F32) + b_ref[...]
    base = CONV_HALO - SSM_CONV // 2
    for k in range(SSM_CONV):
        acc = acc + w_ref[k:k + 1, :] * ext_ref[base + k:base + k + tl, :]
    o_ref[0] = _silu(acc)


def _conv_silu(xz, conv_w, conv_b, tl):
    b, t, _ = xz.shape
    nt = t // tl
    hb = tl // CONV_HALO
    last = t // CONV_HALO - 1
    return pl.pallas_call(
        functools.partial(_conv_kernel, tl=tl, nt=nt),
        out_shape=jax.ShapeDtypeStruct((b, t, XBC_COLS), F32),
        grid=(b, nt),
        in_specs=[
            pl.BlockSpec((1, CONV_HALO, XBC_COLS), lambda i, j: (i, jnp.maximum(j * hb - 1, 0), 0)),
            pl.BlockSpec((1, tl, XBC_COLS), lambda i, j: (i, j, 0)),
            pl.BlockSpec((1, CONV_HALO, XBC_COLS), lambda i, j: (i, jnp.minimum((j + 1) * hb, last), 0)),
            pl.BlockSpec((SUBLANES, XBC_COLS), lambda i, j: (0, 0)),
            pl.BlockSpec((1, XBC_COLS), lambda i, j: (0, 0)),
        ],
        out_specs=pl.BlockSpec((1, tl, XBC_COLS), lambda i, j: (i, j, 0)),
        scratch_shapes=[pltpu.VMEM((tl + 2 * CONV_HALO, XBC_COLS), F32)],
        compiler_params=_cparams(("parallel", "parallel")),
        name="ssm_conv",
    )(xz, xz, xz, conv_w, conv_b)


Q = SSM_CHUNK
GS = SSM_GROUPS * SSM_STATE
HPG = SSM_HEADS // SSM_GROUPS
SSD_STATE_SHAPE = (SSM_GROUPS, SSM_STATE, HPG * SSM_HEAD_DIM)
SSD_BATCH_PER_STEP = 2


def _softplus(v):
    return jnp.maximum(v, 0.0) + jnp.log1p(jnp.exp(-jnp.abs(v)))


def _cumsum_mat(tri, a):
    r = jnp.dot(tri, jnp.concatenate(_split3(a), axis=1), preferred_element_type=F32)
    n = a.shape[1]
    return r[:, 0:n] + r[:, n:2 * n] + r[:, 2 * n:]


def _ssd_kernel(xm_ref, dtm_ref, xb_ref, dtb_ref, bias_ref, alog_ref, h0f_ref, h0b_ref,
                y1_ref, y2_ref, hf_out_ref, hb_out_ref, hf_ref, hb_ref, *, nc):
    i = pl.program_id(1)

    @pl.when(i == 0)
    def _():
        hf_ref[...] = h0f_ref[...]
        hb_ref[...] = h0b_ref[...]

    ii = lax.broadcasted_iota(jnp.int32, (Q, Q), 0)
    jj = lax.broadcasted_iota(jnp.int32, (Q, Q), 1)
    lower = ii >= jj
    diag = ii == jj
    tril = jnp.where(lower, 1.0, 0.0).astype(BF16)
    triu = jnp.where(ii <= jj, 1.0, 0.0).astype(BF16)
    first_half = lax.broadcasted_iota(jnp.int32, (Q, LANES), 1) < SSM_HEAD_DIM
    a_row = -jnp.exp(alog_ref[...])
    gw = HPG * SSM_HEAD_DIM

    def group_operands(xbc, g):
        bg = xbc[:, SSM_INNER + g * SSM_STATE:SSM_INNER + (g + 1) * SSM_STATE]
        cg = xbc[:, SSM_INNER + GS + g * SSM_STATE:SSM_INNER + GS + (g + 1) * SSM_STATE]
        return bg, cg.astype(BF16), xbc[:, g * gw:(g + 1) * gw]

    nbs = range(SSD_BATCH_PER_STEP)
    groups = range(SSM_GROUPS)
    bias = bias_ref[...]
    zero = jnp.zeros((), F32)


    xm = [xm_ref[bi] for bi in nbs]
    xb = [xb_ref[bi] for bi in nbs]
    dtm = [_softplus(dtm_ref[bi] + bias) for bi in nbs]
    dtb = [_softplus(dtb_ref[bi] + bias) for bi in nbs]
    cumf = [_cumsum_mat(tril, d * a_row) for d in dtm]
    cumr = [_cumsum_mat(triu, d * a_row) for d in dtm]
    cumb = [_cumsum_mat(triu, d * a_row) for d in dtb]

    opm = [[group_operands(xm[bi], g) for g in groups] for bi in nbs]
    opb = [[group_operands(xb[bi], g) for g in groups] for bi in nbs]
    gmat = [[_bdot_nt(opm[bi][g][1], opm[bi][g][0]) for g in groups] for bi in nbs]
    hf_prev = [[hf_ref[bi, g] for g in groups] for bi in nbs]
    hb_prev = [[hb_ref[bi, g] for g in groups] for bi in nbs]
    inter_f = [[jnp.dot(opm[bi][g][1], hf_prev[bi][g].astype(BF16), preferred_element_type=F32)
                for g in groups] for bi in nbs]
    inter_b = [[jnp.dot(opb[bi][g][1], hb_prev[bi][g].astype(BF16), preferred_element_type=F32)
                for g in groups] for bi in nbs]
    bt_m = [[opm[bi][g][0].T for g in groups] for bi in nbs]
    bt_b = [[opb[bi][g][0].T for g in groups] for bi in nbs]

    cumf_t = [c.T for c in cumf]
    cumr_t = [c.T for c in cumr]
    dtm_t = [d.T for d in dtm]
    cumb_t = [c.T for c in cumb]
    dtb_t = [d.T for d in dtb]

    heads = [(bi, h) for bi in nbs for h in range(SSM_HEADS)]
    hb_ = lambda h: SSM_HEADS + h
    cf = {k: jnp.broadcast_to(cumf[k[0]][:, k[1]:k[1] + 1], (Q, Q)) for k in heads}
    cr = {k: jnp.broadcast_to(cumr[k[0]][:, hb_(k[1]):hb_(k[1]) + 1], (Q, Q)) for k in heads}
    cb = {k: jnp.broadcast_to(cumb[k[0]][:, hb_(k[1]):hb_(k[1]) + 1], (Q, Q)) for k in heads}
    arg = {k: jnp.where(lower, cf[k] - cumf_t[k[0]][k[1]:k[1] + 1, :],
                        cr[k] - cumr_t[k[0]][hb_(k[1]):hb_(k[1]) + 1, :]) for k in heads}
    ex = {k: jnp.exp(arg[k]) for k in heads}
    ein_f = {k: jnp.exp(cf[k]) for k in heads}
    ein_b = {k: jnp.exp(cb[k]) for k in heads}
    wm = {}
    for bi, h in heads:
        dtf_row = dtm_t[bi][h:h + 1, :]
        dtb_row = dtm_t[bi][hb_(h):hb_(h) + 1, :]
        wm[bi, h] = (gmat[bi][h // HPG] * (ex[bi, h] * jnp.where(lower, dtf_row, dtb_row)
                                           + jnp.where(diag, dtb_row, zero))).astype(BF16)
    wst_f, wst_b = {}, {}
    for bi, h in heads:
        rf = cumf_t[bi][h:h + 1, :]
        rb = cumb_t[bi][hb_(h):hb_(h) + 1, :]
        w_f = jnp.exp(rf[:, Q - 1:Q] - rf) * dtm_t[bi][h:h + 1, :]
        w_b = jnp.exp(rb[:, 0:1] - rb) * dtb_t[bi][hb_(h):hb_(h) + 1, :]
        wst_f[bi, h] = (bt_m[bi][h // HPG] * w_f).astype(BF16)
        wst_b[bi, h] = (bt_b[bi][h // HPG] * w_b).astype(BF16)

    def pair_rhs(x, pr):
        xp = x[:, pr * LANES:(pr + 1) * LANES]
        return jnp.concatenate([jnp.where(first_half, xp, zero), jnp.where(first_half, zero, xp)],
                               axis=0).astype(BF16)

    def pair_dot(mats, bi, pr, rhs):
        lhs = jnp.concatenate([mats[bi, 2 * pr], mats[bi, 2 * pr + 1]], axis=1)
        return jnp.dot(lhs, rhs, preferred_element_type=F32)

    pairs = [(bi, pr) for bi in nbs for pr in range(SSM_HEADS // 2)]
    rhs_m = {k: pair_rhs(xm[k[0]], k[1]) for k in pairs}
    rhs_b = {k: pair_rhs(xb[k[0]], k[1]) for k in pairs}
    y_intra = {k: pair_dot(wm, k[0], k[1], rhs_m[k]) for k in pairs}
    st_f = {k: pair_dot(wst_f, k[0], k[1], rhs_m[k]) for k in pairs}
    st_b = {k: pair_dot(wst_b, k[0], k[1], rhs_b[k]) for k in pairs}

    ppg = HPG // 2
    for bi, pr in pairs:
        g, k = pr // ppg, pr % ppg
        ls = slice(k * LANES, (k + 1) * LANES)
        sin_f = jnp.where(first_half, ein_f[bi, 2 * pr], ein_f[bi, 2 * pr + 1])
        sin_b = jnp.where(first_half, ein_b[bi, 2 * pr], ein_b[bi, 2 * pr + 1])
        y1_ref[bi, :, pr * LANES:(pr + 1) * LANES] = y_intra[bi, pr] + sin_f * inter_f[bi][g][:, ls]
        y2_ref[bi, :, pr * LANES:(pr + 1) * LANES] = sin_b * inter_b[bi][g][:, ls]
        hf_ref[bi, g, :, ls] = hf_prev[bi][g][:, ls] * sin_f[Q - 1:Q, :] + st_f[bi, pr]
        hb_ref[bi, g, :, ls] = hb_prev[bi][g][:, ls] * sin_b[0:1, :] + st_b[bi, pr]

    @pl.when(i == nc - 1)
    def _():
        hf_out_ref[...] = hf_ref[...]
        hb_out_ref[...] = hb_ref[...]


def _ssd(xbc, dt_raw, dt_bias, a_log, h0f, h0b):
    b, t, _ = xbc.shape
    nc = t // Q
    nb = SSD_BATCH_PER_STEP
    st_shape = (b,) + SSD_STATE_SHAPE
    st_spec = pl.BlockSpec((nb,) + SSD_STATE_SHAPE, lambda i, j: (i, 0, 0, 0))
    fwd = lambda i, j: (i, j, 0)
    bwd = lambda i, j: (i, nc - 1 - j, 0)
    return pl.pallas_call(
        functools.partial(_ssd_kernel, nc=nc),
        out_shape=[jax.ShapeDtypeStruct((b, t, SSM_INNER), F32)] * 2
        + [jax.ShapeDtypeStruct(st_shape, F32)] * 2,
        grid=(b // nb, nc),
        in_specs=[
            pl.BlockSpec((nb, Q, XBC_COLS), fwd),
            pl.BlockSpec((nb, Q, DT_PAD), fwd),
            pl.BlockSpec((nb, Q, XBC_COLS), bwd),
            pl.BlockSpec((nb, Q, DT_PAD), bwd),
            pl.BlockSpec((1, DT_PAD), lambda i, j: (0, 0)),
            pl.BlockSpec((1, DT_PAD), lambda i, j: (0, 0)),
            st_spec, st_spec,
        ],
        out_specs=[pl.BlockSpec((nb, Q, SSM_INNER), fwd), pl.BlockSpec((nb, Q, SSM_INNER), bwd),
                   st_spec, st_spec],
        scratch_shapes=[pltpu.VMEM((nb,) + SSD_STATE_SHAPE, F32)] * 2,
        compiler_params=_cparams(("parallel", "arbitrary")),
        name="ssd_scan",
    )(xbc, dt_raw, xbc, dt_raw, dt_bias, a_log, h0f, h0b)


MXU_TILE = 256
FFN_SPLITS = (0, 6 * MXU_TILE, D_FF)


def _mix_ffn_kernel(x_ref, mod_ref, oa_ref, ob_ref, y1_ref, y2_ref, xs_ref, z_ref, d_ref, sg_ref,
                    wo_ref, g_ref, wfi_ref, wfo_ref, gfin_ref, o_ref, *, final):
    y = y1_ref[0] + y2_ref[0] + d_ref[...] * xs_ref[0]
    y = y * _silu(z_ref[0])
    var = jnp.mean(y * y, axis=-1, keepdims=True)
    oc = (y * lax.rsqrt(var + EPS) * sg_ref[...]).astype(BF16)
    wa = 2 * LANES
    mix = (jnp.dot(oa_ref[0].astype(BF16), wo_ref[0, 0:wa, :], preferred_element_type=F32)
           + jnp.dot(ob_ref[0].astype(BF16), wo_ref[0, wa:2 * wa, :], preferred_element_type=F32)
           + jnp.dot(oc, wo_ref[0, 2 * wa:, :], preferred_element_type=F32))
    xn = x_ref[0] + mod_ref[0, 2:3, :] * mix
    h = _norm_mod(xn, g_ref[...], mod_ref[0, 3:4, :], mod_ref[0, 4:5, :]).astype(BF16)
    acc = None
    for lo, hi in zip(FFN_SPLITS[:-1], FFN_SPLITS[1:]):
        gate = jnp.dot(h, wfi_ref[0, :, lo:hi], preferred_element_type=F32)
        up = jnp.dot(h, wfi_ref[0, :, D_FF + lo:D_FF + hi], preferred_element_type=F32)
        act = (_silu(gate) * up).astype(BF16)
        part = jnp.dot(act, wfo_ref[0, lo:hi, :], preferred_element_type=F32)
        acc = part if acc is None else acc + part
    out = xn + mod_ref[0, 5:6, :] * acc
    if final:
        var = jnp.mean(out * out, axis=-1, keepdims=True)
        out = out * lax.rsqrt(var + EPS) * gfin_ref[...]
    o_ref[0] = out


def _mix_ffn(x, mods, o_a, o_b, y1, y2, xbc, xz, d_exp, ssm_g, w_out, g_ffn, w_ffn_in, w_ffn_out,
             g_final, layer, tm, final):
    b, t, _ = x.shape
    per_batch = mods.shape[0] > 1
    row = lambda n: pl.BlockSpec((1, tm, n), lambda i, j: (i, j, 0))
    const = lambda shape: pl.BlockSpec(shape, lambda i, j: (0, 0), pipeline_mode=pl.Buffered(1))
    weight = lambda r, c: pl.BlockSpec((1, r, c), lambda i, j: (layer, 0, 0), pipeline_mode=pl.Buffered(1))
    return pl.pallas_call(
        functools.partial(_mix_ffn_kernel, final=final),
        out_shape=jax.ShapeDtypeStruct((b, t, D_MODEL), F32),
        grid=(b, t // tm),
        in_specs=[
            row(D_MODEL),
            pl.BlockSpec((1, SUBLANES, D_MODEL), (lambda i, j: (i, 0, 0)) if per_batch else (lambda i, j: (0, 0, 0))),
            row(2 * LANES), row(2 * LANES), row(SSM_INNER), row(SSM_INNER),
            pl.BlockSpec((1, tm, SSM_INNER), lambda i, j: (i, j, 0)),
            pl.BlockSpec((1, tm, SSM_INNER), lambda i, j: (i, j, 2)),
            const((1, SSM_INNER)), const((1, SSM_INNER)),
            weight(D_MODEL, D_MODEL),
            const((1, D_MODEL)),
            weight(D_MODEL, 2 * D_FF),
            weight(D_FF, D_MODEL),
            const((1, D_MODEL)),
        ],
        out_specs=row(D_MODEL),
        compiler_params=_cparams(("parallel", "parallel")),
        name="mix_ffn_final" if final else "mix_ffn",
    )(x, mods, o_a, o_b, y1, y2, xbc, xz, d_exp, ssm_g, w_out, g_ffn, w_ffn_in, w_ffn_out, g_final)


def _rope_tables():
    t = np.arange(SEQ)
    pos = np.stack([t // GRID_W, t % GRID_W], axis=1).astype(np.float64)
    quarter = HEAD_DIM // 4
    inv_freq = ROPE_BASE ** (-np.arange(quarter, dtype=np.float64) / quarter)
    lane = np.arange(LANES) % HEAD_DIM
    half = lane // (HEAD_DIM // 2)
    idx = lane % (HEAD_DIM // 2)
    ang = pos[:, half] * inv_freq[idx % quarter][None, :]
    cos, sin = np.cos(ang), np.sin(ang)
    first = (idx < quarter)[None, :]
    tabs = (cos, np.where(first, -sin, 0.0), np.where(first, 0.0, sin))
    return tuple(jnp.asarray(v, F32) for v in tabs)


def _pad_lanes(v, n=LANES):
    v = v.reshape(1, -1)
    return jnp.pad(v, ((0, 0), (0, n - v.shape[1])))


def kernel(x, c, ctx, c_ctx, w_mod, b_mod, g_mix, w_in, wa_sink, na_rpb, ssm_conv_w, ssm_conv_b,
           ssm_dt_bias, ssm_a_log, ssm_d, ssm_norm_g, w_out, g_ffn, w_ffn_in, w_ffn_out, g_final):
    cin = jnp.concatenate([c, c_ctx[None, :], jnp.zeros((SUBLANES - BATCH - 1, D_MODEL), F32)], axis=0)
    mod_all = _modulation(cin, w_mod, b_mod)
    rope_tabs = _rope_tables()
    bias_tabs = _na_bias_tables(na_rpb)
    zeros_state = jnp.zeros((BATCH,) + SSD_STATE_SHAPE, F32)
    gfin = g_final.reshape(1, D_MODEL)
    w_proj = w_in.astype(BF16)
    wo = w_out.astype(BF16)
    wfi = w_ffn_in.astype(BF16)
    wfo = w_ffn_out.astype(BF16)

    xl, xc = x, ctx
    for l in range(DEPTH):
        last = l == DEPTH - 1
        m6 = mod_all[l].reshape(SUBLANES, 6, D_MODEL)
        mods_l = jnp.pad(m6[:BATCH], ((0, 0), (0, 2), (0, 0)))
        mods_c = jnp.pad(m6[BATCH:BATCH + 1], ((0, 0), (0, 2), (0, 0)))
        w_dt = jnp.pad(w_in[l][:, W_DT:], ((0, 0), (0, DT_PAD - DT_COLS))).astype(BF16)
        g1 = g_mix[l].reshape(1, D_MODEL)
        g2 = g_ffn[l].reshape(1, D_MODEL)
        conv_w = jnp.pad(ssm_conv_w[l], ((0, SUBLANES - SSM_CONV), (0, 0)))
        conv_b = ssm_conv_b[l].reshape(1, XBC_COLS)
        dt_bias = _pad_lanes(ssm_dt_bias[l])
        a_log = _pad_lanes(ssm_a_log[l])
        d_exp = jnp.repeat(ssm_d[l], SSM_HEAD_DIM).reshape(1, SSM_INNER)
        sg = ssm_norm_g[l].reshape(1, SSM_INNER)

        a_c, b_c, xz_c, dt_c = _inproj(xc, mods_c, g1, w_proj, l, w_dt, None, CTX_LEN)
        a_l, b_l, xz_l, dt_l = _inproj(xl, mods_l, g1, w_proj, l, w_dt, rope_tabs, 512)

        o_a = _attn_a(wa_sink[l], a_l, a_c)
        o_b = _attn_b(b_l, b_c, bias_tabs[l])

        xbc_c = _conv_silu(xz_c, conv_w, conv_b, CTX_LEN)
        xbc_l = _conv_silu(xz_l, conv_w, conv_b, 512)
        y1_c, y2_c, h_f, h_b = _ssd(xbc_c, dt_c, dt_bias, a_log, zeros_state, zeros_state)
        y1_l, y2_l, _, _ = _ssd(xbc_l, dt_l, dt_bias, a_log, h_f, h_b)

        xl = _mix_ffn(xl, mods_l, o_a, o_b, y1_l, y2_l, xbc_l, xz_l, d_exp, sg, wo, g2, wfi, wfo, gfin,
                      l, 512, last)
        if not last:
            o_ac, o_bc = _ctx_attn(wa_sink[l], a_c, b_c)
            xc = _mix_ffn(xc, mods_c, o_ac, o_bc, y1_c, y2_c, xbc_c, xz_c, d_exp, sg, wo, g2, wfi, wfo,
                          gfin, l, CTX_LEN, False)
    return xl
```

```python
import functools
import math

import numpy as np
import jax
import jax.numpy as jnp
from jax import lax
from jax.experimental import pallas as pl
from jax.experimental.pallas import tpu as pltpu

F32 = jnp.float32
BF16 = jnp.bfloat16

D_MODEL = 1024
BATCH = 4
SEQ = 4096
DEPTH = 2
GRID_W = 64
GRID_ROWS = SEQ // GRID_W
CTX_LEN = 256
EPS = 1e-6
HEAD_DIM = 64
ROPE_BASE = 10000.0
WA_HEADS = 4
WA_KV_HEADS = 2
WA_WINDOW = 128
WA_BLOCK = 128
NA_HEADS = 4
NA_KH = 8
NA_KW = 16
SSM_HEADS = 8
SSM_HEAD_DIM = 64
SSM_INNER = SSM_HEADS * SSM_HEAD_DIM
SSM_GROUPS = 2
SSM_STATE = 128
SSM_CONV = 7
SSM_CHUNK = 128
D_FF = 2816
XBC_COLS = SSM_INNER + 2 * SSM_GROUPS * SSM_STATE
DT_COLS = 2 * SSM_HEADS

LANES = 128
SUBLANES = 8
VMEM_LIMIT = 56 * 1024 * 1024

A_COLS = 512
B_COLS = 768
XZ_COLS = 1536
DT_PAD = LANES
PROJ_COLS = A_COLS + B_COLS + XZ_COLS + DT_PAD

ATT_SCALE = HEAD_DIM ** -0.5
NEG_INF = float("-inf")
NT_DIMS = (((1,), (1,)), ((), ()))


def _silu(v):
    return v / (1.0 + jnp.exp(-v))


def _bdot(a, b):
    return jnp.dot(a.astype(BF16), b.astype(BF16), preferred_element_type=F32)


def _bdot_nt(a, b):
    return lax.dot_general(a.astype(BF16), b.astype(BF16), NT_DIMS, preferred_element_type=F32)


def _cparams(sem):
    return pltpu.CompilerParams(dimension_semantics=sem, vmem_limit_bytes=VMEM_LIMIT)


MOD_TN = 1536


def _mod_kernel(c_ref, w_ref, b_ref, o_ref):
    s = _silu(c_ref[...])
    o_ref[0] = _bdot(s, w_ref[0]) + b_ref[0]


def _modulation(cin, w_mod, b_mod):
    n = 6 * D_MODEL
    return pl.pallas_call(
        _mod_kernel,
        out_shape=jax.ShapeDtypeStruct((DEPTH, SUBLANES, n), F32),
        grid=(DEPTH, n // MOD_TN),
        in_specs=[
            pl.BlockSpec((SUBLANES, D_MODEL), lambda l, j: (0, 0)),
            pl.BlockSpec((1, D_MODEL, MOD_TN), lambda l, j: (l, 0, j)),
            pl.BlockSpec((1, 1, MOD_TN), lambda l, j: (l, 0, j)),
        ],
        out_specs=pl.BlockSpec((1, SUBLANES, MOD_TN), lambda l, j: (l, 0, j)),
        compiler_params=_cparams(("parallel", "parallel")),
        name="modulation",
    )(cin, w_mod, b_mod.reshape(DEPTH, 1, n))


def _norm_mod(x, g, shift, scale):
    var = jnp.mean(x * x, axis=-1, keepdims=True)
    h = x * lax.rsqrt(var + EPS) * g
    return h * (1.0 + scale) + shift


W_QA, W_QB, W_Z, W_KVA, W_KVB, W_XBC, W_DT = 0, 256, 512, 1024, 1280, 1792, 2816
IN_COLS = W_DT + DT_COLS


def _inproj_kernel(x_ref, mod_ref, g_ref, w_ref, wdt_ref, *rest, rope):
    if rope:
        cos_ref, sa_ref, sb_ref, oa_ref, ob_ref, oxz_ref, odt_ref = rest
    else:
        oa_ref, ob_ref, oxz_ref, odt_ref = rest
    h = _norm_mod(x_ref[0], g_ref[...], mod_ref[0, 0:1, :], mod_ref[0, 1:2, :]).astype(BF16)

    def proj(lo, hi):
        return jnp.dot(h, w_ref[0, :, lo:hi], preferred_element_type=F32)

    qa = proj(W_QA, W_QB)
    kva = proj(W_KVA, W_KVB)
    if rope:
        cos, sa, sb = cos_ref[...], sa_ref[...], sb_ref[...]

        def rot(v):
            up = pltpu.roll(v, LANES - 16, axis=1)
            dn = pltpu.roll(v, 16, axis=1)
            return v * cos + up * sa + dn * sb

        oa_ref[0, :, 0:LANES] = rot(qa[:, 0:LANES]).astype(BF16)
        oa_ref[0, :, LANES:2 * LANES] = rot(qa[:, LANES:]).astype(BF16)
        oa_ref[0, :, 2 * LANES:3 * LANES] = rot(kva[:, 0:LANES]).astype(BF16)
        oa_ref[0, :, 3 * LANES:] = kva[:, LANES:].astype(BF16)
    else:
        oa_ref[0, :, 0:2 * LANES] = qa.astype(BF16)
        oa_ref[0, :, 2 * LANES:] = kva.astype(BF16)
    ob_ref[0, :, 0:2 * LANES] = proj(W_QB, W_Z).astype(BF16)
    ob_ref[0, :, 2 * LANES:] = proj(W_KVB, W_XBC).astype(BF16)
    oxz_ref[0, :, 0:XBC_COLS] = proj(W_XBC, W_DT)
    oxz_ref[0, :, XBC_COLS:] = proj(W_Z, W_KVA)
    odt_ref[0] = jnp.dot(h, wdt_ref[...], preferred_element_type=F32)


def _inproj(x, mods, g, w, layer, wdt, rope_tabs, tm):
    b, t, _ = x.shape
    rope = rope_tabs is not None
    per_batch = mods.shape[0] > 1
    in_specs = [
        pl.BlockSpec((1, tm, D_MODEL), lambda i, j: (i, j, 0)),
        pl.BlockSpec((1, SUBLANES, D_MODEL), (lambda i, j: (i, 0, 0)) if per_batch else (lambda i, j: (0, 0, 0))),
        pl.BlockSpec((1, D_MODEL), lambda i, j: (0, 0)),
        pl.BlockSpec((1, D_MODEL, IN_COLS), lambda i, j: (layer, 0, 0)),
        pl.BlockSpec((D_MODEL, DT_PAD), lambda i, j: (0, 0)),
    ]
    args = [x, mods, g, w, wdt]
    if rope:
        in_specs += [pl.BlockSpec((tm, LANES), lambda i, j: (j, 0))] * 3
        args += list(rope_tabs)
    widths = (A_COLS, B_COLS, XZ_COLS, DT_PAD)
    return pl.pallas_call(
        functools.partial(_inproj_kernel, rope=rope),
        out_shape=[jax.ShapeDtypeStruct((b, t, n), dt) for n, dt in zip(widths, (BF16, BF16, F32, F32))],
        grid=(b, t // tm),
        in_specs=in_specs,
        out_specs=[pl.BlockSpec((1, tm, n), lambda i, j: (i, j, 0)) for n in widths],
        compiler_params=_cparams(("parallel", "parallel")),
        name="inproj_rope" if rope else "inproj",
    )(*args)


WA_KEYS = 3 * WA_BLOCK


WA_BLOCKS_PER_STEP = 4


def _attn_a_kernel(sink_ref, q_ref, k_ref, v_ref, kc_ref, vc_ref, o_ref, s_ref, p_ref):
    step = pl.program_id(1)
    nk = WA_KEYS + CTX_LEN
    nq = WA_HEADS * WA_BLOCK
    kc, vc = kc_ref[0], vc_ref[0]
    starts = []
    for bb in range(WA_BLOCKS_PER_STEP):
        n = step * WA_BLOCKS_PER_STEP + bb
        starts.append(pl.multiple_of(jnp.clip((n - 1) * WA_BLOCK, 0, SEQ - WA_KEYS), WA_BLOCK))

    first_half = lax.broadcasted_iota(jnp.int32, (WA_BLOCK, LANES), 1) < HEAD_DIM
    swap = lambda v: pltpu.roll(v, HEAD_DIM, axis=1)

    for bb in range(WA_BLOCKS_PER_STEP):
        q = q_ref[0, bb * WA_BLOCK:(bb + 1) * WA_BLOCK, :].astype(F32) * ATT_SCALE
        q01, q23 = q[:, 0:LANES], q[:, LANES:]
        lhs = jnp.concatenate([
            jnp.where(first_half, q01, 0.0), jnp.where(first_half, swap(q01), 0.0),
            jnp.where(first_half, 0.0, swap(q23)), jnp.where(first_half, 0.0, q23)], axis=0)
        kall = jnp.concatenate([k_ref[0, pl.ds(starts[bb], WA_KEYS), :], kc], axis=0)
        s_ref[bb * nq:(bb + 1) * nq, :] = _bdot_nt(lhs, kall)

    nr = WA_BLOCKS_PER_STEP * nq
    rows = lax.broadcasted_iota(jnp.int32, (nr, nk), 0)
    cols = lax.broadcasted_iota(jnp.int32, (nr, nk), 1)
    row1 = lax.broadcasted_iota(jnp.int32, (nr, 1), 0)
    blk = row1 // nq
    qpos = (step * WA_BLOCKS_PER_STEP + blk) * WA_BLOCK + (rows & (WA_BLOCK - 1))
    kstart = starts[-1]
    for bb in range(WA_BLOCKS_PER_STEP - 2, -1, -1):
        kstart = jnp.where(blk == bb, starts[bb], kstart)
    valid = (cols >= WA_KEYS) | (jnp.abs(qpos - (kstart + cols)) <= WA_WINDOW)
    s = jnp.where(valid, s_ref[...], NEG_INF)
    head = (row1 // WA_BLOCK) % WA_HEADS
    sink = jnp.where(head == 0, sink_ref[0],
                     jnp.where(head == 1, sink_ref[1], jnp.where(head == 2, sink_ref[2], sink_ref[3])))
    m = jnp.maximum(jnp.max(s, axis=1, keepdims=True), sink)
    p = jnp.exp(s - m)
    inv = 1.0 / (jnp.sum(p, axis=1, keepdims=True) + jnp.exp(sink - m))
    p_ref[...] = p.astype(BF16)

    for bb in range(WA_BLOCKS_PER_STEP):
        vall = jnp.concatenate([v_ref[0, pl.ds(starts[bb], WA_KEYS), :], vc], axis=0)
        rs = slice(bb * nq, (bb + 1) * nq)
        o = jnp.dot(p_ref[rs, :], vall, preferred_element_type=F32) * inv[rs]
        o0, o1, o2, o3 = (o[h * WA_BLOCK:(h + 1) * WA_BLOCK] for h in range(WA_HEADS))
        qs = slice(bb * WA_BLOCK, (bb + 1) * WA_BLOCK)
        o_ref[0, qs, 0:LANES] = jnp.where(first_half, o0, swap(o1)).astype(o_ref.dtype)
        o_ref[0, qs, LANES:] = jnp.where(first_half, swap(o2), o3).astype(o_ref.dtype)


def _attn_a(sink, qkv, qkv_c):
    b = qkv.shape[0]
    nk = WA_KEYS + CTX_LEN
    tq = WA_BLOCKS_PER_STEP * WA_BLOCK
    nr = WA_BLOCKS_PER_STEP * WA_HEADS * WA_BLOCK
    return pl.pallas_call(
        _attn_a_kernel,
        out_shape=jax.ShapeDtypeStruct((b, SEQ, WA_HEADS * HEAD_DIM), BF16),
        scratch_shapes=[pltpu.VMEM((nr, nk), F32), pltpu.VMEM((nr, nk), BF16)],
        grid=(b, SEQ // tq),
        in_specs=[
            pl.BlockSpec(memory_space=pltpu.SMEM),
            pl.BlockSpec((1, tq, 2 * LANES), lambda i, j: (i, j, 0)),
            pl.BlockSpec((1, SEQ, LANES), lambda i, j: (i, 0, 2)),
            pl.BlockSpec((1, SEQ, LANES), lambda i, j: (i, 0, 3)),
            pl.BlockSpec((1, CTX_LEN, LANES), lambda i, j: (i, 0, 2)),
            pl.BlockSpec((1, CTX_LEN, LANES), lambda i, j: (i, 0, 3)),
        ],
        out_specs=pl.BlockSpec((1, tq, 2 * LANES), lambda i, j: (i, j, 0)),
        compiler_params=_cparams(("parallel", "arbitrary")),
        name="attn_window",
    )(sink, qkv, qkv, qkv, qkv_c, qkv_c)


NB_ROWS_PER_STEP = 8
NB_LOC = NA_KH * GRID_W
NB_DY_PAIRS = 2 * NA_KH - 2


def _attn_b_kernel(q_ref, k_ref, v_ref, kc_ref, vc_ref, t_ref, o_ref, s_ref, p_ref):
    i = pl.program_id(1)
    kc = kc_ref[0]
    vc = vc_ref[0]
    units = [(rr, pp) for rr in range(NB_ROWS_PER_STEP) for pp in range(NA_HEADS // 2)]
    starts, shifts = [], []
    for rr in range(NB_ROWS_PER_STEP):
        r = i * NB_ROWS_PER_STEP + rr
        rs = jnp.clip(r - NA_KH // 2, 0, GRID_ROWS - NA_KH)
        shifts.append(r - rs)
        starts.append(pl.multiple_of(rs * GRID_W, GRID_W))
    first_half = lax.broadcasted_iota(jnp.int32, (GRID_W, LANES), 1) < HEAD_DIM
    zero = jnp.zeros((), BF16)
    pw = 2 * GRID_W

    for u, (rr, pp) in enumerate(units):
        ls = slice(pp * LANES, (pp + 1) * LANES)
        if pp == 0:
            kw = k_ref[0, pl.ds(starts[rr], NB_LOC), :]
            q = q_ref[0, rr * GRID_W:(rr + 1) * GRID_W, :] * ATT_SCALE
        qs = q[:, ls]
        lhs = jnp.concatenate([jnp.where(first_half, qs, zero), jnp.where(first_half, zero, qs)], axis=0)
        bias = jnp.concatenate(
            [jnp.concatenate([t_ref[h, 2 * k - shifts[rr] + NA_KH - 1] for k in range(NA_KH // 2)], axis=1)
             for h in (2 * pp, 2 * pp + 1)], axis=0)
        s_ref[u * pw:(u + 1) * pw, 0:NB_LOC] = _bdot_nt(lhs, kw[:, ls]) + bias
        s_ref[u * pw:(u + 1) * pw, NB_LOC:] = _bdot_nt(lhs, kc[:, ls])

    s = s_ref[...]
    p = jnp.exp(s - jnp.max(s, axis=1, keepdims=True))
    inv = 1.0 / jnp.sum(p, axis=1, keepdims=True)
    p_ref[...] = p.astype(BF16)

    for u, (rr, pp) in enumerate(units):
        ls = slice(pp * LANES, (pp + 1) * LANES)
        if pp == 0:
            vw = v_ref[0, pl.ds(starts[rr], NB_LOC), :]
        rows = slice(u * pw, (u + 1) * pw)
        o = (jnp.dot(p_ref[rows, 0:NB_LOC], vw[:, ls], preferred_element_type=F32)
             + jnp.dot(p_ref[rows, NB_LOC:], vc[:, ls], preferred_element_type=F32)) * inv[rows]
        o_ref[0, rr * GRID_W:(rr + 1) * GRID_W, ls] = jnp.where(
            first_half, o[:GRID_W], o[GRID_W:]).astype(o_ref.dtype)


def _attn_b(qkv, qkv_c, table):
    b = qkv.shape[0]
    tq = NB_ROWS_PER_STEP * GRID_W
    w = NA_HEADS * HEAD_DIM
    nu = NB_ROWS_PER_STEP * NA_HEADS * GRID_W
    return pl.pallas_call(
        _attn_b_kernel,
        out_shape=jax.ShapeDtypeStruct((b, SEQ, w), BF16),
        scratch_shapes=[pltpu.VMEM((nu, NB_LOC + CTX_LEN), F32), pltpu.VMEM((nu, NB_LOC + CTX_LEN), BF16)],
        grid=(b, SEQ // tq),
        in_specs=[
            pl.BlockSpec((1, tq, w), lambda i, j: (i, j, 0)),
            pl.BlockSpec((1, SEQ, w), lambda i, j: (i, 0, 1)),
            pl.BlockSpec((1, SEQ, w), lambda i, j: (i, 0, 2)),
            pl.BlockSpec((1, CTX_LEN, w), lambda i, j: (i, 0, 1)),
            pl.BlockSpec((1, CTX_LEN, w), lambda i, j: (i, 0, 2)),
            pl.BlockSpec((NA_HEADS, NB_DY_PAIRS, GRID_W, LANES), lambda i, j: (0, 0, 0, 0)),
        ],
        out_specs=pl.BlockSpec((1, tq, w), lambda i, j: (i, j, 0)),
        compiler_params=_cparams(("parallel", "arbitrary")),
        name="attn_neighbourhood",
    )(qkv, qkv, qkv, qkv_c, qkv_c, table)


def _split3(a):
    a1 = a.astype(BF16)
    r1 = a - a1.astype(F32)
    a2 = r1.astype(BF16)
    a3 = (r1 - a2.astype(F32)).astype(BF16)
    return a1, a2, a3


def _bias_kernel(r_ref, oh_ref, o_ref):
    oh = oh_ref[...]
    o_ref[...] = sum(jnp.dot(t, oh, preferred_element_type=F32) for t in _split3(r_ref[...]))


def _na_bias_tables(rpb):
    ndy, ndx = 2 * NA_KH - 1, 2 * NA_KW - 1
    qc = np.arange(GRID_W)[:, None]
    x = np.arange(GRID_W)[None, :]
    dx = np.clip(x - qc, -(NA_KW - 1), NA_KW - 1) + NA_KW - 1
    onehot = (np.arange(LANES)[:, None, None] == dx[None]).reshape(LANES, GRID_W * GRID_W)
    cstart = np.clip(qc - NA_KW // 2, 0, GRID_W - NA_KW)
    inside = (x >= cstart) & (x < cstart + NA_KW)
    rows = DEPTH * NA_HEADS * ndy
    r = jnp.pad(rpb.astype(F32).reshape(rows, ndx), ((0, LANES - rows), (0, LANES - ndx)))
    m = pl.pallas_call(
        _bias_kernel,
        out_shape=jax.ShapeDtypeStruct((LANES, GRID_W * GRID_W), F32),
        name="na_bias_expand",
    )(r, jnp.asarray(onehot, BF16))
    m = m[:rows].reshape(DEPTH, NA_HEADS, ndy, GRID_W, GRID_W)
    m = jnp.where(jnp.asarray(inside), m, NEG_INF)
    return jnp.concatenate([m[:, :, :ndy - 1], m[:, :, 1:]], axis=-1)


def _ctx_attn_kernel(sink_ref, a_ref, b_ref, oa_ref, ob_ref):
    a = a_ref[0]
    row1 = lax.broadcasted_iota(jnp.int32, (2 * CTX_LEN, 1), 0)
    for g in range(WA_KV_HEADS):
        h0, h1 = 2 * g, 2 * g + 1
        q2 = jnp.concatenate([a[:, h0 * HEAD_DIM:(h0 + 1) * HEAD_DIM],
                              a[:, h1 * HEAD_DIM:(h1 + 1) * HEAD_DIM]], axis=0)
        k = a[:, 2 * LANES + g * HEAD_DIM:2 * LANES + (g + 1) * HEAD_DIM]
        v = a[:, 3 * LANES + g * HEAD_DIM:3 * LANES + (g + 1) * HEAD_DIM]
        s = _bdot_nt(q2, k) * ATT_SCALE
        sink = jnp.where(row1 < CTX_LEN, sink_ref[h0], sink_ref[h1])
        m = jnp.maximum(jnp.max(s, axis=1, keepdims=True), sink)
        p = jnp.exp(s - m)
        den = jnp.sum(p, axis=1, keepdims=True) + jnp.exp(sink - m)
        o = _bdot(p, v) / den
        oa_ref[0, :, h0 * HEAD_DIM:(h0 + 1) * HEAD_DIM] = o[:CTX_LEN]
        oa_ref[0, :, h1 * HEAD_DIM:(h1 + 1) * HEAD_DIM] = o[CTX_LEN:]
    bq = b_ref[0]
    w = NA_HEADS * HEAD_DIM
    for h in range(NA_HEADS):
        hs = slice(h * HEAD_DIM, (h + 1) * HEAD_DIM)
        s = _bdot_nt(bq[:, hs], bq[:, w + h * HEAD_DIM:w + (h + 1) * HEAD_DIM]) * ATT_SCALE
        m = jnp.max(s, axis=1, keepdims=True)
        p = jnp.exp(s - m)
        den = jnp.sum(p, axis=1, keepdims=True)
        ob_ref[0, :, hs] = _bdot(p, bq[:, 2 * w + h * HEAD_DIM:2 * w + (h + 1) * HEAD_DIM]) / den


def _ctx_attn(sink, qkv_a_c, qkv_b_c):
    b = qkv_a_c.shape[0]
    w = 2 * LANES
    return pl.pallas_call(
        _ctx_attn_kernel,
        out_shape=[jax.ShapeDtypeStruct((b, CTX_LEN, w), F32)] * 2,
        grid=(b,),
        in_specs=[
            pl.BlockSpec(memory_space=pltpu.SMEM),
            pl.BlockSpec((1, CTX_LEN, A_COLS), lambda i: (i, 0, 0)),
            pl.BlockSpec((1, CTX_LEN, B_COLS), lambda i: (i, 0, 0)),
        ],
        out_specs=[pl.BlockSpec((1, CTX_LEN, w), lambda i: (i, 0, 0))] * 2,
        compiler_params=_cparams(("parallel",)),
        name="attn_context",
    )(sink, qkv_a_c, qkv_b_c)


CONV_HALO = SUBLANES


def _conv_kernel(prev_ref, cur_ref, next_ref, w_ref, b_ref, o_ref, ext_ref, *, tl, nt):
    j = pl.program_id(1)
    ext_ref[0:CONV_HALO, :] = jnp.where(j > 0, prev_ref[0], 0.0)
    ext_ref[CONV_HALO:CONV_HALO + tl, :] = cur_ref[0]
    ext_ref[CONV_HALO + tl:, :] = jnp.where(j < nt - 1, next_ref[0], 0.0)
    acc = jnp.zeros((tl, XBC_COLS), F32) + b_ref[...]
    base = CONV_HALO - SSM_CONV // 2
    for k in range(SSM_CONV):
        acc = acc + w_ref[k:k + 1, :] * ext_ref[base + k:base + k + tl, :]
    o_ref[0] = _silu(acc)


def _conv_silu(xz, conv_w, conv_b, tl):
    b, t, _ = xz.shape
    nt = t // tl
    hb = tl // CONV_HALO
    last = t // CONV_HALO - 1
    return pl.pallas_call(
        functools.partial(_conv_kernel, tl=tl, nt=nt),
        out_shape=jax.ShapeDtypeStruct((b, t, XBC_COLS), F32),
        grid=(b, nt),
        in_specs=[
            pl.BlockSpec((1, CONV_HALO, XBC_COLS), lambda i, j: (i, jnp.maximum(j * hb - 1, 0), 0)),
            pl.BlockSpec((1, tl, XBC_COLS), lambda i, j: (i, j, 0)),
            pl.BlockSpec((1, CONV_HALO, XBC_COLS), lambda i, j: (i, jnp.minimum((j + 1) * hb, last), 0)),
            pl.BlockSpec((SUBLANES, XBC_COLS), lambda i, j: (0, 0)),
            pl.BlockSpec((1, XBC_COLS), lambda i, j: (0, 0)),
        ],
        out_specs=pl.BlockSpec((1, tl, XBC_COLS), lambda i, j: (i, j, 0)),
        scratch_shapes=[pltpu.VMEM((tl + 2 * CONV_HALO, XBC_COLS), F32)],
        compiler_params=_cparams(("parallel", "parallel")),
        name="ssm_conv",
    )(xz, xz, xz, conv_w, conv_b)


Q = SSM_CHUNK
GS = SSM_GROUPS * SSM_STATE
HPG = SSM_HEADS // SSM_GROUPS
SSD_STATE_SHAPE = (SSM_GROUPS, SSM_STATE, HPG * SSM_HEAD_DIM)
SSD_BATCH_PER_STEP = 4


def _softplus(v):
    return jnp.maximum(v, 0.0) + jnp.log1p(jnp.exp(-jnp.abs(v)))


def _cumsum_mat(tri, a):
    r = jnp.dot(tri, jnp.concatenate(_split3(a), axis=1), preferred_element_type=F32)
    n = a.shape[1]
    return r[:, 0:n] + r[:, n:2 * n] + r[:, 2 * n:]


def _ssd_kernel(xm_ref, dtm_ref, xb_ref, dtb_ref, bias_ref, alog_ref, h0f_ref, h0b_ref,
                y1_ref, y2_ref, hf_out_ref, hb_out_ref, hf_ref, hb_ref, *, nc):
    i = pl.program_id(1)

    @pl.when(i == 0)
    def _():
        hf_ref[...] = h0f_ref[...]
        hb_ref[...] = h0b_ref[...]

    ii = lax.broadcasted_iota(jnp.int32, (Q, Q), 0)
    jj = lax.broadcasted_iota(jnp.int32, (Q, Q), 1)
    lower = ii >= jj
    diag = ii == jj
    tril = jnp.where(lower, 1.0, 0.0).astype(BF16)
    triu = jnp.where(ii <= jj, 1.0, 0.0).astype(BF16)
    first_half = lax.broadcasted_iota(jnp.int32, (Q, LANES), 1) < SSM_HEAD_DIM
    a_row = -jnp.exp(alog_ref[...])
    gw = HPG * SSM_HEAD_DIM

    def group_operands(xbc, g):
        bg = xbc[:, SSM_INNER + g * SSM_STATE:SSM_INNER + (g + 1) * SSM_STATE]
        cg = xbc[:, SSM_INNER + GS + g * SSM_STATE:SSM_INNER + GS + (g + 1) * SSM_STATE]
        return bg, cg.astype(BF16), xbc[:, g * gw:(g + 1) * gw]

    nbs = range(SSD_BATCH_PER_STEP)
    groups = range(SSM_GROUPS)
    bias = bias_ref[...]
    zero = jnp.zeros((), F32)


    xm = [xm_ref[bi] for bi in nbs]
    xb = [xb_ref[bi] for bi in nbs]
    dtm = [_softplus(dtm_ref[bi] + bias) for bi in nbs]
    dtb = [_softplus(dtb_ref[bi] + bias) for bi in nbs]
    cumf = [_cumsum_mat(tril, d * a_row) for d in dtm]
    cumr = [_cumsum_mat(triu, d * a_row) for d in dtm]
    cumb = [_cumsum_mat(triu, d * a_row) for d in dtb]

    opm = [[group_operands(xm[bi], g) for g in groups] for bi in nbs]
    opb = [[group_operands(xb[bi], g) for g in groups] for bi in nbs]
    gmat = [[_bdot_nt(opm[bi][g][1], opm[bi][g][0]) for g in groups] for bi in nbs]
    hf_prev = [[hf_ref[bi, g] for g in groups] for bi in nbs]
    hb_prev = [[hb_ref[bi, g] for g in groups] for bi in nbs]
    inter_f = [[jnp.dot(opm[bi][g][1], hf_prev[bi][g].astype(BF16), preferred_element_type=F32)
                for g in groups] for bi in nbs]
    inter_b = [[jnp.dot(opb[bi][g][1], hb_prev[bi][g].astype(BF16), preferred_element_type=F32)
                for g in groups] for bi in nbs]
    bt_m = [[opm[bi][g][0].T for g in groups] for bi in nbs]
    bt_b = [[opb[bi][g][0].T for g in groups] for bi in nbs]

    cumf_t = [c.T for c in cumf]
    cumr_t = [c.T for c in cumr]
    dtm_t = [d.T for d in dtm]
    cumb_t = [c.T for c in cumb]
    dtb_t = [d.T for d in dtb]

    heads = [(bi, h) for bi in nbs for h in range(SSM_HEADS)]
    hb_ = lambda h: SSM_HEADS + h
    cf = {k: jnp.broadcast_to(cumf[k[0]][:, k[1]:k[1] + 1], (Q, Q)) for k in heads}
    cr = {k: jnp.broadcast_to(cumr[k[0]][:, hb_(k[1]):hb_(k[1]) + 1], (Q, Q)) for k in heads}
    cb = {k: jnp.broadcast_to(cumb[k[0]][:, hb_(k[1]):hb_(k[1]) + 1], (Q, Q)) for k in heads}
    arg = {k: jnp.where(lower, cf[k] - cumf_t[k[0]][k[1]:k[1] + 1, :],
                        cr[k] - cumr_t[k[0]][hb_(k[1]):hb_(k[1]) + 1, :]) for k in heads}
    ex = {k: jnp.exp(arg[k]) for k in heads}
    ein_f = {k: jnp.exp(cf[k]) for k in heads}
    ein_b = {k: jnp.exp(cb[k]) for k in heads}
    wm = {}
    for bi, h in heads:
        dtf_row = dtm_t[bi][h:h + 1, :]
        dtb_row = dtm_t[bi][hb_(h):hb_(h) + 1, :]
        wm[bi, h] = (gmat[bi][h // HPG] * (ex[bi, h] * jnp.where(lower, dtf_row, dtb_row)
                                           + jnp.where(diag, dtb_row, zero))).astype(BF16)
    wst_f, wst_b = {}, {}
    for bi, h in heads:
        rf = cumf_t[bi][h:h + 1, :]
        rb = cumb_t[bi][hb_(h):hb_(h) + 1, :]
        w_f = jnp.exp(rf[:, Q - 1:Q] - rf) * dtm_t[bi][h:h + 1, :]
        w_b = jnp.exp(rb[:, 0:1] - rb) * dtb_t[bi][hb_(h):hb_(h) + 1, :]
        wst_f[bi, h] = (bt_m[bi][h // HPG] * w_f).astype(BF16)
        wst_b[bi, h] = (bt_b[bi][h // HPG] * w_b).astype(BF16)

    def pair_rhs(x, pr):
        xp = x[:, pr * LANES:(pr + 1) * LANES]
        return jnp.concatenate([jnp.where(first_half, xp, zero), jnp.where(first_half, zero, xp)],
                               axis=0).astype(BF16)

    def pair_dot(mats, bi, pr, rhs):
        lhs = jnp.concatenate([mats[bi, 2 * pr], mats[bi, 2 * pr + 1]], axis=1)
        return jnp.dot(lhs, rhs, preferred_element_type=F32)

    pairs = [(bi, pr) for bi in nbs for pr in range(SSM_HEADS // 2)]
    rhs_m = {k: pair_rhs(xm[k[0]], k[1]) for k in pairs}
    rhs_b = {k: pair_rhs(xb[k[0]], k[1]) for k in pairs}
    y_intra = {k: pair_dot(wm, k[0], k[1], rhs_m[k]) for k in pairs}
    st_f = {k: pair_dot(wst_f, k[0], k[1], rhs_m[k]) for k in pairs}
    st_b = {k: pair_dot(wst_b, k[0], k[1], rhs_b[k]) for k in pairs}

    ppg = HPG // 2
    for bi, pr in pairs:
        g, k = pr // ppg, pr % ppg
        ls = slice(k * LANES, (k + 1) * LANES)
        sin_f = jnp.where(first_half, ein_f[bi, 2 * pr], ein_f[bi, 2 * pr + 1])
        sin_b = jnp.where(first_half, ein_b[bi, 2 * pr], ein_b[bi, 2 * pr + 1])
        y1_ref[bi, :, pr * LANES:(pr + 1) * LANES] = y_intra[bi, pr] + sin_f * inter_f[bi][g][:, ls]
        y2_ref[bi, :, pr * LANES:(pr + 1) * LANES] = sin_b * inter_b[bi][g][:, ls]
        hf_ref[bi, g, :, ls] = hf_prev[bi][g][:, ls] * sin_f[Q - 1:Q, :] + st_f[bi, pr]
        hb_ref[bi, g, :, ls] = hb_prev[bi][g][:, ls] * sin_b[0:1, :] + st_b[bi, pr]

    @pl.when(i == nc - 1)
    def _():
        hf_out_ref[...] = hf_ref[...]
        hb_out_ref[...] = hb_ref[...]


def _ssd(xbc, dt_raw, dt_bias, a_log, h0f, h0b):
    b, t, _ = xbc.shape
    nc = t // Q
    nb = SSD_BATCH_PER_STEP
    st_shape = (b,) + SSD_STATE_SHAPE
    st_spec = pl.BlockSpec((nb,) + SSD_STATE_SHAPE, lambda i, j: (i, 0, 0, 0))
    fwd = lambda i, j: (i, j, 0)
    bwd = lambda i, j: (i, nc - 1 - j, 0)
    return pl.pallas_call(
        functools.partial(_ssd_kernel, nc=nc),
        out_shape=[jax.ShapeDtypeStruct((b, t, SSM_INNER), F32)] * 2
        + [jax.ShapeDtypeStruct(st_shape, F32)] * 2,
        grid=(b // nb, nc),
        in_specs=[
            pl.BlockSpec((nb, Q, XBC_COLS), fwd),
            pl.BlockSpec((nb, Q, DT_PAD), fwd),
            pl.BlockSpec((nb, Q, XBC_COLS), bwd),
            pl.BlockSpec((nb, Q, DT_PAD), bwd),
            pl.BlockSpec((1, DT_PAD), lambda i, j: (0, 0)),
            pl.BlockSpec((1, DT_PAD), lambda i, j: (0, 0)),
            st_spec, st_spec,
        ],
        out_specs=[pl.BlockSpec((nb, Q, SSM_INNER), fwd), pl.BlockSpec((nb, Q, SSM_INNER), bwd),
                   st_spec, st_spec],
        scratch_shapes=[pltpu.VMEM((nb,) + SSD_STATE_SHAPE, F32)] * 2,
        compiler_params=_cparams(("parallel", "arbitrary")),
        name="ssd_scan",
    )(xbc, dt_raw, xbc, dt_raw, dt_bias, a_log, h0f, h0b)


MXU_TILE = 256
FFN_SPLITS = (0, 6 * MXU_TILE, D_FF)


def _mix_ffn_kernel(x_ref, mod_ref, oa_ref, ob_ref, y1_ref, y2_ref, xs_ref, z_ref, d_ref, sg_ref,
                    wo_ref, g_ref, wfi_ref, wfo_ref, gfin_ref, o_ref, *, final):
    y = y1_ref[0] + y2_ref[0] + d_ref[...] * xs_ref[0]
    y = y * _silu(z_ref[0])
    var = jnp.mean(y * y, axis=-1, keepdims=True)
    oc = (y * lax.rsqrt(var + EPS) * sg_ref[...]).astype(BF16)
    wa = 2 * LANES
    mix = (jnp.dot(oa_ref[0].astype(BF16), wo_ref[0, 0:wa, :], preferred_element_type=F32)
           + jnp.dot(ob_ref[0].astype(BF16), wo_ref[0, wa:2 * wa, :], preferred_element_type=F32)
           + jnp.dot(oc, wo_ref[0, 2 * wa:, :], preferred_element_type=F32))
    xn = x_ref[0] + mod_ref[0, 2:3, :] * mix
    h = _norm_mod(xn, g_ref[...], mod_ref[0, 3:4, :], mod_ref[0, 4:5, :]).astype(BF16)
    acc = None
    for lo, hi in zip(FFN_SPLITS[:-1], FFN_SPLITS[1:]):
        gate = jnp.dot(h, wfi_ref[0, :, lo:hi], preferred_element_type=F32)
        up = jnp.dot(h, wfi_ref[0, :, D_FF + lo:D_FF + hi], preferred_element_type=F32)
        act = (_silu(gate) * up).astype(BF16)
        part = jnp.dot(act, wfo_ref[0, lo:hi, :], preferred_element_type=F32)
        acc = part if acc is None else acc + part
    out = xn + mod_ref[0, 5:6, :] * acc
    if final:
        var = jnp.mean(out * out, axis=-1, keepdims=True)
        out = out * lax.rsqrt(var + EPS) * gfin_ref[...]
    o_ref[0] = out


def _mix_ffn(x, mods, o_a, o_b, y1, y2, xbc, xz, d_exp, ssm_g, w_out, g_ffn, w_ffn_in, w_ffn_out,
             g_final, layer, tm, final):
    b, t, _ = x.shape
    per_batch = mods.shape[0] > 1
    row = lambda n: pl.BlockSpec((1, tm, n), lambda i, j: (i, j, 0))
    const = lambda shape: pl.BlockSpec(shape, lambda i, j: (0, 0), pipeline_mode=pl.Buffered(1))
    weight = lambda r, c: pl.BlockSpec((1, r, c), lambda i, j: (layer, 0, 0), pipeline_mode=pl.Buffered(1))
    return pl.pallas_call(
        functools.partial(_mix_ffn_kernel, final=final),
        out_shape=jax.ShapeDtypeStruct((b, t, D_MODEL), F32),
        grid=(b, t // tm),
        in_specs=[
            row(D_MODEL),
            pl.BlockSpec((1, SUBLANES, D_MODEL), (lambda i, j: (i, 0, 0)) if per_batch else (lambda i, j: (0, 0, 0))),
            row(2 * LANES), row(2 * LANES), row(SSM_INNER), row(SSM_INNER),
            pl.BlockSpec((1, tm, SSM_INNER), lambda i, j: (i, j, 0)),
            pl.BlockSpec((1, tm, SSM_INNER), lambda i, j: (i, j, 2)),
            const((1, SSM_INNER)), const((1, SSM_INNER)),
            weight(D_MODEL, D_MODEL),
            const((1, D_MODEL)),
            weight(D_MODEL, 2 * D_FF),
            weight(D_FF, D_MODEL),
            const((1, D_MODEL)),
        ],
        out_specs=row(D_MODEL),
        compiler_params=_cparams(("parallel", "parallel")),
        name="mix_ffn_final" if final else "mix_ffn",
    )(x, mods, o_a, o_b, y1, y2, xbc, xz, d_exp, ssm_g, w_out, g_ffn, w_ffn_in, w_ffn_out, g_final)


def _rope_tables():
    t = np.arange(SEQ)
    pos = np.stack([t // GRID_W, t % GRID_W], axis=1).astype(np.float64)
    quarter = HEAD_DIM // 4
    inv_freq = ROPE_BASE ** (-np.arange(quarter, dtype=np.float64) / quarter)
    lane = np.arange(LANES) % HEAD_DIM
    half = lane // (HEAD_DIM // 2)
    idx = lane % (HEAD_DIM // 2)
    ang = pos[:, half] * inv_freq[idx % quarter][None, :]
    cos, sin = np.cos(ang), np.sin(ang)
    first = (idx < quarter)[None, :]
    tabs = (cos, np.where(first, -sin, 0.0), np.where(first, 0.0, sin))
    return tuple(jnp.asarray(v, F32) for v in tabs)


def _pad_lanes(v, n=LANES):
    v = v.reshape(1, -1)
    return jnp.pad(v, ((0, 0), (0, n - v.shape[1])))


def kernel(x, c, ctx, c_ctx, w_mod, b_mod, g_mix, w_in, wa_sink, na_rpb, ssm_conv_w, ssm_conv_b,
           ssm_dt_bias, ssm_a_log, ssm_d, ssm_norm_g, w_out, g_ffn, w_ffn_in, w_ffn_out, g_final):
    cin = jnp.concatenate([c, c_ctx[None, :], jnp.zeros((SUBLANES - BATCH - 1, D_MODEL), F32)], axis=0)
    mod_all = _modulation(cin, w_mod, b_mod)
    rope_tabs = _rope_tables()
    bias_tabs = _na_bias_tables(na_rpb)
    zeros_state = jnp.zeros((BATCH,) + SSD_STATE_SHAPE, F32)
    gfin = g_final.reshape(1, D_MODEL)
    w_proj = w_in.astype(BF16)
    wo = w_out.astype(BF16)
    wfi = w_ffn_in.astype(BF16)
    wfo = w_ffn_out.astype(BF16)

    xl, xc = x, ctx
    for l in range(DEPTH):
        last = l == DEPTH - 1
        m6 = mod_all[l].reshape(SUBLANES, 6, D_MODEL)
        mods_l = jnp.pad(m6[:BATCH], ((0, 0), (0, 2), (0, 0)))
        mods_c = jnp.pad(m6[BATCH:BATCH + 1], ((0, 0), (0, 2), (0, 0)))
        w_dt = jnp.pad(w_in[l][:, W_DT:], ((0, 0), (0, DT_PAD - DT_COLS))).astype(BF16)
        g1 = g_mix[l].reshape(1, D_MODEL)
        g2 = g_ffn[l].reshape(1, D_MODEL)
        conv_w = jnp.pad(ssm_conv_w[l], ((0, SUBLANES - SSM_CONV), (0, 0)))
        conv_b = ssm_conv_b[l].reshape(1, XBC_COLS)
        dt_bias = _pad_lanes(ssm_dt_bias[l])
        a_log = _pad_lanes(ssm_a_log[l])
        d_exp = jnp.repeat(ssm_d[l], SSM_HEAD_DIM).reshape(1, SSM_INNER)
        sg = ssm_norm_g[l].reshape(1, SSM_INNER)

        a_c, b_c, xz_c, dt_c = _inproj(xc, mods_c, g1, w_proj, l, w_dt, None, CTX_LEN)
        a_l, b_l, xz_l, dt_l = _inproj(xl, mods_l, g1, w_proj, l, w_dt, rope_tabs, 512)

        o_a = _attn_a(wa_sink[l], a_l, a_c)
        o_b = _attn_b(b_l, b_c, bias_tabs[l])

        xbc_c = _conv_silu(xz_c, conv_w, conv_b, CTX_LEN)
        xbc_l = _conv_silu(xz_l, conv_w, conv_b, 512)
        y1_c, y2_c, h_f, h_b = _ssd(xbc_c, dt_c, dt_bias, a_log, zeros_state, zeros_state)
        y1_l, y2_l, _, _ = _ssd(xbc_l, dt_l, dt_bias, a_log, h_f, h_b)

        xl = _mix_ffn(xl, mods_l, o_a, o_b, y1_l, y2_l, xbc_l, xz_l, d_exp, sg, wo, g2, wfi, wfo, gfin,
                      l, 512, last)
        if not last:
            o_ac, o_bc = _ctx_attn(wa_sink[l], a_c, b_c)
            xc = _mix_ffn(xc, mods_c, o_ac, o_bc, y1_c, y2_c, xbc_c, xz_c, d_exp, sg, wo, g2, wfi, wfo,
                          gfin, l, CTX_LEN, False)
    return xl
```

```python
import functools
import math

import numpy as np
import jax
import jax.numpy as jnp
from jax import lax
from jax.experimental import pallas as pl
from jax.experimental.pallas import tpu as pltpu

F32 = jnp.float32
BF16 = jnp.bfloat16

D_MODEL = 1024
BATCH = 4
SEQ = 4096
DEPTH = 2
GRID_W = 64
GRID_ROWS = SEQ // GRID_W
CTX_LEN = 256
EPS = 1e-6
HEAD_DIM = 64
ROPE_BASE = 10000.0
WA_HEADS = 4
WA_KV_HEADS = 2
WA_WINDOW = 128
WA_BLOCK = 128
NA_HEADS = 4
NA_KH = 8
NA_KW = 16
SSM_HEADS = 8
SSM_HEAD_DIM = 64
SSM_INNER = SSM_HEADS * SSM_HEAD_DIM
SSM_GROUPS = 2
SSM_STATE = 128
SSM_CONV = 7
SSM_CHUNK = 128
D_FF = 2816
XBC_COLS = SSM_INNER + 2 * SSM_GROUPS * SSM_STATE
DT_COLS = 2 * SSM_HEADS

LANES = 128
SUBLANES = 8
VMEM_LIMIT = 56 * 1024 * 1024

TM_INPROJ = 1024
TM_CONV = 1024
TM_MIX = 512

A_COLS = 512
B_COLS = 768
XZ_COLS = 1536
DT_PAD = LANES
PROJ_COLS = A_COLS + B_COLS + XZ_COLS + DT_PAD

ATT_SCALE = HEAD_DIM ** -0.5
NEG_INF = float("-inf")
NT_DIMS = (((1,), (1,)), ((), ()))


def _silu(v):
    return v / (1.0 + jnp.exp(-v))


def _bdot(a, b):
    return jnp.dot(a.astype(BF16), b.astype(BF16), preferred_element_type=F32)


def _bdot_nt(a, b):
    return lax.dot_general(a.astype(BF16), b.astype(BF16), NT_DIMS, preferred_element_type=F32)


def _cparams(sem):
    return pltpu.CompilerParams(dimension_semantics=sem, vmem_limit_bytes=VMEM_LIMIT)


MOD_TN = 1536


def _mod_kernel(c_ref, w_ref, b_ref, o_ref):
    s = _silu(c_ref[...])
    o_ref[0] = _bdot(s, w_ref[0]) + b_ref[0]


def _modulation(cin, w_mod, b_mod):
    n = 6 * D_MODEL
    return pl.pallas_call(
        _mod_kernel,
        out_shape=jax.ShapeDtypeStruct((DEPTH, SUBLANES, n), F32),
        grid=(DEPTH, n // MOD_TN),
        in_specs=[
            pl.BlockSpec((SUBLANES, D_MODEL), lambda l, j: (0, 0)),
            pl.BlockSpec((1, D_MODEL, MOD_TN), lambda l, j: (l, 0, j)),
            pl.BlockSpec((1, 1, MOD_TN), lambda l, j: (l, 0, j)),
        ],
        out_specs=pl.BlockSpec((1, SUBLANES, MOD_TN), lambda l, j: (l, 0, j)),
        compiler_params=_cparams(("parallel", "parallel")),
        name="modulation",
    )(cin, w_mod, b_mod.reshape(DEPTH, 1, n))


def _norm_mod(x, g, shift, scale):
    var = jnp.mean(x * x, axis=-1, keepdims=True)
    h = x * lax.rsqrt(var + EPS) * g
    return h * (1.0 + scale) + shift


W_QA, W_QB, W_Z, W_KVA, W_KVB, W_XBC, W_DT = 0, 256, 512, 1024, 1280, 1792, 2816
IN_COLS = W_DT + DT_COLS


def _inproj_kernel(x_ref, mod_ref, g_ref, w_ref, wdt_ref, *rest, rope):
    if rope:
        cos_ref, sa_ref, sb_ref, oa_ref, ob_ref, oxz_ref, odt_ref = rest
    else:
        oa_ref, ob_ref, oxz_ref, odt_ref = rest
    h = _norm_mod(x_ref[0], g_ref[...], mod_ref[0, 0:1, :], mod_ref[0, 1:2, :]).astype(BF16)

    def proj(lo, hi):
        return jnp.dot(h, w_ref[0, :, lo:hi], preferred_element_type=F32)

    qa = proj(W_QA, W_QB)
    kva = proj(W_KVA, W_KVB)
    if rope:
        cos, sa, sb = cos_ref[...], sa_ref[...], sb_ref[...]

        def rot(v):
            up = pltpu.roll(v, LANES - 16, axis=1)
            dn = pltpu.roll(v, 16, axis=1)
            return v * cos + up * sa + dn * sb

        oa_ref[0, :, 0:LANES] = rot(qa[:, 0:LANES]).astype(BF16)
        oa_ref[0, :, LANES:2 * LANES] = rot(qa[:, LANES:]).astype(BF16)
        oa_ref[0, :, 2 * LANES:3 * LANES] = rot(kva[:, 0:LANES]).astype(BF16)
        oa_ref[0, :, 3 * LANES:] = kva[:, LANES:].astype(BF16)
    else:
        oa_ref[0, :, 0:2 * LANES] = qa.astype(BF16)
        oa_ref[0, :, 2 * LANES:] = kva.astype(BF16)
    ob_ref[0, :, 0:2 * LANES] = proj(W_QB, W_Z).astype(BF16)
    ob_ref[0, :, 2 * LANES:] = proj(W_KVB, W_XBC).astype(BF16)
    oxz_ref[0, :, 0:XBC_COLS] = proj(W_XBC, W_DT)
    oxz_ref[0, :, XBC_COLS:] = proj(W_Z, W_KVA)
    odt_ref[0] = jnp.dot(h, wdt_ref[...], preferred_element_type=F32)


def _inproj(x, mods, g, w, layer, wdt, rope_tabs, tm):
    b, t, _ = x.shape
    rope = rope_tabs is not None
    per_batch = mods.shape[0] > 1
    in_specs = [
        pl.BlockSpec((1, tm, D_MODEL), lambda i, j: (i, j, 0)),
        pl.BlockSpec((1, SUBLANES, D_MODEL), (lambda i, j: (i, 0, 0)) if per_batch else (lambda i, j: (0, 0, 0))),
        pl.BlockSpec((1, D_MODEL), lambda i, j: (0, 0)),
        pl.BlockSpec((1, D_MODEL, IN_COLS), lambda i, j: (layer, 0, 0)),
        pl.BlockSpec((D_MODEL, DT_PAD), lambda i, j: (0, 0)),
    ]
    args = [x, mods, g, w, wdt]
    if rope:
        in_specs += [pl.BlockSpec((tm, LANES), lambda i, j: (j, 0))] * 3
        args += list(rope_tabs)
    widths = (A_COLS, B_COLS, XZ_COLS, DT_PAD)
    return pl.pallas_call(
        functools.partial(_inproj_kernel, rope=rope),
        out_shape=[jax.ShapeDtypeStruct((b, t, n), dt) for n, dt in zip(widths, (BF16, BF16, F32, F32))],
        grid=(b, t // tm),
        in_specs=in_specs,
        out_specs=[pl.BlockSpec((1, tm, n), lambda i, j: (i, j, 0)) for n in widths],
        compiler_params=_cparams(("parallel", "parallel")),
        name="inproj_rope" if rope else "inproj",
    )(*args)


WA_KEYS = 3 * WA_BLOCK


WA_BLOCKS_PER_STEP = 4


def _attn_a_kernel(sink_ref, q_ref, k_ref, v_ref, kc_ref, vc_ref, o_ref, s_ref, p_ref):
    step = pl.program_id(1)
    nk = WA_KEYS + CTX_LEN
    nq = WA_HEADS * WA_BLOCK
    kc, vc = kc_ref[0], vc_ref[0]
    starts = []
    for bb in range(WA_BLOCKS_PER_STEP):
        n = step * WA_BLOCKS_PER_STEP + bb
        starts.append(pl.multiple_of(jnp.clip((n - 1) * WA_BLOCK, 0, SEQ - WA_KEYS), WA_BLOCK))

    first_half = lax.broadcasted_iota(jnp.int32, (WA_BLOCK, LANES), 1) < HEAD_DIM
    swap = lambda v: pltpu.roll(v, HEAD_DIM, axis=1)

    for bb in range(WA_BLOCKS_PER_STEP):
        q = q_ref[0, bb * WA_BLOCK:(bb + 1) * WA_BLOCK, :].astype(F32) * ATT_SCALE
        q01, q23 = q[:, 0:LANES], q[:, LANES:]
        lhs = jnp.concatenate([
            jnp.where(first_half, q01, 0.0), jnp.where(first_half, swap(q01), 0.0),
            jnp.where(first_half, 0.0, swap(q23)), jnp.where(first_half, 0.0, q23)], axis=0)
        kall = jnp.concatenate([k_ref[0, pl.ds(starts[bb], WA_KEYS), :], kc], axis=0)
        s_ref[bb * nq:(bb + 1) * nq, :] = _bdot_nt(lhs, kall)

    nr = WA_BLOCKS_PER_STEP * nq
    rows = lax.broadcasted_iota(jnp.int32, (nr, nk), 0)
    cols = lax.broadcasted_iota(jnp.int32, (nr, nk), 1)
    row1 = lax.broadcasted_iota(jnp.int32, (nr, 1), 0)
    blk = row1 // nq
    qpos = (step * WA_BLOCKS_PER_STEP + blk) * WA_BLOCK + (rows & (WA_BLOCK - 1))
    kstart = starts[-1]
    for bb in range(WA_BLOCKS_PER_STEP - 2, -1, -1):
        kstart = jnp.where(blk == bb, starts[bb], kstart)
    valid = (cols >= WA_KEYS) | (jnp.abs(qpos - (kstart + cols)) <= WA_WINDOW)
    s = jnp.where(valid, s_ref[...], NEG_INF)
    head = (row1 // WA_BLOCK) % WA_HEADS
    sink = jnp.where(head == 0, sink_ref[0],
                     jnp.where(head == 1, sink_ref[1], jnp.where(head == 2, sink_ref[2], sink_ref[3])))
    m = jnp.maximum(jnp.max(s, axis=1, keepdims=True), sink)
    p = jnp.exp(s - m)
    inv = 1.0 / (jnp.sum(p, axis=1, keepdims=True) + jnp.exp(sink - m))
    p_ref[...] = p.astype(BF16)

    for bb in range(WA_BLOCKS_PER_STEP):
        vall = jnp.concatenate([v_ref[0, pl.ds(starts[bb], WA_KEYS), :], vc], axis=0)
        rs = slice(bb * nq, (bb + 1) * nq)
        o = jnp.dot(p_ref[rs, :], vall, preferred_element_type=F32) * inv[rs]
        o0, o1, o2, o3 = (o[h * WA_BLOCK:(h + 1) * WA_BLOCK] for h in range(WA_HEADS))
        qs = slice(bb * WA_BLOCK, (bb + 1) * WA_BLOCK)
        o_ref[0, qs, 0:LANES] = jnp.where(first_half, o0, swap(o1)).astype(o_ref.dtype)
        o_ref[0, qs, LANES:] = jnp.where(first_half, swap(o2), o3).astype(o_ref.dtype)


def _attn_a(sink, qkv, qkv_c):
    b = qkv.shape[0]
    nk = WA_KEYS + CTX_LEN
    tq = WA_BLOCKS_PER_STEP * WA_BLOCK
    nr = WA_BLOCKS_PER_STEP * WA_HEADS * WA_BLOCK
    return pl.pallas_call(
        _attn_a_kernel,
        out_shape=jax.ShapeDtypeStruct((b, SEQ, WA_HEADS * HEAD_DIM), BF16),
        scratch_shapes=[pltpu.VMEM((nr, nk), F32), pltpu.VMEM((nr, nk), BF16)],
        grid=(b, SEQ // tq),
        in_specs=[
            pl.BlockSpec(memory_space=pltpu.SMEM),
            pl.BlockSpec((1, tq, 2 * LANES), lambda i, j: (i, j, 0)),
            pl.BlockSpec((1, SEQ, LANES), lambda i, j: (i, 0, 2)),
            pl.BlockSpec((1, SEQ, LANES), lambda i, j: (i, 0, 3)),
            pl.BlockSpec((1, CTX_LEN, LANES), lambda i, j: (i, 0, 2)),
            pl.BlockSpec((1, CTX_LEN, LANES), lambda i, j: (i, 0, 3)),
        ],
        out_specs=pl.BlockSpec((1, tq, 2 * LANES), lambda i, j: (i, j, 0)),
        compiler_params=_cparams(("parallel", "arbitrary")),
        name="attn_window",
    )(sink, qkv, qkv, qkv, qkv_c, qkv_c)


NB_ROWS_PER_STEP = 8
NB_LOC = NA_KH * GRID_W
NB_DY_PAIRS = 2 * NA_KH - 2


def _attn_b_kernel(q_ref, k_ref, v_ref, kc_ref, vc_ref, t_ref, o_ref, s_ref, p_ref):
    i = pl.program_id(1)
    kc = kc_ref[0]
    vc = vc_ref[0]
    units = [(rr, pp) for rr in range(NB_ROWS_PER_STEP) for pp in range(NA_HEADS // 2)]
    starts, shifts = [], []
    for rr in range(NB_ROWS_PER_STEP):
        r = i * NB_ROWS_PER_STEP + rr
        rs = jnp.clip(r - NA_KH // 2, 0, GRID_ROWS - NA_KH)
        shifts.append(r - rs)
        starts.append(pl.multiple_of(rs * GRID_W, GRID_W))
    first_half = lax.broadcasted_iota(jnp.int32, (GRID_W, LANES), 1) < HEAD_DIM
    zero = jnp.zeros((), BF16)
    pw = 2 * GRID_W

    for u, (rr, pp) in enumerate(units):
        ls = slice(pp * LANES, (pp + 1) * LANES)
        if pp == 0:
            kw = k_ref[0, pl.ds(starts[rr], NB_LOC), :]
            q = q_ref[0, rr * GRID_W:(rr + 1) * GRID_W, :] * ATT_SCALE
        qs = q[:, ls]
        lhs = jnp.concatenate([jnp.where(first_half, qs, zero), jnp.where(first_half, zero, qs)], axis=0)
        bias = jnp.concatenate(
            [jnp.concatenate([t_ref[h, 2 * k - shifts[rr] + NA_KH - 1] for k in range(NA_KH // 2)], axis=1)
             for h in (2 * pp, 2 * pp + 1)], axis=0)
        s_ref[u * pw:(u + 1) * pw, 0:NB_LOC] = _bdot_nt(lhs, kw[:, ls]) + bias
        s_ref[u * pw:(u + 1) * pw, NB_LOC:] = _bdot_nt(lhs, kc[:, ls])

    s = s_ref[...]
    p = jnp.exp(s - jnp.max(s, axis=1, keepdims=True))
    inv = 1.0 / jnp.sum(p, axis=1, keepdims=True)
    p_ref[...] = p.astype(BF16)

    for u, (rr, pp) in enumerate(units):
        ls = slice(pp * LANES, (pp + 1) * LANES)
        if pp == 0:
            vw = v_ref[0, pl.ds(starts[rr], NB_LOC), :]
        rows = slice(u * pw, (u + 1) * pw)
        o = (jnp.dot(p_ref[rows, 0:NB_LOC], vw[:, ls], preferred_element_type=F32)
             + jnp.dot(p_ref[rows, NB_LOC:], vc[:, ls], preferred_element_type=F32)) * inv[rows]
        o_ref[0, rr * GRID_W:(rr + 1) * GRID_W, ls] = jnp.where(
            first_half, o[:GRID_W], o[GRID_W:]).astype(o_ref.dtype)


def _attn_b(qkv, qkv_c, table):
    b = qkv.shape[0]
    tq = NB_ROWS_PER_STEP * GRID_W
    w = NA_HEADS * HEAD_DIM
    nu = NB_ROWS_PER_STEP * NA_HEADS * GRID_W
    return pl.pallas_call(
        _attn_b_kernel,
        out_shape=jax.ShapeDtypeStruct((b, SEQ, w), BF16),
        scratch_shapes=[pltpu.VMEM((nu, NB_LOC + CTX_LEN), F32), pltpu.VMEM((nu, NB_LOC + CTX_LEN), BF16)],
        grid=(b, SEQ // tq),
        in_specs=[
            pl.BlockSpec((1, tq, w), lambda i, j: (i, j, 0)),
            pl.BlockSpec((1, SEQ, w), lambda i, j: (i, 0, 1)),
            pl.BlockSpec((1, SEQ, w), lambda i, j: (i, 0, 2)),
            pl.BlockSpec((1, CTX_LEN, w), lambda i, j: (i, 0, 1)),
            pl.BlockSpec((1, CTX_LEN, w), lambda i, j: (i, 0, 2)),
            pl.BlockSpec((NA_HEADS, NB_DY_PAIRS, GRID_W, LANES), lambda i, j: (0, 0, 0, 0)),
        ],
        out_specs=pl.BlockSpec((1, tq, w), lambda i, j: (i, j, 0)),
        compiler_params=_cparams(("parallel", "arbitrary")),
        name="attn_neighbourhood",
    )(qkv, qkv, qkv, qkv_c, qkv_c, table)


def _split3(a):
    a1 = a.astype(BF16)
    r1 = a - a1.astype(F32)
    a2 = r1.astype(BF16)
    a3 = (r1 - a2.astype(F32)).astype(BF16)
    return a1, a2, a3


def _bias_kernel(r_ref, oh_ref, o_ref):
    oh = oh_ref[...]
    o_ref[...] = sum(jnp.dot(t, oh, preferred_element_type=F32) for t in _split3(r_ref[...]))


def _na_bias_tables(rpb):
    ndy, ndx = 2 * NA_KH - 1, 2 * NA_KW - 1
    qc = np.arange(GRID_W)[:, None]
    x = np.arange(GRID_W)[None, :]
    dx = np.clip(x - qc, -(NA_KW - 1), NA_KW - 1) + NA_KW - 1
    onehot = (np.arange(LANES)[:, None, None] == dx[None]).reshape(LANES, GRID_W * GRID_W)
    cstart = np.clip(qc - NA_KW // 2, 0, GRID_W - NA_KW)
    inside = (x >= cstart) & (x < cstart + NA_KW)
    rows = DEPTH * NA_HEADS * ndy
    r = jnp.pad(rpb.astype(F32).reshape(rows, ndx), ((0, LANES - rows), (0, LANES - ndx)))
    m = pl.pallas_call(
        _bias_kernel,
        out_shape=jax.ShapeDtypeStruct((LANES, GRID_W * GRID_W), F32),
        name="na_bias_expand",
    )(r, jnp.asarray(onehot, BF16))
    m = m[:rows].reshape(DEPTH, NA_HEADS, ndy, GRID_W, GRID_W)
    m = jnp.where(jnp.asarray(inside), m, NEG_INF)
    return jnp.concatenate([m[:, :, :ndy - 1], m[:, :, 1:]], axis=-1)


def _ctx_attn_kernel(sink_ref, a_ref, b_ref, oa_ref, ob_ref):
    a = a_ref[0]
    row1 = lax.broadcasted_iota(jnp.int32, (2 * CTX_LEN, 1), 0)
    for g in range(WA_KV_HEADS):
        h0, h1 = 2 * g, 2 * g + 1
        q2 = jnp.concatenate([a[:, h0 * HEAD_DIM:(h0 + 1) * HEAD_DIM],
                              a[:, h1 * HEAD_DIM:(h1 + 1) * HEAD_DIM]], axis=0)
        k = a[:, 2 * LANES + g * HEAD_DIM:2 * LANES + (g + 1) * HEAD_DIM]
        v = a[:, 3 * LANES + g * HEAD_DIM:3 * LANES + (g + 1) * HEAD_DIM]
        s = _bdot_nt(q2, k) * ATT_SCALE
        sink = jnp.where(row1 < CTX_LEN, sink_ref[h0], sink_ref[h1])
        m = jnp.maximum(jnp.max(s, axis=1, keepdims=True), sink)
        p = jnp.exp(s - m)
        den = jnp.sum(p, axis=1, keepdims=True) + jnp.exp(sink - m)
        o = _bdot(p, v) / den
        oa_ref[0, :, h0 * HEAD_DIM:(h0 + 1) * HEAD_DIM] = o[:CTX_LEN]
        oa_ref[0, :, h1 * HEAD_DIM:(h1 + 1) * HEAD_DIM] = o[CTX_LEN:]
    bq = b_ref[0]
    w = NA_HEADS * HEAD_DIM
    for h in range(NA_HEADS):
        hs = slice(h * HEAD_DIM, (h + 1) * HEAD_DIM)
        s = _bdot_nt(bq[:, hs], bq[:, w + h * HEAD_DIM:w + (h + 1) * HEAD_DIM]) * ATT_SCALE
        m = jnp.max(s, axis=1, keepdims=True)
        p = jnp.exp(s - m)
        den = jnp.sum(p, axis=1, keepdims=True)
        ob_ref[0, :, hs] = _bdot(p, bq[:, 2 * w + h * HEAD_DIM:2 * w + (h + 1) * HEAD_DIM]) / den


def _ctx_attn(sink, qkv_a_c, qkv_b_c):
    b = qkv_a_c.shape[0]
    w = 2 * LANES
    return pl.pallas_call(
        _ctx_attn_kernel,
        out_shape=[jax.ShapeDtypeStruct((b, CTX_LEN, w), F32)] * 2,
        grid=(b,),
        in_specs=[
            pl.BlockSpec(memory_space=pltpu.SMEM),
            pl.BlockSpec((1, CTX_LEN, A_COLS), lambda i: (i, 0, 0)),
            pl.BlockSpec((1, CTX_LEN, B_COLS), lambda i: (i, 0, 0)),
        ],
        out_specs=[pl.BlockSpec((1, CTX_LEN, w), lambda i: (i, 0, 0))] * 2,
        compiler_params=_cparams(("parallel",)),
        name="attn_context",
    )(sink, qkv_a_c, qkv_b_c)


CONV_HALO = SUBLANES


def _conv_kernel(prev_ref, cur_ref, next_ref, w_ref, b_ref, o_ref, ext_ref, *, tl, nt):
    j = pl.program_id(1)
    ext_ref[0:CONV_HALO, :] = jnp.where(j > 0, prev_ref[0], 0.0)
    ext_ref[CONV_HALO:CONV_HALO + tl, :] = cur_ref[0]
    ext_ref[CONV_HALO + tl:, :] = jnp.where(j < nt - 1, next_ref[0], 0.0)
    acc = jnp.zeros((tl, XBC_COLS), F32) + b_ref[...]
    base = CONV_HALO - SSM_CONV // 2
    for k in range(SSM_CONV):
        acc = acc + w_ref[k:k + 1, :] * ext_ref[base + k:base + k + tl, :]
    o_ref[0] = _silu(acc)


def _conv_silu(xz, conv_w, conv_b, tl):
    b, t, _ = xz.shape
    nt = t // tl
    hb = tl // CONV_HALO
    last = t // CONV_HALO - 1
    return pl.pallas_call(
        functools.partial(_conv_kernel, tl=tl, nt=nt),
        out_shape=jax.ShapeDtypeStruct((b, t, XBC_COLS), F32),
        grid=(b, nt),
        in_specs=[
            pl.BlockSpec((1, CONV_HALO, XBC_COLS), lambda i, j: (i, jnp.maximum(j * hb - 1, 0), 0)),
            pl.BlockSpec((1, tl, XBC_COLS), lambda i, j: (i, j, 0)),
            pl.BlockSpec((1, CONV_HALO, XBC_COLS), lambda i, j: (i, jnp.minimum((j + 1) * hb, last), 0)),
            pl.BlockSpec((SUBLANES, XBC_COLS), lambda i, j: (0, 0)),
            pl.BlockSpec((1, XBC_COLS), lambda i, j: (0, 0)),
        ],
        out_specs=pl.BlockSpec((1, tl, XBC_COLS), lambda i, j: (i, j, 0)),
        scratch_shapes=[pltpu.VMEM((tl + 2 * CONV_HALO, XBC_COLS), F32)],
        compiler_params=_cparams(("parallel", "parallel")),
        name="ssm_conv",
    )(xz, xz, xz, conv_w, conv_b)


Q = SSM_CHUNK
GS = SSM_GROUPS * SSM_STATE
HPG = SSM_HEADS // SSM_GROUPS
SSD_STATE_SHAPE = (SSM_GROUPS, SSM_STATE, HPG * SSM_HEAD_DIM)
SSD_BATCH_PER_STEP = 4


def _softplus(v):
    return jnp.maximum(v, 0.0) + jnp.log1p(jnp.exp(-jnp.abs(v)))


def _cumsum_mat(tri, a):
    r = jnp.dot(tri, jnp.concatenate(_split3(a), axis=1), preferred_element_type=F32)
    n = a.shape[1]
    return r[:, 0:n] + r[:, n:2 * n] + r[:, 2 * n:]


def _ssd_kernel(xm_ref, dtm_ref, xb_ref, dtb_ref, bias_ref, alog_ref, h0f_ref, h0b_ref,
                y1_ref, y2_ref, hf_out_ref, hb_out_ref, hf_ref, hb_ref, *, nc):
    i = pl.program_id(1)

    @pl.when(i == 0)
    def _():
        hf_ref[...] = h0f_ref[...]
        hb_ref[...] = h0b_ref[...]

    ii = lax.broadcasted_iota(jnp.int32, (Q, Q), 0)
    jj = lax.broadcasted_iota(jnp.int32, (Q, Q), 1)
    lower = ii >= jj
    diag = ii == jj
    tril =jnp.where(lower, 1.0, 0.0).astype(BF16)
    triu = jnp.where(ii <= jj, 1.0, 0.0).astype(BF16)
    first_half = lax.broadcasted_iota(jnp.int32, (Q, LANES), 1) < SSM_HEAD_DIM
    a_row = -jnp.exp(alog_ref[...])
    gw = HPG * SSM_HEAD_DIM

    def group_operands(xbc, g):
        bg = xbc[:, SSM_INNER + g * SSM_STATE:SSM_INNER + (g + 1) * SSM_STATE]
        cg = xbc[:, SSM_INNER + GS + g * SSM_STATE:SSM_INNER + GS + (g + 1) * SSM_STATE]
        return bg, cg.astype(BF16), xbc[:, g * gw:(g + 1) * gw]

    nbs = range(SSD_BATCH_PER_STEP)
    groups = range(SSM_GROUPS)
    bias = bias_ref[...]
    zero = jnp.zeros((), F32)


    xm = [xm_ref[bi] for bi in nbs]
    xb = [xb_ref[bi] for bi in nbs]
    dtm = [_softplus(dtm_ref[bi] + bias) for bi in nbs]
    dtb = [_softplus(dtb_ref[bi] + bias) for bi in nbs]
    cumf =[_cumsum_mat(tril, d * a_row) for d in dtm]
    cumr = [_cumsum_mat(triu, d * a_row) for d in dtm]
    cumb = [_cumsum_mat(triu, d * a_row) for d in dtb]

    opm = [[group_operands(xm[bi], g) for g in groups] for bi in nbs]
    opb = [[group_operands(xb[bi], g) for g in groups] for bi in nbs]
    gmat = [[_bdot_nt(opm[bi][g][1], opm[bi][g][0]) for g in groups] for bi in nbs]
    hf_prev = [[hf_ref[bi, g] for g in groups] for bi in nbs]
    hb_prev = [[hb_ref[bi, g] for g in groups] for bi in nbs]
    inter_f = [[jnp.dot(opm[bi][g][1], hf_prev[bi][g].astype(BF16), preferred_element_type=F32)
                for g in groups] for bi in nbs]
    inter_b = [[jnp.dot(opb[bi][g][1], hb_prev[bi][g].astype(BF16), preferred_element_type=F32)
                for g in groups] for bi in nbs]
    bt_m = [[opm[bi][g][0].T for g in groups] for bi in nbs]
    bt_b = [[opb[bi][g][0].T for g in groups] for bi in nbs]

    cumf_t = [c.T for c in cumf]
    cumr_t = [c.T for c in cumr]
    dtm_t = [d.T for d in dtm]
    cumb_t = [c.T for c in cumb]
    dtb_t = [d.T for d in dtb]

    heads = [(bi, h) for bi in nbs for h in range(SSM_HEADS)]
    hb_ = lambda h: SSM_HEADS + h
    cf, cr, cb, ein_f, ein_b, wm, wst_f, wst_b = {}, {}, {}, {}, {}, {}, {}, {}
    for t in range(len(heads) + 1):
        if t < len(heads):
            bi, h = k = heads[t]
            cf[k] = jnp.broadcast_to(cumf[bi][:, h:h + 1], (Q, Q))
            cr[k] = jnp.broadcast_to(cumr[bi][:, hb_(h):hb_(h) + 1], (Q, Q))
            cb[k] = jnp.broadcast_to(cumb[bi][:, hb_(h):hb_(h) + 1], (Q, Q))
        if t >= 1:
            bi, h = k = heads[t - 1]
            rf = cumf_t[bi][h:h + 1, :]
            rb = cumb_t[bi][hb_(h):hb_(h) + 1, :]
            dtf_row = dtm_t[bi][h:h + 1, :]
            dtb_row = dtm_t[bi][hb_(h):hb_(h) + 1, :]
            arg = jnp.where(lower, cf[k] - rf, cr[k] - cumr_t[bi][hb_(h):hb_(h) + 1, :])
            wm[k] = (gmat[bi][h // HPG] * (jnp.exp(arg) * jnp.where(lower, dtf_row, dtb_row)
                                           + jnp.where(diag, dtb_row, zero))).astype(BF16)
            ein_f[k] = jnp.exp(cf[k])
            ein_b[k] = jnp.exp(cb[k])
            w_f = jnp.exp(rf[:, Q - 1:Q] - rf) * dtf_row
            w_b = jnp.exp(rb[:, 0:1] - rb) * dtb_t[bi][hb_(h):hb_(h) + 1, :]
            wst_f[k] = (bt_m[bi][h // HPG] * w_f).astype(BF16)
            wst_b[k] = (bt_b[bi][h // HPG] * w_b).astype(BF16)

    def pair_rhs(x, pr):
        xp = x[:, pr * LANES:(pr + 1) * LANES]
        return jnp.concatenate([jnp.where(first_half, xp, zero), jnp.where(first_half, zero, xp)],
                               axis=0).astype(BF16)

    def pair_dot(mats, bi, pr, rhs):
        lhs = jnp.concatenate([mats[bi, 2 * pr], mats[bi, 2 * pr + 1]], axis=1)
        return jnp.dot(lhs, rhs, preferred_element_type=F32)

    pairs = [(bi, pr) for bi in nbs for pr in range(SSM_HEADS // 2)]
    rhs_m = {k: pair_rhs(xm[k[0]], k[1]) for k in pairs}
    rhs_b = {k: pair_rhs(xb[k[0]], k[1]) for k in pairs}
    y_intra = {k: pair_dot(wm, k[0], k[1], rhs_m[k]) for k in pairs}
    st_f = {k: pair_dot(wst_f, k[0], k[1], rhs_m[k]) for k in pairs}
    st_b = {k: pair_dot(wst_b, k[0], k[1], rhs_b[k]) for k in pairs}

    ppg = HPG // 2
    for bi, pr in pairs:
        g, k = pr // ppg, pr % ppg
        ls = slice(k * LANES, (k + 1) * LANES)
        sin_f = jnp.where(first_half, ein_f[bi, 2 * pr], ein_f[bi, 2 * pr + 1])
        sin_b = jnp.where(first_half, ein_b[bi, 2 * pr], ein_b[bi, 2 * pr + 1])
        y1_ref[bi, :, pr * LANES:(pr + 1) * LANES] = y_intra[bi, pr] + sin_f * inter_f[bi][g][:, ls]
        y2_ref[bi, :, pr * LANES:(pr + 1) * LANES] = sin_b * inter_b[bi][g][:, ls]
        hf_ref[bi, g, :, ls] = hf_prev[bi][g][:, ls] * sin_f[Q - 1:Q, :] + st_f[bi, pr]
        hb_ref[bi, g, :, ls] = hb_prev[bi][g][:, ls] * sin_b[0:1, :] + st_b[bi, pr]

    @pl.when(i == nc - 1)
    def _():
        hf_out_ref[...] = hf_ref[...]
        hb_out_ref[...] = hb_ref[...]


def _ssd(xbc, dt_raw, dt_bias, a_log, h0f, h0b):
    b, t, _ = xbc.shape
    nc = t // Q
    nb = SSD_BATCH_PER_STEP
    st_shape = (b,) + SSD_STATE_SHAPE
    st_spec = pl.BlockSpec((nb,) + SSD_STATE_SHAPE, lambda i, j: (i, 0, 0, 0))
    fwd = lambda i, j: (i, j, 0)
    bwd = lambda i, j: (i, nc - 1 - j, 0)
    return pl.pallas_call(
        functools.partial(_ssd_kernel, nc=nc),
        out_shape=[jax.ShapeDtypeStruct((b, t, SSM_INNER), F32)] * 2
        + [jax.ShapeDtypeStruct(st_shape, F32)] * 2,
        grid=(b // nb, nc),
        in_specs=[
            pl.BlockSpec((nb, Q, XBC_COLS), fwd),
            pl.BlockSpec((nb, Q, DT_PAD), fwd),
            pl.BlockSpec((nb, Q, XBC_COLS), bwd),
            pl.BlockSpec((nb, Q, DT_PAD), bwd),
            pl.BlockSpec((1, DT_PAD), lambda i, j: (0, 0)),
            pl.BlockSpec((1, DT_PAD), lambda i, j: (0, 0)),
            st_spec, st_spec,
        ],
        out_specs=[pl.BlockSpec((nb, Q, SSM_INNER), fwd), pl.BlockSpec((nb, Q, SSM_INNER), bwd),
                   st_spec, st_spec],
        scratch_shapes=[pltpu.VMEM((nb,) + SSD_STATE_SHAPE, F32)] * 2,
        compiler_params=_cparams(("parallel", "arbitrary")),
        name="ssd_scan",
    )(xbc, dt_raw, xbc, dt_raw, dt_bias, a_log, h0f, h0b)


MXU_TILE = 256
FFN_SPLITS = (0, 6 * MXU_TILE, D_FF)


def _mix_ffn_kernel(x_ref, mod_ref, oa_ref, ob_ref, y1_ref, y2_ref, xs_ref, z_ref, d_ref, sg_ref,
                    wo_ref, g_ref, wfi_ref, wfo_ref, gfin_ref, o_ref, *, final):
    y = y1_ref[0] + y2_ref[0] + d_ref[...] * xs_ref[0]
    y = y * _silu(z_ref[0])
    var = jnp.mean(y * y, axis=-1, keepdims=True)
    oc = (y * lax.rsqrt(var + EPS) * sg_ref[...]).astype(BF16)
    wa = 2 * LANES
    mix = (jnp.dot(oa_ref[0].astype(BF16), wo_ref[0, 0:wa, :], preferred_element_type=F32)
           + jnp.dot(ob_ref[0].astype(BF16), wo_ref[0, wa:2 * wa, :], preferred_element_type=F32)
           + jnp.dot(oc, wo_ref[0, 2 * wa:, :], preferred_element_type=F32))
    xn = x_ref[0] + mod_ref[0, 2:3, :] * mix
    h = _norm_mod(xn, g_ref[...], mod_ref[0, 3:4, :], mod_ref[0, 4:5, :]).astype(BF16)
    acc = None
    for lo, hi in zip(FFN_SPLITS[:-1], FFN_SPLITS[1:]):
        gate = jnp.dot(h, wfi_ref[0, :, lo:hi], preferred_element_type=F32)
        up = jnp.dot(h, wfi_ref[0, :, D_FF + lo:D_FF + hi], preferred_element_type=F32)
        act = (_silu(gate) * up).astype(BF16)
        part = jnp.dot(act, wfo_ref[0, lo:hi, :], preferred_element_type=F32)
        acc = part if acc is None else acc + part
    out = xn + mod_ref[0, 5:6, :] * acc
    if final:
        var = jnp.mean(out * out, axis=-1, keepdims=True)
        out = out * lax.rsqrt(var + EPS) * gfin_ref[...]
    o_ref[0] = out


def _mix_ffn(x, mods, o_a, o_b, y1, y2, xbc, xz, d_exp, ssm_g, w_out, g_ffn, w_ffn_in, w_ffn_out,
             g_final, layer, tm, final):
    b, t, _ = x.shape
    per_batch = mods.shape[0] > 1
    row = lambda n: pl.BlockSpec((1, tm, n), lambda i, j: (i, j, 0))
    const = lambda shape: pl.BlockSpec(shape, lambda i, j: (0, 0), pipeline_mode=pl.Buffered(1))
    weight = lambda r, c: pl.BlockSpec((1, r, c), lambda i, j: (layer, 0, 0), pipeline_mode=pl.Buffered(1))
    return pl.pallas_call(
        functools.partial(_mix_ffn_kernel, final=final),
        out_shape=jax.ShapeDtypeStruct((b, t, D_MODEL), F32),
        grid=(b, t // tm),
        in_specs=[
            row(D_MODEL),
            pl.BlockSpec((1, SUBLANES, D_MODEL), (lambda i, j: (i, 0, 0)) if per_batch else (lambda i, j: (0, 0, 0))),
            row(2 * LANES), row(2 * LANES), row(SSM_INNER), row(SSM_INNER),
            pl.BlockSpec((1, tm, SSM_INNER), lambda i, j: (i, j, 0)),
            pl.BlockSpec((1, tm, SSM_INNER), lambda i, j: (i, j, 2)),
            const((1, SSM_INNER)), const((1, SSM_INNER)),
            weight(D_MODEL, D_MODEL),
            const((1, D_MODEL)),
            weight(D_MODEL, 2 * D_FF),
            weight(D_FF, D_MODEL),
            const((1, D_MODEL)),
        ],
        out_specs=row(D_MODEL),
        compiler_params=_cparams(("parallel", "parallel")),
        name="mix_ffn_final" if final else "mix_ffn",
    )(x, mods, o_a, o_b, y1, y2, xbc, xz, d_exp, ssm_g, w_out, g_ffn, w_ffn_in, w_ffn_out, g_final)


def _rope_tables():
    t = np.arange(SEQ)
    pos = np.stack([t // GRID_W, t % GRID_W], axis=1).astype(np.float64)
    quarter = HEAD_DIM // 4
    inv_freq = ROPE_BASE ** (-np.arange(quarter, dtype=np.float64) / quarter)
    lane = np.arange(LANES) % HEAD_DIM
    half = lane // (HEAD_DIM // 2)
    idx = lane % (HEAD_DIM // 2)
    ang = pos[:, half] * inv_freq[idx % quarter][None, :]
    cos, sin = np.cos(ang), np.sin(ang)
    first = (idx < quarter)[None, :]
    tabs = (cos, np.where(first, -sin, 0.0), np.where(first, 0.0, sin))
    return tuple(jnp.asarray(v, F32) for v in tabs)


def _pad_lanes(v, n=LANES):
    v = v.reshape(1, -1)
    return jnp.pad(v, ((0, 0), (0, n - v.shape[1])))


def kernel(x, c, ctx, c_ctx, w_mod, b_mod, g_mix, w_in, wa_sink, na_rpb, ssm_conv_w, ssm_conv_b,
           ssm_dt_bias, ssm_a_log, ssm_d, ssm_norm_g, w_out, g_ffn, w_ffn_in, w_ffn_out, g_final):
    cin = jnp.concatenate([c, c_ctx[None, :], jnp.zeros((SUBLANES - BATCH - 1, D_MODEL), F32)], axis=0)
    mod_all = _modulation(cin, w_mod, b_mod)
    rope_tabs = _rope_tables()
    bias_tabs = _na_bias_tables(na_rpb)
    zeros_state = jnp.zeros((BATCH,) + SSD_STATE_SHAPE, F32)
    gfin = g_final.reshape(1, D_MODEL)
    w_proj = w_in.astype(BF16)
    wo = w_out.astype(BF16)
    wfi = w_ffn_in.astype(BF16)
    wfo = w_ffn_out.astype(BF16)

    xl, xc = x, ctx
    for l in range(DEPTH):
        last = l == DEPTH - 1
        m6 = mod_all[l].reshape(SUBLANES, 6, D_MODEL)
        mods_l = jnp.pad(m6[:BATCH], ((0, 0), (0, 2), (0, 0)))
        mods_c = jnp.pad(m6[BATCH:BATCH + 1], ((0, 0), (0, 2), (0, 0)))
        w_dt = jnp.pad(w_in[l][:, W_DT:], ((0, 0), (0, DT_PAD - DT_COLS))).astype(BF16)
        g1 = g_mix[l].reshape(1, D_MODEL)
        g2 = g_ffn[l].reshape(1, D_MODEL)
        conv_w = jnp.pad(ssm_conv_w[l], ((0, SUBLANES - SSM_CONV), (0, 0)))
        conv_b = ssm_conv_b[l].reshape(1, XBC_COLS)
        dt_bias = _pad_lanes(ssm_dt_bias[l])
        a_log = _pad_lanes(ssm_a_log[l])
        d_exp = jnp.repeat(ssm_d[l], SSM_HEAD_DIM).reshape(1, SSM_INNER)
        sg = ssm_norm_g[l].reshape(1, SSM_INNER)

        a_c, b_c, xz_c, dt_c = _inproj(xc, mods_c, g1, w_proj, l, w_dt, None, CTX_LEN)
        a_l, b_l, xz_l, dt_l = _inproj(xl, mods_l, g1, w_proj, l, w_dt, rope_tabs, TM_INPROJ)

        o_a = _attn_a(wa_sink[l], a_l, a_c)
        o_b = _attn_b(b_l, b_c, bias_tabs[l])

        xbc_c = _conv_silu(xz_c, conv_w, conv_b, CTX_LEN)
        xbc_l = _conv_silu(xz_l, conv_w, conv_b, TM_CONV)
        y1_c, y2_c, h_f, h_b = _ssd(xbc_c, dt_c, dt_bias, a_log, zeros_state, zeros_state)
        y1_l, y2_l, _, _ = _ssd(xbc_l, dt_l, dt_bias, a_log, h_f, h_b)

        xl = _mix_ffn(xl, mods_l, o_a, o_b, y1_l, y2_l, xbc_l, xz_l, d_exp, sg, wo, g2, wfi, wfo, gfin,
                      l, TM_MIX, last)
        if not last:
            o_ac, o_bc = _ctx_attn(wa_sink[l], a_c, b_c)
            xc = _mix_ffn(xc, mods_c, o_ac, o_bc, y1_c, y2_c, xbc_c, xz_c, d_exp, sg, wo, g2, wfi, wfo,
                          gfin, l, CTX_LEN, False)
    return xl
```

```python
import functools
import math

import numpy as np
import jax
import jax.numpy as jnp
from jax import lax
from jax.experimental import pallas as pl
from jax.experimental.pallas import tpu as pltpu

F32 = jnp.float32
BF16 = jnp.bfloat16

D_MODEL = 1024
BATCH = 4
SEQ = 4096
DEPTH = 2
GRID_W = 64
GRID_ROWS = SEQ // GRID_W
CTX_LEN = 256
EPS = 1e-6
HEAD_DIM = 64
ROPE_BASE = 10000.0
WA_HEADS = 4
WA_KV_HEADS = 2
WA_WINDOW = 128
WA_BLOCK = 128
NA_HEADS = 4
NA_KH = 8
NA_KW = 16
SSM_HEADS = 8
SSM_HEAD_DIM = 64
SSM_INNER = SSM_HEADS * SSM_HEAD_DIM
SSM_GROUPS = 2
SSM_STATE = 128
SSM_CONV = 7
SSM_CHUNK = 128
D_FF = 2816
XBC_COLS = SSM_INNER + 2 * SSM_GROUPS * SSM_STATE
DT_COLS = 2 * SSM_HEADS

LANES = 128
SUBLANES = 8
VMEM_LIMIT = 56 * 1024 * 1024

TM_INPROJ = 1024
TM_CONV = 1024
TM_MIX = 512

A_COLS = 512
B_COLS = 768
XZ_COLS = 1536
DT_PAD = LANES
PROJ_COLS = A_COLS + B_COLS + XZ_COLS + DT_PAD

ATT_SCALE = HEAD_DIM ** -0.5
NEG_INF = float("-inf")
NT_DIMS = (((1,), (1,)), ((), ()))


def _silu(v):
    return v / (1.0 + jnp.exp(-v))


def _bdot(a, b):
    return jnp.dot(a.astype(BF16), b.astype(BF16), preferred_element_type=F32)


def _bdot_nt(a, b):
    return lax.dot_general(a.astype(BF16), b.astype(BF16), NT_DIMS, preferred_element_type=F32)


def _cparams(sem):
    return pltpu.CompilerParams(dimension_semantics=sem, vmem_limit_bytes=VMEM_LIMIT)


MOD_TN = 1536


def _mod_kernel(c_ref, w_ref, b_ref, o_ref):
    s = _silu(c_ref[...])
    o_ref[0] = _bdot(s, w_ref[0]) + b_ref[0]


def _modulation(cin, w_mod, b_mod):
    n = 6 * D_MODEL
    return pl.pallas_call(
        _mod_kernel,
        out_shape=jax.ShapeDtypeStruct((DEPTH, SUBLANES, n), F32),
        grid=(DEPTH, n // MOD_TN),
        in_specs=[
            pl.BlockSpec((SUBLANES, D_MODEL), lambda l, j: (0, 0)),
            pl.BlockSpec((1, D_MODEL, MOD_TN), lambda l, j: (l, 0, j)),
            pl.BlockSpec((1, 1, MOD_TN), lambda l, j: (l, 0, j)),
        ],
        out_specs=pl.BlockSpec((1, SUBLANES, MOD_TN), lambda l, j: (l, 0, j)),
        compiler_params=_cparams(("parallel", "parallel")),
        name="modulation",
    )(cin, w_mod, b_mod.reshape(DEPTH, 1, n))


def _norm_mod(x, g, shift, scale):
    var = jnp.mean(x * x, axis=-1, keepdims=True)
    h = x * lax.rsqrt(var + EPS) * g
    return h * (1.0 + scale) + shift


W_QA, W_QB, W_Z, W_KVA, W_KVB, W_XBC, W_DT = 0, 256, 512, 1024, 1280, 1792, 2816
IN_COLS = W_DT + DT_COLS


def _inproj_kernel(x_ref, mod_ref, g_ref, w_ref, wdt_ref, *rest, rope):
    if rope:
        cos_ref, sa_ref, sb_ref, oa_ref, ob_ref, oxz_ref, odt_ref = rest
    else:
        oa_ref, ob_ref, oxz_ref, odt_ref = rest
    h = _norm_mod(x_ref[0], g_ref[...], mod_ref[0, 0:1, :], mod_ref[0, 1:2, :]).astype(BF16)

    def proj(lo, hi):
        return jnp.dot(h, w_ref[0, :, lo:hi], preferred_element_type=F32)

    qa = proj(W_QA, W_QB)
    kva = proj(W_KVA, W_KVB)
    if rope:
        cos, sa, sb = cos_ref[...], sa_ref[...], sb_ref[...]

        def rot(v):
            up = pltpu.roll(v, LANES - 16, axis=1)
            dn = pltpu.roll(v, 16, axis=1)
            return v * cos + up * sa + dn * sb

        oa_ref[0, :, 0:LANES] = rot(qa[:, 0:LANES]).astype(BF16)
        oa_ref[0, :, LANES:2 * LANES] = rot(qa[:, LANES:]).astype(BF16)
        oa_ref[0, :, 2 * LANES:3 * LANES] = rot(kva[:, 0:LANES]).astype(BF16)
        oa_ref[0, :, 3 * LANES:] = kva[:, LANES:].astype(BF16)
    else:
        oa_ref[0, :, 0:2 * LANES] = qa.astype(BF16)
        oa_ref[0, :, 2 * LANES:] = kva.astype(BF16)
    ob_ref[0, :, 0:2 * LANES] = proj(W_QB, W_Z).astype(BF16)
    ob_ref[0, :, 2 * LANES:] = proj(W_KVB, W_XBC).astype(BF16)
    oxz_ref[0, :, 0:XBC_COLS] = proj(W_XBC, W_DT)
    oxz_ref[0, :, XBC_COLS:] = proj(W_Z, W_KVA)
    odt_ref[0] = jnp.dot(h, wdt_ref[...], preferred_element_type=F32)


def _inproj(x, mods, g, w, layer, wdt, rope_tabs, tm):
    b, t, _ = x.shape
    rope = rope_tabs is not None
    per_batch = mods.shape[0] > 1
    in_specs = [
        pl.BlockSpec((1, tm, D_MODEL), lambda i, j: (i, j, 0)),
        pl.BlockSpec((1, SUBLANES, D_MODEL), (lambda i, j: (i, 0, 0)) if per_batch else (lambda i, j: (0, 0, 0))),
        pl.BlockSpec((1, D_MODEL), lambda i, j: (0, 0)),
        pl.BlockSpec((1, D_MODEL, IN_COLS), lambda i, j: (layer, 0, 0)),
        pl.BlockSpec((D_MODEL, DT_PAD), lambda i, j: (0, 0)),
    ]
    args = [x, mods, g, w, wdt]
    if rope:
        in_specs += [pl.BlockSpec((tm, LANES), lambda i, j: (j, 0))] * 3
        args += list(rope_tabs)
    widths = (A_COLS, B_COLS, XZ_COLS, DT_PAD)
    return pl.pallas_call(
        functools.partial(_inproj_kernel, rope=rope),
        out_shape=[jax.ShapeDtypeStruct((b, t, n), dt) for n, dt in zip(widths, (BF16, BF16, F32, F32))],
        grid=(b, t // tm),
        in_specs=in_specs,
        out_specs=[pl.BlockSpec((1, tm, n), lambda i, j: (i, j, 0)) for n in widths],
        compiler_params=_cparams(("parallel", "parallel")),
        name="inproj_rope" if rope else "inproj",
    )(*args)


WA_KEYS = 3 * WA_BLOCK


WA_BLOCKS_PER_STEP = 4


def _window_masks():
    d = np.arange(3)[:, None, None] * WA_BLOCK
    qi = np.arange(WA_BLOCK)[None, :, None]
    c = np.arange(WA_KEYS + CTX_LEN)[None, None, :]
    ok = (c >= WA_KEYS) | (np.abs(d + qi - c) <= WA_WINDOW)
    return jnp.asarray(np.where(ok, 0.0, -np.inf), F32)


def _attn_a_kernel(sink_ref, mask_ref, q_ref, k_ref, v_ref, kc_ref, vc_ref, o_ref, s_ref, p_ref, inv_ref):
    step = pl.program_id(1)
    nq = WA_HEADS * WA_BLOCK
    kc, vc = kc_ref[0], vc_ref[0]
    starts = []
    for bb in range(WA_BLOCKS_PER_STEP):
        n = step * WA_BLOCKS_PER_STEP + bb
        starts.append(pl.multiple_of(jnp.clip((n - 1) * WA_BLOCK, 0, SEQ - WA_KEYS), WA_BLOCK))

    first_half = lax.broadcasted_iota(jnp.int32, (WA_BLOCK, LANES), 1) < HEAD_DIM
    swap = lambda v: pltpu.roll(v, HEAD_DIM, axis=1)

    for bb in range(WA_BLOCKS_PER_STEP):
        q = q_ref[0, bb * WA_BLOCK:(bb + 1) * WA_BLOCK, :].astype(F32) * ATT_SCALE
        q01, q23 = q[:, 0:LANES], q[:, LANES:]
        lhs = jnp.concatenate([
            jnp.where(first_half, q01, 0.0), jnp.where(first_half, swap(q01), 0.0),
            jnp.where(first_half, 0.0, swap(q23)), jnp.where(first_half, 0.0, q23)], axis=0)
        kall = jnp.concatenate([k_ref[0, pl.ds(starts[bb], WA_KEYS), :], kc], axis=0)
        scores = _bdot_nt(lhs, kall)
        n = step * WA_BLOCKS_PER_STEP + bb
        mask = mask_ref[n - starts[bb] // WA_BLOCK]
        for h in range(WA_HEADS):
            r0 = bb * nq + h * WA_BLOCK
            s_ref[r0:r0 + WA_BLOCK, :] = scores[h * WA_BLOCK:(h + 1) * WA_BLOCK] + mask

    nr = WA_BLOCKS_PER_STEP * nq
    always = step >= 0

    @pl.when(always)
    def _():
        s = s_ref[...]
        row1 = lax.broadcasted_iota(jnp.int32, (nr, 1), 0)
        head = (row1 // WA_BLOCK) % WA_HEADS
        sink = jnp.where(head == 0, sink_ref[0],
                         jnp.where(head == 1, sink_ref[1], jnp.where(head == 2, sink_ref[2], sink_ref[3])))
        m = jnp.maximum(jnp.max(s, axis=1, keepdims=True), sink)
        p = jnp.exp(s - m)
        inv = 1.0 / (jnp.sum(p, axis=1, keepdims=True) + jnp.exp(sink - m))
        inv_ref[...] = jnp.broadcast_to(inv, (nr, LANES))
        p_ref[...] = p.astype(BF16)

    @pl.when(always)
    def _():
        for bb in range(WA_BLOCKS_PER_STEP):
            vall = jnp.concatenate([v_ref[0, pl.ds(starts[bb], WA_KEYS), :], vc], axis=0)
            rs = slice(bb * nq, (bb + 1) * nq)
            o = jnp.dot(p_ref[rs, :], vall, preferred_element_type=F32) * inv_ref[rs, :]
            o0, o1, o2, o3 = (o[h * WA_BLOCK:(h + 1) * WA_BLOCK] for h in range(WA_HEADS))
            qs = slice(bb * WA_BLOCK, (bb + 1) * WA_BLOCK)
            o_ref[0, qs, 0:LANES] = jnp.where(first_half, o0, swap(o1)).astype(o_ref.dtype)
            o_ref[0, qs, LANES:] = jnp.where(first_half, swap(o2), o3).astype(o_ref.dtype)


def _attn_a(sink, qkv, qkv_c):
    b = qkv.shape[0]
    nk = WA_KEYS + CTX_LEN
    tq = WA_BLOCKS_PER_STEP * WA_BLOCK
    nr = WA_BLOCKS_PER_STEP * WA_HEADS * WA_BLOCK
    return pl.pallas_call(
        _attn_a_kernel,
        out_shape=jax.ShapeDtypeStruct((b, SEQ, WA_HEADS * HEAD_DIM), BF16),
        scratch_shapes=[pltpu.VMEM((nr, nk), F32), pltpu.VMEM((nr, nk), BF16), pltpu.VMEM((nr, LANES), F32)],
        grid=(b, SEQ // tq),
        in_specs=[
            pl.BlockSpec(memory_space=pltpu.SMEM),
            pl.BlockSpec((3, WA_BLOCK, nk), lambda i, j: (0, 0, 0)),
            pl.BlockSpec((1, tq, 2 * LANES), lambda i, j: (i, j, 0)),
            pl.BlockSpec((1, SEQ, LANES), lambda i, j: (i, 0, 2)),
            pl.BlockSpec((1, SEQ, LANES), lambda i, j: (i, 0, 3)),
            pl.BlockSpec((1, CTX_LEN, LANES), lambda i, j: (i, 0, 2)),
            pl.BlockSpec((1, CTX_LEN, LANES), lambda i, j: (i, 0, 3)),
        ],
        out_specs=pl.BlockSpec((1, tq, 2 * LANES), lambda i, j: (i, j, 0)),
        compiler_params=_cparams(("parallel", "arbitrary")),
        name="attn_window",
    )(sink, _window_masks(), qkv, qkv, qkv, qkv_c, qkv_c)


NB_ROWS_PER_STEP = 16
NB_LOC = NA_KH * GRID_W
NB_DY_PAIRS = 2 * NA_KH - 2


def _attn_b_kernel(q_ref, k_ref, v_ref, kc_ref, vc_ref, t_ref, o_ref, s_ref, p_ref):
    i = pl.program_id(1)
    kc = kc_ref[0]
    vc = vc_ref[0]
    units = [(rr, pp) for rr in range(NB_ROWS_PER_STEP) for pp in range(NA_HEADS // 2)]
    starts, shifts = [], []
    for rr in range(NB_ROWS_PER_STEP):
        r = i * NB_ROWS_PER_STEP + rr
        rs = jnp.clip(r - NA_KH // 2, 0, GRID_ROWS - NA_KH)
        shifts.append(r - rs)
        starts.append(pl.multiple_of(rs * GRID_W, GRID_W))
    first_half = lax.broadcasted_iota(jnp.int32, (GRID_W, LANES), 1) < HEAD_DIM
    zero = jnp.zeros((), BF16)
    pw = 2 * GRID_W

    for u, (rr, pp) in enumerate(units):
        ls = slice(pp * LANES, (pp + 1) * LANES)
        if pp == 0:
            kw = k_ref[0, pl.ds(starts[rr], NB_LOC), :]
            q = q_ref[0, rr * GRID_W:(rr + 1) * GRID_W, :] * ATT_SCALE
        qs = q[:, ls]
        lhs = jnp.concatenate([jnp.where(first_half, qs, zero), jnp.where(first_half, zero, qs)], axis=0)
        bias = jnp.concatenate(
            [jnp.concatenate([t_ref[h, 2 * k - shifts[rr] + NA_KH - 1] for k in range(NA_KH // 2)], axis=1)
             for h in (2 * pp, 2 * pp + 1)], axis=0)
        s_ref[u * pw:(u + 1) * pw, 0:NB_LOC] = _bdot_nt(lhs, kw[:, ls]) + bias
        s_ref[u * pw:(u + 1) * pw, NB_LOC:] = _bdot_nt(lhs, kc[:, ls])

    s = s_ref[...]
    p = jnp.exp(s - jnp.max(s, axis=1, keepdims=True))
    inv = 1.0 / jnp.sum(p, axis=1, keepdims=True)
    p_ref[...] = p.astype(BF16)

    for u, (rr, pp) in enumerate(units):
        ls = slice(pp * LANES, (pp + 1) * LANES)
        if pp == 0:
            vw = v_ref[0, pl.ds(starts[rr], NB_LOC), :]
        rows = slice(u * pw, (u + 1) * pw)
        o = (jnp.dot(p_ref[rows, 0:NB_LOC], vw[:, ls], preferred_element_type=F32)
             + jnp.dot(p_ref[rows, NB_LOC:], vc[:, ls], preferred_element_type=F32)) * inv[rows]
        o_ref[0, rr * GRID_W:(rr + 1) * GRID_W, ls] = jnp.where(
            first_half, o[:GRID_W], o[GRID_W:]).astype(o_ref.dtype)


def _attn_b(qkv, qkv_c, table):
    b = qkv.shape[0]
    tq = NB_ROWS_PER_STEP * GRID_W
    w = NA_HEADS * HEAD_DIM
    nu = NB_ROWS_PER_STEP * NA_HEADS * GRID_W
    return pl.pallas_call(
        _attn_b_kernel,
        out_shape=jax.ShapeDtypeStruct((b, SEQ, w), BF16),
        scratch_shapes=[pltpu.VMEM((nu, NB_LOC + CTX_LEN), F32), pltpu.VMEM((nu, NB_LOC + CTX_LEN), BF16)],
        grid=(b, SEQ // tq),
        in_specs=[
            pl.BlockSpec((1, tq, w), lambda i, j: (i, j, 0)),
            pl.BlockSpec((1, SEQ, w), lambda i, j: (i, 0, 1)),
            pl.BlockSpec((1, SEQ, w), lambda i, j: (i, 0, 2)),
            pl.BlockSpec((1, CTX_LEN, w), lambda i, j: (i, 0, 1)),
            pl.BlockSpec((1, CTX_LEN, w), lambda i, j: (i, 0, 2)),
            pl.BlockSpec((NA_HEADS, NB_DY_PAIRS, GRID_W, LANES), lambda i, j: (0, 0, 0, 0)),
        ],
        out_specs=pl.BlockSpec((1, tq, w), lambda i, j: (i, j, 0)),
        compiler_params=_cparams(("parallel", "arbitrary")),
        name="attn_neighbourhood",
    )(qkv, qkv, qkv, qkv_c, qkv_c, table)


def _split3(a):
    a1 = a.astype(BF16)
    r1 = a - a1.astype(F32)
    a2 = r1.astype(BF16)
    a3 = (r1 - a2.astype(F32)).astype(BF16)
    return a1, a2, a3


def _bias_kernel(r_ref, oh_ref, o_ref):
    oh = oh_ref[...]
    o_ref[...] = sum(jnp.dot(t, oh, preferred_element_type=F32) for t in _split3(r_ref[...]))


def _na_bias_tables(rpb):
    ndy, ndx = 2 * NA_KH - 1, 2 * NA_KW - 1
    qc = np.arange(GRID_W)[:, None]
    x = np.arange(GRID_W)[None, :]
    dx = np.clip(x - qc, -(NA_KW - 1), NA_KW - 1) + NA_KW - 1
    onehot = (np.arange(LANES)[:, None, None] == dx[None]).reshape(LANES, GRID_W * GRID_W)
    cstart = np.clip(qc - NA_KW // 2, 0, GRID_W - NA_KW)
    inside = (x >= cstart) & (x < cstart + NA_KW)
    rows = DEPTH * NA_HEADS * ndy
    r = jnp.pad(rpb.astype(F32).reshape(rows, ndx), ((0, LANES - rows), (0, LANES - ndx)))
    m = pl.pallas_call(
        _bias_kernel,
        out_shape=jax.ShapeDtypeStruct((LANES, GRID_W * GRID_W), F32),
        name="na_bias_expand",
    )(r, jnp.asarray(onehot, BF16))
    m = m[:rows].reshape(DEPTH, NA_HEADS, ndy, GRID_W, GRID_W)
    m = jnp.where(jnp.asarray(inside), m, NEG_INF)
    return jnp.concatenate([m[:, :, :ndy - 1], m[:, :, 1:]], axis=-1)


def _ctx_attn_kernel(sink_ref, a_ref, b_ref, oa_ref, ob_ref):
    a = a_ref[0]
    row1 = lax.broadcasted_iota(jnp.int32, (2 * CTX_LEN, 1), 0)
    for g in range(WA_KV_HEADS):
        h0, h1 = 2 * g, 2 * g + 1
        q2 = jnp.concatenate([a[:, h0 * HEAD_DIM:(h0 + 1) * HEAD_DIM],
                              a[:, h1 * HEAD_DIM:(h1 + 1) * HEAD_DIM]], axis=0)
        k = a[:, 2 * LANES + g * HEAD_DIM:2 * LANES + (g + 1) * HEAD_DIM]
        v = a[:, 3 * LANES + g * HEAD_DIM:3 * LANES + (g + 1) * HEAD_DIM]
        s = _bdot_nt(q2, k) * ATT_SCALE
        sink = jnp.where(row1 < CTX_LEN, sink_ref[h0], sink_ref[h1])
        m = jnp.maximum(jnp.max(s, axis=1, keepdims=True), sink)
        p = jnp.exp(s - m)
        den = jnp.sum(p, axis=1, keepdims=True) + jnp.exp(sink - m)
        o = _bdot(p, v) / den
        oa_ref[0, :, h0 * HEAD_DIM:(h0 + 1) * HEAD_DIM] = o[:CTX_LEN]
        oa_ref[0, :, h1 * HEAD_DIM:(h1 + 1) * HEAD_DIM] = o[CTX_LEN:]
    bq = b_ref[0]
    w = NA_HEADS * HEAD_DIM
    for h in range(NA_HEADS):
        hs = slice(h * HEAD_DIM, (h + 1) * HEAD_DIM)
        s = _bdot_nt(bq[:, hs], bq[:, w + h * HEAD_DIM:w + (h + 1) * HEAD_DIM]) * ATT_SCALE
        m = jnp.max(s, axis=1, keepdims=True)
        p = jnp.exp(s - m)
        den = jnp.sum(p, axis=1, keepdims=True)
        ob_ref[0, :, hs] = _bdot(p, bq[:, 2 * w + h * HEAD_DIM:2 * w + (h + 1) * HEAD_DIM]) / den


def _ctx_attn(sink, qkv_a_c, qkv_b_c):
    b = qkv_a_c.shape[0]
    w = 2 * LANES
    return pl.pallas_call(
        _ctx_attn_kernel,
        out_shape=[jax.ShapeDtypeStruct((b, CTX_LEN, w), F32)] * 2,
        grid=(b,),
        in_specs=[
            pl.BlockSpec(memory_space=pltpu.SMEM),
            pl.BlockSpec((1, CTX_LEN, A_COLS), lambda i: (i, 0, 0)),
            pl.BlockSpec((1, CTX_LEN, B_COLS), lambda i: (i, 0, 0)),
        ],
        out_specs=[pl.BlockSpec((1, CTX_LEN, w), lambda i: (i, 0, 0))] * 2,
        compiler_params=_cparams(("parallel",)),
        name="attn_context",
    )(sink, qkv_a_c, qkv_b_c)


CONV_HALO = SUBLANES


def _conv_kernel(prev_ref, cur_ref, next_ref, w_ref, b_ref, o_ref, ext_ref, *, tl, nt):
    j = pl.program_id(1)
    ext_ref[0:CONV_HALO, :] = jnp.where(j > 0, prev_ref[0], 0.0)
    ext_ref[CONV_HALO:CONV_HALO + tl, :] = cur_ref[0]
    ext_ref[CONV_HALO + tl:, :] = jnp.where(j < nt - 1, next_ref[0], 0.0)
    acc = jnp.zeros((tl, XBC_COLS), F32) + b_ref[...]
    base = CONV_HALO - SSM_CONV // 2
    for k in range(SSM_CONV):
        acc = acc + w_ref[k:k + 1, :] * ext_ref[base + k:base + k + tl, :]
    o_ref[0] = _silu(acc)


def _conv_silu(xz, conv_w, conv_b, tl):
    b, t, _ = xz.shape
    nt = t // tl
    hb = tl // CONV_HALO
    last = t // CONV_HALO - 1
    return pl.pallas_call(
        functools.partial(_conv_kernel, tl=tl, nt=nt),
        out_shape=jax.ShapeDtypeStruct((b, t, XBC_COLS), F32),
        grid=(b, nt),
        in_specs=[
            pl.BlockSpec((1, CONV_HALO, XBC_COLS), lambda i, j: (i, jnp.maximum(j * hb - 1, 0), 0)),
            pl.BlockSpec((1, tl, XBC_COLS), lambda i, j: (i, j, 0)),
            pl.BlockSpec((1, CONV_HALO, XBC_COLS), lambda i, j: (i, jnp.minimum((j + 1) * hb, last), 0)),
            pl.BlockSpec((SUBLANES, XBC_COLS), lambda i, j: (0, 0)),
            pl.BlockSpec((1, XBC_COLS), lambda i, j: (0, 0)),
        ],
        out_specs=pl.BlockSpec((1, tl, XBC_COLS), lambda i, j: (i, j, 0)),
        scratch_shapes=[pltpu.VMEM((tl + 2 * CONV_HALO, XBC_COLS), F32)],
        compiler_params=_cparams(("parallel", "parallel")),
        name="ssm_conv",
    )(xz, xz, xz, conv_w, conv_b)


Q = SSM_CHUNK
GS = SSM_GROUPS * SSM_STATE
HPG = SSM_HEADS // SSM_GROUPS
SSD_STATE_SHAPE = (SSM_GROUPS, SSM_STATE, HPG * SSM_HEAD_DIM)
SSD_BATCH_PER_STEP = 4


def _softplus(v):
    return jnp.maximum(v, 0.0) + jnp.log1p(jnp.exp(-jnp.abs(v)))


def _cumsum_mat(tri, a):
    r = jnp.dot(tri, jnp.concatenate(_split3(a), axis=1), preferred_element_type=F32)
    n = a.shape[1]
    return r[:, 0:n] + r[:, n:2 * n] + r[:, 2 * n:]


def _ssd_kernel(xm_ref, dtm_ref, xb_ref, dtb_ref, bias_ref, alog_ref, h0f_ref, h0b_ref,
                y1_ref, y2_ref, hf_out_ref, hb_out_ref, hf_ref, hb_ref, *, nc):
    i = pl.program_id(1)

    @pl.when(i == 0)
    def _():
        hf_ref[...] = h0f_ref[...]
        hb_ref[...] = h0b_ref[...]

    ii = lax.broadcasted_iota(jnp.int32, (Q, Q), 0)
    jj = lax.broadcasted_iota(jnp.int32, (Q, Q), 1)
    lower = ii >= jj
    diag = ii == jj
    tril =jnp.where(lower, 1.0, 0.0).astype(BF16)
    triu = jnp.where(ii <= jj, 1.0, 0.0).astype(BF16)
    first_half = lax.broadcasted_iota(jnp.int32, (Q, LANES), 1) < SSM_HEAD_DIM
    a_row = -jnp.exp(alog_ref[...])
    gw = HPG * SSM_HEAD_DIM

    def group_operands(xbc, g):
        bg = xbc[:, SSM_INNER + g * SSM_STATE:SSM_INNER + (g + 1) * SSM_STATE]
        cg = xbc[:, SSM_INNER + GS + g * SSM_STATE:SSM_INNER + GS + (g + 1) * SSM_STATE]
        return bg, cg.astype(BF16), xbc[:, g * gw:(g + 1) * gw]

    nbs = range(SSD_BATCH_PER_STEP)
    groups = range(SSM_GROUPS)
    bias = bias_ref[...]
    zero = jnp.zeros((), F32)


    xm = [xm_ref[bi] for bi in nbs]
    xb = [xb_ref[bi] for bi in nbs]
    dtm = [_softplus(dtm_ref[bi] + bias) for bi in nbs]
    dtb = [_softplus(dtb_ref[bi] + bias) for bi in nbs]
    cumf =[_cumsum_mat(tril, d * a_row) for d in dtm]
    cumr = [_cumsum_mat(triu, d * a_row) for d in dtm]
    cumb = [_cumsum_mat(triu, d * a_row) for d in dtb]

    opm = [[group_operands(xm[bi], g) for g in groups] for bi in nbs]
    opb = [[group_operands(xb[bi], g) for g in groups] for bi in nbs]
    gmat = [[_bdot_nt(opm[bi][g][1], opm[bi][g][0]) for g in groups] for bi in nbs]
    hf_prev = [[hf_ref[bi, g] for g in groups] for bi in nbs]
    hb_prev = [[hb_ref[bi, g] for g in groups] for bi in nbs]
    inter_f = [[jnp.dot(opm[bi][g][1], hf_prev[bi][g].astype(BF16), preferred_element_type=F32)
                for g in groups] for bi in nbs]
    inter_b = [[jnp.dot(opb[bi][g][1], hb_prev[bi][g].astype(BF16), preferred_element_type=F32)
                for g in groups] for bi in nbs]
    bt_m = [[opm[bi][g][0].T for g in groups] for bi in nbs]
    bt_b = [[opb[bi][g][0].T for g in groups] for bi in nbs]

    cumf_t = [c.T for c in cumf]
    cumr_t = [c.T for c in cumr]
    dtm_t = [d.T for d in dtm]
    cumb_t = [c.T for c in cumb]
    dtb_t = [d.T for d in dtb]

    heads = [(bi, h) for bi in nbs for h in range(SSM_HEADS)]
    hb_ = lambda h: SSM_HEADS + h
    cf, cr, cb, ein_f, ein_b, wm, wst_f, wst_b = {}, {}, {}, {}, {}, {}, {}, {}
    for t in range(len(heads) + 1):
        if t < len(heads):
            bi, h = k = heads[t]
            cf[k] = jnp.broadcast_to(cumf[bi][:, h:h + 1], (Q, Q))
            cr[k] = jnp.broadcast_to(cumr[bi][:, hb_(h):hb_(h) + 1], (Q, Q))
            cb[k] = jnp.broadcast_to(cumb[bi][:, hb_(h):hb_(h) + 1], (Q, Q))
        if t >= 1:
            bi, h = k = heads[t - 1]
            rf = cumf_t[bi][h:h + 1, :]
            rb = cumb_t[bi][hb_(h):hb_(h) + 1, :]
            dtf_row = dtm_t[bi][h:h + 1, :]
            dtb_row = dtm_t[bi][hb_(h):hb_(h) + 1, :]
            arg = jnp.where(lower, cf[k] - rf, cr[k] - cumr_t[bi][hb_(h):hb_(h) + 1, :])
            wm[k] = (gmat[bi][h // HPG] * (jnp.exp(arg) * jnp.where(lower, dtf_row, dtb_row)
                                           + jnp.where(diag, dtb_row, zero))).astype(BF16)
            ein_f[k] = jnp.exp(cf[k])
            ein_b[k] = jnp.exp(cb[k])
            w_f = jnp.exp(rf[:, Q - 1:Q] - rf) * dtf_row
            w_b = jnp.exp(rb[:, 0:1] - rb) * dtb_t[bi][hb_(h):hb_(h) + 1, :]
            wst_f[k] = (bt_m[bi][h // HPG] * w_f).astype(BF16)
            wst_b[k] = (bt_b[bi][h // HPG] * w_b).astype(BF16)

    def pair_rhs(x, pr):
        xp = x[:, pr * LANES:(pr + 1) * LANES]
        return jnp.concatenate([jnp.where(first_half, xp, zero), jnp.where(first_half, zero, xp)],
                               axis=0).astype(BF16)

    def pair_dot(mats, bi, pr, rhs):
        lhs = jnp.concatenate([mats[bi, 2 * pr], mats[bi, 2 * pr + 1]], axis=1)
        return jnp.dot(lhs, rhs, preferred_element_type=F32)

    pairs = [(bi, pr) for bi in nbs for pr in range(SSM_HEADS // 2)]
    rhs_m = {k: pair_rhs(xm[k[0]], k[1]) for k in pairs}
    rhs_b = {k: pair_rhs(xb[k[0]], k[1]) for k in pairs}
    y_intra = {k: pair_dot(wm, k[0], k[1], rhs_m[k]) for k in pairs}
    st_f = {k: pair_dot(wst_f, k[0], k[1], rhs_m[k]) for k in pairs}
    st_b = {k: pair_dot(wst_b, k[0], k[1], rhs_b[k]) for k in pairs}

    ppg = HPG // 2
    for bi, pr in pairs:
        g, k = pr // ppg, pr % ppg
        ls = slice(k * LANES, (k + 1) * LANES)
        sin_f = jnp.where(first_half, ein_f[bi, 2 * pr], ein_f[bi, 2 * pr + 1])
        sin_b = jnp.where(first_half, ein_b[bi, 2 * pr], ein_b[bi, 2 * pr + 1])
        y1_ref[bi, :, pr * LANES:(pr + 1) * LANES] = y_intra[bi, pr] + sin_f * inter_f[bi][g][:, ls]
        y2_ref[bi, :, pr * LANES:(pr + 1) * LANES] = sin_b * inter_b[bi][g][:, ls]
        hf_ref[bi, g, :, ls] = hf_prev[bi][g][:, ls] * sin_f[Q - 1:Q, :] + st_f[bi, pr]
        hb_ref[bi, g, :, ls] = hb_prev[bi][g][:, ls] * sin_b[0:1, :] + st_b[bi, pr]

    @pl.when(i == nc - 1)
    def _():
        hf_out_ref[...] = hf_ref[...]
        hb_out_ref[...] = hb_ref[...]


def _ssd(xbc, dt_raw, dt_bias, a_log, h0f, h0b):
    b, t, _ = xbc.shape
    nc = t // Q
    nb = SSD_BATCH_PER_STEP
    st_shape = (b,) + SSD_STATE_SHAPE
    st_spec = pl.BlockSpec((nb,) + SSD_STATE_SHAPE, lambda i, j: (i, 0, 0, 0))
    fwd = lambda i, j: (i, j, 0)
    bwd = lambda i, j: (i, nc - 1 - j, 0)
    return pl.pallas_call(
        functools.partial(_ssd_kernel, nc=nc),
        out_shape=[jax.ShapeDtypeStruct((b, t, SSM_INNER), F32)] * 2
        + [jax.ShapeDtypeStruct(st_shape, F32)] * 2,
        grid=(b // nb, nc),
        in_specs=[
            pl.BlockSpec((nb, Q, XBC_COLS), fwd),
            pl.BlockSpec((nb, Q, DT_PAD), fwd),
            pl.BlockSpec((nb, Q, XBC_COLS), bwd),
            pl.BlockSpec((nb, Q, DT_PAD), bwd),
            pl.BlockSpec((1, DT_PAD), lambda i, j: (0, 0)),
            pl.BlockSpec((1, DT_PAD), lambda i, j: (0, 0)),
            st_spec, st_spec,
        ],
        out_specs=[pl.BlockSpec((nb, Q, SSM_INNER), fwd), pl.BlockSpec((nb, Q, SSM_INNER), bwd),
                   st_spec, st_spec],
        scratch_shapes=[pltpu.VMEM((nb,) + SSD_STATE_SHAPE, F32)] * 2,
        compiler_params=_cparams(("parallel", "arbitrary")),
        name="ssd_scan",
    )(xbc, dt_raw, xbc, dt_raw, dt_bias, a_log, h0f, h0b)


MXU_TILE = 256
FFN_SPLITS = (0, 6 * MXU_TILE, D_FF)


def _mix_ffn_kernel(x_ref, mod_ref, oa_ref, ob_ref, y1_ref, y2_ref, xs_ref, z_ref, d_ref, sg_ref,
                    wo_ref, g_ref, wfi_ref, wfo_ref, gfin_ref, o_ref, *, final):
    y = y1_ref[0] + y2_ref[0] + d_ref[...] * xs_ref[0]
    y = y * _silu(z_ref[0])
    var = jnp.mean(y * y, axis=-1, keepdims=True)
    oc = (y * lax.rsqrt(var + EPS) * sg_ref[...]).astype(BF16)
    wa = 2 * LANES
    mix = (jnp.dot(oa_ref[0].astype(BF16), wo_ref[0, 0:wa, :], preferred_element_type=F32)
           + jnp.dot(ob_ref[0].astype(BF16), wo_ref[0, wa:2 * wa, :], preferred_element_type=F32)
           + jnp.dot(oc, wo_ref[0, 2 * wa:, :], preferred_element_type=F32))
    xn = x_ref[0] + mod_ref[0, 2:3, :] * mix
    h = _norm_mod(xn, g_ref[...], mod_ref[0, 3:4, :], mod_ref[0, 4:5, :]).astype(BF16)
    acc = None
    for lo, hi in zip(FFN_SPLITS[:-1], FFN_SPLITS[1:]):
        gate = jnp.dot(h, wfi_ref[0, :, lo:hi], preferred_element_type=F32)
        up = jnp.dot(h, wfi_ref[0, :, D_FF + lo:D_FF + hi], preferred_element_type=F32)
        act = (_silu(gate) * up).astype(BF16)
        part = jnp.dot(act, wfo_ref[0, lo:hi, :], preferred_element_type=F32)
        acc = part if acc is None else acc + part
    out = xn + mod_ref[0, 5:6, :] * acc
    if final:
        var = jnp.mean(out * out, axis=-1, keepdims=True)
        out = out * lax.rsqrt(var + EPS) * gfin_ref[...]
    o_ref[0] = out


def _mix_ffn(x, mods, o_a, o_b, y1, y2, xbc, xz, d_exp, ssm_g, w_out, g_ffn, w_ffn_in, w_ffn_out,
             g_final, layer, tm, final):
    b, t, _ = x.shape
    per_batch = mods.shape[0] > 1
    row = lambda n: pl.BlockSpec((1, tm, n), lambda i, j: (i, j, 0))
    const = lambda shape: pl.BlockSpec(shape, lambda i, j: (0, 0), pipeline_mode=pl.Buffered(1))
    weight = lambda r, c: pl.BlockSpec((1, r, c), lambda i, j: (layer, 0, 0), pipeline_mode=pl.Buffered(1))
    return pl.pallas_call(
        functools.partial(_mix_ffn_kernel, final=final),
        out_shape=jax.ShapeDtypeStruct((b, t, D_MODEL), F32),
        grid=(b, t // tm),
        in_specs=[
            row(D_MODEL),
            pl.BlockSpec((1, SUBLANES, D_MODEL), (lambda i, j: (i, 0, 0)) if per_batch else (lambda i, j: (0, 0, 0))),
            row(2 * LANES), row(2 * LANES), row(SSM_INNER), row(SSM_INNER),
            pl.BlockSpec((1, tm, SSM_INNER), lambda i, j: (i, j, 0)),
            pl.BlockSpec((1, tm, SSM_INNER), lambda i, j: (i, j, 2)),
            const((1, SSM_INNER)), const((1, SSM_INNER)),
            weight(D_MODEL, D_MODEL),
            const((1, D_MODEL)),
            weight(D_MODEL, 2 * D_FF),
            weight(D_FF, D_MODEL),
            const((1, D_MODEL)),
        ],
        out_specs=row(D_MODEL),
        compiler_params=_cparams(("parallel", "parallel")),
        name="mix_ffn_final" if final else "mix_ffn",
    )(x, mods, o_a, o_b, y1, y2, xbc, xz, d_exp, ssm_g, w_out, g_ffn, w_ffn_in, w_ffn_out, g_final)


def _rope_tables():
    t = np.arange(SEQ)
    pos = np.stack([t // GRID_W, t % GRID_W], axis=1).astype(np.float64)
    quarter = HEAD_DIM // 4
    inv_freq = ROPE_BASE ** (-np.arange(quarter, dtype=np.float64) / quarter)
    lane = np.arange(LANES) % HEAD_DIM
    half = lane // (HEAD_DIM // 2)
    idx = lane % (HEAD_DIM // 2)
    ang = pos[:, half] * inv_freq[idx % quarter][None, :]
    cos, sin = np.cos(ang), np.sin(ang)
    first = (idx < quarter)[None, :]
    tabs = (cos, np.where(first, -sin, 0.0), np.where(first, 0.0, sin))
    return tuple(jnp.asarray(v, F32) for v in tabs)


def _pad_lanes(v, n=LANES):
    v = v.reshape(1, -1)
    return jnp.pad(v, ((0, 0), (0, n - v.shape[1])))


def kernel(x, c, ctx, c_ctx, w_mod, b_mod, g_mix, w_in, wa_sink, na_rpb, ssm_conv_w, ssm_conv_b,
           ssm_dt_bias, ssm_a_log, ssm_d, ssm_norm_g, w_out, g_ffn, w_ffn_in, w_ffn_out, g_final):
    cin = jnp.concatenate([c, c_ctx[None, :], jnp.zeros((SUBLANES - BATCH - 1, D_MODEL), F32)], axis=0)
    mod_all = _modulation(cin, w_mod, b_mod)
    rope_tabs = _rope_tables()
    bias_tabs = _na_bias_tables(na_rpb)
    zeros_state = jnp.zeros((BATCH,) + SSD_STATE_SHAPE, F32)
    gfin = g_final.reshape(1, D_MODEL)
    w_proj = w_in.astype(BF16)
    wo = w_out.astype(BF16)
    wfi = w_ffn_in.astype(BF16)
    wfo = w_ffn_out.astype(BF16)

    xl, xc = x, ctx
    for l in range(DEPTH):
        last = l == DEPTH - 1
        m6 = mod_all[l].reshape(SUBLANES, 6, D_MODEL)
        mods_l = jnp.pad(m6[:BATCH], ((0, 0), (0, 2), (0, 0)))
        mods_c = jnp.pad(m6[BATCH:BATCH + 1], ((0, 0), (0, 2), (0, 0)))
        w_dt = jnp.pad(w_in[l][:, W_DT:], ((0, 0), (0, DT_PAD - DT_COLS))).astype(BF16)
        g1 = g_mix[l].reshape(1, D_MODEL)
        g2 = g_ffn[l].reshape(1, D_MODEL)
        conv_w = jnp.pad(ssm_conv_w[l], ((0, SUBLANES - SSM_CONV), (0, 0)))
        conv_b = ssm_conv_b[l].reshape(1, XBC_COLS)
        dt_bias = _pad_lanes(ssm_dt_bias[l])
        a_log = _pad_lanes(ssm_a_log[l])
        d_exp = jnp.repeat(ssm_d[l], SSM_HEAD_DIM).reshape(1, SSM_INNER)
        sg = ssm_norm_g[l].reshape(1, SSM_INNER)

        a_c, b_c, xz_c, dt_c = _inproj(xc, mods_c, g1, w_proj, l, w_dt, None, CTX_LEN)
        a_l, b_l, xz_l, dt_l = _inproj(xl, mods_l, g1, w_proj, l, w_dt, rope_tabs, TM_INPROJ)

        o_a = _attn_a(wa_sink[l], a_l, a_c)
        o_b = _attn_b(b_l, b_c, bias_tabs[l])

        xbc_c = _conv_silu(xz_c, conv_w, conv_b, CTX_LEN)
        xbc_l = _conv_silu(xz_l, conv_w, conv_b, TM_CONV)
        y1_c, y2_c, h_f, h_b = _ssd(xbc_c, dt_c, dt_bias, a_log, zeros_state, zeros_state)
        y1_l, y2_l, _, _ = _ssd(xbc_l, dt_l, dt_bias, a_log, h_f, h_b)

        xl = _mix_ffn(xl, mods_l, o_a, o_b, y1_l, y2_l, xbc_l, xz_l, d_exp, sg, wo, g2, wfi, wfo, gfin,
                      l, TM_MIX, last)
        if not last:
            o_ac, o_bc = _ctx_attn(wa_sink[l], a_c, b_c)
            xc = _mix_ffn(xc, mods_c, o_ac, o_bc, y1_c, y2_c, xbc_c, xz_c, d_exp, sg, wo, g2, wfi, wfo,
                          gfin, l, CTX_LEN, False)
    return xl
```

```python
import functools

import numpy as np
import jax
import jax.numpy as jnp
from jax import lax
from jax.experimental import pallas as pl
from jax.experimental.pallas import tpu as pltpu

F32 = jnp.float32
BF16 = jnp.bfloat16

D_MODEL = 1024
BATCH = 4
SEQ = 4096
DEPTH = 2
GRID_W = 64
GRID_ROWS = SEQ // GRID_W
CTX_LEN = 256
EPS = 1e-6
HEAD_DIM = 64
ROPE_BASE = 10000.0
WA_HEADS = 4
WA_KV_HEADS = 2
WA_WINDOW = 128
WA_BLOCK = 128
NA_HEADS = 4
NA_KH = 8
NA_KW = 16
SSM_HEADS = 8
SSM_HEAD_DIM = 64
SSM_INNER = SSM_HEADS * SSM_HEAD_DIM
SSM_GROUPS = 2
SSM_STATE = 128
SSM_CONV = 7
SSM_CHUNK = 128
D_FF = 2816
XBC_COLS = SSM_INNER + 2 * SSM_GROUPS * SSM_STATE
DT_COLS = 2 * SSM_HEADS

LANES = 128
SUBLANES = 8
V7X_VMEM_BYTES = 64 * 1024 * 1024
VMEM_LIMIT = V7X_VMEM_BYTES - 8 * 1024 * 1024

TM_INPROJ = 1024
TM_CONV = 1024
TM_MIX = 512

A_COLS = 512
B_COLS = 768
XZ_COLS = 1536
DT_PAD = LANES

ATT_SCALE = HEAD_DIM ** -0.5
NEG_INF = float("-inf")
NT_DIMS = (((1,), (1,)), ((), ()))


def _silu(v):
    return v / (1.0 + jnp.exp(-v))


def _bdot(a, b):
    return jnp.dot(a.astype(BF16), b.astype(BF16), preferred_element_type=F32)


def _bdot_nt(a, b):
    return lax.dot_general(a.astype(BF16), b.astype(BF16), NT_DIMS, preferred_element_type=F32)


def _cparams(sem):
    return pltpu.CompilerParams(dimension_semantics=sem, vmem_limit_bytes=VMEM_LIMIT)


MOD_TN = 6 * D_MODEL // 4


def _mod_kernel(c_ref, w_ref, b_ref, o_ref):
    s = _silu(c_ref[...])
    o_ref[0] = _bdot(s, w_ref[0]) + b_ref[0]


def _modulation(cin, w_mod, b_mod):
    n = 6 * D_MODEL
    return pl.pallas_call(
        _mod_kernel,
        out_shape=jax.ShapeDtypeStruct((DEPTH, SUBLANES, n), F32),
        grid=(DEPTH, n // MOD_TN),
        in_specs=[
            pl.BlockSpec((SUBLANES, D_MODEL), lambda l, j: (0, 0)),
            pl.BlockSpec((1, D_MODEL, MOD_TN), lambda l, j: (l, 0, j)),
            pl.BlockSpec((1, 1, MOD_TN), lambda l, j: (l, 0, j)),
        ],
        out_specs=pl.BlockSpec((1, SUBLANES, MOD_TN), lambda l, j: (l, 0, j)),
        compiler_params=_cparams(("parallel", "parallel")),
        name="modulation",
    )(cin, w_mod, b_mod.reshape(DEPTH, 1, n))


def _norm_mod(x, g, shift, scale):
    var = jnp.mean(x * x, axis=-1, keepdims=True)
    h = x * lax.rsqrt(var + EPS) * g
    return h * (1.0 + scale) + shift


W_QA = 0
W_QB = W_QA + WA_HEADS * HEAD_DIM
W_Z = W_QB + NA_HEADS * HEAD_DIM
W_KVA = W_Z + SSM_INNER
W_KVB = W_KVA + 2 * WA_KV_HEADS * HEAD_DIM
W_XBC = W_KVB + 2 * NA_HEADS * HEAD_DIM
W_DT = W_XBC + XBC_COLS
IN_COLS = W_DT + DT_COLS
ROPE_QUARTER = HEAD_DIM // 4


def _inproj_kernel(x_ref, mod_ref, g_ref, w_ref, wdt_ref, *rest, rope):
    if rope:
        cos_ref, sa_ref, sb_ref, oa_ref, ob_ref, oxz_ref, odt_ref = rest
    else:
        oa_ref, ob_ref, oxz_ref, odt_ref = rest
    h = _norm_mod(x_ref[0], g_ref[...], mod_ref[0, 0:1, :], mod_ref[0, 1:2, :]).astype(BF16)

    def proj(lo, hi):
        return jnp.dot(h, w_ref[0, :, lo:hi], preferred_element_type=F32)

    qa = proj(W_QA, W_QB)
    kva = proj(W_KVA, W_KVB)
    if rope:
        cos, sa, sb = cos_ref[...], sa_ref[...], sb_ref[...]

        def rot(v):
            up = pltpu.roll(v, LANES - ROPE_QUARTER, axis=1)
            dn = pltpu.roll(v, ROPE_QUARTER, axis=1)
            return v * cos + up * sa + dn * sb

        oa_ref[0, :, 0:LANES] = rot(qa[:, 0:LANES]).astype(BF16)
        oa_ref[0, :, LANES:2 * LANES] = rot(qa[:, LANES:]).astype(BF16)
        oa_ref[0, :, 2 * LANES:3 * LANES] = rot(kva[:, 0:LANES]).astype(BF16)
        oa_ref[0, :, 3 * LANES:] = kva[:, LANES:].astype(BF16)
    else:
        oa_ref[0, :, 0:2 * LANES] = qa.astype(BF16)
        oa_ref[0, :, 2 * LANES:] = kva.astype(BF16)
    ob_ref[0, :, 0:2 * LANES] = proj(W_QB, W_Z).astype(BF16)
    ob_ref[0, :, 2 * LANES:] = proj(W_KVB, W_XBC).astype(BF16)
    oxz_ref[0, :, 0:XBC_COLS] = proj(W_XBC, W_DT)
    oxz_ref[0, :, XBC_COLS:] = proj(W_Z, W_KVA)
    odt_ref[0] = jnp.dot(h, wdt_ref[...], preferred_element_type=F32)


def _inproj(x, mods, g, w, layer, wdt, rope_tabs, tm):
    b, t, _ = x.shape
    rope = rope_tabs is not None
    per_batch = mods.shape[0] > 1
    in_specs = [
        pl.BlockSpec((1, tm, D_MODEL), lambda i, j: (i, j, 0)),
        pl.BlockSpec((1, SUBLANES, D_MODEL), (lambda i, j: (i, 0, 0)) if per_batch else (lambda i, j: (0, 0, 0))),
        pl.BlockSpec((1, D_MODEL), lambda i, j: (0, 0)),
        pl.BlockSpec((1, D_MODEL, IN_COLS), lambda i, j: (layer, 0, 0)),
        pl.BlockSpec((D_MODEL, DT_PAD), lambda i, j: (0, 0)),
    ]
    args = [x, mods, g, w, wdt]
    if rope:
        in_specs += [pl.BlockSpec((tm, LANES), lambda i, j: (j, 0))] * 3
        args += list(rope_tabs)
    widths = (A_COLS, B_COLS, XZ_COLS, DT_PAD)
    return pl.pallas_call(
        functools.partial(_inproj_kernel, rope=rope),
        out_shape=[jax.ShapeDtypeStruct((b, t, n), dt) for n, dt in zip(widths, (BF16, BF16, F32, F32))],
        grid=(b, t // tm),
        in_specs=in_specs,
        out_specs=[pl.BlockSpec((1, tm, n), lambda i, j: (i, j, 0)) for n in widths],
        compiler_params=_cparams(("parallel", "parallel")),
        name="inproj_rope" if rope else "inproj",
    )(*args)


WA_KEYS = 3 * WA_BLOCK


WA_BLOCKS_PER_STEP = 4


def _attn_a_kernel(sink_ref, q_ref, k_ref, v_ref, kc_ref, vc_ref, o_ref, s_ref, p_ref):
    step = pl.program_id(1)
    nk = WA_KEYS + CTX_LEN
    nq = WA_HEADS * WA_BLOCK
    kc, vc = kc_ref[0], vc_ref[0]
    starts = []
    for bb in range(WA_BLOCKS_PER_STEP):
        n = step * WA_BLOCKS_PER_STEP + bb
        starts.append(pl.multiple_of(jnp.clip((n - 1) * WA_BLOCK, 0, SEQ - WA_KEYS), WA_BLOCK))

    first_half = lax.broadcasted_iota(jnp.int32, (WA_BLOCK, LANES), 1) < HEAD_DIM
    swap = lambda v: pltpu.roll(v, HEAD_DIM, axis=1)

    for bb in range(WA_BLOCKS_PER_STEP):
        q = q_ref[0, bb * WA_BLOCK:(bb + 1) * WA_BLOCK, :].astype(F32) * ATT_SCALE
        q01, q23 = q[:, 0:LANES], q[:, LANES:]
        lhs = jnp.concatenate([
            jnp.where(first_half, q01, 0.0), jnp.where(first_half, swap(q01), 0.0),
            jnp.where(first_half, 0.0, swap(q23)), jnp.where(first_half, 0.0, q23)], axis=0)
        kall = jnp.concatenate([k_ref[0, pl.ds(starts[bb], WA_KEYS), :], kc], axis=0)
        s_ref[bb * nq:(bb + 1) * nq, :] = _bdot_nt(lhs, kall)

    nr = WA_BLOCKS_PER_STEP * nq
    rows = lax.broadcasted_iota(jnp.int32, (nr, nk), 0)
    cols = lax.broadcasted_iota(jnp.int32, (nr, nk), 1)
    row1 = lax.broadcasted_iota(jnp.int32, (nr, 1), 0)
    blk = row1 // nq
    qpos = (step * WA_BLOCKS_PER_STEP + blk) * WA_BLOCK + (rows & (WA_BLOCK - 1))
    kstart = starts[-1]
    for bb in range(WA_BLOCKS_PER_STEP - 2, -1, -1):
        kstart = jnp.where(blk == bb, starts[bb], kstart)
    valid = (cols >= WA_KEYS) | (jnp.abs(qpos - (kstart + cols)) <= WA_WINDOW)
    s = jnp.where(valid, s_ref[...], NEG_INF)
    head = (row1 // WA_BLOCK) % WA_HEADS
    sink = jnp.where(head == 0, sink_ref[0],
                     jnp.where(head == 1, sink_ref[1], jnp.where(head == 2, sink_ref[2], sink_ref[3])))
    m = jnp.maximum(jnp.max(s, axis=1, keepdims=True), sink)
    p = jnp.exp(s - m)
    inv = 1.0 / (jnp.sum(p, axis=1, keepdims=True) + jnp.exp(sink - m))
    p_ref[...] = p.astype(BF16)

    for bb in range(WA_BLOCKS_PER_STEP):
        vall = jnp.concatenate([v_ref[0, pl.ds(starts[bb], WA_KEYS), :], vc], axis=0)
        rs = slice(bb * nq, (bb + 1) * nq)
        o = jnp.dot(p_ref[rs, :], vall, preferred_element_type=F32) * inv[rs]
        o0, o1, o2, o3 = (o[h * WA_BLOCK:(h + 1) * WA_BLOCK] for h in range(WA_HEADS))
        qs = slice(bb * WA_BLOCK, (bb + 1) * WA_BLOCK)
        o_ref[0, qs, 0:LANES] = jnp.where(first_half, o0, swap(o1)).astype(o_ref.dtype)
        o_ref[0, qs, LANES:] = jnp.where(first_half, swap(o2), o3).astype(o_ref.dtype)


def _attn_a(sink, qkv, qkv_c):
    b = qkv.shape[0]
    nk = WA_KEYS + CTX_LEN
    tq = WA_BLOCKS_PER_STEP * WA_BLOCK
    nr = WA_BLOCKS_PER_STEP * WA_HEADS * WA_BLOCK
    return pl.pallas_call(
        _attn_a_kernel,
        out_shape=jax.ShapeDtypeStruct((b, SEQ, WA_HEADS * HEAD_DIM), BF16),
        scratch_shapes=[pltpu.VMEM((nr, nk), F32), pltpu.VMEM((nr, nk), BF16)],
        grid=(b, SEQ // tq),
        in_specs=[
            pl.BlockSpec(memory_space=pltpu.SMEM),
            pl.BlockSpec((1, tq, 2 * LANES), lambda i, j: (i, j, 0)),
            pl.BlockSpec((1, SEQ, LANES), lambda i, j: (i, 0, 2)),
            pl.BlockSpec((1, SEQ, LANES), lambda i, j: (i, 0, 3)),
            pl.BlockSpec((1, CTX_LEN, LANES), lambda i, j: (i, 0, 2)),
            pl.BlockSpec((1, CTX_LEN, LANES), lambda i, j: (i, 0, 3)),
        ],
        out_specs=pl.BlockSpec((1, tq, 2 * LANES), lambda i, j: (i, j, 0)),
        compiler_params=_cparams(("parallel", "arbitrary")),
        name="attn_window",
    )(sink, qkv, qkv, qkv, qkv_c, qkv_c)


NB_ROWS_PER_STEP = 16
NB_LOC = NA_KH * GRID_W
NB_DY_PAIRS = 2 * NA_KH - 2


def _attn_b_kernel(q_ref, k_ref, v_ref, kc_ref, vc_ref, t_ref, o_ref, s_ref, p_ref):
    i = pl.program_id(1)
    kc = kc_ref[0]
    vc = vc_ref[0]
    units = [(rr, pp) for rr in range(NB_ROWS_PER_STEP) for pp in range(NA_HEADS // 2)]
    starts, shifts = [], []
    for rr in range(NB_ROWS_PER_STEP):
        r = i * NB_ROWS_PER_STEP + rr
        rs = jnp.clip(r - NA_KH // 2, 0, GRID_ROWS - NA_KH)
        shifts.append(r - rs)
        starts.append(pl.multiple_of(rs * GRID_W, GRID_W))
    first_half = lax.broadcasted_iota(jnp.int32, (GRID_W, LANES), 1) < HEAD_DIM
    zero = jnp.zeros((), BF16)
    pw = 2 * GRID_W

    for u, (rr, pp) in enumerate(units):
        ls = slice(pp * LANES, (pp + 1) * LANES)
        if pp == 0:
            kw = k_ref[0, pl.ds(starts[rr], NB_LOC), :]
            q = q_ref[0, rr * GRID_W:(rr + 1) * GRID_W, :] * ATT_SCALE
        qs = q[:, ls]
        lhs = jnp.concatenate([jnp.where(first_half, qs, zero), jnp.where(first_half, zero, qs)], axis=0)
        bias = jnp.concatenate(
            [jnp.concatenate([t_ref[h, 2 * k - shifts[rr] + NA_KH - 1] for k in range(NA_KH // 2)], axis=1)
             for h in (2 * pp, 2 * pp + 1)], axis=0)
        s_ref[u * pw:(u + 1) * pw, 0:NB_LOC] = _bdot_nt(lhs, kw[:, ls]) + bias
        s_ref[u * pw:(u + 1) * pw, NB_LOC:] = _bdot_nt(lhs, kc[:, ls])

    s = s_ref[...]
    p = jnp.exp(s - jnp.max(s, axis=1, keepdims=True))
    inv = 1.0 / jnp.sum(p, axis=1, keepdims=True)
    p_ref[...] = p.astype(BF16)

    for u, (rr, pp) in enumerate(units):
        ls = slice(pp * LANES, (pp + 1) * LANES)
        if pp == 0:
            vw = v_ref[0, pl.ds(starts[rr], NB_LOC), :]
        rows = slice(u * pw, (u + 1) * pw)
        o = (jnp.dot(p_ref[rows, 0:NB_LOC], vw[:, ls], preferred_element_type=F32)
             + jnp.dot(p_ref[rows, NB_LOC:], vc[:, ls], preferred_element_type=F32)) * inv[rows]
        o_ref[0, rr * GRID_W:(rr + 1) * GRID_W, ls] = jnp.where(
            first_half, o[:GRID_W], o[GRID_W:]).astype(o_ref.dtype)


def _attn_b(qkv, qkv_c, table):
    b = qkv.shape[0]
    tq = NB_ROWS_PER_STEP * GRID_W
    w = NA_HEADS * HEAD_DIM
    nu = NB_ROWS_PER_STEP * NA_HEADS * GRID_W
    return pl.pallas_call(
        _attn_b_kernel,
        out_shape=jax.ShapeDtypeStruct((b, SEQ, w), BF16),
        scratch_shapes=[pltpu.VMEM((nu, NB_LOC + CTX_LEN), F32), pltpu.VMEM((nu, NB_LOC + CTX_LEN), BF16)],
        grid=(b, SEQ // tq),
        in_specs=[
            pl.BlockSpec((1, tq, w), lambda i, j: (i, j, 0)),
            pl.BlockSpec((1, SEQ, w), lambda i, j: (i, 0, 1)),
            pl.BlockSpec((1, SEQ, w), lambda i, j: (i, 0, 2)),
            pl.BlockSpec((1, CTX_LEN, w), lambda i, j: (i, 0, 1)),
            pl.BlockSpec((1, CTX_LEN, w), lambda i, j: (i, 0, 2)),
            pl.BlockSpec((NA_HEADS, NB_DY_PAIRS, GRID_W, LANES), lambda i, j: (0, 0, 0, 0)),
        ],
        out_specs=pl.BlockSpec((1, tq, w), lambda i, j: (i, j, 0)),
        compiler_params=_cparams(("parallel", "arbitrary")),
        name="attn_neighbourhood",
    )(qkv, qkv, qkv, qkv_c, qkv_c, table)


def _split3(a):
    a1 = a.astype(BF16)
    r1 = a - a1.astype(F32)
    a2 = r1.astype(BF16)
    a3 = (r1 - a2.astype(F32)).astype(BF16)
    return a1, a2, a3


def _bias_kernel(r_ref, oh_ref, o_ref):
    oh = oh_ref[...]
    o_ref[...] = sum(jnp.dot(t, oh, preferred_element_type=F32) for t in _split3(r_ref[...]))


def _na_bias_tables(rpb):
    ndy, ndx = 2 * NA_KH - 1, 2 * NA_KW - 1
    qc = np.arange(GRID_W)[:, None]
    x = np.arange(GRID_W)[None, :]
    dx = np.clip(x - qc, -(NA_KW - 1), NA_KW - 1) + NA_KW - 1
    onehot = (np.arange(LANES)[:, None, None] == dx[None]).reshape(LANES, GRID_W * GRID_W)
    cstart = np.clip(qc - NA_KW // 2, 0, GRID_W - NA_KW)
    inside = (x >= cstart) & (x < cstart + NA_KW)
    rows = DEPTH * NA_HEADS * ndy
    r = jnp.pad(rpb.astype(F32).reshape(rows, ndx), ((0, LANES - rows), (0, LANES - ndx)))
    m = pl.pallas_call(
        _bias_kernel,
        out_shape=jax.ShapeDtypeStruct((LANES, GRID_W * GRID_W), F32),
        name="na_bias_expand",
    )(r, jnp.asarray(onehot, BF16))
    m = m[:rows].reshape(DEPTH, NA_HEADS, ndy, GRID_W, GRID_W)
    m = jnp.where(jnp.asarray(inside), m, NEG_INF)
    return jnp.concatenate([m[:, :, :ndy - 1], m[:, :, 1:]], axis=-1)


def _ctx_attn_kernel(sink_ref, a_ref, b_ref, oa_ref, ob_ref):
    a = a_ref[0]
    row1 = lax.broadcasted_iota(jnp.int32, (2 * CTX_LEN, 1), 0)
    for g in range(WA_KV_HEADS):
        h0, h1 = 2 * g, 2 * g + 1
        q2 = jnp.concatenate([a[:, h0 * HEAD_DIM:(h0 + 1) * HEAD_DIM],
                              a[:, h1 * HEAD_DIM:(h1 + 1) * HEAD_DIM]], axis=0)
        k = a[:, 2 * LANES + g * HEAD_DIM:2 * LANES + (g + 1) * HEAD_DIM]
        v = a[:, 3 * LANES + g * HEAD_DIM:3 * LANES + (g + 1) * HEAD_DIM]
        s = _bdot_nt(q2, k) * ATT_SCALE
        sink = jnp.where(row1 < CTX_LEN, sink_ref[h0], sink_ref[h1])
        m = jnp.maximum(jnp.max(s, axis=1, keepdims=True), sink)
        p = jnp.exp(s - m)
        den = jnp.sum(p, axis=1, keepdims=True) + jnp.exp(sink - m)
        o = _bdot(p, v) / den
        oa_ref[0, :, h0 * HEAD_DIM:(h0 + 1) * HEAD_DIM] = o[:CTX_LEN]
        oa_ref[0, :, h1 * HEAD_DIM:(h1 + 1) * HEAD_DIM] = o[CTX_LEN:]
    bq = b_ref[0]
    w = NA_HEADS * HEAD_DIM
    for h in range(NA_HEADS):
        hs = slice(h * HEAD_DIM, (h + 1) * HEAD_DIM)
        s = _bdot_nt(bq[:, hs], bq[:, w + h * HEAD_DIM:w + (h + 1) * HEAD_DIM]) * ATT_SCALE
        m = jnp.max(s, axis=1, keepdims=True)
        p = jnp.exp(s - m)
        den = jnp.sum(p, axis=1, keepdims=True)
        ob_ref[0, :, hs] = _bdot(p, bq[:, 2 * w + h * HEAD_DIM:2 * w + (h + 1) * HEAD_DIM]) / den


def _ctx_attn(sink, qkv_a_c, qkv_b_c):
    b = qkv_a_c.shape[0]
    w = 2 * LANES
    return pl.pallas_call(
        _ctx_attn_kernel,
        out_shape=[jax.ShapeDtypeStruct((b, CTX_LEN, w), F32)] * 2,
        grid=(b,),
        in_specs=[
            pl.BlockSpec(memory_space=pltpu.SMEM),
            pl.BlockSpec((1, CTX_LEN, A_COLS), lambda i: (i, 0, 0)),
            pl.BlockSpec((1, CTX_LEN, B_COLS), lambda i: (i, 0, 0)),
        ],
        out_specs=[pl.BlockSpec((1, CTX_LEN, w), lambda i: (i, 0, 0))] * 2,
        compiler_params=_cparams(("parallel",)),
        name="attn_context",
    )(sink, qkv_a_c, qkv_b_c)


CONV_HALO = SUBLANES


def _conv_kernel(prev_ref, cur_ref, next_ref, w_ref, b_ref, o_ref, ext_ref, *, tl, nt):
    j = pl.program_id(1)
    ext_ref[0:CONV_HALO, :] = jnp.where(j > 0, prev_ref[0], 0.0)
    ext_ref[CONV_HALO:CONV_HALO + tl, :] = cur_ref[0]
    ext_ref[CONV_HALO + tl:, :] = jnp.where(j < nt - 1, next_ref[0], 0.0)
    acc = jnp.zeros((tl, XBC_COLS), F32) + b_ref[...]
    base = CONV_HALO - SSM_CONV // 2
    for k in range(SSM_CONV):
        acc = acc + w_ref[k:k + 1, :] * ext_ref[base + k:base + k + tl, :]
    o_ref[0] = _silu(acc)


def _conv_silu(xz, conv_w, conv_b, tl):
    b, t, _ = xz.shape
    nt = t // tl
    hb = tl // CONV_HALO
    last = t // CONV_HALO - 1
    return pl.pallas_call(
        functools.partial(_conv_kernel, tl=tl, nt=nt),
        out_shape=jax.ShapeDtypeStruct((b, t, XBC_COLS), F32),
        grid=(b, nt),
        in_specs=[
            pl.BlockSpec((1, CONV_HALO, XBC_COLS), lambda i, j: (i, jnp.maximum(j * hb - 1, 0), 0)),
            pl.BlockSpec((1, tl, XBC_COLS), lambda i, j: (i, j, 0)),
            pl.BlockSpec((1, CONV_HALO, XBC_COLS), lambda i, j: (i, jnp.minimum((j + 1) * hb, last), 0)),
            pl.BlockSpec((SUBLANES, XBC_COLS), lambda i, j: (0, 0)),
            pl.BlockSpec((1, XBC_COLS), lambda i, j: (0, 0)),
        ],
        out_specs=pl.BlockSpec((1, tl, XBC_COLS), lambda i, j: (i, j, 0)),
        scratch_shapes=[pltpu.VMEM((tl + 2 * CONV_HALO, XBC_COLS), F32)],
        compiler_params=_cparams(("parallel", "parallel")),
        name="ssm_conv",
    )(xz, xz, xz, conv_w, conv_b)


Q = SSM_CHUNK
GS = SSM_GROUPS * SSM_STATE
HPG = SSM_HEADS // SSM_GROUPS
SSD_STATE_SHAPE = (SSM_GROUPS, SSM_STATE, HPG * SSM_HEAD_DIM)
SSD_BATCH_PER_STEP = 4


def _softplus(v):
    return jnp.maximum(v, 0.0) + jnp.log1p(jnp.exp(-jnp.abs(v)))


def _cumsum_mat(tri, a):
    r = jnp.dot(tri, jnp.concatenate(_split3(a), axis=1), preferred_element_type=F32)
    n = a.shape[1]
    return r[:, 0:n] + r[:, n:2 * n] + r[:, 2 * n:]


def _ssd_kernel(xm_ref, dtm_ref, xb_ref, dtb_ref, bias_ref, alog_ref, h0f_ref, h0b_ref,
                y1_ref, y2_ref, hf_out_ref, hb_out_ref, hf_ref, hb_ref, *, nc):
    i = pl.program_id(1)

    @pl.when(i == 0)
    def _():
        hf_ref[...] = h0f_ref[...]
        hb_ref[...] = h0b_ref[...]

    ii = lax.broadcasted_iota(jnp.int32, (Q, Q), 0)
    jj = lax.broadcasted_iota(jnp.int32, (Q, Q), 1)
    lower = ii >= jj
    diag = ii == jj
    tril =jnp.where(lower, 1.0, 0.0).astype(BF16)
    triu = jnp.where(ii <= jj, 1.0, 0.0).astype(BF16)
    first_half = lax.broadcasted_iota(jnp.int32, (Q, LANES), 1) < SSM_HEAD_DIM
    a_row = -jnp.exp(alog_ref[...])
    gw = HPG * SSM_HEAD_DIM

    def group_operands(xbc, g):
        bg = xbc[:, SSM_INNER + g * SSM_STATE:SSM_INNER + (g + 1) * SSM_STATE]
        cg = xbc[:, SSM_INNER + GS + g * SSM_STATE:SSM_INNER + GS + (g + 1) * SSM_STATE]
        return bg, cg.astype(BF16), xbc[:, g * gw:(g + 1) * gw]

    nbs = range(SSD_BATCH_PER_STEP)
    groups = range(SSM_GROUPS)
    bias = bias_ref[...]
    zero = jnp.zeros((), F32)


    xm = [xm_ref[bi] for bi in nbs]
    xb = [xb_ref[bi] for bi in nbs]
    dtm = [_softplus(dtm_ref[bi] + bias) for bi in nbs]
    dtb = [_softplus(dtb_ref[bi] + bias) for bi in nbs]
    cumf =[_cumsum_mat(tril, d * a_row) for d in dtm]
    cumr = [_cumsum_mat(triu, d * a_row) for d in dtm]
    cumb = [_cumsum_mat(triu, d * a_row) for d in dtb]

    opm = [[group_operands(xm[bi], g) for g in groups] for bi in nbs]
    opb = [[group_operands(xb[bi], g) for g in groups] for bi in nbs]
    gmat = [[_bdot_nt(opm[bi][g][1], opm[bi][g][0]) for g in groups] for bi in nbs]
    hf_prev = [[hf_ref[bi, g] for g in groups] for bi in nbs]
    hb_prev = [[hb_ref[bi, g] for g in groups] for bi in nbs]
    inter_f = [[jnp.dot(opm[bi][g][1], hf_prev[bi][g].astype(BF16), preferred_element_type=F32)
                for g in groups] for bi in nbs]
    inter_b = [[jnp.dot(opb[bi][g][1], hb_prev[bi][g].astype(BF16), preferred_element_type=F32)
                for g in groups] for bi in nbs]
    bt_m = [[opm[bi][g][0].T for g in groups] for bi in nbs]
    bt_b = [[opb[bi][g][0].T for g in groups] for bi in nbs]

    cumf_t = [c.T for c in cumf]
    cumr_t = [c.T for c in cumr]
    dtm_t = [d.T for d in dtm]
    cumb_t = [c.T for c in cumb]
    dtb_t = [d.T for d in dtb]

    heads = [(bi, h) for bi in nbs for h in range(SSM_HEADS)]
    hb_ = lambda h: SSM_HEADS + h
    cf, cr, cb, ein_f, ein_b, wm, wst_f, wst_b = {}, {}, {}, {}, {}, {}, {}, {}
    for t in range(len(heads) + 1):
        if t < len(heads):
            bi, h = k = heads[t]
            cf[k] = jnp.broadcast_to(cumf[bi][:, h:h + 1], (Q, Q))
            cr[k] = jnp.broadcast_to(cumr[bi][:, hb_(h):hb_(h) + 1], (Q, Q))
            cb[k] = jnp.broadcast_to(cumb[bi][:, hb_(h):hb_(h) + 1], (Q, Q))
        if t >= 1:
            bi, h = k = heads[t - 1]
            rf = cumf_t[bi][h:h + 1, :]
            rb = cumb_t[bi][hb_(h):hb_(h) + 1, :]
            dtf_row = dtm_t[bi][h:h + 1, :]
            dtb_row = dtm_t[bi][hb_(h):hb_(h) + 1, :]
            arg = jnp.where(lower, cf[k] - rf, cr[k] - cumr_t[bi][hb_(h):hb_(h) + 1, :])
            wm[k] = (gmat[bi][h // HPG] * (jnp.exp(arg) * jnp.where(lower, dtf_row, dtb_row)
                                           + jnp.where(diag, dtb_row, zero))).astype(BF16)
            ein_f[k] = jnp.exp(cf[k])
            ein_b[k] = jnp.exp(cb[k])
            w_f = jnp.exp(rf[:, Q - 1:Q] - rf) * dtf_row
            w_b = jnp.exp(rb[:, 0:1] - rb) * dtb_t[bi][hb_(h):hb_(h) + 1, :]
            wst_f[k] = (bt_m[bi][h // HPG] * w_f).astype(BF16)
            wst_b[k] = (bt_b[bi][h // HPG] * w_b).astype(BF16)

    def pair_rhs(x, pr):
        xp = x[:, pr * LANES:(pr + 1) * LANES]
        return jnp.concatenate([jnp.where(first_half, xp, zero), jnp.where(first_half, zero, xp)],
                               axis=0).astype(BF16)

    def pair_dot(mats, bi, pr, rhs):
        lhs = jnp.concatenate([mats[bi, 2 * pr], mats[bi, 2 * pr + 1]], axis=1)
        return jnp.dot(lhs, rhs, preferred_element_type=F32)

    pairs = [(bi, pr) for bi in nbs for pr in range(SSM_HEADS // 2)]
    rhs_m = {k: pair_rhs(xm[k[0]], k[1]) for k in pairs}
    rhs_b = {k: pair_rhs(xb[k[0]], k[1]) for k in pairs}
    y_intra = {k: pair_dot(wm, k[0], k[1], rhs_m[k]) for k in pairs}
    st_f = {k: pair_dot(wst_f, k[0], k[1], rhs_m[k]) for k in pairs}
    st_b = {k: pair_dot(wst_b, k[0], k[1], rhs_b[k]) for k in pairs}

    ppg = HPG // 2
    for bi, pr in pairs:
        g, k = pr // ppg, pr % ppg
        ls = slice(k * LANES, (k + 1) * LANES)
        sin_f = jnp.where(first_half, ein_f[bi, 2 * pr], ein_f[bi, 2 * pr + 1])
        sin_b = jnp.where(first_half, ein_b[bi, 2 * pr], ein_b[bi, 2 * pr + 1])
        y1_ref[bi, :, pr * LANES:(pr + 1) * LANES] = y_intra[bi, pr] + sin_f * inter_f[bi][g][:, ls]
        y2_ref[bi, :, pr * LANES:(pr + 1) * LANES] = sin_b * inter_b[bi][g][:, ls]
        hf_ref[bi, g, :, ls] = hf_prev[bi][g][:, ls] * sin_f[Q - 1:Q, :] + st_f[bi, pr]
        hb_ref[bi, g, :, ls] = hb_prev[bi][g][:, ls] * sin_b[0:1, :] + st_b[bi, pr]

    @pl.when(i == nc - 1)
    def _():
        hf_out_ref[...] = hf_ref[...]
        hb_out_ref[...] = hb_ref[...]


def _ssd(xbc, dt_raw, dt_bias, a_log, h0f, h0b):
    b, t, _ = xbc.shape
    nc = t // Q
    nb = SSD_BATCH_PER_STEP
    st_shape = (b,) + SSD_STATE_SHAPE
    st_spec = pl.BlockSpec((nb,) + SSD_STATE_SHAPE, lambda i, j: (i, 0, 0, 0))
    fwd = lambda i, j: (i, j, 0)
    bwd = lambda i, j: (i, nc - 1 - j, 0)
    return pl.pallas_call(
        functools.partial(_ssd_kernel, nc=nc),
        out_shape=[jax.ShapeDtypeStruct((b, t, SSM_INNER), F32)] * 2
        + [jax.ShapeDtypeStruct(st_shape, F32)] * 2,
        grid=(b // nb, nc),
        in_specs=[
            pl.BlockSpec((nb, Q, XBC_COLS), fwd),
            pl.BlockSpec((nb, Q, DT_PAD), fwd),
            pl.BlockSpec((nb, Q, XBC_COLS), bwd),
            pl.BlockSpec((nb, Q, DT_PAD), bwd),
            pl.BlockSpec((1, DT_PAD), lambda i, j: (0, 0)),
            pl.BlockSpec((1, DT_PAD), lambda i, j: (0, 0)),
            st_spec, st_spec,
        ],
        out_specs=[pl.BlockSpec((nb, Q, SSM_INNER), fwd), pl.BlockSpec((nb, Q, SSM_INNER), bwd),
                   st_spec, st_spec],
        scratch_shapes=[pltpu.VMEM((nb,) + SSD_STATE_SHAPE, F32)] * 2,
        compiler_params=_cparams(("parallel", "arbitrary")),
        name="ssd_scan",
    )(xbc, dt_raw, xbc, dt_raw, dt_bias, a_log, h0f, h0b)


MXU_TILE = 256
FFN_SPLITS = (0, 6 * MXU_TILE, D_FF)


def _mix_ffn_kernel(x_ref, mod_ref, oa_ref, ob_ref, y1_ref, y2_ref, xs_ref, z_ref, d_ref, sg_ref,
                    wo_ref, g_ref, wfi_ref, wfo_ref, gfin_ref, o_ref, *, final):
    y = y1_ref[0] + y2_ref[0] + d_ref[...] * xs_ref[0]
    y = y * _silu(z_ref[0])
    var = jnp.mean(y * y, axis=-1, keepdims=True)
    oc = (y * lax.rsqrt(var + EPS) * sg_ref[...]).astype(BF16)
    wa = 2 * LANES
    mix = (jnp.dot(oa_ref[0].astype(BF16), wo_ref[0, 0:wa, :], preferred_element_type=F32)
           + jnp.dot(ob_ref[0].astype(BF16), wo_ref[0, wa:2 * wa, :], preferred_element_type=F32)
           + jnp.dot(oc, wo_ref[0, 2 * wa:, :], preferred_element_type=F32))
    xn = x_ref[0] + mod_ref[0, 2:3, :] * mix
    h = _norm_mod(xn, g_ref[...], mod_ref[0, 3:4, :], mod_ref[0, 4:5, :]).astype(BF16)
    acc = None
    for lo, hi in zip(FFN_SPLITS[:-1], FFN_SPLITS[1:]):
        gate = jnp.dot(h, wfi_ref[0, :, lo:hi], preferred_element_type=F32)
        up = jnp.dot(h, wfi_ref[0, :, D_FF + lo:D_FF + hi], preferred_element_type=F32)
        act = (_silu(gate) * up).astype(BF16)
        part = jnp.dot(act, wfo_ref[0, lo:hi, :], preferred_element_type=F32)
        acc = part if acc is None else acc + part
    out = xn + mod_ref[0, 5:6, :] * acc
    if final:
        var = jnp.mean(out * out, axis=-1, keepdims=True)
        out = out * lax.rsqrt(var + EPS) * gfin_ref[...]
    o_ref[0] = out


def _mix_ffn(x, mods, o_a, o_b, y1, y2, xbc, xz, d_exp, ssm_g, w_out, g_ffn, w_ffn_in, w_ffn_out,
             g_final, layer, tm, final):
    b, t, _ = x.shape
    per_batch = mods.shape[0] > 1
    row = lambda n: pl.BlockSpec((1, tm, n), lambda i, j: (i, j, 0))
    const = lambda shape: pl.BlockSpec(shape, lambda i, j: (0, 0), pipeline_mode=pl.Buffered(1))
    weight = lambda r, c: pl.BlockSpec((1, r, c), lambda i, j: (layer, 0, 0), pipeline_mode=pl.Buffered(1))
    return pl.pallas_call(
        functools.partial(_mix_ffn_kernel, final=final),
        out_shape=jax.ShapeDtypeStruct((b, t, D_MODEL), F32),
        grid=(b, t // tm),
        in_specs=[
            row(D_MODEL),
            pl.BlockSpec((1, SUBLANES, D_MODEL), (lambda i, j: (i, 0, 0)) if per_batch else (lambda i, j: (0, 0, 0))),
            row(2 * LANES), row(2 * LANES), row(SSM_INNER), row(SSM_INNER),
            pl.BlockSpec((1, tm, SSM_INNER), lambda i, j: (i, j, 0)),
            pl.BlockSpec((1, tm, SSM_INNER), lambda i, j: (i, j, 2)),
            const((1, SSM_INNER)), const((1, SSM_INNER)),
            weight(D_MODEL, D_MODEL),
            const((1, D_MODEL)),
            weight(D_MODEL, 2 * D_FF),
            weight(D_FF, D_MODEL),
            const((1, D_MODEL)),
        ],
        out_specs=row(D_MODEL),
        compiler_params=_cparams(("parallel", "parallel")),
        name="mix_ffn_final" if final else "mix_ffn",
    )(x, mods, o_a, o_b, y1, y2, xbc, xz, d_exp, ssm_g, w_out, g_ffn, w_ffn_in, w_ffn_out, g_final)


def _rope_tables():
    t = np.arange(SEQ)
    pos = np.stack([t // GRID_W, t % GRID_W], axis=1).astype(np.float64)
    quarter = HEAD_DIM // 4
    inv_freq = ROPE_BASE ** (-np.arange(quarter, dtype=np.float64) / quarter)
    lane = np.arange(LANES) % HEAD_DIM
    half = lane // (HEAD_DIM // 2)
    idx = lane % (HEAD_DIM // 2)
    ang = pos[:, half] * inv_freq[idx % quarter][None, :]
    cos, sin = np.cos(ang), np.sin(ang)
    first = (idx < quarter)[None, :]
    tabs = (cos, np.where(first, -sin, 0.0), np.where(first, 0.0, sin))
    return tuple(jnp.asarray(v, F32) for v in tabs)


def _pad_lanes(v, n=LANES):
    v = v.reshape(1, -1)
    return jnp.pad(v, ((0, 0), (0, n - v.shape[1])))


def kernel(x, c, ctx, c_ctx, w_mod, b_mod, g_mix, w_in, wa_sink, na_rpb, ssm_conv_w, ssm_conv_b,
           ssm_dt_bias, ssm_a_log, ssm_d, ssm_norm_g, w_out, g_ffn, w_ffn_in, w_ffn_out, g_final):
    cin = jnp.concatenate([c, c_ctx[None, :], jnp.zeros((SUBLANES - BATCH - 1, D_MODEL), F32)], axis=0)
    mod_all = _modulation(cin, w_mod, b_mod)
    rope_tabs = _rope_tables()
    bias_tabs = _na_bias_tables(na_rpb)
    zeros_state = jnp.zeros((BATCH,) + SSD_STATE_SHAPE, F32)
    gfin = g_final.reshape(1, D_MODEL)
    w_proj = w_in.astype(BF16)
    wo = w_out.astype(BF16)
    wfi = w_ffn_in.astype(BF16)
    wfo = w_ffn_out.astype(BF16)

    xl, xc = x, ctx
    for l in range(DEPTH):
        last = l == DEPTH - 1
        m6 = mod_all[l].reshape(SUBLANES, 6, D_MODEL)
        mods_l = jnp.pad(m6[:BATCH], ((0, 0), (0, 2), (0, 0)))
        mods_c = jnp.pad(m6[BATCH:BATCH + 1], ((0, 0), (0, 2), (0, 0)))
        w_dt = jnp.pad(w_in[l][:, W_DT:], ((0, 0), (0, DT_PAD - DT_COLS))).astype(BF16)
        g1 = g_mix[l].reshape(1, D_MODEL)
        g2 = g_ffn[l].reshape(1, D_MODEL)
        conv_w = jnp.pad(ssm_conv_w[l], ((0, SUBLANES - SSM_CONV), (0, 0)))
        conv_b = ssm_conv_b[l].reshape(1, XBC_COLS)
        dt_bias = _pad_lanes(ssm_dt_bias[l])
        a_log = _pad_lanes(ssm_a_log[l])
        d_exp = jnp.repeat(ssm_d[l], SSM_HEAD_DIM).reshape(1, SSM_INNER)
        sg = ssm_norm_g[l].reshape(1, SSM_INNER)

        a_c, b_c, xz_c, dt_c = _inproj(xc, mods_c, g1, w_proj, l, w_dt, None, CTX_LEN)
        a_l, b_l, xz_l, dt_l = _inproj(xl, mods_l, g1, w_proj, l, w_dt, rope_tabs, TM_INPROJ)

        o_a = _attn_a(wa_sink[l], a_l, a_c)
        o_b = _attn_b(b_l, b_c, bias_tabs[l])

        xbc_c = _conv_silu(xz_c, conv_w, conv_b, CTX_LEN)
        xbc_l = _conv_silu(xz_l, conv_w, conv_b, TM_CONV)
        y1_c, y2_c, h_f, h_b = _ssd(xbc_c, dt_c, dt_bias, a_log, zeros_state, zeros_state)
        y1_l, y2_l, _, _ = _ssd(xbc_l, dt_l, dt_bias, a_log, h_f, h_b)

        xl = _mix_ffn(xl, mods_l, o_a, o_b, y1_l, y2_l, xbc_l, xz_l, d_exp, sg, wo, g2, wfi, wfo, gfin,
                      l, TM_MIX, last)
        if not last:
            o_ac, o_bc = _ctx_attn(wa_sink[l], a_c, b_c)
            xc = _mix_ffn(xc, mods_c, o_ac, o_bc, y1_c, y2_c, xbc_c, xz_c, d_exp, sg, wo, g2, wfi, wfo,
                          gfin, l, CTX_LEN, False)
    return xl
```

```python
import functools

import numpy as np
import jax
import jax.numpy as jnp
from jax import lax
from jax.experimental import pallas as pl
from jax.experimental.pallas import tpu as pltpu

F32 = jnp.float32
BF16 = jnp.bfloat16

D_MODEL = 1024
BATCH = 4
SEQ = 4096
DEPTH = 2
GRID_W = 64
GRID_ROWS = SEQ // GRID_W
CTX_LEN = 256
EPS = 1e-6
HEAD_DIM = 64
ROPE_BASE = 10000.0
WA_HEADS = 4
WA_KV_HEADS = 2
WA_WINDOW = 128
WA_BLOCK = 128
NA_HEADS = 4
NA_KH = 8
NA_KW = 16
SSM_HEADS = 8
SSM_HEAD_DIM = 64
SSM_INNER = SSM_HEADS * SSM_HEAD_DIM
SSM_GROUPS = 2
SSM_STATE = 128
SSM_CONV = 7
SSM_CHUNK = 128
D_FF = 2816
XBC_COLS = SSM_INNER + 2 * SSM_GROUPS * SSM_STATE
DT_COLS = 2 * SSM_HEADS

LANES = 128
SUBLANES = 8
V7X_VMEM_BYTES = 64 * 1024 * 1024
VMEM_LIMIT = V7X_VMEM_BYTES - 8 * 1024 * 1024

TM_INPROJ = 1024
TM_CONV = 1024
TM_MIX = 512

A_COLS = 512
B_COLS = 768
XZ_COLS = 1536
DT_PAD = LANES

ATT_SCALE = HEAD_DIM ** -0.5
NEG_INF = float("-inf")
NT_DIMS = (((1,), (1,)), ((), ()))


def _silu(v):
    return v / (1.0 + jnp.exp(-v))


def _bdot(a, b):
    return jnp.dot(a.astype(BF16), b.astype(BF16), preferred_element_type=F32)


def _bdot_nt(a, b):
    return lax.dot_general(a.astype(BF16), b.astype(BF16), NT_DIMS, preferred_element_type=F32)


def _cparams(sem):
    return pltpu.CompilerParams(dimension_semantics=sem, vmem_limit_bytes=VMEM_LIMIT)


MOD_TN = 6 * D_MODEL // 4


def _mod_kernel(c_ref, w_ref, b_ref, o_ref):
    s = _silu(c_ref[...])
    o_ref[0] = _bdot(s, w_ref[0]) + b_ref[0]


def _modulation(cin, w_mod, b_mod):
    n = 6 * D_MODEL
    return pl.pallas_call(
        _mod_kernel,
        out_shape=jax.ShapeDtypeStruct((DEPTH, SUBLANES, n), F32),
        grid=(DEPTH, n // MOD_TN),
        in_specs=[
            pl.BlockSpec((SUBLANES, D_MODEL), lambda l, j: (0, 0)),
            pl.BlockSpec((1, D_MODEL, MOD_TN), lambda l, j: (l, 0, j)),
            pl.BlockSpec((1, 1, MOD_TN), lambda l, j: (l, 0, j)),
        ],
        out_specs=pl.BlockSpec((1, SUBLANES, MOD_TN), lambda l, j: (l, 0, j)),
        compiler_params=_cparams(("parallel", "parallel")),
        name="modulation",
    )(cin, w_mod, b_mod.reshape(DEPTH, 1, n))


def _norm_mod(x, g, shift, scale):
    var = jnp.mean(x * x, axis=-1, keepdims=True)
    h = x * lax.rsqrt(var + EPS) * g
    return h * (1.0 + scale) + shift


W_QA = 0
W_QB = W_QA + WA_HEADS * HEAD_DIM
W_Z = W_QB + NA_HEADS * HEAD_DIM
W_KVA = W_Z + SSM_INNER
W_KVB = W_KVA + 2 * WA_KV_HEADS * HEAD_DIM
W_XBC = W_KVB + 2 * NA_HEADS * HEAD_DIM
W_DT = W_XBC + XBC_COLS
IN_COLS = W_DT + DT_COLS
ROPE_QUARTER = HEAD_DIM // 4


def _inproj_kernel(x_ref, mod_ref, g_ref, w_ref, wdt_ref, *rest, rope):
    if rope:
        cos_ref, sa_ref, sb_ref, oa_ref, ob_ref, oxz_ref, odt_ref = rest
    else:
        oa_ref, ob_ref, oxz_ref, odt_ref = rest
    h = _norm_mod(x_ref[0], g_ref[...], mod_ref[0, 0:1, :], mod_ref[0, 1:2, :]).astype(BF16)

    def proj(lo, hi):
        return jnp.dot(h, w_ref[0, :, lo:hi], preferred_element_type=F32)

    qa = proj(W_QA, W_QB)
    kva = proj(W_KVA, W_KVB)
    if rope:
        cos, sa, sb = cos_ref[...], sa_ref[...], sb_ref[...]

        def rot(v):
            up = pltpu.roll(v, LANES - ROPE_QUARTER, axis=1)
            dn = pltpu.roll(v, ROPE_QUARTER, axis=1)
            return v * cos + up * sa + dn * sb

        oa_ref[0, :, 0:LANES] = rot(qa[:, 0:LANES]).astype(BF16)
        oa_ref[0, :, LANES:2 * LANES] = rot(qa[:, LANES:]).astype(BF16)
        oa_ref[0, :, 2 * LANES:3 * LANES] = rot(kva[:, 0:LANES]).astype(BF16)
        oa_ref[0, :, 3 * LANES:] = kva[:, LANES:].astype(BF16)
    else:
        oa_ref[0, :, 0:2 * LANES] = qa.astype(BF16)
        oa_ref[0, :, 2 * LANES:] = kva.astype(BF16)
    ob_ref[0, :, 0:2 * LANES] = proj(W_QB, W_Z).astype(BF16)
    ob_ref[0, :, 2 * LANES:] = proj(W_KVB, W_XBC).astype(BF16)
    oxz_ref[0, :, 0:XBC_COLS] = proj(W_XBC, W_DT)
    oxz_ref[0, :, XBC_COLS:] = proj(W_Z, W_KVA)
    odt_ref[0] = jnp.dot(h, wdt_ref[...], preferred_element_type=F32)


def _inproj(x, mods, g, w, layer, wdt, rope_tabs, tm):
    b, t, _ = x.shape
    rope = rope_tabs is not None
    per_batch = mods.shape[0] > 1
    in_specs = [
        pl.BlockSpec((1, tm, D_MODEL), lambda i, j: (i, j, 0)),
        pl.BlockSpec((1, SUBLANES, D_MODEL), (lambda i, j: (i, 0, 0)) if per_batch else (lambda i, j: (0, 0, 0))),
        pl.BlockSpec((1, D_MODEL), lambda i, j: (0, 0)),
        pl.BlockSpec((1, D_MODEL, IN_COLS), lambda i, j: (layer, 0, 0)),
        pl.BlockSpec((D_MODEL, DT_PAD), lambda i, j: (0, 0)),
    ]
    args = [x, mods, g, w, wdt]
    if rope:
        in_specs += [pl.BlockSpec((tm, LANES), lambda i, j: (j, 0))] * 3
        args += list(rope_tabs)
    widths = (A_COLS, B_COLS, XZ_COLS, DT_PAD)
    return pl.pallas_call(
        functools.partial(_inproj_kernel, rope=rope),
        out_shape=[jax.ShapeDtypeStruct((b, t, n), dt) for n, dt in zip(widths, (BF16, BF16, F32, F32))],
        grid=(b, t // tm),
        in_specs=in_specs,
        out_specs=[pl.BlockSpec((1, tm, n), lambda i, j: (i, j, 0)) for n in widths],
        compiler_params=_cparams(("parallel", "parallel")),
        name="inproj_rope" if rope else "inproj",
    )(*args)


WA_KEYS = 3 * WA_BLOCK


WA_BLOCKS_PER_STEP = 4


def _attn_a_kernel(sink_ref, q_ref, k_ref, v_ref, kc_ref, vc_ref, o_ref, s_ref, p_ref):
    step = pl.program_id(1)
    nk = WA_KEYS + CTX_LEN
    nq = WA_HEADS * WA_BLOCK
    kc, vc = kc_ref[0], vc_ref[0]
    starts = []
    for bb in range(WA_BLOCKS_PER_STEP):
        n = step * WA_BLOCKS_PER_STEP + bb
        starts.append(pl.multiple_of(jnp.clip((n - 1) * WA_BLOCK, 0, SEQ - WA_KEYS), WA_BLOCK))

    first_half = lax.broadcasted_iota(jnp.int32, (WA_BLOCK, LANES), 1) < HEAD_DIM
    swap = lambda v: pltpu.roll(v, HEAD_DIM, axis=1)

    for bb in range(WA_BLOCKS_PER_STEP):
        q = q_ref[0, bb * WA_BLOCK:(bb + 1) * WA_BLOCK, :].astype(F32) * ATT_SCALE
        q01, q23 = q[:, 0:LANES], q[:, LANES:]
        lhs = jnp.concatenate([
            jnp.where(first_half, q01, 0.0), jnp.where(first_half, swap(q01), 0.0),
            jnp.where(first_half, 0.0, swap(q23)), jnp.where(first_half, 0.0, q23)], axis=0)
        kall = jnp.concatenate([k_ref[0, pl.ds(starts[bb], WA_KEYS), :], kc], axis=0)
        s_ref[bb * nq:(bb + 1) * nq, :] = _bdot_nt(lhs, kall)

    nr = WA_BLOCKS_PER_STEP * nq
    rows = lax.broadcasted_iota(jnp.int32, (nr, nk), 0)
    cols = lax.broadcasted_iota(jnp.int32, (nr, nk), 1)
    row1 = lax.broadcasted_iota(jnp.int32, (nr, 1), 0)
    blk = row1 // nq
    qpos = (step * WA_BLOCKS_PER_STEP + blk) * WA_BLOCK + (rows & (WA_BLOCK - 1))
    kstart = starts[-1]
    for bb in range(WA_BLOCKS_PER_STEP - 2, -1, -1):
        kstart = jnp.where(blk == bb, starts[bb], kstart)
    valid = (cols >= WA_KEYS) | (jnp.abs(qpos - (kstart + cols)) <= WA_WINDOW)
    s = jnp.where(valid, s_ref[...], NEG_INF)
    head = (row1 // WA_BLOCK) % WA_HEADS
    sink = jnp.where(head == 0, sink_ref[0],
                     jnp.where(head == 1, sink_ref[1], jnp.where(head == 2, sink_ref[2], sink_ref[3])))
    m = jnp.maximum(jnp.max(s, axis=1, keepdims=True), sink)
    p = jnp.exp(s - m)
    inv = 1.0 / (jnp.sum(p, axis=1, keepdims=True) + jnp.exp(sink - m))
    p_ref[...] = p.astype(BF16)

    for bb in range(WA_BLOCKS_PER_STEP):
        vall = jnp.concatenate([v_ref[0, pl.ds(starts[bb], WA_KEYS), :], vc], axis=0)
        rs = slice(bb * nq, (bb + 1) * nq)
        o = jnp.dot(p_ref[rs, :], vall, preferred_element_type=F32) * inv[rs]
        o0, o1, o2, o3 = (o[h * WA_BLOCK:(h + 1) * WA_BLOCK] for h in range(WA_HEADS))
        qs = slice(bb * WA_BLOCK, (bb + 1) * WA_BLOCK)
        o_ref[0, qs, 0:LANES] = jnp.where(first_half, o0, swap(o1)).astype(o_ref.dtype)
        o_ref[0, qs, LANES:] = jnp.where(first_half, swap(o2), o3).astype(o_ref.dtype)


def _attn_a(sink, qkv, qkv_c):
    b = qkv.shape[0]
    nk = WA_KEYS + CTX_LEN
    tq = WA_BLOCKS_PER_STEP * WA_BLOCK
    nr = WA_BLOCKS_PER_STEP * WA_HEADS * WA_BLOCK
    return pl.pallas_call(
        _attn_a_kernel,
        out_shape=jax.ShapeDtypeStruct((b, SEQ, WA_HEADS * HEAD_DIM), BF16),
        scratch_shapes=[pltpu.VMEM((nr, nk), F32), pltpu.VMEM((nr, nk), BF16)],
        grid=(b, SEQ // tq),
        in_specs=[
            pl.BlockSpec(memory_space=pltpu.SMEM),
            pl.BlockSpec((1, tq, 2 * LANES), lambda i, j: (i, j, 0)),
            pl.BlockSpec((1, SEQ, LANES), lambda i, j: (i, 0, 2)),
            pl.BlockSpec((1, SEQ, LANES), lambda i, j: (i, 0, 3)),
            pl.BlockSpec((1, CTX_LEN, LANES), lambda i, j: (i, 0, 2)),
            pl.BlockSpec((1, CTX_LEN, LANES), lambda i, j: (i, 0, 3)),
        ],
        out_specs=pl.BlockSpec((1, tq, 2 * LANES), lambda i, j: (i, j, 0)),
        compiler_params=_cparams(("parallel", "arbitrary")),
        name="attn_window",
    )(sink, qkv, qkv, qkv, qkv_c, qkv_c)


NB_ROWS_PER_STEP = 16
NB_LOC = NA_KH * GRID_W
NB_DY_PAIRS = 2 * NA_KH - 2


def _attn_b_kernel(q_ref, k_ref, v_ref, kc_ref, vc_ref, t_ref, o_ref, s_ref, p_ref):
    i = pl.program_id(1)
    kc = kc_ref[0]
    vc = vc_ref[0]
    units = [(rr, pp) for rr in range(NB_ROWS_PER_STEP) for pp in range(NA_HEADS // 2)]
    starts, shifts = [], []
    for rr in range(NB_ROWS_PER_STEP):
        r = i * NB_ROWS_PER_STEP + rr
        rs = jnp.clip(r - NA_KH // 2, 0, GRID_ROWS - NA_KH)
        shifts.append(r - rs)
        starts.append(pl.multiple_of(rs * GRID_W, GRID_W))
    first_half = lax.broadcasted_iota(jnp.int32, (GRID_W, LANES), 1) < HEAD_DIM
    zero = jnp.zeros((), BF16)
    pw = 2 * GRID_W

    for u, (rr, pp) in enumerate(units):
        ls = slice(pp * LANES, (pp + 1) * LANES)
        if pp == 0:
            kw = k_ref[0, pl.ds(starts[rr], NB_LOC), :]
            q = q_ref[0, rr * GRID_W:(rr + 1) * GRID_W, :] * ATT_SCALE
        qs = q[:, ls]
        lhs = jnp.concatenate([jnp.where(first_half, qs, zero), jnp.where(first_half, zero, qs)], axis=0)
        bias = jnp.concatenate(
            [jnp.concatenate([t_ref[h, 2 * k - shifts[rr] + NA_KH - 1] for k in range(NA_KH // 2)], axis=1)
             for h in (2 * pp, 2 * pp + 1)], axis=0)
        s_ref[u * pw:(u + 1) * pw, 0:NB_LOC] = _bdot_nt(lhs, kw[:, ls]) + bias
        s_ref[u * pw:(u + 1) * pw, NB_LOC:] = _bdot_nt(lhs, kc[:, ls])

    s = s_ref[...]
    p = jnp.exp(s - jnp.max(s, axis=1, keepdims=True))
    inv = 1.0 / jnp.sum(p, axis=1, keepdims=True)
    p_ref[...] = p.astype(BF16)

    for u, (rr, pp) in enumerate(units):
        ls = slice(pp * LANES, (pp + 1) * LANES)
        if pp == 0:
            vw = v_ref[0, pl.ds(starts[rr], NB_LOC), :]
        rows = slice(u * pw, (u + 1) * pw)
        o = (jnp.dot(p_ref[rows, 0:NB_LOC], vw[:, ls], preferred_element_type=F32)
             + jnp.dot(p_ref[rows, NB_LOC:], vc[:, ls], preferred_element_type=F32)) * inv[rows]
        o_ref[0, rr * GRID_W:(rr + 1) * GRID_W, ls] = jnp.where(
            first_half, o[:GRID_W], o[GRID_W:]).astype(o_ref.dtype)


def _attn_b(qkv, qkv_c, table):
    b = qkv.shape[0]
    tq = NB_ROWS_PER_STEP * GRID_W
    w = NA_HEADS * HEAD_DIM
    nu = NB_ROWS_PER_STEP * NA_HEADS * GRID_W
    return pl.pallas_call(
        _attn_b_kernel,
        out_shape=jax.ShapeDtypeStruct((b, SEQ, w), BF16),
        scratch_shapes=[pltpu.VMEM((nu, NB_LOC + CTX_LEN), F32), pltpu.VMEM((nu, NB_LOC + CTX_LEN), BF16)],
        grid=(b, SEQ // tq),
        in_specs=[
            pl.BlockSpec((1, tq, w), lambda i, j: (i, j, 0)),
            pl.BlockSpec((1, SEQ, w), lambda i, j: (i, 0, 1)),
            pl.BlockSpec((1, SEQ, w), lambda i, j: (i, 0, 2)),
            pl.BlockSpec((1, CTX_LEN, w), lambda i, j: (i, 0, 1)),
            pl.BlockSpec((1, CTX_LEN, w), lambda i, j: (i, 0, 2)),
            pl.BlockSpec((NA_HEADS, NB_DY_PAIRS, GRID_W, LANES), lambda i, j: (0, 0, 0, 0)),
        ],
        out_specs=pl.BlockSpec((1, tq, w), lambda i, j: (i, j, 0)),
        compiler_params=_cparams(("parallel", "arbitrary")),
        name="attn_neighbourhood",
    )(qkv, qkv, qkv, qkv_c, qkv_c, table)


def _split3(a):
    a1 = a.astype(BF16)
    r1 = a - a1.astype(F32)
    a2 = r1.astype(BF16)
    a3 = (r1 - a2.astype(F32)).astype(BF16)
    return a1, a2, a3


def _bias_kernel(r_ref, oh_ref, o_ref):
    oh = oh_ref[...]
    o_ref[...] = sum(jnp.dot(t, oh, preferred_element_type=F32) for t in _split3(r_ref[...]))


def _na_bias_tables(rpb):
    ndy, ndx = 2 * NA_KH - 1, 2 * NA_KW - 1
    qc = np.arange(GRID_W)[:, None]
    x = np.arange(GRID_W)[None, :]
    dx = np.clip(x - qc, -(NA_KW - 1), NA_KW - 1) + NA_KW - 1
    onehot = (np.arange(LANES)[:, None, None] == dx[None]).reshape(LANES, GRID_W * GRID_W)
    cstart = np.clip(qc - NA_KW // 2, 0, GRID_W - NA_KW)
    inside = (x >= cstart) & (x < cstart + NA_KW)
    rows = DEPTH * NA_HEADS * ndy
    r = jnp.pad(rpb.astype(F32).reshape(rows, ndx), ((0, LANES - rows), (0, LANES - ndx)))
    m = pl.pallas_call(
        _bias_kernel,
        out_shape=jax.ShapeDtypeStruct((LANES, GRID_W * GRID_W), F32),
        name="na_bias_expand",
    )(r, jnp.asarray(onehot, BF16))
    m = m[:rows].reshape(DEPTH, NA_HEADS, ndy, GRID_W, GRID_W)
    m = jnp.where(jnp.asarray(inside), m, NEG_INF)
    return jnp.concatenate([m[:, :, :ndy - 1], m[:, :, 1:]], axis=-1)


def _ctx_attn_kernel(sink_ref, a_ref, b_ref, oa_ref, ob_ref):
    a = a_ref[0]
    row1 = lax.broadcasted_iota(jnp.int32, (2 * CTX_LEN, 1), 0)
    for g in range(WA_KV_HEADS):
        h0, h1 = 2 * g, 2 * g + 1
        q2 = jnp.concatenate([a[:, h0 * HEAD_DIM:(h0 + 1) * HEAD_DIM],
                              a[:, h1 * HEAD_DIM:(h1 + 1) * HEAD_DIM]], axis=0)
        k = a[:, 2 * LANES + g * HEAD_DIM:2 * LANES + (g + 1) * HEAD_DIM]
        v = a[:, 3 * LANES + g * HEAD_DIM:3 * LANES + (g + 1) * HEAD_DIM]
        s = _bdot_nt(q2, k) * ATT_SCALE
        sink = jnp.where(row1 < CTX_LEN, sink_ref[h0], sink_ref[h1])
        m = jnp.maximum(jnp.max(s, axis=1, keepdims=True), sink)
        p = jnp.exp(s - m)
        den = jnp.sum(p, axis=1, keepdims=True) + jnp.exp(sink - m)
        o = _bdot(p, v) / den
        oa_ref[0, :, h0 * HEAD_DIM:(h0 + 1) * HEAD_DIM] = o[:CTX_LEN]
        oa_ref[0, :, h1 * HEAD_DIM:(h1 + 1) * HEAD_DIM] = o[CTX_LEN:]
    bq = b_ref[0]
    w = NA_HEADS * HEAD_DIM
    for h in range(NA_HEADS):
        hs = slice(h * HEAD_DIM, (h + 1) * HEAD_DIM)
        s = _bdot_nt(bq[:, hs], bq[:, w + h * HEAD_DIM:w + (h + 1) * HEAD_DIM]) * ATT_SCALE
        m = jnp.max(s, axis=1, keepdims=True)
        p = jnp.exp(s - m)
        den = jnp.sum(p, axis=1, keepdims=True)
        ob_ref[0, :, hs] = _bdot(p, bq[:, 2 * w + h * HEAD_DIM:2 * w + (h + 1) * HEAD_DIM]) / den


def _ctx_attn(sink, qkv_a_c, qkv_b_c):
    b = qkv_a_c.shape[0]
    w = 2 * LANES
    return pl.pallas_call(
        _ctx_attn_kernel,
        out_shape=[jax.ShapeDtypeStruct((b, CTX_LEN, w), F32)] * 2,
        grid=(b,),
        in_specs=[
            pl.BlockSpec(memory_space=pltpu.SMEM),
            pl.BlockSpec((1, CTX_LEN, A_COLS), lambda i: (i, 0, 0)),
            pl.BlockSpec((1, CTX_LEN, B_COLS), lambda i: (i, 0, 0)),
        ],
        out_specs=[pl.BlockSpec((1, CTX_LEN, w), lambda i: (i, 0, 0))] * 2,
        compiler_params=_cparams(("parallel",)),
        name="attn_context",
    )(sink, qkv_a_c, qkv_b_c)


CONV_HALO = SUBLANES


def _conv_kernel(prev_ref, cur_ref, next_ref, w_ref, b_ref, o_ref, ext_ref, *, tl, nt):
    j = pl.program_id(1)
    ext_ref[0:CONV_HALO, :] = jnp.where(j > 0, prev_ref[0], 0.0)
    ext_ref[CONV_HALO:CONV_HALO + tl, :] = cur_ref[0]
    ext_ref[CONV_HALO + tl:, :] = jnp.where(j < nt - 1, next_ref[0], 0.0)
    acc = jnp.zeros((tl, XBC_COLS), F32) + b_ref[...]
    base = CONV_HALO - SSM_CONV // 2
    for k in range(SSM_CONV):
        acc = acc + w_ref[k:k + 1, :] * ext_ref[base + k:base + k + tl, :]
    o_ref[0] = _silu(acc)


def _conv_silu(xz, conv_w, conv_b, tl):
    b, t, _ = xz.shape
    nt = t // tl
    hb = tl // CONV_HALO
    last = t // CONV_HALO - 1
    return pl.pallas_call(
        functools.partial(_conv_kernel, tl=tl, nt=nt),
        out_shape=jax.ShapeDtypeStruct((b, t, XBC_COLS), F32),
        grid=(b, nt),
        in_specs=[
            pl.BlockSpec((1, CONV_HALO, XBC_COLS), lambda i, j: (i, jnp.maximum(j * hb - 1, 0), 0)),
            pl.BlockSpec((1, tl, XBC_COLS), lambda i, j: (i, j, 0)),
            pl.BlockSpec((1, CONV_HALO, XBC_COLS), lambda i, j: (i, jnp.minimum((j + 1) * hb, last), 0)),
            pl.BlockSpec((SUBLANES, XBC_COLS), lambda i, j: (0, 0)),
            pl.BlockSpec((1, XBC_COLS), lambda i, j: (0, 0)),
        ],
        out_specs=pl.BlockSpec((1, tl, XBC_COLS), lambda i, j: (i, j, 0)),
        scratch_shapes=[pltpu.VMEM((tl + 2 * CONV_HALO, XBC_COLS), F32)],
        compiler_params=_cparams(("parallel", "parallel")),
        name="ssm_conv",
    )(xz, xz, xz, conv_w, conv_b)


Q = SSM_CHUNK
GS = SSM_GROUPS * SSM_STATE
HPG = SSM_HEADS // SSM_GROUPS
SSD_STATE_SHAPE = (SSM_GROUPS, SSM_STATE, HPG * SSM_HEAD_DIM)
SSD_BATCH_PER_STEP = 4


def _softplus(v):
    return jnp.maximum(v, 0.0) + jnp.log1p(jnp.exp(-jnp.abs(v)))


def _cumsum_mat(tri, a):
    r = jnp.dot(tri, jnp.concatenate(_split3(a), axis=1), preferred_element_type=F32)
    n = a.shape[1]
    return r[:, 0:n] + r[:, n:2 * n] + r[:, 2 * n:]


def _ssd_kernel(xm_ref, dtm_ref, xb_ref, dtb_ref, bias_ref, alog_ref, h0f_ref, h0b_ref,
                y1_ref, y2_ref, hf_out_ref, hb_out_ref, hf_ref, hb_ref, *, nc):
    i = pl.program_id(1)

    @pl.when(i == 0)
    def _():
        hf_ref[...] = h0f_ref[...]
        hb_ref[...] = h0b_ref[...]

    ii = lax.broadcasted_iota(jnp.int32, (Q, Q), 0)
    jj = lax.broadcasted_iota(jnp.int32, (Q, Q), 1)
    lower = ii >= jj
    diag = ii == jj
    tril =jnp.where(lower, 1.0, 0.0).astype(BF16)
    triu = jnp.where(ii <= jj, 1.0, 0.0).astype(BF16)
    first_half = lax.broadcasted_iota(jnp.int32, (Q, LANES), 1) < SSM_HEAD_DIM
    a_row = -jnp.exp(alog_ref[...])
    gw = HPG * SSM_HEAD_DIM

    def group_operands(xbc, g):
        bg = xbc[:, SSM_INNER + g * SSM_STATE:SSM_INNER + (g + 1) * SSM_STATE]
        cg = xbc[:, SSM_INNER + GS + g * SSM_STATE:SSM_INNER + GS + (g + 1) * SSM_STATE]
        return bg, cg.astype(BF16), xbc[:, g * gw:(g + 1) * gw]

    nbs = range(SSD_BATCH_PER_STEP)
    groups = range(SSM_GROUPS)
    bias = bias_ref[...]
    zero = jnp.zeros((), F32)


    xm = [xm_ref[bi] for bi in nbs]
    xb = [xb_ref[bi] for bi in nbs]
    dtm = [_softplus(dtm_ref[bi] + bias) for bi in nbs]
    dtb = [_softplus(dtb_ref[bi] + bias) for bi in nbs]
    cumf =[_cumsum_mat(tril, d * a_row) for d in dtm]
    cumr = [_cumsum_mat(triu, d * a_row) for d in dtm]
    cumb = [_cumsum_mat(triu, d * a_row) for d in dtb]

    opm = [[group_operands(xm[bi], g) for g in groups] for bi in nbs]
    opb = [[group_operands(xb[bi], g) for g in groups] for bi in nbs]
    gmat = [[_bdot_nt(opm[bi][g][1], opm[bi][g][0]) for g in groups] for bi in nbs]
    hf_prev = [[hf_ref[bi, g] for g in groups] for bi in nbs]
    hb_prev = [[hb_ref[bi, g] for g in groups] for bi in nbs]
    inter_f = [[jnp.dot(opm[bi][g][1], hf_prev[bi][g].astype(BF16), preferred_element_type=F32)
                for g in groups] for bi in nbs]
    inter_b = [[jnp.dot(opb[bi][g][1], hb_prev[bi][g].astype(BF16), preferred_element_type=F32)
                for g in groups] for bi in nbs]
    bt_m = [[opm[bi][g][0].T for g in groups] for bi in nbs]
    bt_b = [[opb[bi][g][0].T for g in groups] for bi in nbs]

    cumf_t = [c.T for c in cumf]
    cumr_t = [c.T for c in cumr]
    dtm_t = [d.T for d in dtm]
    cumb_t = [c.T for c in cumb]
    dtb_t = [d.T for d in dtb]

    sel_r = lax.broadcasted_iota(jnp.int32, (2 * LANES, SSM_INNER), 0) % LANES
    sel_c = lax.broadcasted_iota(jnp.int32, (2 * LANES, SSM_INNER), 1) // SSM_HEAD_DIM
    sel_f = jnp.where(sel_r == sel_c, 1.0, 0.0).astype(BF16)
    sel_b = jnp.where(sel_r == sel_c + SSM_HEADS, 1.0, 0.0).astype(BF16)

    def spread(e, sel):
        hi = e.astype(BF16)
        lo = (e - hi.astype(F32)).astype(BF16)
        return jnp.dot(jnp.concatenate([hi, lo], axis=1), sel, preferred_element_type=F32)

    sl_in_f = [spread(jnp.exp(c), sel_f) for c in cumf]
    sl_in_b = [spread(jnp.exp(c), sel_b) for c in cumb]

    heads = [(bi, h) for bi in nbs for h in range(SSM_HEADS)]
    hb_ = lambda h: SSM_HEADS + h
    cf, cr, wm, wst_f, wst_b = {}, {}, {}, {}, {}
    for t in range(len(heads) + 1):
        if t < len(heads):
            bi, h = k = heads[t]
            cf[k] = jnp.broadcast_to(cumf[bi][:, h:h + 1], (Q, Q))
            cr[k] = jnp.broadcast_to(cumr[bi][:, hb_(h):hb_(h) + 1], (Q, Q))
        if t >= 1:
            bi, h = k = heads[t - 1]
            rf = cumf_t[bi][h:h + 1, :]
            rb = cumb_t[bi][hb_(h):hb_(h) + 1, :]
            dtf_row = dtm_t[bi][h:h + 1, :]
            dtb_row = dtm_t[bi][hb_(h):hb_(h) + 1, :]
            arg = jnp.where(lower, cf[k] - rf, cr[k] - cumr_t[bi][hb_(h):hb_(h) + 1, :])
            wm[k] = (gmat[bi][h // HPG] * (jnp.exp(arg) * jnp.where(lower, dtf_row, dtb_row)
                                           + jnp.where(diag, dtb_row, zero))).astype(BF16)
            w_f = jnp.exp(rf[:, Q - 1:Q] - rf) * dtf_row
            w_b = jnp.exp(rb[:, 0:1] - rb) * dtb_t[bi][hb_(h):hb_(h) + 1, :]
            wst_f[k] = (bt_m[bi][h // HPG] * w_f).astype(BF16)
            wst_b[k] = (bt_b[bi][h // HPG] * w_b).astype(BF16)

    def pair_rhs(x, pr):
        xp = x[:, pr * LANES:(pr + 1) * LANES]
        return jnp.concatenate([jnp.where(first_half, xp, zero), jnp.where(first_half, zero, xp)],
                               axis=0).astype(BF16)

    def pair_dot(mats, bi, pr, rhs):
        lhs = jnp.concatenate([mats[bi, 2 * pr], mats[bi, 2 * pr + 1]], axis=1)
        return jnp.dot(lhs, rhs, preferred_element_type=F32)

    pairs = [(bi, pr) for bi in nbs for pr in range(SSM_HEADS // 2)]
    rhs_m = {k: pair_rhs(xm[k[0]], k[1]) for k in pairs}
    rhs_b = {k: pair_rhs(xb[k[0]], k[1]) for k in pairs}
    y_intra = {k: pair_dot(wm, k[0], k[1], rhs_m[k]) for k in pairs}
    st_f = {k: pair_dot(wst_f, k[0], k[1], rhs_m[k]) for k in pairs}
    st_b = {k: pair_dot(wst_b, k[0], k[1], rhs_b[k]) for k in pairs}

    ppg = HPG // 2
    for bi, pr in pairs:
        g, k = pr // ppg, pr % ppg
        ls = slice(k * LANES, (k + 1) * LANES)
        sin_f = sl_in_f[bi][:, pr * LANES:(pr + 1) * LANES]
        sin_b = sl_in_b[bi][:, pr * LANES:(pr + 1) * LANES]
        y1_ref[bi, :, pr * LANES:(pr + 1) * LANES] = y_intra[bi, pr] + sin_f * inter_f[bi][g][:, ls]
        y2_ref[bi, :, pr * LANES:(pr + 1) * LANES] = sin_b * inter_b[bi][g][:, ls]
        hf_ref[bi, g, :, ls] = hf_prev[bi][g][:, ls] * sin_f[Q - 1:Q, :] + st_f[bi, pr]
        hb_ref[bi, g, :, ls] = hb_prev[bi][g][:, ls] * sin_b[0:1, :] + st_b[bi, pr]

    @pl.when(i == nc - 1)
    def _():
        hf_out_ref[...] = hf_ref[...]
        hb_out_ref[...] = hb_ref[...]


def _ssd(xbc, dt_raw, dt_bias, a_log, h0f, h0b):
    b, t, _ = xbc.shape
    nc = t // Q
    nb = SSD_BATCH_PER_STEP
    st_shape = (b,) + SSD_STATE_SHAPE
    st_spec = pl.BlockSpec((nb,) + SSD_STATE_SHAPE, lambda i, j: (i, 0, 0, 0))
    fwd = lambda i, j: (i, j, 0)
    bwd = lambda i, j: (i, nc - 1 - j, 0)
    return pl.pallas_call(
        functools.partial(_ssd_kernel, nc=nc),
        out_shape=[jax.ShapeDtypeStruct((b, t, SSM_INNER), F32)] * 2
        + [jax.ShapeDtypeStruct(st_shape, F32)] * 2,
        grid=(b // nb, nc),
        in_specs=[
            pl.BlockSpec((nb, Q, XBC_COLS), fwd),
            pl.BlockSpec((nb, Q, DT_PAD), fwd),
            pl.BlockSpec((nb, Q, XBC_COLS), bwd),
            pl.BlockSpec((nb, Q, DT_PAD), bwd),
            pl.BlockSpec((1, DT_PAD), lambda i, j: (0, 0)),
            pl.BlockSpec((1, DT_PAD), lambda i, j: (0, 0)),
            st_spec, st_spec,
        ],
        out_specs=[pl.BlockSpec((nb, Q, SSM_INNER), fwd), pl.BlockSpec((nb, Q, SSM_INNER), bwd),
                   st_spec, st_spec],
        scratch_shapes=[pltpu.VMEM((nb,) + SSD_STATE_SHAPE, F32)] * 2,
        compiler_params=_cparams(("parallel", "arbitrary")),
        name="ssd_scan",
    )(xbc, dt_raw, xbc, dt_raw, dt_bias, a_log, h0f, h0b)


MXU_TILE = 256
FFN_SPLITS = (0, 6 * MXU_TILE, D_FF)


def _mix_ffn_kernel(x_ref, mod_ref, oa_ref, ob_ref, y1_ref, y2_ref, xs_ref, z_ref, d_ref, sg_ref,
                    wo_ref, g_ref, wfi_ref, wfo_ref, gfin_ref, o_ref, *, final):
    y = y1_ref[0] + y2_ref[0] + d_ref[...] * xs_ref[0]
    y = y * _silu(z_ref[0])
    var = jnp.mean(y * y, axis=-1, keepdims=True)
    oc = (y * lax.rsqrt(var + EPS) * sg_ref[...]).astype(BF16)
    wa = 2 * LANES
    mix = (jnp.dot(oa_ref[0].astype(BF16), wo_ref[0, 0:wa, :], preferred_element_type=F32)
           + jnp.dot(ob_ref[0].astype(BF16), wo_ref[0, wa:2 * wa, :], preferred_element_type=F32)
           + jnp.dot(oc, wo_ref[0, 2 * wa:, :], preferred_element_type=F32))
    xn = x_ref[0] + mod_ref[0, 2:3, :] * mix
    h = _norm_mod(xn, g_ref[...], mod_ref[0, 3:4, :], mod_ref[0, 4:5, :]).astype(BF16)
    acc = None
    for lo, hi in zip(FFN_SPLITS[:-1], FFN_SPLITS[1:]):
        gate = jnp.dot(h, wfi_ref[0, :, lo:hi], preferred_element_type=F32)
        up = jnp.dot(h, wfi_ref[0, :, D_FF + lo:D_FF + hi], preferred_element_type=F32)
        act = (_silu(gate) * up).astype(BF16)
        part = jnp.dot(act, wfo_ref[0, lo:hi, :], preferred_element_type=F32)
        acc = part if acc is None else acc + part
    out = xn + mod_ref[0, 5:6, :] * acc
    if final:
        var = jnp.mean(out * out, axis=-1, keepdims=True)
        out = out * lax.rsqrt(var + EPS) * gfin_ref[...]
    o_ref[0] = out


def _mix_ffn(x, mods, o_a, o_b, y1, y2, xbc, xz, d_exp, ssm_g, w_out, g_ffn, w_ffn_in, w_ffn_out,
             g_final, layer, tm, final):
    b, t, _ = x.shape
    per_batch = mods.shape[0] > 1
    row = lambda n: pl.BlockSpec((1, tm, n), lambda i, j: (i, j, 0))
    const = lambda shape: pl.BlockSpec(shape, lambda i, j: (0, 0), pipeline_mode=pl.Buffered(1))
    weight = lambda r, c: pl.BlockSpec((1, r, c), lambda i, j: (layer, 0, 0), pipeline_mode=pl.Buffered(1))
    return pl.pallas_call(
        functools.partial(_mix_ffn_kernel, final=final),
        out_shape=jax.ShapeDtypeStruct((b, t, D_MODEL), F32),
        grid=(b, t // tm),
        in_specs=[
            row(D_MODEL),
            pl.BlockSpec((1, SUBLANES, D_MODEL), (lambda i, j: (i, 0, 0)) if per_batch else (lambda i, j: (0, 0, 0))),
            row(2 * LANES), row(2 * LANES), row(SSM_INNER), row(SSM_INNER),
            pl.BlockSpec((1, tm, SSM_INNER), lambda i, j: (i, j, 0)),
            pl.BlockSpec((1, tm, SSM_INNER), lambda i, j: (i, j, 2)),
            const((1, SSM_INNER)), const((1, SSM_INNER)),
            weight(D_MODEL, D_MODEL),
            const((1, D_MODEL)),
            weight(D_MODEL, 2 * D_FF),
            weight(D_FF, D_MODEL),
            const((1, D_MODEL)),
        ],
        out_specs=row(D_MODEL),
        compiler_params=_cparams(("parallel", "parallel")),
        name="mix_ffn_final" if final else "mix_ffn",
    )(x, mods, o_a, o_b, y1, y2, xbc, xz, d_exp, ssm_g, w_out, g_ffn, w_ffn_in, w_ffn_out, g_final)


def _rope_tables():
    t = np.arange(SEQ)
    pos = np.stack([t // GRID_W, t % GRID_W], axis=1).astype(np.float64)
    quarter = HEAD_DIM // 4
    inv_freq = ROPE_BASE ** (-np.arange(quarter, dtype=np.float64) / quarter)
    lane = np.arange(LANES) % HEAD_DIM
    half = lane // (HEAD_DIM // 2)
    idx = lane % (HEAD_DIM // 2)
    ang = pos[:, half] * inv_freq[idx % quarter][None, :]
    cos, sin = np.cos(ang), np.sin(ang)
    first = (idx < quarter)[None, :]
    tabs = (cos, np.where(first, -sin, 0.0), np.where(first, 0.0, sin))
    return tuple(jnp.asarray(v, F32) for v in tabs)


def _pad_lanes(v, n=LANES):
    v = v.reshape(1, -1)
    return jnp.pad(v, ((0, 0), (0, n - v.shape[1])))


def kernel(x, c, ctx, c_ctx, w_mod, b_mod, g_mix, w_in, wa_sink, na_rpb, ssm_conv_w, ssm_conv_b,
           ssm_dt_bias, ssm_a_log, ssm_d, ssm_norm_g, w_out, g_ffn, w_ffn_in, w_ffn_out, g_final):
    cin = jnp.concatenate([c, c_ctx[None, :], jnp.zeros((SUBLANES - BATCH - 1, D_MODEL), F32)], axis=0)
    mod_all = _modulation(cin, w_mod, b_mod)
    rope_tabs = _rope_tables()
    bias_tabs = _na_bias_tables(na_rpb)
    zeros_state = jnp.zeros((BATCH,) + SSD_STATE_SHAPE, F32)
    gfin = g_final.reshape(1, D_MODEL)
    w_proj = w_in.astype(BF16)
    wo = w_out.astype(BF16)
    wfi = w_ffn_in.astype(BF16)
    wfo = w_ffn_out.astype(BF16)

    xl, xc = x, ctx
    for l in range(DEPTH):
        last = l == DEPTH - 1
        m6 = mod_all[l].reshape(SUBLANES, 6, D_MODEL)
        mods_l = jnp.pad(m6[:BATCH], ((0, 0), (0, 2), (0, 0)))
        mods_c = jnp.pad(m6[BATCH:BATCH + 1], ((0, 0), (0, 2), (0, 0)))
        w_dt = jnp.pad(w_in[l][:, W_DT:], ((0, 0), (0, DT_PAD - DT_COLS))).astype(BF16)
        g1 = g_mix[l].reshape(1, D_MODEL)
        g2 = g_ffn[l].reshape(1, D_MODEL)
        conv_w = jnp.pad(ssm_conv_w[l], ((0, SUBLANES - SSM_CONV), (0, 0)))
        conv_b = ssm_conv_b[l].reshape(1, XBC_COLS)
        dt_bias = _pad_lanes(ssm_dt_bias[l])
        a_log = _pad_lanes(ssm_a_log[l])
        d_exp = jnp.repeat(ssm_d[l], SSM_HEAD_DIM).reshape(1, SSM_INNER)
        sg = ssm_norm_g[l].reshape(1, SSM_INNER)

        a_c, b_c, xz_c, dt_c = _inproj(xc, mods_c, g1, w_proj, l, w_dt, None, CTX_LEN)
        a_l, b_l, xz_l, dt_l = _inproj(xl, mods_l, g1, w_proj, l, w_dt, rope_tabs, TM_INPROJ)

        o_a = _attn_a(wa_sink[l], a_l, a_c)
        o_b = _attn_b(b_l, b_c, bias_tabs[l])

        xbc_c = _conv_silu(xz_c, conv_w, conv_b, CTX_LEN)
        xbc_l = _conv_silu(xz_l, conv_w, conv_b, TM_CONV)
        y1_c, y2_c, h_f, h_b = _ssd(xbc_c, dt_c, dt_bias, a_log, zeros_state, zeros_state)
        y1_l, y2_l, _, _ = _ssd(xbc_l, dt_l, dt_bias, a_log, h_f, h_b)

        xl = _mix_ffn(xl, mods_l, o_a, o_b, y1_l, y2_l, xbc_l, xz_l, d_exp, sg, wo, g2, wfi, wfo, gfin,
                      l, TM_MIX, last)
        if not last:
            o_ac, o_bc = _ctx_attn(wa_sink[l], a_c, b_c)
            xc = _mix_ffn(xc, mods_c, o_ac, o_bc, y1_c, y2_c, xbc_c, xz_c, d_exp, sg, wo, g2, wfi, wfo,
                          gfin, l, CTX_LEN, False)
    return xl
```

```python
import functools

import numpy as np
import jax
import jax.numpy as jnp
from jax import lax
from jax.experimental import pallas as pl
from jax.experimental.pallas import tpu as pltpu

F32 = jnp.float32
BF16 = jnp.bfloat16

D_MODEL = 1024
BATCH = 4
SEQ = 4096
DEPTH = 2
GRID_W = 64
GRID_ROWS = SEQ // GRID_W
CTX_LEN = 256
EPS = 1e-6
HEAD_DIM = 64
ROPE_BASE = 10000.0
WA_HEADS = 4
WA_KV_HEADS = 2
WA_WINDOW = 128
WA_BLOCK = 128
NA_HEADS = 4
NA_KH = 8
NA_KW = 16
SSM_HEADS = 8
SSM_HEAD_DIM = 64
SSM_INNER = SSM_HEADS * SSM_HEAD_DIM
SSM_GROUPS = 2
SSM_STATE = 128
SSM_CONV = 7
SSM_CHUNK = 128
D_FF = 2816
XBC_COLS = SSM_INNER + 2 * SSM_GROUPS * SSM_STATE
DT_COLS = 2 * SSM_HEADS

LANES = 128
SUBLANES = 8
V7X_VMEM_BYTES = 64 * 1024 * 1024
VMEM_LIMIT = V7X_VMEM_BYTES - 8 * 1024 * 1024

TM_INPROJ = 1024
TM_CONV = 1024
TM_MIX = 512

A_COLS = 512
B_COLS = 768
XZ_COLS = 1536
DT_PAD = LANES

ATT_SCALE = HEAD_DIM ** -0.5
NEG_INF = float("-inf")
NT_DIMS = (((1,), (1,)), ((), ()))


def _silu(v):
    return v / (1.0 + jnp.exp(-v))


def _bdot(a, b):
    return jnp.dot(a.astype(BF16), b.astype(BF16), preferred_element_type=F32)


def _bdot_nt(a, b):
    return lax.dot_general(a.astype(BF16), b.astype(BF16), NT_DIMS, preferred_element_type=F32)


def _cparams(sem):
    return pltpu.CompilerParams(dimension_semantics=sem, vmem_limit_bytes=VMEM_LIMIT)


MOD_TN = 6 * D_MODEL // 4


def _mod_kernel(c_ref, w_ref, b_ref, o_ref):
    s = _silu(c_ref[...])
    o_ref[0] = _bdot(s, w_ref[0]) + b_ref[0]


def _modulation(cin, w_mod, b_mod):
    n = 6 * D_MODEL
    return pl.pallas_call(
        _mod_kernel,
        out_shape=jax.ShapeDtypeStruct((DEPTH, SUBLANES, n), F32),
        grid=(DEPTH, n // MOD_TN),
        in_specs=[
            pl.BlockSpec((SUBLANES, D_MODEL), lambda l, j: (0, 0)),
            pl.BlockSpec((1, D_MODEL, MOD_TN), lambda l, j: (l, 0, j)),
            pl.BlockSpec((1, 1, MOD_TN), lambda l, j: (l, 0, j)),
        ],
        out_specs=pl.BlockSpec((1, SUBLANES, MOD_TN), lambda l, j: (l, 0, j)),
        compiler_params=_cparams(("parallel", "parallel")),
        name="modulation",
    )(cin, w_mod, b_mod.reshape(DEPTH, 1, n))


def _norm_mod(x, g, shift, scale):
    var = jnp.mean(x * x, axis=-1, keepdims=True)
    h = x * lax.rsqrt(var + EPS) * g
    return h * (1.0 + scale) + shift


W_QA = 0
W_QB = W_QA + WA_HEADS * HEAD_DIM
W_Z = W_QB + NA_HEADS * HEAD_DIM
W_KVA = W_Z + SSM_INNER
W_KVB = W_KVA + 2 * WA_KV_HEADS * HEAD_DIM
W_XBC = W_KVB + 2 * NA_HEADS * HEAD_DIM
W_DT = W_XBC + XBC_COLS
IN_COLS = W_DT + DT_COLS
ROPE_QUARTER = HEAD_DIM // 4


def _inproj_kernel(x_ref, mod_ref, g_ref, w_ref, wdt_ref, *rest, rope):
    if rope:
        cos_ref, sa_ref, sb_ref, oa_ref, ob_ref, oxz_ref, odt_ref = rest
    else:
        oa_ref, ob_ref, oxz_ref, odt_ref = rest
    h = _norm_mod(x_ref[0], g_ref[...], mod_ref[0, 0:1, :], mod_ref[0, 1:2, :]).astype(BF16)

    def proj(lo, hi):
        return jnp.dot(h, w_ref[0, :, lo:hi], preferred_element_type=F32)

    qa = proj(W_QA, W_QB)
    kva = proj(W_KVA, W_KVB)
    if rope:
        cos, sa, sb = cos_ref[...], sa_ref[...], sb_ref[...]

        def rot(v):
            up = pltpu.roll(v, LANES - ROPE_QUARTER, axis=1)
            dn = pltpu.roll(v, ROPE_QUARTER, axis=1)
            return v * cos + up * sa + dn * sb

        oa_ref[0, :, 0:LANES] = rot(qa[:, 0:LANES]).astype(BF16)
        oa_ref[0, :, LANES:2 * LANES] = rot(qa[:, LANES:]).astype(BF16)
        oa_ref[0, :, 2 * LANES:3 * LANES] = rot(kva[:, 0:LANES]).astype(BF16)
        oa_ref[0, :, 3 * LANES:] = kva[:, LANES:].astype(BF16)
    else:
        oa_ref[0, :, 0:2 * LANES] = qa.astype(BF16)
        oa_ref[0, :, 2 * LANES:] = kva.astype(BF16)
    ob_ref[0, :, 0:2 * LANES] = proj(W_QB, W_Z).astype(BF16)
    ob_ref[0, :, 2 * LANES:] = proj(W_KVB, W_XBC).astype(BF16)
    oxz_ref[0, :, 0:XBC_COLS] = proj(W_XBC, W_DT)
    oxz_ref[0, :, XBC_COLS:] = proj(W_Z, W_KVA)
    odt_ref[0] = jnp.dot(h, wdt_ref[...], preferred_element_type=F32)


def _inproj(x, mods, g, w, layer, wdt, rope_tabs, tm):
    b, t, _ = x.shape
    rope = rope_tabs is not None
    per_batch = mods.shape[0] > 1
    in_specs = [
        pl.BlockSpec((1, tm, D_MODEL), lambda i, j: (i, j, 0)),
        pl.BlockSpec((1, SUBLANES, D_MODEL), (lambda i, j: (i, 0, 0)) if per_batch else (lambda i, j: (0, 0, 0))),
        pl.BlockSpec((1, D_MODEL), lambda i, j: (0, 0)),
        pl.BlockSpec((1, D_MODEL, IN_COLS), lambda i, j: (layer, 0, 0)),
        pl.BlockSpec((D_MODEL, DT_PAD), lambda i, j: (0, 0)),
    ]
    args = [x, mods, g, w, wdt]
    if rope:
        in_specs += [pl.BlockSpec((tm, LANES), lambda i, j: (j, 0))] * 3
        args += list(rope_tabs)
    widths = (A_COLS, B_COLS, XZ_COLS, DT_PAD)
    return pl.pallas_call(
        functools.partial(_inproj_kernel, rope=rope),
        out_shape=[jax.ShapeDtypeStruct((b, t, n), dt) for n, dt in zip(widths, (BF16, BF16, F32, F32))],
        grid=(b, t // tm),
        in_specs=in_specs,
        out_specs=[pl.BlockSpec((1, tm, n), lambda i, j: (i, j, 0)) for n in widths],
        compiler_params=_cparams(("parallel", "parallel")),
        name="inproj_rope" if rope else "inproj",
    )(*args)


WA_KEYS = 3 * WA_BLOCK


WA_BLOCKS_PER_STEP = 4


def _attn_a_kernel(sink_ref, q_ref, k_ref, v_ref, kc_ref, vc_ref, o_ref, s_ref, p_ref):
    step = pl.program_id(1)
    nk = WA_KEYS + CTX_LEN
    nq = WA_HEADS * WA_BLOCK
    kc, vc = kc_ref[0], vc_ref[0]
    starts = []
    for bb in range(WA_BLOCKS_PER_STEP):
        n = step * WA_BLOCKS_PER_STEP + bb
        starts.append(pl.multiple_of(jnp.clip((n - 1) * WA_BLOCK, 0, SEQ - WA_KEYS), WA_BLOCK))

    first_half = lax.broadcasted_iota(jnp.int32, (WA_BLOCK, LANES), 1) < HEAD_DIM
    swap = lambda v: pltpu.roll(v, HEAD_DIM, axis=1)

    for bb in range(WA_BLOCKS_PER_STEP):
        q = q_ref[0, bb * WA_BLOCK:(bb + 1) * WA_BLOCK, :].astype(F32) * ATT_SCALE
        q01, q23 = q[:, 0:LANES], q[:, LANES:]
        lhs = jnp.concatenate([
            jnp.where(first_half, q01, 0.0), jnp.where(first_half, swap(q01), 0.0),
            jnp.where(first_half, 0.0, swap(q23)), jnp.where(first_half, 0.0, q23)], axis=0)
        kall = jnp.concatenate([k_ref[0, pl.ds(starts[bb], WA_KEYS), :], kc], axis=0)
        s_ref[bb * nq:(bb + 1) * nq, :] = _bdot_nt(lhs, kall)

    nr = WA_BLOCKS_PER_STEP * nq
    rows = lax.broadcasted_iota(jnp.int32, (nr, nk), 0)
    cols = lax.broadcasted_iota(jnp.int32, (nr, nk), 1)
    row1 = lax.broadcasted_iota(jnp.int32, (nr, 1), 0)
    blk = row1 // nq
    qpos = (step * WA_BLOCKS_PER_STEP + blk) * WA_BLOCK + (rows & (WA_BLOCK - 1))
    kstart = starts[-1]
    for bb in range(WA_BLOCKS_PER_STEP - 2, -1, -1):
        kstart = jnp.where(blk == bb, starts[bb], kstart)
    valid = (cols >= WA_KEYS) | (jnp.abs(qpos - (kstart + cols)) <= WA_WINDOW)
    s = jnp.where(valid, s_ref[...], NEG_INF)
    head = (row1 // WA_BLOCK) % WA_HEADS
    sink = jnp.where(head == 0, sink_ref[0],
                     jnp.where(head == 1, sink_ref[1], jnp.where(head == 2, sink_ref[2], sink_ref[3])))
    m = jnp.maximum(jnp.max(s, axis=1, keepdims=True), sink)
    p_ref[...] = jnp.exp(s - m).astype(BF16)
    sink_term = jnp.exp(sink - m)

    ones = jnp.ones((nk, LANES), BF16)
    for bb in range(WA_BLOCKS_PER_STEP):
        vall = jnp.concatenate([v_ref[0, pl.ds(starts[bb], WA_KEYS), :], vc], axis=0)
        rs = slice(bb * nq, (bb + 1) * nq)
        pv = jnp.dot(p_ref[rs, :], jnp.concatenate([vall, ones], axis=1), preferred_element_type=F32)
        o = pv[:, 0:LANES] / (pv[:, LANES:] + sink_term[rs])
        o0, o1, o2, o3 = (o[h * WA_BLOCK:(h + 1) * WA_BLOCK] for h in range(WA_HEADS))
        qs = slice(bb * WA_BLOCK, (bb + 1) * WA_BLOCK)
        o_ref[0, qs, 0:LANES] = jnp.where(first_half, o0, swap(o1)).astype(o_ref.dtype)
        o_ref[0, qs, LANES:] = jnp.where(first_half, swap(o2), o3).astype(o_ref.dtype)


def _attn_a(sink, qkv, qkv_c):
    b = qkv.shape[0]
    nk = WA_KEYS + CTX_LEN
    tq = WA_BLOCKS_PER_STEP * WA_BLOCK
    nr = WA_BLOCKS_PER_STEP * WA_HEADS * WA_BLOCK
    return pl.pallas_call(
        _attn_a_kernel,
        out_shape=jax.ShapeDtypeStruct((b, SEQ, WA_HEADS * HEAD_DIM), BF16),
        scratch_shapes=[pltpu.VMEM((nr, nk), F32), pltpu.VMEM((nr, nk), BF16)],
        grid=(b, SEQ // tq),
        in_specs=[
            pl.BlockSpec(memory_space=pltpu.SMEM),
            pl.BlockSpec((1, tq, 2 * LANES), lambda i, j: (i, j, 0)),
            pl.BlockSpec((1, SEQ, LANES), lambda i, j: (i, 0, 2)),
            pl.BlockSpec((1, SEQ, LANES), lambda i, j: (i, 0, 3)),
            pl.BlockSpec((1, CTX_LEN, LANES), lambda i, j: (i, 0, 2)),
            pl.BlockSpec((1, CTX_LEN, LANES), lambda i, j: (i, 0, 3)),
        ],
        out_specs=pl.BlockSpec((1, tq, 2 * LANES), lambda i, j: (i, j, 0)),
        compiler_params=_cparams(("parallel", "arbitrary")),
        name="attn_window",
    )(sink, qkv, qkv, qkv, qkv_c, qkv_c)


NB_ROWS_PER_STEP = 16
NB_LOC = NA_KH * GRID_W
NB_DY_PAIRS = 2 * NA_KH - 2


def _attn_b_kernel(q_ref, k_ref, v_ref, kc_ref, vc_ref, t_ref, o_ref, s_ref, p_ref):
    i = pl.program_id(1)
    kc = kc_ref[0]
    vc = vc_ref[0]
    units = [(rr, pp) for rr in range(NB_ROWS_PER_STEP) for pp in range(NA_HEADS // 2)]
    starts, shifts = [], []
    for rr in range(NB_ROWS_PER_STEP):
        r = i * NB_ROWS_PER_STEP + rr
        rs = jnp.clip(r - NA_KH // 2, 0, GRID_ROWS - NA_KH)
        shifts.append(r - rs)
        starts.append(pl.multiple_of(rs * GRID_W, GRID_W))
    first_half = lax.broadcasted_iota(jnp.int32, (GRID_W, LANES), 1) < HEAD_DIM
    zero = jnp.zeros((), BF16)
    pw = 2 * GRID_W

    for u, (rr, pp) in enumerate(units):
        ls = slice(pp * LANES, (pp + 1) * LANES)
        if pp == 0:
            kw = k_ref[0, pl.ds(starts[rr], NB_LOC), :]
            q = q_ref[0, rr * GRID_W:(rr + 1) * GRID_W, :] * ATT_SCALE
        qs = q[:, ls]
        lhs = jnp.concatenate([jnp.where(first_half, qs, zero), jnp.where(first_half, zero, qs)], axis=0)
        bias = jnp.concatenate(
            [jnp.concatenate([t_ref[h, 2 * k - shifts[rr] + NA_KH - 1] for k in range(NA_KH // 2)], axis=1)
             for h in (2 * pp, 2 * pp + 1)], axis=0)
        s_ref[u * pw:(u + 1) * pw, 0:NB_LOC] = _bdot_nt(lhs, kw[:, ls]) + bias
        s_ref[u * pw:(u + 1) * pw, NB_LOC:] = _bdot_nt(lhs, kc[:, ls])

    s = s_ref[...]
    p_ref[...] = jnp.exp(s - jnp.max(s, axis=1, keepdims=True)).astype(BF16)

    ones_loc = jnp.ones((NB_LOC, LANES), BF16)
    ones_ctx = jnp.ones((CTX_LEN, LANES), BF16)
    vc_ext = [jnp.concatenate([vc[:, pp * LANES:(pp + 1) * LANES], ones_ctx], axis=1)
              for pp in range(NA_HEADS // 2)]
    for u, (rr, pp) in enumerate(units):
        ls = slice(pp * LANES, (pp + 1) * LANES)
        if pp == 0:
            vw = v_ref[0, pl.ds(starts[rr], NB_LOC), :]
        rows = slice(u * pw, (u + 1) * pw)
        pv = (jnp.dot(p_ref[rows, 0:NB_LOC], jnp.concatenate([vw[:, ls], ones_loc], axis=1),
                      preferred_element_type=F32)
              + jnp.dot(p_ref[rows, NB_LOC:], vc_ext[pp], preferred_element_type=F32))
        o = pv[:, 0:LANES] / pv[:, LANES:]
        o_ref[0, rr * GRID_W:(rr + 1) * GRID_W, ls] = jnp.where(
            first_half, o[:GRID_W], o[GRID_W:]).astype(o_ref.dtype)


def _attn_b(qkv, qkv_c, table):
    b = qkv.shape[0]
    tq = NB_ROWS_PER_STEP * GRID_W
    w = NA_HEADS * HEAD_DIM
    nu = NB_ROWS_PER_STEP * NA_HEADS * GRID_W
    return pl.pallas_call(
        _attn_b_kernel,
        out_shape=jax.ShapeDtypeStruct((b, SEQ, w), BF16),
        scratch_shapes=[pltpu.VMEM((nu, NB_LOC + CTX_LEN), F32), pltpu.VMEM((nu, NB_LOC + CTX_LEN), BF16)],
        grid=(b, SEQ // tq),
        in_specs=[
            pl.BlockSpec((1, tq, w), lambda i, j: (i, j, 0)),
            pl.BlockSpec((1, SEQ, w), lambda i, j: (i, 0, 1)),
            pl.BlockSpec((1, SEQ, w), lambda i, j: (i, 0, 2)),
            pl.BlockSpec((1, CTX_LEN, w), lambda i, j: (i, 0, 1)),
            pl.BlockSpec((1, CTX_LEN, w), lambda i, j: (i, 0, 2)),
            pl.BlockSpec((NA_HEADS, NB_DY_PAIRS, GRID_W, LANES), lambda i, j: (0, 0, 0, 0)),
        ],
        out_specs=pl.BlockSpec((1, tq, w), lambda i, j: (i, j, 0)),
        compiler_params=_cparams(("parallel", "arbitrary")),
        name="attn_neighbourhood",
    )(qkv, qkv, qkv, qkv_c, qkv_c, table)


def _split3(a):
    a1 = a.astype(BF16)
    r1 = a - a1.astype(F32)
    a2 = r1.astype(BF16)
    a3 = (r1 - a2.astype(F32)).astype(BF16)
    return a1, a2, a3


def _bias_kernel(r_ref, oh_ref, o_ref):
    oh = oh_ref[...]
    o_ref[...] = sum(jnp.dot(t, oh, preferred_element_type=F32) for t in _split3(r_ref[...]))


def _na_bias_tables(rpb):
    ndy, ndx = 2 * NA_KH - 1, 2 * NA_KW - 1
    qc = np.arange(GRID_W)[:, None]
    x = np.arange(GRID_W)[None, :]
    dx = np.clip(x - qc, -(NA_KW - 1), NA_KW - 1) + NA_KW - 1
    onehot = (np.arange(LANES)[:, None, None] == dx[None]).reshape(LANES, GRID_W * GRID_W)
    cstart = np.clip(qc - NA_KW // 2, 0, GRID_W - NA_KW)
    inside = (x >= cstart) & (x < cstart + NA_KW)
    rows = DEPTH * NA_HEADS * ndy
    r = jnp.pad(rpb.astype(F32).reshape(rows, ndx), ((0, LANES - rows), (0, LANES - ndx)))
    m = pl.pallas_call(
        _bias_kernel,
        out_shape=jax.ShapeDtypeStruct((LANES, GRID_W * GRID_W), F32),
        name="na_bias_expand",
    )(r, jnp.asarray(onehot, BF16))
    m = m[:rows].reshape(DEPTH, NA_HEADS, ndy, GRID_W, GRID_W)
    m = jnp.where(jnp.asarray(inside), m, NEG_INF)
    return jnp.concatenate([m[:, :, :ndy - 1], m[:, :, 1:]], axis=-1)


def _ctx_attn_kernel(sink_ref, a_ref, b_ref, oa_ref, ob_ref):
    a = a_ref[0]
    row1 = lax.broadcasted_iota(jnp.int32, (2 * CTX_LEN, 1), 0)
    for g in range(WA_KV_HEADS):
        h0, h1 = 2 * g, 2 * g + 1
        q2 = jnp.concatenate([a[:, h0 * HEAD_DIM:(h0 + 1) * HEAD_DIM],
                              a[:, h1 * HEAD_DIM:(h1 + 1) * HEAD_DIM]], axis=0)
        k = a[:, 2 * LANES + g * HEAD_DIM:2 * LANES + (g + 1) * HEAD_DIM]
        v = a[:, 3 * LANES + g * HEAD_DIM:3 * LANES + (g + 1) * HEAD_DIM]
        s = _bdot_nt(q2, k) * ATT_SCALE
        sink = jnp.where(row1 < CTX_LEN, sink_ref[h0], sink_ref[h1])
        m = jnp.maximum(jnp.max(s, axis=1, keepdims=True), sink)
        p = jnp.exp(s - m)
        den = jnp.sum(p, axis=1, keepdims=True) + jnp.exp(sink - m)
        o = _bdot(p, v) / den
        oa_ref[0, :, h0 * HEAD_DIM:(h0 + 1) * HEAD_DIM] = o[:CTX_LEN]
        oa_ref[0, :, h1 * HEAD_DIM:(h1 + 1) * HEAD_DIM] = o[CTX_LEN:]
    bq = b_ref[0]
    w = NA_HEADS * HEAD_DIM
    for h in range(NA_HEADS):
        hs = slice(h * HEAD_DIM, (h + 1) * HEAD_DIM)
        s = _bdot_nt(bq[:, hs], bq[:, w + h * HEAD_DIM:w + (h + 1) * HEAD_DIM]) * ATT_SCALE
        m = jnp.max(s, axis=1, keepdims=True)
        p = jnp.exp(s - m)
        den = jnp.sum(p, axis=1, keepdims=True)
        ob_ref[0, :, hs] = _bdot(p, bq[:, 2 * w + h * HEAD_DIM:2 * w + (h + 1) * HEAD_DIM]) / den


def _ctx_attn(sink, qkv_a_c, qkv_b_c):
    b = qkv_a_c.shape[0]
    w = 2 * LANES
    return pl.pallas_call(
        _ctx_attn_kernel,
        out_shape=[jax.ShapeDtypeStruct((b, CTX_LEN, w), F32)] * 2,
        grid=(b,),
        in_specs=[
            pl.BlockSpec(memory_space=pltpu.SMEM),
            pl.BlockSpec((1, CTX_LEN, A_COLS), lambda i: (i, 0, 0)),
            pl.BlockSpec((1, CTX_LEN, B_COLS), lambda i: (i, 0, 0)),
        ],
        out_specs=[pl.BlockSpec((1, CTX_LEN, w), lambda i: (i, 0, 0))] * 2,
        compiler_params=_cparams(("parallel",)),
        name="attn_context",
    )(sink, qkv_a_c, qkv_b_c)


CONV_HALO = SUBLANES


def _conv_kernel(prev_ref, cur_ref, next_ref, w_ref, b_ref, o_ref, ext_ref, *, tl, nt):
    j = pl.program_id(1)
    ext_ref[0:CONV_HALO, :] = jnp.where(j > 0, prev_ref[0], 0.0)
    ext_ref[CONV_HALO:CONV_HALO + tl, :] = cur_ref[0]
    ext_ref[CONV_HALO + tl:, :] = jnp.where(j < nt - 1, next_ref[0], 0.0)
    acc = jnp.zeros((tl, XBC_COLS), F32) + b_ref[...]
    base = CONV_HALO - SSM_CONV // 2
    for k in range(SSM_CONV):
        acc = acc + w_ref[k:k + 1, :] * ext_ref[base + k:base + k + tl, :]
    o_ref[0] = _silu(acc)


def _conv_silu(xz, conv_w, conv_b, tl):
    b, t, _ = xz.shape
    nt = t // tl
    hb = tl // CONV_HALO
    last = t // CONV_HALO - 1
    return pl.pallas_call(
        functools.partial(_conv_kernel, tl=tl, nt=nt),
        out_shape=jax.ShapeDtypeStruct((b, t, XBC_COLS), F32),
        grid=(b, nt),
        in_specs=[
            pl.BlockSpec((1, CONV_HALO, XBC_COLS), lambda i, j: (i, jnp.maximum(j * hb - 1, 0), 0)),
            pl.BlockSpec((1, tl, XBC_COLS), lambda i, j: (i, j, 0)),
            pl.BlockSpec((1, CONV_HALO, XBC_COLS), lambda i, j: (i, jnp.minimum((j + 1) * hb, last), 0)),
            pl.BlockSpec((SUBLANES, XBC_COLS), lambda i, j: (0, 0)),
            pl.BlockSpec((1, XBC_COLS), lambda i, j: (0, 0)),
        ],
        out_specs=pl.BlockSpec((1, tl, XBC_COLS), lambda i, j: (i, j, 0)),
        scratch_shapes=[pltpu.VMEM((tl + 2 * CONV_HALO, XBC_COLS), F32)],
        compiler_params=_cparams(("parallel", "parallel")),
        name="ssm_conv",
    )(xz, xz, xz, conv_w, conv_b)


Q = SSM_CHUNK
GS = SSM_GROUPS * SSM_STATE
HPG = SSM_HEADS // SSM_GROUPS
SSD_STATE_SHAPE = (SSM_GROUPS, SSM_STATE, HPG * SSM_HEAD_DIM)
SSD_BATCH_PER_STEP = 4


def _softplus(v):
    return jnp.maximum(v, 0.0) + jnp.log1p(jnp.exp(-jnp.abs(v)))


def _cumsum_mat(tri, a):
    r = jnp.dot(tri, jnp.concatenate(_split3(a), axis=1), preferred_element_type=F32)
    n = a.shape[1]
    return r[:, 0:n] + r[:, n:2 * n] + r[:, 2 * n:]


def _ssd_kernel(xm_ref, dtm_ref, xb_ref, dtb_ref, bias_ref, alog_ref, h0f_ref, h0b_ref,
                y1_ref, y2_ref, hf_out_ref, hb_out_ref, hf_ref, hb_ref, *, nc):
    i = pl.program_id(1)

    @pl.when(i == 0)
    def _():
        hf_ref[...] = h0f_ref[...]
        hb_ref[...] = h0b_ref[...]

    ii = lax.broadcasted_iota(jnp.int32, (Q, Q), 0)
    jj = lax.broadcasted_iota(jnp.int32, (Q, Q), 1)
    lower = ii >= jj
    diag = ii == jj
    tril =jnp.where(lower, 1.0, 0.0).astype(BF16)
    triu = jnp.where(ii <= jj, 1.0, 0.0).astype(BF16)
    first_half = lax.broadcasted_iota(jnp.int32, (Q, LANES), 1) < SSM_HEAD_DIM
    a_row = -jnp.exp(alog_ref[...])
    gw = HPG * SSM_HEAD_DIM

    def group_operands(xbc, g):
        bg = xbc[:, SSM_INNER + g * SSM_STATE:SSM_INNER + (g + 1) * SSM_STATE]
        cg = xbc[:, SSM_INNER + GS + g * SSM_STATE:SSM_INNER + GS + (g + 1) * SSM_STATE]
        return bg, cg.astype(BF16), xbc[:, g * gw:(g + 1) * gw]

    nbs = range(SSD_BATCH_PER_STEP)
    groups = range(SSM_GROUPS)
    bias = bias_ref[...]
    zero = jnp.zeros((), F32)


    xm = [xm_ref[bi] for bi in nbs]
    xb = [xb_ref[bi] for bi in nbs]
    dtm = [_softplus(dtm_ref[bi] + bias) for bi in nbs]
    dtb = [_softplus(dtb_ref[bi] + bias) for bi in nbs]
    cumf =[_cumsum_mat(tril, d * a_row) for d in dtm]
    cumr = [_cumsum_mat(triu, d * a_row) for d in dtm]
    cumb = [_cumsum_mat(triu, d * a_row) for d in dtb]

    opm = [[group_operands(xm[bi], g) for g in groups] for bi in nbs]
    opb = [[group_operands(xb[bi], g) for g in groups] for bi in nbs]
    gmat = [[_bdot_nt(opm[bi][g][1], opm[bi][g][0]) for g in groups] for bi in nbs]
    hf_prev = [[hf_ref[bi, g] for g in groups] for bi in nbs]
    hb_prev = [[hb_ref[bi, g] for g in groups] for bi in nbs]
    inter_f = [[jnp.dot(opm[bi][g][1], hf_prev[bi][g].astype(BF16), preferred_element_type=F32)
                for g in groups] for bi in nbs]
    inter_b = [[jnp.dot(opb[bi][g][1], hb_prev[bi][g].astype(BF16), preferred_element_type=F32)
                for g in groups] for bi in nbs]
    bt_m = [[opm[bi][g][0].T for g in groups] for bi in nbs]
    bt_b = [[opb[bi][g][0].T for g in groups] for bi in nbs]

    cumf_t = [c.T for c in cumf]
    cumr_t = [c.T for c in cumr]
    dtm_t = [d.T for d in dtm]
    cumb_t = [c.T for c in cumb]
    dtb_t = [d.T for d in dtb]

    sel_r = lax.broadcasted_iota(jnp.int32, (2 * LANES, SSM_INNER), 0) % LANES
    sel_c = lax.broadcasted_iota(jnp.int32, (2 * LANES, SSM_INNER), 1) // SSM_HEAD_DIM
    sel_f = jnp.where(sel_r == sel_c, 1.0, 0.0).astype(BF16)
    sel_b = jnp.where(sel_r == sel_c + SSM_HEADS, 1.0, 0.0).astype(BF16)

    def spread(e, sel):
        hi = e.astype(BF16)
        lo = (e - hi.astype(F32)).astype(BF16)
        return jnp.dot(jnp.concatenate([hi, lo], axis=1), sel, preferred_element_type=F32)

    sl_in_f = [spread(jnp.exp(c), sel_f) for c in cumf]
    sl_in_b = [spread(jnp.exp(c), sel_b) for c in cumb]

    heads = [(bi, h) for bi in nbs for h in range(SSM_HEADS)]
    hb_ = lambda h: SSM_HEADS + h
    cf, cr, wm, wst_f, wst_b = {}, {}, {}, {}, {}
    for t in range(len(heads) + 1):
        if t < len(heads):
            bi, h = k = heads[t]
            cf[k] = jnp.broadcast_to(cumf[bi][:, h:h + 1], (Q, Q))
            cr[k] = jnp.broadcast_to(cumr[bi][:, hb_(h):hb_(h) + 1], (Q, Q))
        if t >= 1:
            bi, h = k = heads[t - 1]
            rf = cumf_t[bi][h:h + 1, :]
            rb = cumb_t[bi][hb_(h):hb_(h) + 1, :]
            dtf_row = dtm_t[bi][h:h + 1, :]
            dtb_row = dtm_t[bi][hb_(h):hb_(h) + 1, :]
            arg = jnp.where(lower, cf[k] - rf, cr[k] - cumr_t[bi][hb_(h):hb_(h) + 1, :])
            wm[k] = (gmat[bi][h // HPG] * (jnp.exp(arg) * jnp.where(lower, dtf_row, dtb_row)
                                           + jnp.where(diag, dtb_row, zero))).astype(BF16)
            w_f = jnp.exp(rf[:, Q - 1:Q] - rf) * dtf_row
            w_b = jnp.exp(rb[:, 0:1] - rb) * dtb_t[bi][hb_(h):hb_(h) + 1, :]
            wst_f[k] = (bt_m[bi][h // HPG] * w_f).astype(BF16)
            wst_b[k] = (bt_b[bi][h // HPG] * w_b).astype(BF16)

    def pair_rhs(x, pr):
        xp = x[:, pr * LANES:(pr + 1) * LANES]
        return jnp.concatenate([jnp.where(first_half, xp, zero), jnp.where(first_half, zero, xp)],
                               axis=0).astype(BF16)

    def pair_dot(mats, bi, pr, rhs):
        lhs = jnp.concatenate([mats[bi, 2 * pr], mats[bi, 2 * pr + 1]], axis=1)
        return jnp.dot(lhs, rhs, preferred_element_type=F32)

    pairs = [(bi, pr) for bi in nbs for pr in range(SSM_HEADS // 2)]
    rhs_m = {k: pair_rhs(xm[k[0]], k[1]) for k in pairs}
    rhs_b = {k: pair_rhs(xb[k[0]], k[1]) for k in pairs}
    y_intra = {k: pair_dot(wm, k[0], k[1], rhs_m[k]) for k in pairs}
    st_f = {k: pair_dot(wst_f, k[0], k[1], rhs_m[k]) for k in pairs}
    st_b = {k: pair_dot(wst_b, k[0], k[1], rhs_b[k]) for k in pairs}

    ppg = HPG // 2
    for bi, pr in pairs:
        g, k = pr // ppg, pr % ppg
        ls = slice(k * LANES, (k + 1) * LANES)
        sin_f = sl_in_f[bi][:, pr * LANES:(pr + 1) * LANES]
        sin_b = sl_in_b[bi][:, pr * LANES:(pr + 1) * LANES]
        y1_ref[bi, :, pr * LANES:(pr + 1) * LANES] = y_intra[bi, pr] + sin_f * inter_f[bi][g][:, ls]
        y2_ref[bi, :, pr * LANES:(pr + 1) * LANES] = sin_b * inter_b[bi][g][:, ls]
        hf_ref[bi, g, :, ls] = hf_prev[bi][g][:, ls] * sin_f[Q - 1:Q, :] + st_f[bi, pr]
        hb_ref[bi, g, :, ls] = hb_prev[bi][g][:, ls] * sin_b[0:1, :] + st_b[bi, pr]

    @pl.when(i == nc - 1)
    def _():
        hf_out_ref[...] = hf_ref[...]
        hb_out_ref[...] = hb_ref[...]


def _ssd(xbc, dt_raw, dt_bias, a_log, h0f, h0b):
    b, t, _ = xbc.shape
    nc = t // Q
    nb = SSD_BATCH_PER_STEP
    st_shape = (b,) + SSD_STATE_SHAPE
    st_spec = pl.BlockSpec((nb,) + SSD_STATE_SHAPE, lambda i, j: (i, 0, 0, 0))
    fwd = lambda i, j: (i, j, 0)
    bwd = lambda i, j: (i, nc - 1 - j, 0)
    return pl.pallas_call(
        functools.partial(_ssd_kernel, nc=nc),
        out_shape=[jax.ShapeDtypeStruct((b, t, SSM_INNER), F32)] * 2
        + [jax.ShapeDtypeStruct(st_shape, F32)] * 2,
        grid=(b // nb, nc),
        in_specs=[
            pl.BlockSpec((nb, Q, XBC_COLS), fwd),
            pl.BlockSpec((nb, Q, DT_PAD), fwd),
            pl.BlockSpec((nb, Q, XBC_COLS), bwd),
            pl.BlockSpec((nb, Q, DT_PAD), bwd),
            pl.BlockSpec((1, DT_PAD), lambda i, j: (0, 0)),
            pl.BlockSpec((1, DT_PAD), lambda i, j: (0, 0)),
            st_spec, st_spec,
        ],
        out_specs=[pl.BlockSpec((nb, Q, SSM_INNER), fwd), pl.BlockSpec((nb, Q, SSM_INNER), bwd),
                   st_spec, st_spec],
        scratch_shapes=[pltpu.VMEM((nb,) + SSD_STATE_SHAPE, F32)] * 2,
        compiler_params=_cparams(("parallel", "arbitrary")),
        name="ssd_scan",
    )(xbc, dt_raw, xbc, dt_raw, dt_bias, a_log, h0f, h0b)


MXU_TILE = 256
FFN_SPLITS = (0, 6 * MXU_TILE, D_FF)


def _mix_ffn_kernel(x_ref, mod_ref, oa_ref, ob_ref, y1_ref, y2_ref, xs_ref, z_ref, d_ref, sg_ref,
                    wo_ref, g_ref, wfi_ref, wfo_ref, gfin_ref, o_ref, *, final):
    y = y1_ref[0] + y2_ref[0] + d_ref[...] * xs_ref[0]
    y = y * _silu(z_ref[0])
    var = jnp.mean(y * y, axis=-1, keepdims=True)
    oc = (y * lax.rsqrt(var + EPS) * sg_ref[...]).astype(BF16)
    wa = 2 * LANES
    mix = (jnp.dot(oa_ref[0].astype(BF16), wo_ref[0, 0:wa, :], preferred_element_type=F32)
           + jnp.dot(ob_ref[0].astype(BF16), wo_ref[0, wa:2 * wa, :], preferred_element_type=F32)
           + jnp.dot(oc, wo_ref[0, 2 * wa:, :], preferred_element_type=F32))
    xn = x_ref[0] + mod_ref[0, 2:3, :] * mix
    h = _norm_mod(xn, g_ref[...], mod_ref[0, 3:4, :], mod_ref[0, 4:5, :]).astype(BF16)
    acc = None
    for lo, hi in zip(FFN_SPLITS[:-1], FFN_SPLITS[1:]):
        gate = jnp.dot(h, wfi_ref[0, :, lo:hi], preferred_element_type=F32)
        up = jnp.dot(h, wfi_ref[0, :, D_FF + lo:D_FF + hi], preferred_element_type=F32)
        act = (_silu(gate) * up).astype(BF16)
        part = jnp.dot(act, wfo_ref[0, lo:hi, :], preferred_element_type=F32)
        acc = part if acc is None else acc + part
    out = xn + mod_ref[0, 5:6, :] * acc
    if final:
        var = jnp.mean(out * out, axis=-1, keepdims=True)
        out = out * lax.rsqrt(var + EPS) * gfin_ref[...]
    o_ref[0] = out


def _mix_ffn(x, mods, o_a, o_b, y1, y2, xbc, xz, d_exp, ssm_g, w_out, g_ffn, w_ffn_in, w_ffn_out,
             g_final, layer, tm, final):
    b, t, _ = x.shape
    per_batch = mods.shape[0] > 1
    row = lambda n: pl.BlockSpec((1, tm, n), lambda i, j: (i, j, 0))
    const = lambda shape: pl.BlockSpec(shape, lambda i, j: (0, 0), pipeline_mode=pl.Buffered(1))
    weight = lambda r, c: pl.BlockSpec((1, r, c), lambda i, j: (layer, 0, 0), pipeline_mode=pl.Buffered(1))
    return pl.pallas_call(
        functools.partial(_mix_ffn_kernel, final=final),
        out_shape=jax.ShapeDtypeStruct((b, t, D_MODEL), F32),
        grid=(b, t // tm),
        in_specs=[
            row(D_MODEL),
            pl.BlockSpec((1, SUBLANES, D_MODEL), (lambda i, j: (i, 0, 0)) if per_batch else (lambda i, j: (0, 0, 0))),
            row(2 * LANES), row(2 * LANES), row(SSM_INNER), row(SSM_INNER),
            pl.BlockSpec((1, tm, SSM_INNER), lambda i, j: (i, j, 0)),
            pl.BlockSpec((1, tm, SSM_INNER), lambda i, j: (i, j, 2)),
            const((1, SSM_INNER)), const((1, SSM_INNER)),
            weight(D_MODEL, D_MODEL),
            const((1, D_MODEL)),
            weight(D_MODEL, 2 * D_FF),
            weight(D_FF, D_MODEL),
            const((1, D_MODEL)),
        ],
        out_specs=row(D_MODEL),
        compiler_params=_cparams(("parallel", "parallel")),
        name="mix_ffn_final" if final else "mix_ffn",
    )(x, mods, o_a, o_b, y1, y2, xbc, xz, d_exp, ssm_g, w_out, g_ffn, w_ffn_in, w_ffn_out, g_final)


def _rope_tables():
    t = np.arange(SEQ)
    pos = np.stack([t // GRID_W, t % GRID_W], axis=1).astype(np.float64)
    quarter = HEAD_DIM // 4
    inv_freq = ROPE_BASE ** (-np.arange(quarter, dtype=np.float64) / quarter)
    lane = np.arange(LANES) % HEAD_DIM
    half = lane // (HEAD_DIM // 2)
    idx = lane % (HEAD_DIM // 2)
    ang = pos[:, half] * inv_freq[idx % quarter][None, :]
    cos, sin = np.cos(ang), np.sin(ang)
    first = (idx < quarter)[None, :]
    tabs = (cos, np.where(first, -sin, 0.0), np.where(first, 0.0, sin))
    return tuple(jnp.asarray(v, F32) for v in tabs)


def _pad_lanes(v, n=LANES):
    v = v.reshape(1, -1)
    return jnp.pad(v, ((0, 0), (0, n - v.shape[1])))


def kernel(x, c, ctx, c_ctx, w_mod, b_mod, g_mix, w_in, wa_sink, na_rpb, ssm_conv_w, ssm_conv_b,
           ssm_dt_bias, ssm_a_log, ssm_d, ssm_norm_g, w_out, g_ffn, w_ffn_in, w_ffn_out, g_final):
    cin = jnp.concatenate([c, c_ctx[None, :], jnp.zeros((SUBLANES - BATCH - 1, D_MODEL), F32)], axis=0)
    mod_all = _modulation(cin, w_mod, b_mod)
    rope_tabs = _rope_tables()
    bias_tabs = _na_bias_tables(na_rpb)
    zeros_state = jnp.zeros((BATCH,) + SSD_STATE_SHAPE, F32)
    gfin = g_final.reshape(1, D_MODEL)
    w_proj = w_in.astype(BF16)
    wo = w_out.astype(BF16)
    wfi = w_ffn_in.astype(BF16)
    wfo = w_ffn_out.astype(BF16)

    xl, xc = x, ctx
    for l in range(DEPTH):
        last = l == DEPTH - 1
        m6 = mod_all[l].reshape(SUBLANES, 6, D_MODEL)
        mods_l = jnp.pad(m6[:BATCH], ((0, 0), (0, 2), (0, 0)))
        mods_c = jnp.pad(m6[BATCH:BATCH + 1], ((0, 0), (0, 2), (0, 0)))
        w_dt = jnp.pad(w_in[l][:, W_DT:], ((0, 0), (0, DT_PAD - DT_COLS))).astype(BF16)
        g1 = g_mix[l].reshape(1, D_MODEL)
        g2 = g_ffn[l].reshape(1, D_MODEL)
        conv_w = jnp.pad(ssm_conv_w[l], ((0, SUBLANES - SSM_CONV), (0, 0)))
        conv_b = ssm_conv_b[l].reshape(1, XBC_COLS)
        dt_bias = _pad_lanes(ssm_dt_bias[l])
        a_log = _pad_lanes(ssm_a_log[l])
        d_exp = jnp.repeat(ssm_d[l], SSM_HEAD_DIM).reshape(1, SSM_INNER)
        sg = ssm_norm_g[l].reshape(1, SSM_INNER)

        a_c, b_c, xz_c, dt_c = _inproj(xc, mods_c, g1, w_proj, l, w_dt, None, CTX_LEN)
        a_l, b_l, xz_l, dt_l = _inproj(xl, mods_l, g1, w_proj, l, w_dt, rope_tabs, TM_INPROJ)

        o_a = _attn_a(wa_sink[l], a_l, a_c)
        o_b = _attn_b(b_l, b_c, bias_tabs[l])

        xbc_c = _conv_silu(xz_c, conv_w, conv_b, CTX_LEN)
        xbc_l = _conv_silu(xz_l, conv_w, conv_b, TM_CONV)
        y1_c, y2_c, h_f, h_b = _ssd(xbc_c, dt_c, dt_bias, a_log, zeros_state, zeros_state)
        y1_l, y2_l, _, _ = _ssd(xbc_l, dt_l, dt_bias, a_log, h_f, h_b)

        xl = _mix_ffn(xl, mods_l, o_a, o_b, y1_l, y2_l, xbc_l, xz_l, d_exp, sg, wo, g2, wfi, wfo, gfin,
                      l, TM_MIX, last)
        if not last:
            o_ac, o_bc = _ctx_attn(wa_sink[l], a_c, b_c)
            xc = _mix_ffn(xc, mods_c, o_ac, o_bc, y1_c, y2_c, xbc_c, xz_c, d_exp, sg, wo, g2, wfi, wfo,
                          gfin, l, CTX_LEN, False)
    return xl
```

```python
import functools

import numpy as np
import jax
import jax.numpy as jnp
from jax import lax
from jax.experimental import pallas as pl
from jax.experimental.pallas import tpu as pltpu

F32 = jnp.float32
BF16 = jnp.bfloat16

D_MODEL = 1024
BATCH = 4
SEQ = 4096
DEPTH = 2
GRID_W = 64
GRID_ROWS = SEQ // GRID_W
CTX_LEN = 256
EPS = 1e-6
HEAD_DIM = 64
ROPE_BASE = 10000.0
WA_HEADS = 4
WA_KV_HEADS = 2
WA_WINDOW = 128
WA_BLOCK = 128
NA_HEADS = 4
NA_KH = 8
NA_KW = 16
SSM_HEADS = 8
SSM_HEAD_DIM = 64
SSM_INNER = SSM_HEADS * SSM_HEAD_DIM
SSM_GROUPS = 2
SSM_STATE = 128
SSM_CONV = 7
SSM_CHUNK = 128
D_FF = 2816
XBC_COLS = SSM_INNER + 2 * SSM_GROUPS * SSM_STATE
DT_COLS = 2 * SSM_HEADS

LANES = 128
SUBLANES = 8
V7X_VMEM_BYTES = 64 * 1024 * 1024
VMEM_LIMIT = V7X_VMEM_BYTES - 8 * 1024 * 1024

TM_INPROJ = 1024
TM_CONV = 1024
TM_MIX = 512

A_COLS = 512
B_COLS = 768
XZ_COLS = 1536
DT_PAD = LANES

ATT_SCALE = HEAD_DIM ** -0.5
NEG_INF = float("-inf")
NT_DIMS = (((1,), (1,)), ((), ()))


def _silu(v):
    return v / (1.0 + jnp.exp(-v))


def _bdot(a, b):
    return jnp.dot(a.astype(BF16), b.astype(BF16), preferred_element_type=F32)


def _bdot_nt(a, b):
    return lax.dot_general(a.astype(BF16), b.astype(BF16), NT_DIMS, preferred_element_type=F32)


def _cparams(sem):
    return pltpu.CompilerParams(dimension_semantics=sem, vmem_limit_bytes=VMEM_LIMIT)


MOD_TN = 6 * D_MODEL // 4


def _mod_kernel(c_ref, w_ref, b_ref, o_ref):
    s = _silu(c_ref[...])
    o_ref[0] = _bdot(s, w_ref[0]) + b_ref[0]


def _modulation(cin, w_mod, b_mod):
    n = 6 * D_MODEL
    return pl.pallas_call(
        _mod_kernel,
        out_shape=jax.ShapeDtypeStruct((DEPTH, SUBLANES, n), F32),
        grid=(DEPTH, n // MOD_TN),
        in_specs=[
            pl.BlockSpec((SUBLANES, D_MODEL), lambda l, j: (0, 0)),
            pl.BlockSpec((1, D_MODEL, MOD_TN), lambda l, j: (l, 0, j)),
            pl.BlockSpec((1, 1, MOD_TN), lambda l, j: (l, 0, j)),
        ],
        out_specs=pl.BlockSpec((1, SUBLANES, MOD_TN), lambda l, j: (l, 0, j)),
        compiler_params=_cparams(("parallel", "parallel")),
        name="modulation",
    )(cin, w_mod, b_mod.reshape(DEPTH, 1, n))


def _norm_mod(x, g, shift, scale):
    var = jnp.mean(x * x, axis=-1, keepdims=True)
    h = x * lax.rsqrt(var + EPS) * g
    return h * (1.0 + scale) + shift


W_QA = 0
W_QB = W_QA + WA_HEADS * HEAD_DIM
W_Z = W_QB + NA_HEADS * HEAD_DIM
W_KVA = W_Z + SSM_INNER
W_KVB = W_KVA + 2 * WA_KV_HEADS * HEAD_DIM
W_XBC = W_KVB + 2 * NA_HEADS * HEAD_DIM
W_DT = W_XBC + XBC_COLS
IN_COLS = W_DT + DT_COLS
ROPE_QUARTER = HEAD_DIM // 4


def _inproj_kernel(x_ref, mod_ref, g_ref, w_ref, wdt_ref, *rest, rope):
    if rope:
        cos_ref, sa_ref, sb_ref, oa_ref, ob_ref, oxz_ref, odt_ref = rest
    else:
        oa_ref, ob_ref, oxz_ref, odt_ref = rest
    h = _norm_mod(x_ref[0], g_ref[...], mod_ref[0, 0:1, :], mod_ref[0, 1:2, :]).astype(BF16)

    def proj(lo, hi):
        return jnp.dot(h, w_ref[0, :, lo:hi], preferred_element_type=F32)

    qa = proj(W_QA, W_QB)
    kva = proj(W_KVA, W_KVB)
    if rope:
        cos, sa, sb = cos_ref[...], sa_ref[...], sb_ref[...]

        def rot(v):
            up = pltpu.roll(v, LANES - ROPE_QUARTER, axis=1)
            dn = pltpu.roll(v, ROPE_QUARTER, axis=1)
            return v * cos + up * sa + dn * sb

        oa_ref[0, :, 0:LANES] = rot(qa[:, 0:LANES]).astype(BF16)
        oa_ref[0, :, LANES:2 * LANES] = rot(qa[:, LANES:]).astype(BF16)
        oa_ref[0, :, 2 * LANES:3 * LANES] = rot(kva[:, 0:LANES]).astype(BF16)
        oa_ref[0, :, 3 * LANES:] = kva[:, LANES:].astype(BF16)
    else:
        oa_ref[0, :, 0:2 * LANES] = qa.astype(BF16)
        oa_ref[0, :, 2 * LANES:] = kva.astype(BF16)
    ob_ref[0, :, 0:2 * LANES] = proj(W_QB, W_Z).astype(BF16)
    ob_ref[0, :, 2 * LANES:] = proj(W_KVB, W_XBC).astype(BF16)
    oxz_ref[0, :, 0:XBC_COLS] = proj(W_XBC, W_DT)
    oxz_ref[0, :, XBC_COLS:] = proj(W_Z, W_KVA)
    odt_ref[0] = jnp.dot(h, wdt_ref[...], preferred_element_type=F32)


def _inproj(x, mods, g, w, layer, wdt, rope_tabs, tm):
    b, t, _ = x.shape
    rope = rope_tabs is not None
    per_batch = mods.shape[0] > 1
    in_specs = [
        pl.BlockSpec((1, tm, D_MODEL), lambda i, j: (i, j, 0)),
        pl.BlockSpec((1, SUBLANES, D_MODEL), (lambda i, j: (i, 0, 0)) if per_batch else (lambda i, j: (0, 0, 0))),
        pl.BlockSpec((1, D_MODEL), lambda i, j: (0, 0)),
        pl.BlockSpec((1, D_MODEL, IN_COLS), lambda i, j: (layer, 0, 0)),
        pl.BlockSpec((D_MODEL, DT_PAD), lambda i, j: (0, 0)),
    ]
    args = [x, mods, g, w, wdt]
    if rope:
        in_specs += [pl.BlockSpec((tm, LANES), lambda i, j: (j, 0))] * 3
        args += list(rope_tabs)
    widths = (A_COLS, B_COLS, XZ_COLS, DT_PAD)
    return pl.pallas_call(
        functools.partial(_inproj_kernel, rope=rope),
        out_shape=[jax.ShapeDtypeStruct((b, t, n), dt) for n, dt in zip(widths, (BF16, BF16, F32, F32))],
        grid=(b, t // tm),
        in_specs=in_specs,
        out_specs=[pl.BlockSpec((1, tm, n), lambda i, j: (i, j, 0)) for n in widths],
        compiler_params=_cparams(("parallel", "parallel")),
        name="inproj_rope" if rope else "inproj",
    )(*args)


WA_KEYS = 3 * WA_BLOCK


WA_BLOCKS_PER_STEP = 4


def _attn_a_kernel(sink_ref, q_ref, k_ref, v_ref, kc_ref, vc_ref, o_ref, s_ref, p_ref):
    step = pl.program_id(1)
    nk = WA_KEYS + CTX_LEN
    nq = WA_HEADS * WA_BLOCK
    kc, vc = kc_ref[0], vc_ref[0]
    starts = []
    for bb in range(WA_BLOCKS_PER_STEP):
        n = step * WA_BLOCKS_PER_STEP + bb
        starts.append(pl.multiple_of(jnp.clip((n - 1) * WA_BLOCK, 0, SEQ - WA_KEYS), WA_BLOCK))

    first_half = lax.broadcasted_iota(jnp.int32, (WA_BLOCK, LANES), 1) < HEAD_DIM
    swap = lambda v: pltpu.roll(v, HEAD_DIM, axis=1)

    for bb in range(WA_BLOCKS_PER_STEP):
        q = q_ref[0, bb * WA_BLOCK:(bb + 1) * WA_BLOCK, :].astype(F32) * ATT_SCALE
        q01, q23 = q[:, 0:LANES], q[:, LANES:]
        lhs = jnp.concatenate([
            jnp.where(first_half, q01, 0.0), jnp.where(first_half, swap(q01), 0.0),
            jnp.where(first_half, 0.0, swap(q23)), jnp.where(first_half, 0.0, q23)], axis=0)
        kall = jnp.concatenate([k_ref[0, pl.ds(starts[bb], WA_KEYS), :], kc], axis=0)
        s_ref[bb * nq:(bb + 1) * nq, :] = _bdot_nt(lhs, kall)

    nr = WA_BLOCKS_PER_STEP * nq
    rows = lax.broadcasted_iota(jnp.int32, (nr, nk), 0)
    cols = lax.broadcasted_iota(jnp.int32, (nr, nk), 1)
    row1 = lax.broadcasted_iota(jnp.int32, (nr, 1), 0)
    blk = row1 // nq
    qpos = (step * WA_BLOCKS_PER_STEP + blk) * WA_BLOCK + (rows & (WA_BLOCK - 1))
    kstart = starts[-1]
    for bb in range(WA_BLOCKS_PER_STEP - 2, -1, -1):
        kstart = jnp.where(blk == bb, starts[bb], kstart)
    valid = (cols >= WA_KEYS) | (jnp.abs(qpos - (kstart + cols)) <= WA_WINDOW)
    s = jnp.where(valid, s_ref[...], NEG_INF)
    head = (row1 // WA_BLOCK) % WA_HEADS
    sink = jnp.where(head == 0, sink_ref[0],
                     jnp.where(head == 1, sink_ref[1], jnp.where(head == 2, sink_ref[2], sink_ref[3])))
    m = jnp.maximum(jnp.max(s, axis=1, keepdims=True), sink)
    p = jnp.exp(s - m)
    inv = 1.0 / (jnp.sum(p, axis=1, keepdims=True) + jnp.exp(sink - m))
    p_ref[...] = p.astype(BF16)

    for bb in range(WA_BLOCKS_PER_STEP):
        vall = jnp.concatenate([v_ref[0, pl.ds(starts[bb], WA_KEYS), :], vc], axis=0)
        rs = slice(bb * nq, (bb + 1) * nq)
        o = jnp.dot(p_ref[rs, :], vall, preferred_element_type=F32) * inv[rs]
        o0, o1, o2, o3 = (o[h * WA_BLOCK:(h + 1) * WA_BLOCK] for h in range(WA_HEADS))
        qs = slice(bb * WA_BLOCK, (bb + 1) * WA_BLOCK)
        o_ref[0, qs, 0:LANES] = jnp.where(first_half, o0, swap(o1)).astype(o_ref.dtype)
        o_ref[0, qs, LANES:] = jnp.where(first_half, swap(o2), o3).astype(o_ref.dtype)


def _attn_a(sink, qkv, qkv_c):
    b = qkv.shape[0]
    nk = WA_KEYS + CTX_LEN
    tq = WA_BLOCKS_PER_STEP * WA_BLOCK
    nr = WA_BLOCKS_PER_STEP * WA_HEADS * WA_BLOCK
    return pl.pallas_call(
        _attn_a_kernel,
        out_shape=jax.ShapeDtypeStruct((b, SEQ, WA_HEADS * HEAD_DIM), BF16),
        scratch_shapes=[pltpu.VMEM((nr, nk), F32), pltpu.VMEM((nr, nk), BF16)],
        grid=(b, SEQ // tq),
        in_specs=[
            pl.BlockSpec(memory_space=pltpu.SMEM),
            pl.BlockSpec((1, tq, 2 * LANES), lambda i, j: (i, j, 0)),
            pl.BlockSpec((1, SEQ, LANES), lambda i, j: (i, 0, 2)),
            pl.BlockSpec((1, SEQ, LANES), lambda i, j: (i, 0, 3)),
            pl.BlockSpec((1, CTX_LEN, LANES), lambda i, j: (i, 0, 2)),
            pl.BlockSpec((1, CTX_LEN, LANES), lambda i, j: (i, 0, 3)),
        ],
        out_specs=pl.BlockSpec((1, tq, 2 * LANES), lambda i, j: (i, j, 0)),
        compiler_params=_cparams(("parallel", "arbitrary")),
        name="attn_window",
    )(sink, qkv, qkv, qkv, qkv_c, qkv_c)


NB_ROWS_PER_STEP = 16
NB_LOC = NA_KH * GRID_W
NB_DY_PAIRS = 2 * NA_KH - 2


def _attn_b_kernel(q_ref, k_ref, v_ref, kc_ref, vc_ref, t_ref, o_ref, s_ref, p_ref):
    i = pl.program_id(1)
    kc = kc_ref[0]
    vc = vc_ref[0]
    units = [(rr, pp) for rr in range(NB_ROWS_PER_STEP) for pp in range(NA_HEADS // 2)]
    starts, shifts = [], []
    for rr in range(NB_ROWS_PER_STEP):
        r = i * NB_ROWS_PER_STEP + rr
        rs = jnp.clip(r - NA_KH // 2, 0, GRID_ROWS - NA_KH)
        shifts.append(r - rs)
        starts.append(pl.multiple_of(rs * GRID_W, GRID_W))
    first_half = lax.broadcasted_iota(jnp.int32, (GRID_W, LANES), 1) < HEAD_DIM
    zero = jnp.zeros((), BF16)
    pw = 2 * GRID_W

    for u, (rr, pp) in enumerate(units):
        ls = slice(pp * LANES, (pp + 1) * LANES)
        if pp == 0:
            kw = k_ref[0, pl.ds(starts[rr], NB_LOC), :]
            q = q_ref[0, rr * GRID_W:(rr + 1) * GRID_W, :] * ATT_SCALE
        qs = q[:, ls]
        lhs = jnp.concatenate([jnp.where(first_half, qs, zero), jnp.where(first_half, zero, qs)], axis=0)
        bias = jnp.concatenate(
            [jnp.concatenate([t_ref[h, 2 * k - shifts[rr] + NA_KH - 1] for k in range(NA_KH // 2)], axis=1)
             for h in (2 * pp, 2 * pp + 1)], axis=0)
        s_ref[u * pw:(u + 1) * pw, 0:NB_LOC] = _bdot_nt(lhs, kw[:, ls]) + bias
        s_ref[u * pw:(u + 1) * pw, NB_LOC:] = _bdot_nt(lhs, kc[:, ls])

    s = s_ref[...]
    p_ref[...] = jnp.exp(s - jnp.max(s, axis=1, keepdims=True)).astype(BF16)

    ones_loc = jnp.ones((NB_LOC, LANES), BF16)
    ones_ctx = jnp.ones((CTX_LEN, LANES), BF16)
    vc_ext = [jnp.concatenate([vc[:, pp * LANES:(pp + 1) * LANES], ones_ctx], axis=1)
              for pp in range(NA_HEADS // 2)]
    for u, (rr, pp) in enumerate(units):
        ls = slice(pp * LANES, (pp + 1) * LANES)
        if pp == 0:
            vw = v_ref[0, pl.ds(starts[rr], NB_LOC), :]
        rows = slice(u * pw, (u + 1) * pw)
        pv = (jnp.dot(p_ref[rows, 0:NB_LOC], jnp.concatenate([vw[:, ls], ones_loc], axis=1),
                      preferred_element_type=F32)
              + jnp.dot(p_ref[rows, NB_LOC:], vc_ext[pp], preferred_element_type=F32))
        o = pv[:, 0:LANES] / pv[:, LANES:]
        o_ref[0, rr * GRID_W:(rr + 1) * GRID_W, ls] = jnp.where(
            first_half, o[:GRID_W], o[GRID_W:]).astype(o_ref.dtype)


def _attn_b(qkv, qkv_c, table):
    b = qkv.shape[0]
    tq = NB_ROWS_PER_STEP * GRID_W
    w = NA_HEADS * HEAD_DIM
    nu = NB_ROWS_PER_STEP * NA_HEADS * GRID_W
    return pl.pallas_call(
        _attn_b_kernel,
        out_shape=jax.ShapeDtypeStruct((b, SEQ, w), BF16),
        scratch_shapes=[pltpu.VMEM((nu, NB_LOC + CTX_LEN), F32), pltpu.VMEM((nu, NB_LOC + CTX_LEN), BF16)],
        grid=(b, SEQ // tq),
        in_specs=[
            pl.BlockSpec((1, tq, w), lambda i, j: (i, j, 0)),
            pl.BlockSpec((1, SEQ, w), lambda i, j: (i, 0, 1)),
            pl.BlockSpec((1, SEQ, w), lambda i, j: (i, 0, 2)),
            pl.BlockSpec((1, CTX_LEN, w), lambda i, j: (i, 0, 1)),
            pl.BlockSpec((1, CTX_LEN, w), lambda i, j: (i, 0, 2)),
            pl.BlockSpec((NA_HEADS, NB_DY_PAIRS, GRID_W, LANES), lambda i, j: (0, 0, 0, 0)),
        ],
        out_specs=pl.BlockSpec((1, tq, w), lambda i, j: (i, j, 0)),
        compiler_params=_cparams(("parallel", "arbitrary")),
        name="attn_neighbourhood",
    )(qkv, qkv, qkv, qkv_c, qkv_c, table)


def _split3(a):
    a1 = a.astype(BF16)
    r1 = a - a1.astype(F32)
    a2 = r1.astype(BF16)
    a3 = (r1 - a2.astype(F32)).astype(BF16)
    return a1, a2, a3


def _bias_kernel(r_ref, oh_ref, o_ref):
    oh = oh_ref[...]
    o_ref[...] = sum(jnp.dot(t, oh, preferred_element_type=F32) for t in _split3(r_ref[...]))


def _na_bias_tables(rpb):
    ndy, ndx = 2 * NA_KH - 1, 2 * NA_KW - 1
    qc = np.arange(GRID_W)[:, None]
    x = np.arange(GRID_W)[None, :]
    dx = np.clip(x - qc, -(NA_KW - 1), NA_KW - 1) + NA_KW - 1
    onehot = (np.arange(LANES)[:, None, None] == dx[None]).reshape(LANES, GRID_W * GRID_W)
    cstart = np.clip(qc - NA_KW // 2, 0, GRID_W - NA_KW)
    inside = (x >= cstart) & (x < cstart + NA_KW)
    rows = DEPTH * NA_HEADS * ndy
    r = jnp.pad(rpb.astype(F32).reshape(rows, ndx), ((0, LANES - rows), (0, LANES - ndx)))
    m = pl.pallas_call(
        _bias_kernel,
        out_shape=jax.ShapeDtypeStruct((LANES, GRID_W * GRID_W), F32),
        name="na_bias_expand",
    )(r, jnp.asarray(onehot, BF16))
    m = m[:rows].reshape(DEPTH, NA_HEADS, ndy, GRID_W, GRID_W)
    m = jnp.where(jnp.asarray(inside), m, NEG_INF)
    return jnp.concatenate([m[:, :, :ndy - 1], m[:, :, 1:]], axis=-1)


def _ctx_attn_kernel(sink_ref, a_ref, b_ref, oa_ref, ob_ref):
    a = a_ref[0]
    row1 = lax.broadcasted_iota(jnp.int32, (2 * CTX_LEN, 1), 0)
    for g in range(WA_KV_HEADS):
        h0, h1 = 2 * g, 2 * g + 1
        q2 = jnp.concatenate([a[:, h0 * HEAD_DIM:(h0 + 1) * HEAD_DIM],
                              a[:, h1 * HEAD_DIM:(h1 + 1) * HEAD_DIM]], axis=0)
        k = a[:, 2 * LANES + g * HEAD_DIM:2 * LANES + (g + 1) * HEAD_DIM]
        v = a[:, 3 * LANES + g * HEAD_DIM:3 * LANES + (g + 1) * HEAD_DIM]
        s = _bdot_nt(q2, k) * ATT_SCALE
        sink = jnp.where(row1 < CTX_LEN, sink_ref[h0], sink_ref[h1])
        m = jnp.maximum(jnp.max(s, axis=1, keepdims=True), sink)
        p = jnp.exp(s - m)
        den = jnp.sum(p, axis=1, keepdims=True) + jnp.exp(sink - m)
        o = _bdot(p, v) / den
        oa_ref[0, :, h0 * HEAD_DIM:(h0 + 1) * HEAD_DIM] = o[:CTX_LEN]
        oa_ref[0, :, h1 * HEAD_DIM:(h1 + 1) * HEAD_DIM] = o[CTX_LEN:]
    bq = b_ref[0]
    w = NA_HEADS * HEAD_DIM
    for h in range(NA_HEADS):
        hs = slice(h * HEAD_DIM, (h + 1) * HEAD_DIM)
        s = _bdot_nt(bq[:, hs], bq[:, w + h * HEAD_DIM:w + (h + 1) * HEAD_DIM]) * ATT_SCALE
        m = jnp.max(s, axis=1, keepdims=True)
        p = jnp.exp(s - m)
        den = jnp.sum(p, axis=1, keepdims=True)
        ob_ref[0, :, hs] = _bdot(p, bq[:, 2 * w + h * HEAD_DIM:2 * w + (h + 1) * HEAD_DIM]) / den


def _ctx_attn(sink, qkv_a_c, qkv_b_c):
    b = qkv_a_c.shape[0]
    w = 2 * LANES
    return pl.pallas_call(
        _ctx_attn_kernel,
        out_shape=[jax.ShapeDtypeStruct((b, CTX_LEN, w), F32)] * 2,
        grid=(b,),
        in_specs=[
            pl.BlockSpec(memory_space=pltpu.SMEM),
            pl.BlockSpec((1, CTX_LEN, A_COLS), lambda i: (i, 0, 0)),
            pl.BlockSpec((1, CTX_LEN, B_COLS), lambda i: (i, 0, 0)),
        ],
        out_specs=[pl.BlockSpec((1, CTX_LEN, w), lambda i: (i, 0, 0))] * 2,
        compiler_params=_cparams(("parallel",)),
        name="attn_context",
    )(sink, qkv_a_c, qkv_b_c)


CONV_HALO = SUBLANES


def _conv_kernel(prev_ref, cur_ref, next_ref, w_ref, b_ref, o_ref, ext_ref, *, tl, nt):
    j = pl.program_id(1)
    ext_ref[0:CONV_HALO, :] = jnp.where(j > 0, prev_ref[0], 0.0)
    ext_ref[CONV_HALO:CONV_HALO + tl, :] = cur_ref[0]
    ext_ref[CONV_HALO + tl:, :] = jnp.where(j < nt - 1, next_ref[0], 0.0)
    acc = jnp.zeros((tl, XBC_COLS), F32) + b_ref[...]
    base = CONV_HALO - SSM_CONV // 2
    for k in range(SSM_CONV):
        acc = acc + w_ref[k:k + 1, :] * ext_ref[base + k:base + k + tl, :]
    o_ref[0] = _silu(acc)


def _conv_silu(xz, conv_w, conv_b, tl):
    b, t, _ = xz.shape
    nt = t // tl
    hb = tl // CONV_HALO
    last = t // CONV_HALO - 1
    return pl.pallas_call(
        functools.partial(_conv_kernel, tl=tl, nt=nt),
        out_shape=jax.ShapeDtypeStruct((b, t, XBC_COLS), F32),
        grid=(b, nt),
        in_specs=[
            pl.BlockSpec((1, CONV_HALO, XBC_COLS), lambda i, j: (i, jnp.maximum(j * hb - 1, 0), 0)),
            pl.BlockSpec((1, tl, XBC_COLS), lambda i, j: (i, j, 0)),
            pl.BlockSpec((1, CONV_HALO, XBC_COLS), lambda i, j: (i, jnp.minimum((j + 1) * hb, last), 0)),
            pl.BlockSpec((SUBLANES, XBC_COLS), lambda i, j: (0, 0)),
            pl.BlockSpec((1, XBC_COLS), lambda i, j: (0, 0)),
        ],
        out_specs=pl.BlockSpec((1, tl, XBC_COLS), lambda i, j: (i, j, 0)),
        scratch_shapes=[pltpu.VMEM((tl + 2 * CONV_HALO, XBC_COLS), F32)],
        compiler_params=_cparams(("parallel", "parallel")),
        name="ssm_conv",
    )(xz, xz, xz, conv_w, conv_b)


Q = SSM_CHUNK
GS = SSM_GROUPS * SSM_STATE
HPG = SSM_HEADS // SSM_GROUPS
SSD_STATE_SHAPE = (SSM_GROUPS, SSM_STATE, HPG * SSM_HEAD_DIM)
SSD_BATCH_PER_STEP = 4


def _softplus(v):
    return jnp.maximum(v, 0.0) + jnp.log1p(jnp.exp(-jnp.abs(v)))


def _cumsum_mat(tri, a):
    r = jnp.dot(tri, jnp.concatenate(_split3(a), axis=1), preferred_element_type=F32)
    n = a.shape[1]
    return r[:, 0:n] + r[:, n:2 * n] + r[:, 2 * n:]


def _ssd_kernel(xm_ref, dtm_ref, xb_ref, dtb_ref, bias_ref, alog_ref, h0f_ref, h0b_ref,
                y1_ref, y2_ref, hf_out_ref, hb_out_ref, hf_ref, hb_ref, *, nc):
    i = pl.program_id(1)

    @pl.when(i == 0)
    def _():
        hf_ref[...] = h0f_ref[...]
        hb_ref[...] = h0b_ref[...]

    ii = lax.broadcasted_iota(jnp.int32, (Q, Q), 0)
    jj = lax.broadcasted_iota(jnp.int32, (Q, Q), 1)
    lower = ii >= jj
    diag = ii == jj
    tril =jnp.where(lower, 1.0, 0.0).astype(BF16)
    triu = jnp.where(ii <= jj, 1.0, 0.0).astype(BF16)
    first_half = lax.broadcasted_iota(jnp.int32, (Q, LANES), 1) < SSM_HEAD_DIM
    a_row = -jnp.exp(alog_ref[...])
    gw = HPG * SSM_HEAD_DIM

    def group_operands(xbc, g):
        bg = xbc[:, SSM_INNER + g * SSM_STATE:SSM_INNER + (g + 1) * SSM_STATE]
        cg = xbc[:, SSM_INNER + GS + g * SSM_STATE:SSM_INNER + GS + (g + 1) * SSM_STATE]
        return bg, cg.astype(BF16), xbc[:, g * gw:(g + 1) * gw]

    nbs = range(SSD_BATCH_PER_STEP)
    groups = range(SSM_GROUPS)
    bias = bias_ref[...]
    zero = jnp.zeros((), F32)


    xm = [xm_ref[bi] for bi in nbs]
    xb = [xb_ref[bi] for bi in nbs]
    dtm = [_softplus(dtm_ref[bi] + bias) for bi in nbs]
    dtb = [_softplus(dtb_ref[bi] + bias) for bi in nbs]
    cumf =[_cumsum_mat(tril, d * a_row) for d in dtm]
    cumr = [_cumsum_mat(triu, d * a_row) for d in dtm]
    cumb = [_cumsum_mat(triu, d * a_row) for d in dtb]

    opm = [[group_operands(xm[bi], g) for g in groups] for bi in nbs]
    opb = [[group_operands(xb[bi], g) for g in groups] for bi in nbs]
    gmat = [[_bdot_nt(opm[bi][g][1], opm[bi][g][0]) for g in groups] for bi in nbs]
    hf_prev = [[hf_ref[bi, g] for g in groups] for bi in nbs]
    hb_prev = [[hb_ref[bi, g] for g in groups] for bi in nbs]
    inter_f = [[jnp.dot(opm[bi][g][1], hf_prev[bi][g].astype(BF16), preferred_element_type=F32)
                for g in groups] for bi in nbs]
    inter_b = [[jnp.dot(opb[bi][g][1], hb_prev[bi][g].astype(BF16), preferred_element_type=F32)
                for g in groups] for bi in nbs]
    bt_m = [[opm[bi][g][0].T for g in groups] for bi in nbs]
    bt_b = [[opb[bi][g][0].T for g in groups] for bi in nbs]

    cumf_t = [c.T for c in cumf]
    cumr_t = [c.T for c in cumr]
    dtm_t = [d.T for d in dtm]
    cumb_t = [c.T for c in cumb]
    dtb_t = [d.T for d in dtb]

    sel_r = lax.broadcasted_iota(jnp.int32, (2 * LANES, SSM_INNER), 0) % LANES
    sel_c = lax.broadcasted_iota(jnp.int32, (2 * LANES, SSM_INNER), 1) // SSM_HEAD_DIM
    sel_f = jnp.where(sel_r == sel_c, 1.0, 0.0).astype(BF16)
    sel_b = jnp.where(sel_r == sel_c + SSM_HEADS, 1.0, 0.0).astype(BF16)

    def spread(e, sel):
        hi = e.astype(BF16)
        lo = (e - hi.astype(F32)).astype(BF16)
        return jnp.dot(jnp.concatenate([hi, lo], axis=1), sel, preferred_element_type=F32)

    sl_in_f = [spread(jnp.exp(c), sel_f) for c in cumf]
    sl_in_b = [spread(jnp.exp(c), sel_b) for c in cumb]

    heads = [(bi, h) for bi in nbs for h in range(SSM_HEADS)]
    hb_ = lambda h: SSM_HEADS + h
    cf, cr, wm, wst_f, wst_b = {}, {}, {}, {}, {}
    for t in range(len(heads) + 1):
        if t < len(heads):
            bi, h = k = heads[t]
            cf[k] = jnp.broadcast_to(cumf[bi][:, h:h + 1], (Q, Q))
            cr[k] = jnp.broadcast_to(cumr[bi][:, hb_(h):hb_(h) + 1], (Q, Q))
        if t >= 1:
            bi, h = k = heads[t - 1]
            rf = cumf_t[bi][h:h + 1, :]
            rb = cumb_t[bi][hb_(h):hb_(h) + 1, :]
            dtf_row = dtm_t[bi][h:h + 1, :]
            dtb_row = dtm_t[bi][hb_(h):hb_(h) + 1, :]
            arg = jnp.where(lower, cf[k] - rf, cr[k] - cumr_t[bi][hb_(h):hb_(h) + 1, :])
            wm[k] = (gmat[bi][h // HPG] * (jnp.exp(arg) * jnp.where(lower, dtf_row, dtb_row)
                                           + jnp.where(diag, dtb_row, zero))).astype(BF16)
            w_f = jnp.exp(rf[:, Q - 1:Q] - rf) * dtf_row
            w_b = jnp.exp(rb[:, 0:1] - rb) * dtb_t[bi][hb_(h):hb_(h) + 1, :]
            wst_f[k] = (bt_m[bi][h // HPG] * w_f).astype(BF16)
            wst_b[k] = (bt_b[bi][h // HPG] * w_b).astype(BF16)

    def pair_rhs(x, pr):
        xp = x[:, pr * LANES:(pr + 1) * LANES]
        return jnp.concatenate([jnp.where(first_half, xp, zero), jnp.where(first_half, zero, xp)],
                               axis=0).astype(BF16)

    def pair_dot(mats, bi, pr, rhs):
        lhs = jnp.concatenate([mats[bi, 2 * pr], mats[bi, 2 * pr + 1]], axis=1)
        return jnp.dot(lhs, rhs, preferred_element_type=F32)

    pairs = [(bi, pr) for bi in nbs for pr in range(SSM_HEADS // 2)]
    rhs_m = {k: pair_rhs(xm[k[0]], k[1]) for k in pairs}
    rhs_b = {k: pair_rhs(xb[k[0]], k[1]) for k in pairs}
    y_intra = {k: pair_dot(wm, k[0], k[1], rhs_m[k]) for k in pairs}
    st_f = {k: pair_dot(wst_f, k[0], k[1], rhs_m[k]) for k in pairs}
    st_b = {k: pair_dot(wst_b, k[0], k[1], rhs_b[k]) for k in pairs}

    ppg = HPG // 2
    for bi, pr in pairs:
        g, k = pr // ppg, pr % ppg
        ls = slice(k * LANES, (k + 1) * LANES)
        sin_f = sl_in_f[bi][:, pr * LANES:(pr + 1) * LANES]
        sin_b = sl_in_b[bi][:, pr * LANES:(pr + 1) * LANES]
        y1_ref[bi, :, pr * LANES:(pr + 1) * LANES] = y_intra[bi, pr] + sin_f * inter_f[bi][g][:, ls]
        y2_ref[bi, :, pr * LANES:(pr + 1) * LANES] = sin_b * inter_b[bi][g][:, ls]
        hf_ref[bi, g, :, ls] = hf_prev[bi][g][:, ls] * sin_f[Q - 1:Q, :] + st_f[bi, pr]
        hb_ref[bi, g, :, ls] = hb_prev[bi][g][:, ls] * sin_b[0:1, :] + st_b[bi, pr]

    @pl.when(i == nc - 1)
    def _():
        hf_out_ref[...] = hf_ref[...]
        hb_out_ref[...] = hb_ref[...]


def _ssd(xbc, dt_raw, dt_bias, a_log, h0f, h0b):
    b, t, _ = xbc.shape
    nc = t // Q
    nb = SSD_BATCH_PER_STEP
    st_shape = (b,) + SSD_STATE_SHAPE
    st_spec = pl.BlockSpec((nb,) + SSD_STATE_SHAPE, lambda i, j: (i, 0, 0, 0))
    fwd = lambda i, j: (i, j, 0)
    bwd = lambda i, j: (i, nc - 1 - j, 0)
    return pl.pallas_call(
        functools.partial(_ssd_kernel, nc=nc),
        out_shape=[jax.ShapeDtypeStruct((b, t, SSM_INNER), F32)] * 2
        + [jax.ShapeDtypeStruct(st_shape, F32)] * 2,
        grid=(b // nb, nc),
        in_specs=[
            pl.BlockSpec((nb, Q, XBC_COLS), fwd),
            pl.BlockSpec((nb, Q, DT_PAD), fwd),
            pl.BlockSpec((nb, Q, XBC_COLS), bwd),
            pl.BlockSpec((nb, Q, DT_PAD), bwd),
            pl.BlockSpec((1, DT_PAD), lambda i, j: (0, 0)),
            pl.BlockSpec((1, DT_PAD), lambda i, j: (0, 0)),
            st_spec, st_spec,
        ],
        out_specs=[pl.BlockSpec((nb, Q, SSM_INNER), fwd), pl.BlockSpec((nb, Q, SSM_INNER), bwd),
                   st_spec, st_spec],
        scratch_shapes=[pltpu.VMEM((nb,) + SSD_STATE_SHAPE, F32)] * 2,
        compiler_params=_cparams(("parallel", "arbitrary")),
        name="ssd_scan",
    )(xbc, dt_raw, xbc, dt_raw, dt_bias, a_log, h0f, h0b)


MXU_TILE = 256
FFN_SPLITS = (0, 6 * MXU_TILE, D_FF)


def _mix_ffn_kernel(x_ref, mod_ref, oa_ref, ob_ref, y1_ref, y2_ref, xs_ref, z_ref, d_ref, sg_ref,
                    wo_ref, g_ref, wfi_ref, wfo_ref, gfin_ref, o_ref, *, final):
    y = y1_ref[0] + y2_ref[0] + d_ref[...] * xs_ref[0]
    y = y * _silu(z_ref[0])
    var = jnp.mean(y * y, axis=-1, keepdims=True)
    oc = (y * lax.rsqrt(var + EPS) * sg_ref[...]).astype(BF16)
    wa = 2 * LANES
    mix = (jnp.dot(oa_ref[0].astype(BF16), wo_ref[0, 0:wa, :], preferred_element_type=F32)
           + jnp.dot(ob_ref[0].astype(BF16), wo_ref[0, wa:2 * wa, :], preferred_element_type=F32)
           + jnp.dot(oc, wo_ref[0, 2 * wa:, :], preferred_element_type=F32))
    xn = x_ref[0] + mod_ref[0, 2:3, :] * mix
    h = _norm_mod(xn, g_ref[...], mod_ref[0, 3:4, :], mod_ref[0, 4:5, :]).astype(BF16)
    acc = None
    for lo, hi in zip(FFN_SPLITS[:-1], FFN_SPLITS[1:]):
        gate = jnp.dot(h, wfi_ref[0, :, lo:hi], preferred_element_type=F32)
        up = jnp.dot(h, wfi_ref[0, :, D_FF + lo:D_FF + hi], preferred_element_type=F32)
        act = (_silu(gate) * up).astype(BF16)
        part = jnp.dot(act, wfo_ref[0, lo:hi, :], preferred_element_type=F32)
        acc = part if acc is None else acc + part
    out = xn + mod_ref[0, 5:6, :] * acc
    if final:
        var = jnp.mean(out * out, axis=-1, keepdims=True)
        out = out * lax.rsqrt(var + EPS) * gfin_ref[...]
    o_ref[0] = out


def _mix_ffn(x, mods, o_a, o_b, y1, y2, xbc, xz, d_exp, ssm_g, w_out, g_ffn, w_ffn_in, w_ffn_out,
             g_final, layer, tm, final):
    b, t, _ = x.shape
    per_batch = mods.shape[0] > 1
    row = lambda n: pl.BlockSpec((1, tm, n), lambda i, j: (i, j, 0))
    const = lambda shape: pl.BlockSpec(shape, lambda i, j: (0, 0), pipeline_mode=pl.Buffered(1))
    weight = lambda r, c: pl.BlockSpec((1, r, c), lambda i, j: (layer, 0, 0), pipeline_mode=pl.Buffered(1))
    return pl.pallas_call(
        functools.partial(_mix_ffn_kernel, final=final),
        out_shape=jax.ShapeDtypeStruct((b, t, D_MODEL), F32),
        grid=(b, t // tm),
        in_specs=[
            row(D_MODEL),
            pl.BlockSpec((1, SUBLANES, D_MODEL), (lambda i, j: (i, 0, 0)) if per_batch else (lambda i, j: (0, 0, 0))),
            row(2 * LANES), row(2 * LANES), row(SSM_INNER), row(SSM_INNER),
            pl.BlockSpec((1, tm, SSM_INNER), lambda i, j: (i, j, 0)),
            pl.BlockSpec((1, tm, SSM_INNER), lambda i, j: (i, j, 2)),
            const((1, SSM_INNER)), const((1, SSM_INNER)),
            weight(D_MODEL, D_MODEL),
            const((1, D_MODEL)),
            weight(D_MODEL, 2 * D_FF),
            weight(D_FF, D_MODEL),
            const((1, D_MODEL)),
        ],
        out_specs=row(D_MODEL),
        compiler_params=_cparams(("parallel", "parallel")),
        name="mix_ffn_final" if final else "mix_ffn",
    )(x, mods, o_a, o_b, y1, y2, xbc, xz, d_exp, ssm_g, w_out, g_ffn, w_ffn_in, w_ffn_out, g_final)


def _rope_tables():
    t = np.arange(SEQ)
    pos = np.stack([t // GRID_W, t % GRID_W], axis=1).astype(np.float64)
    quarter = HEAD_DIM // 4
    inv_freq = ROPE_BASE ** (-np.arange(quarter, dtype=np.float64) / quarter)
    lane = np.arange(LANES) % HEAD_DIM
    half = lane // (HEAD_DIM // 2)
    idx = lane % (HEAD_DIM // 2)
    ang = pos[:, half] * inv_freq[idx % quarter][None, :]
    cos, sin = np.cos(ang), np.sin(ang)
    first = (idx < quarter)[None, :]
    tabs = (cos, np.where(first, -sin, 0.0), np.where(first, 0.0, sin))
    return tuple(jnp.asarray(v, F32) for v in tabs)


def _pad_lanes(v, n=LANES):
    v = v.reshape(1, -1)
    return jnp.pad(v, ((0, 0), (0, n - v.shape[1])))


def kernel(x, c, ctx, c_ctx, w_mod, b_mod, g_mix, w_in, wa_sink, na_rpb, ssm_conv_w, ssm_conv_b,
           ssm_dt_bias, ssm_a_log, ssm_d, ssm_norm_g, w_out, g_ffn, w_ffn_in, w_ffn_out, g_final):
    cin = jnp.concatenate([c, c_ctx[None, :], jnp.zeros((SUBLANES - BATCH - 1, D_MODEL), F32)], axis=0)
    mod_all = _modulation(cin, w_mod, b_mod)
    rope_tabs = _rope_tables()
    bias_tabs = _na_bias_tables(na_rpb)
    zeros_state = jnp.zeros((BATCH,) + SSD_STATE_SHAPE, F32)
    gfin = g_final.reshape(1, D_MODEL)
    w_proj = w_in.astype(BF16)
    wo = w_out.astype(BF16)
    wfi = w_ffn_in.astype(BF16)
    wfo = w_ffn_out.astype(BF16)

    xl, xc = x, ctx
    for l in range(DEPTH):
        last = l == DEPTH - 1
        m6 = mod_all[l].reshape(SUBLANES, 6, D_MODEL)
        mods_l = jnp.pad(m6[:BATCH], ((0, 0), (0, 2), (0, 0)))
        mods_c = jnp.pad(m6[BATCH:BATCH + 1], ((0, 0), (0, 2), (0, 0)))
        w_dt = jnp.pad(w_in[l][:, W_DT:], ((0, 0), (0, DT_PAD - DT_COLS))).astype(BF16)
        g1 = g_mix[l].reshape(1, D_MODEL)
        g2 = g_ffn[l].reshape(1, D_MODEL)
        conv_w = jnp.pad(ssm_conv_w[l], ((0, SUBLANES - SSM_CONV), (0, 0)))
        conv_b = ssm_conv_b[l].reshape(1, XBC_COLS)
        dt_bias = _pad_lanes(ssm_dt_bias[l])
        a_log = _pad_lanes(ssm_a_log[l])
        d_exp = jnp.repeat(ssm_d[l], SSM_HEAD_DIM).reshape(1, SSM_INNER)
        sg = ssm_norm_g[l].reshape(1, SSM_INNER)

        a_c, b_c, xz_c, dt_c = _inproj(xc, mods_c, g1, w_proj, l, w_dt, None, CTX_LEN)
        a_l, b_l, xz_l, dt_l = _inproj(xl, mods_l, g1, w_proj, l, w_dt, rope_tabs, TM_INPROJ)

        o_a = _attn_a(wa_sink[l], a_l, a_c)
        o_b = _attn_b(b_l, b_c, bias_tabs[l])

        xbc_c = _conv_silu(xz_c, conv_w, conv_b, CTX_LEN)
        xbc_l = _conv_silu(xz_l, conv_w, conv_b, TM_CONV)
        y1_c, y2_c, h_f, h_b = _ssd(xbc_c, dt_c, dt_bias, a_log, zeros_state, zeros_state)
        y1_l, y2_l, _, _ = _ssd(xbc_l, dt_l, dt_bias, a_log, h_f, h_b)

        xl = _mix_ffn(xl, mods_l, o_a, o_b, y1_l, y2_l, xbc_l, xz_l, d_exp, sg, wo, g2, wfi, wfo, gfin,
                      l, TM_MIX, last)
        if not last:
            o_ac, o_bc = _ctx_attn(wa_sink[l], a_c, b_c)
            xc = _mix_ffn(xc, mods_c, o_ac, o_bc, y1_c, y2_c, xbc_c, xz_c, d_exp, sg, wo, g2, wfi, wfo,
                          gfin, l, CTX_LEN, False)
    return xl
```

```python
import functools

import numpy as np
import jax
import jax.numpy as jnp
from jax import lax
from jax.experimental import pallas as pl
from jax.experimental.pallas import tpu as pltpu

F32 = jnp.float32
BF16 = jnp.bfloat16

D_MODEL = 1024
BATCH = 4
SEQ = 4096
DEPTH = 2
GRID_W = 64
GRID_ROWS = SEQ // GRID_W
CTX_LEN = 256
EPS = 1e-6
HEAD_DIM = 64
ROPE_BASE = 10000.0
WA_HEADS = 4
WA_KV_HEADS = 2
WA_WINDOW = 128
WA_BLOCK = 128
NA_HEADS = 4
NA_KH = 8
NA_KW = 16
SSM_HEADS = 8
SSM_HEAD_DIM = 64
SSM_INNER = SSM_HEADS * SSM_HEAD_DIM
SSM_GROUPS = 2
SSM_STATE = 128
SSM_CONV = 7
SSM_CHUNK = 128
D_FF = 2816
XBC_COLS = SSM_INNER + 2 * SSM_GROUPS * SSM_STATE
DT_COLS = 2 * SSM_HEADS

LANES = 128
SUBLANES = 8
V7X_VMEM_BYTES = 64 * 1024 * 1024
VMEM_LIMIT = V7X_VMEM_BYTES - 8 * 1024 * 1024

TM_INPROJ = 1024
TM_CONV = 1024
TM_MIX = 512

A_COLS = 512
B_COLS = 768
XZ_COLS = 1536
DT_PAD = LANES

ATT_SCALE = HEAD_DIM ** -0.5
NEG_INF = float("-inf")
NT_DIMS = (((1,), (1,)), ((), ()))


def _silu(v):
    return v / (1.0 + jnp.exp(-v))


def _bdot(a, b):
    return jnp.dot(a.astype(BF16), b.astype(BF16), preferred_element_type=F32)


def _bdot_nt(a, b):
    return lax.dot_general(a.astype(BF16), b.astype(BF16), NT_DIMS, preferred_element_type=F32)


def _cparams(sem):
    return pltpu.CompilerParams(dimension_semantics=sem, vmem_limit_bytes=VMEM_LIMIT)


MOD_TN = 6 * D_MODEL // 4


def _mod_kernel(c_ref, w_ref, b_ref, o_ref):
    s = _silu(c_ref[...])
    o_ref[0] = _bdot(s, w_ref[0]) + b_ref[0]


def _modulation(cin, w_mod, b_mod):
    n = 6 * D_MODEL
    return pl.pallas_call(
        _mod_kernel,
        out_shape=jax.ShapeDtypeStruct((DEPTH, SUBLANES, n), F32),
        grid=(DEPTH, n // MOD_TN),
        in_specs=[
            pl.BlockSpec((SUBLANES, D_MODEL), lambda l, j: (0, 0)),
            pl.BlockSpec((1, D_MODEL, MOD_TN), lambda l, j: (l, 0, j)),
            pl.BlockSpec((1, 1, MOD_TN), lambda l, j: (l, 0, j)),
        ],
        out_specs=pl.BlockSpec((1, SUBLANES, MOD_TN), lambda l, j: (l, 0, j)),
        compiler_params=_cparams(("parallel", "parallel")),
        name="modulation",
    )(cin, w_mod, b_mod.reshape(DEPTH, 1, n))


def _norm_mod(x, g, shift, scale):
    var = jnp.mean(x * x, axis=-1, keepdims=True)
    h = x * lax.rsqrt(var + EPS) * g
    return h * (1.0 + scale) + shift


W_QA = 0
W_QB = W_QA + WA_HEADS * HEAD_DIM
W_Z = W_QB + NA_HEADS * HEAD_DIM
W_KVA = W_Z + SSM_INNER
W_KVB = W_KVA + 2 * WA_KV_HEADS * HEAD_DIM
W_XBC = W_KVB + 2 * NA_HEADS * HEAD_DIM
W_DT = W_XBC + XBC_COLS
IN_COLS = W_DT + DT_COLS
ROPE_QUARTER = HEAD_DIM // 4


def _inproj_kernel(x_ref, mod_ref, g_ref, w_ref, wdt_ref, *rest, rope):
    if rope:
        cos_ref, sa_ref, sb_ref, oa_ref, ob_ref, oxz_ref, odt_ref = rest
    else:
        oa_ref, ob_ref, oxz_ref, odt_ref = rest
    h = _norm_mod(x_ref[0], g_ref[...], mod_ref[0, 0:1, :], mod_ref[0, 1:2, :]).astype(BF16)

    def proj(lo, hi):
        return jnp.dot(h, w_ref[0, :, lo:hi], preferred_element_type=F32)

    qa = proj(W_QA, W_QB)
    kva = proj(W_KVA, W_KVB)
    if rope:
        cos, sa, sb = cos_ref[...], sa_ref[...], sb_ref[...]

        def rot(v):
            up = pltpu.roll(v, LANES - ROPE_QUARTER, axis=1)
            dn = pltpu.roll(v, ROPE_QUARTER, axis=1)
            return v * cos + up * sa + dn * sb

        oa_ref[0, :, 0:LANES] = rot(qa[:, 0:LANES]).astype(BF16)
        oa_ref[0, :, LANES:2 * LANES] = rot(qa[:, LANES:]).astype(BF16)
        oa_ref[0, :, 2 * LANES:3 * LANES] = rot(kva[:, 0:LANES]).astype(BF16)
        oa_ref[0, :, 3 * LANES:] = kva[:, LANES:].astype(BF16)
    else:
        oa_ref[0, :, 0:2 * LANES] = qa.astype(BF16)
        oa_ref[0, :, 2 * LANES:] = kva.astype(BF16)
    ob_ref[0, :, 0:2 * LANES] = proj(W_QB, W_Z).astype(BF16)
    ob_ref[0, :, 2 * LANES:] = proj(W_KVB, W_XBC).astype(BF16)
    oxz_ref[0, :, 0:XBC_COLS] = proj(W_XBC, W_DT)
    oxz_ref[0, :, XBC_COLS:] = proj(W_Z, W_KVA)
    odt_ref[0] = jnp.dot(h, wdt_ref[...], preferred_element_type=F32)


def _inproj(x, mods, g, w, layer, wdt, rope_tabs, tm):
    b, t, _ = x.shape
    rope = rope_tabs is not None
    per_batch = mods.shape[0] > 1
    in_specs = [
        pl.BlockSpec((1, tm, D_MODEL), lambda i, j: (i, j, 0)),
        pl.BlockSpec((1, SUBLANES, D_MODEL), (lambda i, j: (i, 0, 0)) if per_batch else (lambda i, j: (0, 0, 0))),
        pl.BlockSpec((1, D_MODEL), lambda i, j: (0, 0)),
        pl.BlockSpec((1, D_MODEL, IN_COLS), lambda i, j: (layer, 0, 0)),
        pl.BlockSpec((D_MODEL, DT_PAD), lambda i, j: (0, 0)),
    ]
    args = [x, mods, g, w, wdt]
    if rope:
        in_specs += [pl.BlockSpec((tm, LANES), lambda i, j: (j, 0))] * 3
        args += list(rope_tabs)
    widths = (A_COLS, B_COLS, XZ_COLS, DT_PAD)
    return pl.pallas_call(
        functools.partial(_inproj_kernel, rope=rope),
        out_shape=[jax.ShapeDtypeStruct((b, t, n), dt) for n, dt in zip(widths, (BF16, BF16, F32, F32))],
        grid=(b, t // tm),
        in_specs=in_specs,
        out_specs=[pl.BlockSpec((1, tm, n), lambda i, j: (i, j, 0)) for n in widths],
        compiler_params=_cparams(("parallel", "parallel")),
        name="inproj_rope" if rope else "inproj",
    )(*args)


WA_KEYS = 3 * WA_BLOCK


WA_BLOCKS_PER_STEP = 4


def _attn_a_kernel(sink_ref, q_ref, k_ref, v_ref, kc_ref, vc_ref, o_ref, s_ref, p_ref):
    step = pl.program_id(1)
    nk = WA_KEYS + CTX_LEN
    nq = WA_HEADS * WA_BLOCK
    kc, vc = kc_ref[0], vc_ref[0]
    starts = []
    for bb in range(WA_BLOCKS_PER_STEP):
        n = step * WA_BLOCKS_PER_STEP + bb
        starts.append(pl.multiple_of(jnp.clip((n - 1) * WA_BLOCK, 0, SEQ - WA_KEYS), WA_BLOCK))

    first_half = lax.broadcasted_iota(jnp.int32, (WA_BLOCK, LANES), 1) < HEAD_DIM
    swap = lambda v: pltpu.roll(v, HEAD_DIM, axis=1)

    for bb in range(WA_BLOCKS_PER_STEP):
        q = q_ref[0, bb * WA_BLOCK:(bb + 1) * WA_BLOCK, :].astype(F32) * ATT_SCALE
        q01, q23 = q[:, 0:LANES], q[:, LANES:]
        lhs = jnp.concatenate([
            jnp.where(first_half, q01, 0.0), jnp.where(first_half, swap(q01), 0.0),
            jnp.where(first_half, 0.0, swap(q23)), jnp.where(first_half, 0.0, q23)], axis=0)
        kall = jnp.concatenate([k_ref[0, pl.ds(starts[bb], WA_KEYS), :], kc], axis=0)
        s_ref[bb * nq:(bb + 1) * nq, :] = _bdot_nt(lhs, kall)

    nr = WA_BLOCKS_PER_STEP * nq
    rows = lax.broadcasted_iota(jnp.int32, (nr, nk), 0)
    cols = lax.broadcasted_iota(jnp.int32, (nr, nk), 1)
    row1 = lax.broadcasted_iota(jnp.int32, (nr, 1), 0)
    blk = row1 // nq
    qpos = (step * WA_BLOCKS_PER_STEP + blk) * WA_BLOCK + (rows & (WA_BLOCK - 1))
    kstart = starts[-1]
    for bb in range(WA_BLOCKS_PER_STEP - 2, -1, -1):
        kstart = jnp.where(blk == bb, starts[bb], kstart)
    valid = (cols >= WA_KEYS) | (jnp.abs(qpos - (kstart + cols)) <= WA_WINDOW)
    s = jnp.where(valid, s_ref[...], NEG_INF)
    head = (row1 // WA_BLOCK) % WA_HEADS
    sink = jnp.where(head == 0, sink_ref[0],
                     jnp.where(head == 1, sink_ref[1], jnp.where(head == 2, sink_ref[2], sink_ref[3])))
    m = jnp.maximum(jnp.max(s, axis=1, keepdims=True), sink)
    p = jnp.exp(s - m)
    inv = 1.0 / (jnp.sum(p, axis=1, keepdims=True) + jnp.exp(sink - m))
    p_ref[...] = p.astype(BF16)

    for bb in range(WA_BLOCKS_PER_STEP):
        vall = jnp.concatenate([v_ref[0, pl.ds(starts[bb], WA_KEYS), :], vc], axis=0)
        rs = slice(bb * nq, (bb + 1) * nq)
        o = jnp.dot(p_ref[rs, :], vall, preferred_element_type=F32) * inv[rs]
        o0, o1, o2, o3 = (o[h * WA_BLOCK:(h + 1) * WA_BLOCK] for h in range(WA_HEADS))
        qs = slice(bb * WA_BLOCK, (bb + 1) * WA_BLOCK)
        o_ref[0, qs, 0:LANES] = jnp.where(first_half, o0, swap(o1)).astype(o_ref.dtype)
        o_ref[0, qs, LANES:] = jnp.where(first_half, swap(o2), o3).astype(o_ref.dtype)


def _attn_a(sink, qkv, qkv_c):
    b = qkv.shape[0]
    nk = WA_KEYS + CTX_LEN
    tq = WA_BLOCKS_PER_STEP * WA_BLOCK
    nr = WA_BLOCKS_PER_STEP * WA_HEADS * WA_BLOCK
    return pl.pallas_call(
        _attn_a_kernel,
        out_shape=jax.ShapeDtypeStruct((b, SEQ, WA_HEADS * HEAD_DIM), BF16),
        scratch_shapes=[pltpu.VMEM((nr, nk), F32), pltpu.VMEM((nr, nk), BF16)],
        grid=(b, SEQ // tq),
        in_specs=[
            pl.BlockSpec(memory_space=pltpu.SMEM),
            pl.BlockSpec((1, tq, 2 * LANES), lambda i, j: (i, j, 0)),
            pl.BlockSpec((1, SEQ, LANES), lambda i, j: (i, 0, 2)),
            pl.BlockSpec((1, SEQ, LANES), lambda i, j: (i, 0, 3)),
            pl.BlockSpec((1, CTX_LEN, LANES), lambda i, j: (i, 0, 2)),
            pl.BlockSpec((1, CTX_LEN, LANES), lambda i, j: (i, 0, 3)),
        ],
        out_specs=pl.BlockSpec((1, tq, 2 * LANES), lambda i, j: (i, j, 0)),
        compiler_params=_cparams(("parallel", "arbitrary")),
        name="attn_window",
    )(sink, qkv, qkv, qkv, qkv_c, qkv_c)


NB_ROWS_PER_STEP = 16
NB_LOC = NA_KH * GRID_W
NB_DY_PAIRS = 2 * NA_KH - 2


def _attn_b_kernel(q_ref, k_ref, v_ref, kc_ref, vc_ref, t_ref, o_ref, s_ref, p_ref):
    i = pl.program_id(1)
    kc = kc_ref[0]
    vc = vc_ref[0]
    units = [(rr, pp) for rr in range(NB_ROWS_PER_STEP) for pp in range(NA_HEADS // 2)]
    starts, shifts = [], []
    for rr in range(NB_ROWS_PER_STEP):
        r = i * NB_ROWS_PER_STEP + rr
        rs = jnp.clip(r - NA_KH // 2, 0, GRID_ROWS - NA_KH)
        shifts.append(r - rs)
        starts.append(pl.multiple_of(rs * GRID_W, GRID_W))
    first_half = lax.broadcasted_iota(jnp.int32, (GRID_W, LANES), 1) < HEAD_DIM
    zero = jnp.zeros((), BF16)
    pw = 2 * GRID_W

    for u, (rr, pp) in enumerate(units):
        ls = slice(pp * LANES, (pp + 1) * LANES)
        if pp == 0:
            kw = k_ref[0, pl.ds(starts[rr], NB_LOC), :]
            q = q_ref[0, rr * GRID_W:(rr + 1) * GRID_W, :] * ATT_SCALE
        qs = q[:, ls]
        lhs = jnp.concatenate([jnp.where(first_half, qs, zero), jnp.where(first_half, zero, qs)], axis=0)
        bias = jnp.concatenate(
            [jnp.concatenate([t_ref[h, 2 * k - shifts[rr] + NA_KH - 1] for k in range(NA_KH // 2)], axis=1)
             for h in (2 * pp, 2 * pp + 1)], axis=0)
        s_ref[u * pw:(u + 1) * pw, 0:NB_LOC] = _bdot_nt(lhs, kw[:, ls]) + bias
        s_ref[u * pw:(u + 1) * pw, NB_LOC:] = _bdot_nt(lhs, kc[:, ls])

    s = s_ref[...]
    p_ref[...] = jnp.exp(s - jnp.max(s, axis=1, keepdims=True)).astype(BF16)

    ones_loc = jnp.ones((NB_LOC, LANES), BF16)
    ones_ctx = jnp.ones((CTX_LEN, LANES), BF16)
    vc_ext = [jnp.concatenate([vc[:, pp * LANES:(pp + 1) * LANES], ones_ctx], axis=1)
              for pp in range(NA_HEADS // 2)]
    for u, (rr, pp) in enumerate(units):
        ls = slice(pp * LANES, (pp + 1) * LANES)
        if pp == 0:
            vw = v_ref[0, pl.ds(starts[rr], NB_LOC), :]
        rows = slice(u * pw, (u + 1) * pw)
        pv = (jnp.dot(p_ref[rows, 0:NB_LOC], jnp.concatenate([vw[:, ls], ones_loc], axis=1),
                      preferred_element_type=F32)
              + jnp.dot(p_ref[rows, NB_LOC:], vc_ext[pp], preferred_element_type=F32))
        o = pv[:, 0:LANES] / pv[:, LANES:]
        o_ref[0, rr * GRID_W:(rr + 1) * GRID_W, ls] = jnp.where(
            first_half, o[:GRID_W], o[GRID_W:]).astype(o_ref.dtype)


def _attn_b(qkv, qkv_c, table):
    b = qkv.shape[0]
    tq = NB_ROWS_PER_STEP * GRID_W
    w = NA_HEADS * HEAD_DIM
    nu = NB_ROWS_PER_STEP * NA_HEADS * GRID_W
    return pl.pallas_call(
        _attn_b_kernel,
        out_shape=jax.ShapeDtypeStruct((b, SEQ, w), BF16),
        scratch_shapes=[pltpu.VMEM((nu, NB_LOC + CTX_LEN), F32), pltpu.VMEM((nu, NB_LOC + CTX_LEN), BF16)],
        grid=(b, SEQ // tq),
        in_specs=[
            pl.BlockSpec((1, tq, w), lambda i, j: (i, j, 0)),
            pl.BlockSpec((1, SEQ, w), lambda i, j: (i, 0, 1)),
            pl.BlockSpec((1, SEQ, w), lambda i, j: (i, 0, 2)),
            pl.BlockSpec((1, CTX_LEN, w), lambda i, j: (i, 0, 1)),
            pl.BlockSpec((1, CTX_LEN, w), lambda i, j: (i, 0, 2)),
            pl.BlockSpec((NA_HEADS, NB_DY_PAIRS, GRID_W, LANES), lambda i, j: (0, 0, 0, 0)),
        ],
        out_specs=pl.BlockSpec((1, tq, w), lambda i, j: (i, j, 0)),
        compiler_params=_cparams(("parallel", "arbitrary")),
        name="attn_neighbourhood",
    )(qkv, qkv, qkv, qkv_c, qkv_c, table)


def _split3(a):
    a1 = a.astype(BF16)
    r1 = a - a1.astype(F32)
    a2 = r1.astype(BF16)
    a3 = (r1 - a2.astype(F32)).astype(BF16)
    return a1, a2, a3


def _bias_kernel(r_ref, oh_ref, o_ref):
    oh = oh_ref[...]
    o_ref[...] = sum(jnp.dot(t, oh, preferred_element_type=F32) for t in _split3(r_ref[...]))


def _na_bias_tables(rpb):
    ndy, ndx = 2 * NA_KH - 1, 2 * NA_KW - 1
    qc = np.arange(GRID_W)[:, None]
    x = np.arange(GRID_W)[None, :]
    dx = np.clip(x - qc, -(NA_KW - 1), NA_KW - 1) + NA_KW - 1
    onehot = (np.arange(LANES)[:, None, None] == dx[None]).reshape(LANES, GRID_W * GRID_W)
    cstart = np.clip(qc - NA_KW // 2, 0, GRID_W - NA_KW)
    inside = (x >= cstart) & (x < cstart + NA_KW)
    rows = DEPTH * NA_HEADS * ndy
    r = jnp.pad(rpb.astype(F32).reshape(rows, ndx), ((0, LANES - rows), (0, LANES - ndx)))
    m = pl.pallas_call(
        _bias_kernel,
        out_shape=jax.ShapeDtypeStruct((LANES, GRID_W * GRID_W), F32),
        name="na_bias_expand",
    )(r, jnp.asarray(onehot, BF16))
    m = m[:rows].reshape(DEPTH, NA_HEADS, ndy, GRID_W, GRID_W)
    m = jnp.where(jnp.asarray(inside), m, NEG_INF)
    return jnp.concatenate([m[:, :, :ndy - 1], m[:, :, 1:]], axis=-1)


def _ctx_attn_kernel(sink_ref, a_ref, b_ref, oa_ref, ob_ref):
    a = a_ref[0]
    row1 = lax.broadcasted_iota(jnp.int32, (2 * CTX_LEN, 1), 0)
    for g in range(WA_KV_HEADS):
        h0, h1 = 2 * g, 2 * g + 1
        q2 = jnp.concatenate([a[:, h0 * HEAD_DIM:(h0 + 1) * HEAD_DIM],
                              a[:, h1 * HEAD_DIM:(h1 + 1) * HEAD_DIM]], axis=0)
        k = a[:, 2 * LANES + g * HEAD_DIM:2 * LANES + (g + 1) * HEAD_DIM]
        v = a[:, 3 * LANES + g * HEAD_DIM:3 * LANES + (g + 1) * HEAD_DIM]
        s = _bdot_nt(q2, k) * ATT_SCALE
        sink = jnp.where(row1 < CTX_LEN, sink_ref[h0], sink_ref[h1])
        m = jnp.maximum(jnp.max(s, axis=1, keepdims=True), sink)
        p = jnp.exp(s - m)
        den = jnp.sum(p, axis=1, keepdims=True) + jnp.exp(sink - m)
        o = _bdot(p, v) / den
        oa_ref[0, :, h0 * HEAD_DIM:(h0 + 1) * HEAD_DIM] = o[:CTX_LEN]
        oa_ref[0, :, h1 * HEAD_DIM:(h1 + 1) * HEAD_DIM] = o[CTX_LEN:]
    bq = b_ref[0]
    w = NA_HEADS * HEAD_DIM
    for h in range(NA_HEADS):
        hs = slice(h * HEAD_DIM, (h + 1) * HEAD_DIM)
        s = _bdot_nt(bq[:, hs], bq[:, w + h * HEAD_DIM:w + (h + 1) * HEAD_DIM]) * ATT_SCALE
        m = jnp.max(s, axis=1, keepdims=True)
        p = jnp.exp(s - m)
        den = jnp.sum(p, axis=1, keepdims=True)
        ob_ref[0, :, hs] = _bdot(p, bq[:, 2 * w + h * HEAD_DIM:2 * w + (h + 1) * HEAD_DIM]) / den


def _ctx_attn(sink, qkv_a_c, qkv_b_c):
    b = qkv_a_c.shape[0]
    w = 2 * LANES
    return pl.pallas_call(
        _ctx_attn_kernel,
        out_shape=[jax.ShapeDtypeStruct((b, CTX_LEN, w), F32)] * 2,
        grid=(b,),
        in_specs=[
            pl.BlockSpec(memory_space=pltpu.SMEM),
            pl.BlockSpec((1, CTX_LEN, A_COLS), lambda i: (i, 0, 0)),
            pl.BlockSpec((1, CTX_LEN, B_COLS), lambda i: (i, 0, 0)),
        ],
        out_specs=[pl.BlockSpec((1, CTX_LEN, w), lambda i: (i, 0, 0))] * 2,
        compiler_params=_cparams(("parallel",)),
        name="attn_context",
    )(sink, qkv_a_c, qkv_b_c)


CONV_HALO = SUBLANES
CONV_BLOCK = 128
CONV_MAIN = 112


def _conv_kernel(prev_ref, cur_ref, next_ref, w_ref, b_ref, o_ref, ext_ref, hl_ref, *, tl, nt):
    j = pl.program_id(1)
    ext_ref[0:CONV_HALO, :] = jnp.where(j > 0, prev_ref[0], 0.0)
    ext_ref[CONV_HALO:CONV_HALO + tl, :] = cur_ref[0]
    ext_ref[CONV_HALO + tl:2 * CONV_HALO + tl, :] = jnp.where(j < nt - 1, next_ref[0], 0.0)
    ext_ref[2 * CONV_HALO + tl:, :] = jnp.zeros((CONV_MAIN, XBC_COLS), F32)
    e = ext_ref[...]
    hi = e.astype(BF16)
    hl_ref[0] = hi
    hl_ref[1] = (e - hi.astype(F32)).astype(BF16)

    base = CONV_HALO - SSM_CONV // 2
    centre = SSM_CONV // 2
    ri = lax.broadcasted_iota(jnp.int32, (CONV_MAIN, 2 * CONV_BLOCK), 0)
    cj = lax.broadcasted_iota(jnp.int32, (CONV_MAIN, 2 * CONV_BLOCK), 1) % CONV_BLOCK
    sel = {k: jnp.where(cj == ri + base + k, 1.0, 0.0).astype(BF16) for k in range(SSM_CONV) if k != centre}
    tail = CONV_BLOCK - CONV_MAIN
    taps = list(sel)
    sel_main = jnp.concatenate([sel[k] for k in taps], axis=0)
    sel_tail = jnp.concatenate([sel[k][0:tail] for k in taps], axis=0)
    for r0 in range(0, tl, CONV_BLOCK):
        for o0, rows, w0, smat in ((r0, CONV_MAIN, r0, sel_main), (r0 + CONV_MAIN, tail, r0 + CONV_MAIN, sel_tail)):
            for c0 in range(0, XBC_COLS, MXU_TILE):
                cs = slice(c0, c0 + MXU_TILE)
                win = jnp.concatenate([hl_ref[0, w0:w0 + CONV_BLOCK, cs], hl_ref[1, w0:w0 + CONV_BLOCK, cs]],
                                      axis=0)
                shifted = jnp.dot(smat, win, preferred_element_type=F32)
                acc = b_ref[:, cs] + w_ref[centre:centre + 1, cs] * ext_ref[o0 + CONV_HALO:o0 + CONV_HALO + rows, cs]
                for n, k in enumerate(taps):
                    acc = acc + w_ref[k:k + 1, cs] * shifted[n * rows:(n + 1) * rows]
                o_ref[0, o0:o0 + rows, cs] = _silu(acc)


def _conv_silu(xz, conv_w, conv_b, tl):
    b, t, _ = xz.shape
    nt = t // tl
    hb = tl // CONV_HALO
    last = t // CONV_HALO - 1
    return pl.pallas_call(
        functools.partial(_conv_kernel, tl=tl, nt=nt),
        out_shape=jax.ShapeDtypeStruct((b, t, XBC_COLS), F32),
        grid=(b, nt),
        in_specs=[
            pl.BlockSpec((1, CONV_HALO, XBC_COLS), lambda i, j: (i, jnp.maximum(j * hb - 1, 0), 0)),
            pl.BlockSpec((1, tl, XBC_COLS), lambda i, j: (i, j, 0)),
            pl.BlockSpec((1, CONV_HALO, XBC_COLS), lambda i, j: (i, jnp.minimum((j + 1) * hb, last), 0)),
            pl.BlockSpec((SUBLANES, XBC_COLS), lambda i, j: (0, 0)),
            pl.BlockSpec((1, XBC_COLS), lambda i, j: (0, 0)),
        ],
        out_specs=pl.BlockSpec((1, tl, XBC_COLS), lambda i, j: (i, j, 0)),
        scratch_shapes=[pltpu.VMEM((tl + CONV_BLOCK, XBC_COLS), F32),
                        pltpu.VMEM((2, tl + CONV_BLOCK, XBC_COLS), BF16)],
        compiler_params=_cparams(("parallel", "parallel")),
        name="ssm_conv",
    )(xz, xz, xz, conv_w, conv_b)


Q = SSM_CHUNK
GS = SSM_GROUPS * SSM_STATE
HPG = SSM_HEADS // SSM_GROUPS
SSD_STATE_SHAPE = (SSM_GROUPS, SSM_STATE, HPG * SSM_HEAD_DIM)
SSD_BATCH_PER_STEP = 4


def _softplus(v):
    return jnp.maximum(v, 0.0) + jnp.log1p(jnp.exp(-jnp.abs(v)))


def _cumsum_mat(tri, a):
    r = jnp.dot(tri, jnp.concatenate(_split3(a), axis=1), preferred_element_type=F32)
    n = a.shape[1]
    return r[:, 0:n] + r[:, n:2 * n] + r[:, 2 * n:]


def _ssd_kernel(xm_ref, dtm_ref, xb_ref, dtb_ref, bias_ref, alog_ref, h0f_ref, h0b_ref,
                y1_ref, y2_ref, hf_out_ref, hb_out_ref, hf_ref, hb_ref, *, nc):
    i = pl.program_id(1)

    @pl.when(i == 0)
    def _():
        hf_ref[...] = h0f_ref[...]
        hb_ref[...] = h0b_ref[...]

    ii = lax.broadcasted_iota(jnp.int32, (Q, Q), 0)
    jj = lax.broadcasted_iota(jnp.int32, (Q, Q), 1)
    lower = ii >= jj
    diag = ii == jj
    tril =jnp.where(lower, 1.0, 0.0).astype(BF16)
    triu = jnp.where(ii <= jj, 1.0, 0.0).astype(BF16)
    first_half = lax.broadcasted_iota(jnp.int32, (Q, LANES), 1) < SSM_HEAD_DIM
    a_row = -jnp.exp(alog_ref[...])
    gw = HPG * SSM_HEAD_DIM

    def group_operands(xbc, g):
        bg = xbc[:, SSM_INNER + g * SSM_STATE:SSM_INNER + (g + 1) * SSM_STATE]
        cg = xbc[:, SSM_INNER + GS + g * SSM_STATE:SSM_INNER + GS + (g + 1) * SSM_STATE]
        return bg, cg.astype(BF16), xbc[:, g * gw:(g + 1) * gw]

    nbs = range(SSD_BATCH_PER_STEP)
    groups = range(SSM_GROUPS)
    bias = bias_ref[...]
    zero = jnp.zeros((), F32)


    xm = [xm_ref[bi] for bi in nbs]
    xb = [xb_ref[bi] for bi in nbs]
    dtm = [_softplus(dtm_ref[bi] + bias) for bi in nbs]
    dtb = [_softplus(dtb_ref[bi] + bias) for bi in nbs]
    cumf =[_cumsum_mat(tril, d * a_row) for d in dtm]
    cumr = [_cumsum_mat(triu, d * a_row) for d in dtm]
    cumb = [_cumsum_mat(triu, d * a_row) for d in dtb]

    opm = [[group_operands(xm[bi], g) for g in groups] for bi in nbs]
    opb = [[group_operands(xb[bi], g) for g in groups] for bi in nbs]
    gmat = [[_bdot_nt(opm[bi][g][1], opm[bi][g][0]) for g in groups] for bi in nbs]
    hf_prev = [[hf_ref[bi, g] for g in groups] for bi in nbs]
    hb_prev = [[hb_ref[bi, g] for g in groups] for bi in nbs]
    inter_f = [[jnp.dot(opm[bi][g][1], hf_prev[bi][g].astype(BF16), preferred_element_type=F32)
                for g in groups] for bi in nbs]
    inter_b = [[jnp.dot(opb[bi][g][1], hb_prev[bi][g].astype(BF16), preferred_element_type=F32)
                for g in groups] for bi in nbs]
    bt_m = [[opm[bi][g][0].T for g in groups] for bi in nbs]
    bt_b = [[opb[bi][g][0].T for g in groups] for bi in nbs]

    cumf_t = [c.T for c in cumf]
    cumr_t = [c.T for c in cumr]
    dtm_t = [d.T for d in dtm]
    cumb_t = [c.T for c in cumb]
    dtb_t = [d.T for d in dtb]

    sel_r = lax.broadcasted_iota(jnp.int32, (2 * LANES, SSM_INNER), 0) % LANES
    sel_c = lax.broadcasted_iota(jnp.int32, (2 * LANES, SSM_INNER), 1) // SSM_HEAD_DIM
    sel_f = jnp.where(sel_r == sel_c, 1.0, 0.0).astype(BF16)
    sel_b = jnp.where(sel_r == sel_c + SSM_HEADS, 1.0, 0.0).astype(BF16)

    def spread(e, sel):
        hi = e.astype(BF16)
        lo = (e - hi.astype(F32)).astype(BF16)
        return jnp.dot(jnp.concatenate([hi, lo], axis=1), sel, preferred_element_type=F32)

    sl_in_f = [spread(jnp.exp(c), sel_f) for c in cumf]
    sl_in_b = [spread(jnp.exp(c), sel_b) for c in cumb]

    heads = [(bi, h) for bi in nbs for h in range(SSM_HEADS)]
    hb_ = lambda h: SSM_HEADS + h
    cf, cr, wm, wst_f, wst_b = {}, {}, {}, {}, {}
    for t in range(len(heads) + 1):
        if t < len(heads):
            bi, h = k = heads[t]
            cf[k] = jnp.broadcast_to(cumf[bi][:, h:h + 1], (Q, Q))
            cr[k] = jnp.broadcast_to(cumr[bi][:, hb_(h):hb_(h) + 1], (Q, Q))
        if t >= 1:
            bi, h = k = heads[t - 1]
            rf = cumf_t[bi][h:h + 1, :]
            rb = cumb_t[bi][hb_(h):hb_(h) + 1, :]
            dtf_row = dtm_t[bi][h:h + 1, :]
            dtb_row = dtm_t[bi][hb_(h):hb_(h) + 1, :]
            arg = jnp.where(lower, cf[k] - rf, cr[k] - cumr_t[bi][hb_(h):hb_(h) + 1, :])
            wm[k] = (gmat[bi][h // HPG] * (jnp.exp(arg) * jnp.where(lower, dtf_row, dtb_row)
                                           + jnp.where(diag, dtb_row, zero))).astype(BF16)
            w_f = jnp.exp(rf[:, Q - 1:Q] - rf) * dtf_row
            w_b = jnp.exp(rb[:, 0:1] - rb) * dtb_t[bi][hb_(h):hb_(h) + 1, :]
            wst_f[k] = (bt_m[bi][h // HPG] * w_f).astype(BF16)
            wst_b[k] = (bt_b[bi][h // HPG] * w_b).astype(BF16)

    def pair_rhs(x, pr):
        xp = x[:, pr * LANES:(pr + 1) * LANES]
        return jnp.concatenate([jnp.where(first_half, xp, zero), jnp.where(first_half, zero, xp)],
                               axis=0).astype(BF16)

    def pair_dot(mats, bi, pr, rhs):
        lhs = jnp.concatenate([mats[bi, 2 * pr], mats[bi, 2 * pr + 1]], axis=1)
        return jnp.dot(lhs, rhs, preferred_element_type=F32)

    pairs = [(bi, pr) for bi in nbs for pr in range(SSM_HEADS // 2)]
    rhs_m = {k: pair_rhs(xm[k[0]], k[1]) for k in pairs}
    rhs_b = {k: pair_rhs(xb[k[0]], k[1]) for k in pairs}
    y_intra = {k: pair_dot(wm, k[0], k[1], rhs_m[k]) for k in pairs}
    st_f = {k: pair_dot(wst_f, k[0], k[1], rhs_m[k]) for k in pairs}
    st_b = {k: pair_dot(wst_b, k[0], k[1], rhs_b[k]) for k in pairs}

    ppg = HPG // 2
    for bi, pr in pairs:
        g, k = pr // ppg, pr % ppg
        ls = slice(k * LANES, (k + 1) * LANES)
        sin_f = sl_in_f[bi][:, pr * LANES:(pr + 1) * LANES]
        sin_b = sl_in_b[bi][:, pr * LANES:(pr + 1) * LANES]
        y1_ref[bi, :, pr * LANES:(pr + 1) * LANES] = y_intra[bi, pr] + sin_f * inter_f[bi][g][:, ls]
        y2_ref[bi, :, pr * LANES:(pr + 1) * LANES] = sin_b * inter_b[bi][g][:, ls]
        hf_ref[bi, g, :, ls] = hf_prev[bi][g][:, ls] * sin_f[Q - 1:Q, :] + st_f[bi, pr]
        hb_ref[bi, g, :, ls] = hb_prev[bi][g][:, ls] * sin_b[0:1, :] + st_b[bi, pr]

    @pl.when(i == nc - 1)
    def _():
        hf_out_ref[...] = hf_ref[...]
        hb_out_ref[...] = hb_ref[...]


def _ssd(xbc, dt_raw, dt_bias, a_log, h0f, h0b):
    b, t, _ = xbc.shape
    nc = t // Q
    nb = SSD_BATCH_PER_STEP
    st_shape = (b,) + SSD_STATE_SHAPE
    st_spec = pl.BlockSpec((nb,) + SSD_STATE_SHAPE, lambda i, j: (i, 0, 0, 0))
    fwd = lambda i, j: (i, j, 0)
    bwd = lambda i, j: (i, nc - 1 - j, 0)
    return pl.pallas_call(
        functools.partial(_ssd_kernel, nc=nc),
        out_shape=[jax.ShapeDtypeStruct((b, t, SSM_INNER), F32)] * 2
        + [jax.ShapeDtypeStruct(st_shape, F32)] * 2,
        grid=(b // nb, nc),
        in_specs=[
            pl.BlockSpec((nb, Q, XBC_COLS), fwd),
            pl.BlockSpec((nb, Q, DT_PAD), fwd),
            pl.BlockSpec((nb, Q, XBC_COLS), bwd),
            pl.BlockSpec((nb, Q, DT_PAD), bwd),
            pl.BlockSpec((1, DT_PAD), lambda i, j: (0, 0)),
            pl.BlockSpec((1, DT_PAD), lambda i, j: (0, 0)),
            st_spec, st_spec,
        ],
        out_specs=[pl.BlockSpec((nb, Q, SSM_INNER), fwd), pl.BlockSpec((nb, Q, SSM_INNER), bwd),
                   st_spec, st_spec],
        scratch_shapes=[pltpu.VMEM((nb,) + SSD_STATE_SHAPE, F32)] * 2,
        compiler_params=_cparams(("parallel", "arbitrary")),
        name="ssd_scan",
    )(xbc, dt_raw, xbc, dt_raw, dt_bias, a_log, h0f, h0b)


MXU_TILE = 256
FFN_SPLITS = (0, 6 * MXU_TILE, D_FF)


def _mix_ffn_kernel(x_ref, mod_ref, oa_ref, ob_ref, y1_ref, y2_ref, xs_ref, z_ref, d_ref, sg_ref,
                    wo_ref, g_ref, wfi_ref, wfo_ref, gfin_ref, o_ref, *, final):
    y = y1_ref[0] + y2_ref[0] + d_ref[...] * xs_ref[0]
    y = y * _silu(z_ref[0])
    var = jnp.mean(y * y, axis=-1, keepdims=True)
    oc = (y * lax.rsqrt(var + EPS) * sg_ref[...]).astype(BF16)
    wa = 2 * LANES
    mix = (jnp.dot(oa_ref[0].astype(BF16), wo_ref[0, 0:wa, :], preferred_element_type=F32)
           + jnp.dot(ob_ref[0].astype(BF16), wo_ref[0, wa:2 * wa, :], preferred_element_type=F32)
           + jnp.dot(oc, wo_ref[0, 2 * wa:, :], preferred_element_type=F32))
    xn = x_ref[0] + mod_ref[0, 2:3, :] * mix
    h = _norm_mod(xn, g_ref[...], mod_ref[0, 3:4, :], mod_ref[0, 4:5, :]).astype(BF16)
    acc = None
    for lo, hi in zip(FFN_SPLITS[:-1], FFN_SPLITS[1:]):
        gate = jnp.dot(h, wfi_ref[0, :, lo:hi], preferred_element_type=F32)
        up = jnp.dot(h, wfi_ref[0, :, D_FF + lo:D_FF + hi], preferred_element_type=F32)
        act = (_silu(gate) * up).astype(BF16)
        part = jnp.dot(act, wfo_ref[0, lo:hi, :], preferred_element_type=F32)
        acc = part if acc is None else acc + part
    out = xn + mod_ref[0, 5:6, :] * acc
    if final:
        var = jnp.mean(out * out, axis=-1, keepdims=True)
        out = out * lax.rsqrt(var + EPS) * gfin_ref[...]
    o_ref[0] = out


def _mix_ffn(x, mods, o_a, o_b, y1, y2, xbc, xz, d_exp, ssm_g, w_out, g_ffn, w_ffn_in, w_ffn_out,
             g_final, layer, tm, final):
    b, t, _ = x.shape
    per_batch = mods.shape[0] > 1
    row = lambda n: pl.BlockSpec((1, tm, n), lambda i, j: (i, j, 0))
    const = lambda shape: pl.BlockSpec(shape, lambda i, j: (0, 0), pipeline_mode=pl.Buffered(1))
    weight = lambda r, c: pl.BlockSpec((1, r, c), lambda i, j: (layer, 0, 0), pipeline_mode=pl.Buffered(1))
    return pl.pallas_call(
        functools.partial(_mix_ffn_kernel, final=final),
        out_shape=jax.ShapeDtypeStruct((b, t, D_MODEL), F32),
        grid=(b, t // tm),
        in_specs=[
            row(D_MODEL),
            pl.BlockSpec((1, SUBLANES, D_MODEL), (lambda i, j: (i, 0, 0)) if per_batch else (lambda i, j: (0, 0, 0))),
            row(2 * LANES), row(2 * LANES), row(SSM_INNER), row(SSM_INNER),
            pl.BlockSpec((1, tm, SSM_INNER), lambda i, j: (i, j, 0)),
            pl.BlockSpec((1, tm, SSM_INNER), lambda i, j: (i, j, 2)),
            const((1, SSM_INNER)), const((1, SSM_INNER)),
            weight(D_MODEL, D_MODEL),
            const((1, D_MODEL)),
            weight(D_MODEL, 2 * D_FF),
            weight(D_FF, D_MODEL),
            const((1, D_MODEL)),
        ],
        out_specs=row(D_MODEL),
        compiler_params=_cparams(("parallel", "parallel")),
        name="mix_ffn_final" if final else "mix_ffn",
    )(x, mods, o_a, o_b, y1, y2, xbc, xz, d_exp, ssm_g, w_out, g_ffn, w_ffn_in, w_ffn_out, g_final)


def _rope_tables():
    t = np.arange(SEQ)
    pos = np.stack([t // GRID_W, t % GRID_W], axis=1).astype(np.float64)
    quarter = HEAD_DIM // 4
    inv_freq = ROPE_BASE ** (-np.arange(quarter, dtype=np.float64) / quarter)
    lane = np.arange(LANES) % HEAD_DIM
    half = lane // (HEAD_DIM // 2)
    idx = lane % (HEAD_DIM // 2)
    ang = pos[:, half] * inv_freq[idx % quarter][None, :]
    cos, sin = np.cos(ang), np.sin(ang)
    first = (idx < quarter)[None, :]
    tabs = (cos, np.where(first, -sin, 0.0), np.where(first, 0.0, sin))
    return tuple(jnp.asarray(v, F32) for v in tabs)


def _pad_lanes(v, n=LANES):
    v = v.reshape(1, -1)
    return jnp.pad(v, ((0, 0), (0, n - v.shape[1])))


def kernel(x, c, ctx, c_ctx, w_mod, b_mod, g_mix, w_in, wa_sink, na_rpb, ssm_conv_w, ssm_conv_b,
           ssm_dt_bias, ssm_a_log, ssm_d, ssm_norm_g, w_out, g_ffn, w_ffn_in, w_ffn_out, g_final):
    cin = jnp.concatenate([c, c_ctx[None, :], jnp.zeros((SUBLANES - BATCH - 1, D_MODEL), F32)], axis=0)
    mod_all = _modulation(cin, w_mod, b_mod)
    rope_tabs = _rope_tables()
    bias_tabs = _na_bias_tables(na_rpb)
    zeros_state = jnp.zeros((BATCH,) + SSD_STATE_SHAPE, F32)
    gfin = g_final.reshape(1, D_MODEL)
    w_proj = w_in.astype(BF16)
    wo = w_out.astype(BF16)
    wfi = w_ffn_in.astype(BF16)
    wfo = w_ffn_out.astype(BF16)

    xl, xc = x, ctx
    for l in range(DEPTH):
        last = l == DEPTH - 1
        m6 = mod_all[l].reshape(SUBLANES, 6, D_MODEL)
        mods_l = jnp.pad(m6[:BATCH], ((0, 0), (0, 2), (0, 0)))
        mods_c = jnp.pad(m6[BATCH:BATCH + 1], ((0, 0), (0, 2), (0, 0)))
        w_dt = jnp.pad(w_in[l][:, W_DT:], ((0, 0), (0, DT_PAD - DT_COLS))).astype(BF16)
        g1 = g_mix[l].reshape(1, D_MODEL)
        g2 = g_ffn[l].reshape(1, D_MODEL)
        conv_w = jnp.pad(ssm_conv_w[l], ((0, SUBLANES - SSM_CONV), (0, 0)))
        conv_b = ssm_conv_b[l].reshape(1, XBC_COLS)
        dt_bias = _pad_lanes(ssm_dt_bias[l])
        a_log = _pad_lanes(ssm_a_log[l])
        d_exp = jnp.repeat(ssm_d[l], SSM_HEAD_DIM).reshape(1, SSM_INNER)
        sg = ssm_norm_g[l].reshape(1, SSM_INNER)

        a_c, b_c, xz_c, dt_c = _inproj(xc, mods_c, g1, w_proj, l, w_dt, None, CTX_LEN)
        a_l, b_l, xz_l, dt_l = _inproj(xl, mods_l, g1, w_proj, l, w_dt, rope_tabs, TM_INPROJ)

        o_a = _attn_a(wa_sink[l], a_l, a_c)
        o_b = _attn_b(b_l, b_c, bias_tabs[l])

        xbc_c = _conv_silu(xz_c, conv_w, conv_b, CTX_LEN)
        xbc_l = _conv_silu(xz_l, conv_w, conv_b, TM_CONV)
        y1_c, y2_c, h_f, h_b = _ssd(xbc_c, dt_c, dt_bias, a_log, zeros_state, zeros_state)
        y1_l, y2_l, _, _ = _ssd(xbc_l, dt_l, dt_bias, a_log, h_f, h_b)

        xl = _mix_ffn(xl, mods_l, o_a, o_b, y1_l, y2_l, xbc_l, xz_l, d_exp, sg, wo, g2, wfi, wfo, gfin,
                      l, TM_MIX, last)
        if not last:
            o_ac, o_bc = _ctx_attn(wa_sink[l], a_c, b_c)
            xc = _mix_ffn(xc, mods_c, o_ac, o_bc, y1_c, y2_c, xbc_c, xz_c, d_exp, sg, wo, g2, wfi, wfo,
                          gfin, l, CTX_LEN, False)
    return xl
```

```python
import functools

import numpy as np
import jax
import jax.numpy as jnp
from jax import lax
from jax.experimental import pallas as pl
from jax.experimental.pallas import tpu as pltpu

F32 = jnp.float32
BF16 = jnp.bfloat16

D_MODEL = 1024
BATCH = 4
SEQ = 4096
DEPTH = 2
GRID_W = 64
GRID_ROWS = SEQ // GRID_W
CTX_LEN = 256
EPS = 1e-6
HEAD_DIM = 64
ROPE_BASE = 10000.0
WA_HEADS = 4
WA_KV_HEADS = 2
WA_WINDOW = 128
WA_BLOCK = 128
NA_HEADS = 4
NA_KH = 8
NA_KW = 16
SSM_HEADS = 8
SSM_HEAD_DIM = 64
SSM_INNER = SSM_HEADS * SSM_HEAD_DIM
SSM_GROUPS = 2
SSM_STATE = 128
SSM_CONV = 7
SSM_CHUNK = 128
D_FF = 2816
XBC_COLS = SSM_INNER + 2 * SSM_GROUPS * SSM_STATE
DT_COLS = 2 * SSM_HEADS

LANES = 128
SUBLANES = 8
V7X_VMEM_BYTES = 64 * 1024 * 1024
VMEM_LIMIT = V7X_VMEM_BYTES - 8 * 1024 * 1024

TM_INPROJ = 1024
TM_CONV = 1024
TM_MIX = 512

A_COLS = 512
B_COLS = 768
XZ_COLS = 1536
DT_PAD = LANES

ATT_SCALE = HEAD_DIM ** -0.5
NEG_INF = float("-inf")
NT_DIMS = (((1,), (1,)), ((), ()))


def _silu(v):
    return v / (1.0 + jnp.exp(-v))


def _bdot(a, b):
    return jnp.dot(a.astype(BF16), b.astype(BF16), preferred_element_type=F32)


def _bdot_nt(a, b):
    return lax.dot_general(a.astype(BF16), b.astype(BF16), NT_DIMS, preferred_element_type=F32)


def _cparams(sem):
    return pltpu.CompilerParams(dimension_semantics=sem, vmem_limit_bytes=VMEM_LIMIT)


MOD_TN = 6 * D_MODEL // 4


def _mod_kernel(c_ref, w_ref, b_ref, o_ref):
    s = _silu(c_ref[...])
    o_ref[0] = _bdot(s, w_ref[0]) + b_ref[0]


def _modulation(cin, w_mod, b_mod):
    n = 6 * D_MODEL
    return pl.pallas_call(
        _mod_kernel,
        out_shape=jax.ShapeDtypeStruct((DEPTH, SUBLANES, n), F32),
        grid=(DEPTH, n // MOD_TN),
        in_specs=[
            pl.BlockSpec((SUBLANES, D_MODEL), lambda l, j: (0, 0)),
            pl.BlockSpec((1, D_MODEL, MOD_TN), lambda l, j: (l, 0, j)),
            pl.BlockSpec((1, 1, MOD_TN), lambda l, j: (l, 0, j)),
        ],
        out_specs=pl.BlockSpec((1, SUBLANES, MOD_TN), lambda l, j: (l, 0, j)),
        compiler_params=_cparams(("parallel", "parallel")),
        name="modulation",
    )(cin, w_mod, b_mod.reshape(DEPTH, 1, n))


def _norm_mod(x, g, shift, scale):
    var = jnp.mean(x * x, axis=-1, keepdims=True)
    h = x * lax.rsqrt(var + EPS) * g
    return h * (1.0 + scale) + shift


W_QA = 0
W_QB = W_QA + WA_HEADS * HEAD_DIM
W_Z = W_QB + NA_HEADS * HEAD_DIM
W_KVA = W_Z + SSM_INNER
W_KVB = W_KVA + 2 * WA_KV_HEADS * HEAD_DIM
W_XBC = W_KVB + 2 * NA_HEADS * HEAD_DIM
W_DT = W_XBC + XBC_COLS
IN_COLS = W_DT + DT_COLS
ROPE_QUARTER = HEAD_DIM // 4


def _inproj_kernel(x_ref, mod_ref, g_ref, w_ref, wdt_ref, *rest, rope):
    if rope:
        cos_ref, sa_ref, sb_ref, oa_ref, ob_ref, oxz_ref, odt_ref = rest
    else:
        oa_ref, ob_ref, oxz_ref, odt_ref = rest
    h = _norm_mod(x_ref[0], g_ref[...], mod_ref[0, 0:1, :], mod_ref[0, 1:2, :]).astype(BF16)

    def proj(lo, hi):
        return jnp.dot(h, w_ref[0, :, lo:hi].astype(BF16), preferred_element_type=F32)

    qa = proj(W_QA, W_QB)
    kva = proj(W_KVA, W_KVB)
    if rope:
        cos, sa, sb = cos_ref[...], sa_ref[...], sb_ref[...]

        def rot(v):
            up = pltpu.roll(v, LANES - ROPE_QUARTER, axis=1)
            dn = pltpu.roll(v, ROPE_QUARTER, axis=1)
            return v * cos + up * sa + dn * sb

        oa_ref[0, :, 0:LANES] = rot(qa[:, 0:LANES]).astype(BF16)
        oa_ref[0, :, LANES:2 * LANES] = rot(qa[:, LANES:]).astype(BF16)
        oa_ref[0, :, 2 * LANES:3 * LANES] = rot(kva[:, 0:LANES]).astype(BF16)
        oa_ref[0, :, 3 * LANES:] = kva[:, LANES:].astype(BF16)
    else:
        oa_ref[0, :, 0:2 * LANES] = qa.astype(BF16)
        oa_ref[0, :, 2 * LANES:] = kva.astype(BF16)
    ob_ref[0, :, 0:2 * LANES] = proj(W_QB, W_Z).astype(BF16)
    ob_ref[0, :, 2 * LANES:] = proj(W_KVB, W_XBC).astype(BF16)
    oxz_ref[0, :, 0:XBC_COLS] = proj(W_XBC, W_DT)
    oxz_ref[0, :, XBC_COLS:] = proj(W_Z, W_KVA)
    odt_ref[0] = jnp.dot(h, wdt_ref[...], preferred_element_type=F32)


def _inproj(x, mods, g, w, layer, wdt, rope_tabs, tm):
    b, t, _ = x.shape
    rope = rope_tabs is not None
    per_batch = mods.shape[0] > 1
    in_specs = [
        pl.BlockSpec((1, tm, D_MODEL), lambda i, j: (i, j, 0)),
        pl.BlockSpec((1, SUBLANES, D_MODEL), (lambda i, j: (i, 0, 0)) if per_batch else (lambda i, j: (0, 0, 0))),
        pl.BlockSpec((1, D_MODEL), lambda i, j: (0, 0)),
        pl.BlockSpec((1, D_MODEL, IN_COLS), lambda i, j: (layer, 0, 0), pipeline_mode=pl.Buffered(1)),
        pl.BlockSpec((D_MODEL, DT_PAD), lambda i, j: (0, 0)),
    ]
    args = [x, mods, g, w, wdt]
    if rope:
        in_specs += [pl.BlockSpec((tm, LANES), lambda i, j: (j, 0))] * 3
        args += list(rope_tabs)
    widths = (A_COLS, B_COLS, XZ_COLS, DT_PAD)
    return pl.pallas_call(
        functools.partial(_inproj_kernel, rope=rope),
        out_shape=[jax.ShapeDtypeStruct((b, t, n), dt) for n, dt in zip(widths, (BF16, BF16, F32, F32))],
        grid=(b, t // tm),
        in_specs=in_specs,
        out_specs=[pl.BlockSpec((1, tm, n), lambda i, j: (i, j, 0)) for n in widths],
        compiler_params=_cparams(("parallel", "parallel")),
        name="inproj_rope" if rope else "inproj",
    )(*args)


WA_KEYS = 3 * WA_BLOCK


WA_BLOCKS_PER_STEP = 4


def _attn_a_kernel(sink_ref, q_ref, k_ref, v_ref, kc_ref, vc_ref, o_ref, s_ref, p_ref):
    step = pl.program_id(1)
    nk = WA_KEYS + CTX_LEN
    nq = WA_HEADS * WA_BLOCK
    kc, vc = kc_ref[0], vc_ref[0]
    starts = []
    for bb in range(WA_BLOCKS_PER_STEP):
        n = step * WA_BLOCKS_PER_STEP + bb
        starts.append(pl.multiple_of(jnp.clip((n - 1) * WA_BLOCK, 0, SEQ - WA_KEYS), WA_BLOCK))

    first_half = lax.broadcasted_iota(jnp.int32, (WA_BLOCK, LANES), 1) < HEAD_DIM
    swap = lambda v: pltpu.roll(v, HEAD_DIM, axis=1)

    for bb in range(WA_BLOCKS_PER_STEP):
        q = q_ref[0, bb * WA_BLOCK:(bb + 1) * WA_BLOCK, :].astype(F32) * ATT_SCALE
        q01, q23 = q[:, 0:LANES], q[:, LANES:]
        lhs = jnp.concatenate([
            jnp.where(first_half, q01, 0.0), jnp.where(first_half, swap(q01), 0.0),
            jnp.where(first_half, 0.0, swap(q23)), jnp.where(first_half, 0.0, q23)], axis=0)
        kall = jnp.concatenate([k_ref[0, pl.ds(starts[bb], WA_KEYS), :], kc], axis=0)
        s_ref[bb * nq:(bb + 1) * nq, :] = _bdot_nt(lhs, kall)

    nr = WA_BLOCKS_PER_STEP * nq
    rows = lax.broadcasted_iota(jnp.int32, (nr, nk), 0)
    cols = lax.broadcasted_iota(jnp.int32, (nr, nk), 1)
    row1 = lax.broadcasted_iota(jnp.int32, (nr, 1), 0)
    blk = row1 // nq
    qpos = (step * WA_BLOCKS_PER_STEP + blk) * WA_BLOCK + (rows & (WA_BLOCK - 1))
    kstart = starts[-1]
    for bb in range(WA_BLOCKS_PER_STEP - 2, -1, -1):
        kstart = jnp.where(blk == bb, starts[bb], kstart)
    valid = (cols >= WA_KEYS) | (jnp.abs(qpos - (kstart + cols)) <= WA_WINDOW)
    s = jnp.where(valid, s_ref[...], NEG_INF)
    head = (row1 // WA_BLOCK) % WA_HEADS
    sink = jnp.where(head == 0, sink_ref[0],
                     jnp.where(head == 1, sink_ref[1], jnp.where(head == 2, sink_ref[2], sink_ref[3])))
    m = jnp.maximum(jnp.max(s, axis=1, keepdims=True), sink)
    p = jnp.exp(s - m)
    inv = 1.0 / (jnp.sum(p, axis=1, keepdims=True) + jnp.exp(sink - m))
    p_ref[...] = p.astype(BF16)

    for bb in range(WA_BLOCKS_PER_STEP):
        vall = jnp.concatenate([v_ref[0, pl.ds(starts[bb], WA_KEYS), :], vc], axis=0)
        rs = slice(bb * nq, (bb + 1) * nq)
        o = jnp.dot(p_ref[rs, :], vall, preferred_element_type=F32) * inv[rs]
        o0, o1, o2, o3 = (o[h * WA_BLOCK:(h + 1) * WA_BLOCK] for h in range(WA_HEADS))
        qs = slice(bb * WA_BLOCK, (bb + 1) * WA_BLOCK)
        o_ref[0, qs, 0:LANES] = jnp.where(first_half, o0, swap(o1)).astype(o_ref.dtype)
        o_ref[0, qs, LANES:] = jnp.where(first_half, swap(o2), o3).astype(o_ref.dtype)


def _attn_a(sink, qkv, qkv_c):
    b = qkv.shape[0]
    nk = WA_KEYS + CTX_LEN
    tq = WA_BLOCKS_PER_STEP * WA_BLOCK
    nr = WA_BLOCKS_PER_STEP * WA_HEADS * WA_BLOCK
    return pl.pallas_call(
        _attn_a_kernel,
        out_shape=jax.ShapeDtypeStruct((b, SEQ, WA_HEADS * HEAD_DIM), BF16),
        scratch_shapes=[pltpu.VMEM((nr, nk), F32), pltpu.VMEM((nr, nk), BF16)],
        grid=(b, SEQ // tq),
        in_specs=[
            pl.BlockSpec(memory_space=pltpu.SMEM),
            pl.BlockSpec((1, tq, 2 * LANES), lambda i, j: (i, j, 0)),
            pl.BlockSpec((1, SEQ, LANES), lambda i, j: (i, 0, 2)),
            pl.BlockSpec((1, SEQ, LANES), lambda i, j: (i, 0, 3)),
            pl.BlockSpec((1, CTX_LEN, LANES), lambda i, j: (i, 0, 2)),
            pl.BlockSpec((1, CTX_LEN, LANES), lambda i, j: (i, 0, 3)),
        ],
        out_specs=pl.BlockSpec((1, tq, 2 * LANES), lambda i, j: (i, j, 0)),
        compiler_params=_cparams(("parallel", "arbitrary")),
        name="attn_window",
    )(sink, qkv, qkv, qkv, qkv_c, qkv_c)


NB_ROWS_PER_STEP = 16
NB_LOC = NA_KH * GRID_W
NB_DY_PAIRS = 2 * NA_KH - 2


def _attn_b_kernel(q_ref, k_ref, v_ref, kc_ref, vc_ref, t_ref, o_ref, s_ref, p_ref):
    i = pl.program_id(1)
    kc = kc_ref[0]
    vc = vc_ref[0]
    units = [(rr, pp) for rr in range(NB_ROWS_PER_STEP) for pp in range(NA_HEADS // 2)]
    starts, shifts = [], []
    for rr in range(NB_ROWS_PER_STEP):
        r = i * NB_ROWS_PER_STEP + rr
        rs = jnp.clip(r - NA_KH // 2, 0, GRID_ROWS - NA_KH)
        shifts.append(r - rs)
        starts.append(pl.multiple_of(rs * GRID_W, GRID_W))
    first_half = lax.broadcasted_iota(jnp.int32, (GRID_W, LANES), 1) < HEAD_DIM
    zero = jnp.zeros((), BF16)
    pw = 2 * GRID_W

    for u, (rr, pp) in enumerate(units):
        ls = slice(pp * LANES, (pp + 1) * LANES)
        if pp == 0:
            kw = k_ref[0, pl.ds(starts[rr], NB_LOC), :]
            q = q_ref[0, rr * GRID_W:(rr + 1) * GRID_W, :] * ATT_SCALE
        qs = q[:, ls]
        lhs = jnp.concatenate([jnp.where(first_half, qs, zero), jnp.where(first_half, zero, qs)], axis=0)
        bias = jnp.concatenate(
            [jnp.concatenate([t_ref[h, 2 * k - shifts[rr] + NA_KH - 1] for k in range(NA_KH // 2)], axis=1)
             for h in (2 * pp, 2 * pp + 1)], axis=0)
        s_ref[u * pw:(u + 1) * pw, 0:NB_LOC] = _bdot_nt(lhs, kw[:, ls]) + bias
        s_ref[u * pw:(u + 1) * pw, NB_LOC:] = _bdot_nt(lhs, kc[:, ls])

    s = s_ref[...]
    p_ref[...] = jnp.exp(s - jnp.max(s, axis=1, keepdims=True)).astype(BF16)

    ones_loc = jnp.ones((NB_LOC, LANES), BF16)
    ones_ctx = jnp.ones((CTX_LEN, LANES), BF16)
    vc_ext = [jnp.concatenate([vc[:, pp * LANES:(pp + 1) * LANES], ones_ctx], axis=1)
              for pp in range(NA_HEADS // 2)]
    for u, (rr, pp) in enumerate(units):
        ls = slice(pp * LANES, (pp + 1) * LANES)
        if pp == 0:
            vw = v_ref[0, pl.ds(starts[rr], NB_LOC), :]
        rows = slice(u * pw, (u + 1) * pw)
        pv = (jnp.dot(p_ref[rows, 0:NB_LOC], jnp.concatenate([vw[:, ls], ones_loc], axis=1),
                      preferred_element_type=F32)
              + jnp.dot(p_ref[rows, NB_LOC:], vc_ext[pp], preferred_element_type=F32))
        o = pv[:, 0:LANES] / pv[:, LANES:]
        o_ref[0, rr * GRID_W:(rr + 1) * GRID_W, ls] = jnp.where(
            first_half, o[:GRID_W], o[GRID_W:]).astype(o_ref.dtype)


def _attn_b(qkv, qkv_c, table):
    b = qkv.shape[0]
    tq = NB_ROWS_PER_STEP * GRID_W
    w = NA_HEADS * HEAD_DIM
    nu = NB_ROWS_PER_STEP * NA_HEADS * GRID_W
    return pl.pallas_call(
        _attn_b_kernel,
        out_shape=jax.ShapeDtypeStruct((b, SEQ, w), BF16),
        scratch_shapes=[pltpu.VMEM((nu, NB_LOC + CTX_LEN), F32), pltpu.VMEM((nu, NB_LOC + CTX_LEN), BF16)],
        grid=(b, SEQ // tq),
        in_specs=[
            pl.BlockSpec((1, tq, w), lambda i, j: (i, j, 0)),
            pl.BlockSpec((1, SEQ, w), lambda i, j: (i, 0, 1)),
            pl.BlockSpec((1, SEQ, w), lambda i, j: (i, 0, 2)),
            pl.BlockSpec((1, CTX_LEN, w), lambda i, j: (i, 0, 1)),
            pl.BlockSpec((1, CTX_LEN, w), lambda i, j: (i, 0, 2)),
            pl.BlockSpec((NA_HEADS, NB_DY_PAIRS, GRID_W, LANES), lambda i, j: (0, 0, 0, 0)),
        ],
        out_specs=pl.BlockSpec((1, tq, w), lambda i, j: (i, j, 0)),
        compiler_params=_cparams(("parallel", "arbitrary")),
        name="attn_neighbourhood",
    )(qkv, qkv, qkv, qkv_c, qkv_c, table)


def _split3(a):
    a1 = a.astype(BF16)
    r1 = a - a1.astype(F32)
    a2 = r1.astype(BF16)
    a3 = (r1 - a2.astype(F32)).astype(BF16)
    return a1, a2, a3


def _bias_kernel(r_ref, oh_ref, o_ref):
    oh = oh_ref[...]
    o_ref[...] = sum(jnp.dot(t, oh, preferred_element_type=F32) for t in _split3(r_ref[...]))


def _na_bias_tables(rpb):
    ndy, ndx = 2 * NA_KH - 1, 2 * NA_KW - 1
    qc = np.arange(GRID_W)[:, None]
    x = np.arange(GRID_W)[None, :]
    dx = np.clip(x - qc, -(NA_KW - 1), NA_KW - 1) + NA_KW - 1
    onehot = (np.arange(LANES)[:, None, None] == dx[None]).reshape(LANES, GRID_W * GRID_W)
    cstart = np.clip(qc - NA_KW // 2, 0, GRID_W - NA_KW)
    inside = (x >= cstart) & (x < cstart + NA_KW)
    rows = DEPTH * NA_HEADS * ndy
    r = jnp.pad(rpb.astype(F32).reshape(rows, ndx), ((0, LANES - rows), (0, LANES - ndx)))
    m = pl.pallas_call(
        _bias_kernel,
        out_shape=jax.ShapeDtypeStruct((LANES, GRID_W * GRID_W), F32),
        name="na_bias_expand",
    )(r, jnp.asarray(onehot, BF16))
    m = m[:rows].reshape(DEPTH, NA_HEADS, ndy, GRID_W, GRID_W)
    m = jnp.where(jnp.asarray(inside), m, NEG_INF)
    return jnp.concatenate([m[:, :, :ndy - 1], m[:, :, 1:]], axis=-1)


def _ctx_attn_kernel(sink_ref, a_ref, b_ref, oa_ref, ob_ref):
    a = a_ref[0]
    row1 = lax.broadcasted_iota(jnp.int32, (2 * CTX_LEN, 1), 0)
    for g in range(WA_KV_HEADS):
        h0, h1 = 2 * g, 2 * g + 1
        q2 = jnp.concatenate([a[:, h0 * HEAD_DIM:(h0 + 1) * HEAD_DIM],
                              a[:, h1 * HEAD_DIM:(h1 + 1) * HEAD_DIM]], axis=0)
        k = a[:, 2 * LANES + g * HEAD_DIM:2 * LANES + (g + 1) * HEAD_DIM]
        v = a[:, 3 * LANES + g * HEAD_DIM:3 * LANES + (g + 1) * HEAD_DIM]
        s = _bdot_nt(q2, k) * ATT_SCALE
        sink = jnp.where(row1 < CTX_LEN, sink_ref[h0], sink_ref[h1])
        m = jnp.maximum(jnp.max(s, axis=1, keepdims=True), sink)
        p = jnp.exp(s - m)
        den = jnp.sum(p, axis=1, keepdims=True) + jnp.exp(sink - m)
        o = _bdot(p, v) / den
        oa_ref[0, :, h0 * HEAD_DIM:(h0 + 1) * HEAD_DIM] = o[:CTX_LEN]
        oa_ref[0, :, h1 * HEAD_DIM:(h1 + 1) * HEAD_DIM] = o[CTX_LEN:]
    bq = b_ref[0]
    w = NA_HEADS * HEAD_DIM
    for h in range(NA_HEADS):
        hs = slice(h * HEAD_DIM, (h + 1) * HEAD_DIM)
        s = _bdot_nt(bq[:, hs], bq[:, w + h * HEAD_DIM:w + (h + 1) * HEAD_DIM]) * ATT_SCALE
        m = jnp.max(s, axis=1, keepdims=True)
        p = jnp.exp(s - m)
        den = jnp.sum(p, axis=1, keepdims=True)
        ob_ref[0, :, hs] = _bdot(p, bq[:, 2 * w + h * HEAD_DIM:2 * w + (h + 1) * HEAD_DIM]) / den


def _ctx_attn(sink, qkv_a_c, qkv_b_c):
    b = qkv_a_c.shape[0]
    w = 2 * LANES
    return pl.pallas_call(
        _ctx_attn_kernel,
        out_shape=[jax.ShapeDtypeStruct((b, CTX_LEN, w), F32)] * 2,
        grid=(b,),
        in_specs=[
            pl.BlockSpec(memory_space=pltpu.SMEM),
            pl.BlockSpec((1, CTX_LEN, A_COLS), lambda i: (i, 0, 0)),
            pl.BlockSpec((1, CTX_LEN, B_COLS), lambda i: (i, 0, 0)),
        ],
        out_specs=[pl.BlockSpec((1, CTX_LEN, w), lambda i: (i, 0, 0))] * 2,
        compiler_params=_cparams(("parallel",)),
        name="attn_context",
    )(sink, qkv_a_c, qkv_b_c)


CONV_HALO = SUBLANES
CONV_BLOCK = 128
CONV_MAIN = 112


def _conv_kernel(prev_ref, cur_ref, next_ref, w_ref, b_ref, o_ref, ext_ref, hl_ref, *, tl, nt):
    j = pl.program_id(1)
    ext_ref[0:CONV_HALO, :] = jnp.where(j > 0, prev_ref[0], 0.0)
    ext_ref[CONV_HALO:CONV_HALO + tl, :] = cur_ref[0]
    ext_ref[CONV_HALO + tl:2 * CONV_HALO + tl, :] = jnp.where(j < nt - 1, next_ref[0], 0.0)
    ext_ref[2 * CONV_HALO + tl:, :] = jnp.zeros((CONV_MAIN, XBC_COLS), F32)
    e = ext_ref[...]
    hi = e.astype(BF16)
    hl_ref[0] = hi
    hl_ref[1] = (e - hi.astype(F32)).astype(BF16)

    base = CONV_HALO - SSM_CONV // 2
    centre = SSM_CONV // 2
    ri = lax.broadcasted_iota(jnp.int32, (CONV_MAIN, 2 * CONV_BLOCK), 0)
    cj = lax.broadcasted_iota(jnp.int32, (CONV_MAIN, 2 * CONV_BLOCK), 1) % CONV_BLOCK
    sel = {k: jnp.where(cj == ri + base + k, 1.0, 0.0).astype(BF16) for k in range(SSM_CONV) if k != centre}
    tail = CONV_BLOCK - CONV_MAIN
    taps = list(sel)
    sel_main = jnp.concatenate([sel[k] for k in taps], axis=0)
    sel_tail = jnp.concatenate([sel[k][0:tail] for k in taps], axis=0)
    for r0 in range(0, tl, CONV_BLOCK):
        for o0, rows, w0, smat in ((r0, CONV_MAIN, r0, sel_main), (r0 + CONV_MAIN, tail, r0 + CONV_MAIN, sel_tail)):
            for c0 in range(0, XBC_COLS, MXU_TILE):
                cs = slice(c0, c0 + MXU_TILE)
                win = jnp.concatenate([hl_ref[0, w0:w0 + CONV_BLOCK, cs], hl_ref[1, w0:w0 + CONV_BLOCK, cs]],
                                      axis=0)
                shifted = jnp.dot(smat, win, preferred_element_type=F32)
                acc = b_ref[:, cs] + w_ref[centre:centre + 1, cs] * ext_ref[o0 + CONV_HALO:o0 + CONV_HALO + rows, cs]
                for n, k in enumerate(taps):
                    acc = acc + w_ref[k:k + 1, cs] * shifted[n * rows:(n + 1) * rows]
                o_ref[0, o0:o0 + rows, cs] = _silu(acc)


def _conv_silu(xz, conv_w, conv_b, tl):
    b, t, _ = xz.shape
    nt = t // tl
    hb = tl // CONV_HALO
    last = t // CONV_HALO - 1
    return pl.pallas_call(
        functools.partial(_conv_kernel, tl=tl, nt=nt),
        out_shape=jax.ShapeDtypeStruct((b, t, XBC_COLS), F32),
        grid=(b, nt),
        in_specs=[
            pl.BlockSpec((1, CONV_HALO, XBC_COLS), lambda i, j: (i, jnp.maximum(j * hb - 1, 0), 0)),
            pl.BlockSpec((1, tl, XBC_COLS), lambda i, j: (i, j, 0)),
            pl.BlockSpec((1, CONV_HALO, XBC_COLS), lambda i, j: (i, jnp.minimum((j + 1) * hb, last), 0)),
            pl.BlockSpec((SUBLANES, XBC_COLS), lambda i, j: (0, 0)),
            pl.BlockSpec((1, XBC_COLS), lambda i, j: (0, 0)),
        ],
        out_specs=pl.BlockSpec((1, tl, XBC_COLS), lambda i, j: (i, j, 0)),
        scratch_shapes=[pltpu.VMEM((tl + CONV_BLOCK, XBC_COLS), F32),
                        pltpu.VMEM((2, tl + CONV_BLOCK, XBC_COLS), BF16)],
        compiler_params=_cparams(("parallel", "parallel")),
        name="ssm_conv",
    )(xz, xz, xz, conv_w, conv_b)


Q = SSM_CHUNK
GS = SSM_GROUPS * SSM_STATE
HPG = SSM_HEADS // SSM_GROUPS
SSD_STATE_SHAPE = (SSM_GROUPS, SSM_STATE, HPG * SSM_HEAD_DIM)
SSD_BATCH_PER_STEP = 4


def _softplus(v):
    return jnp.maximum(v, 0.0) + jnp.log1p(jnp.exp(-jnp.abs(v)))


def _cumsum_mat(tri, a):
    r = jnp.dot(tri, jnp.concatenate(_split3(a), axis=1), preferred_element_type=F32)
    n = a.shape[1]
    return r[:, 0:n] + r[:, n:2 * n] + r[:, 2 * n:]


def _ssd_kernel(xm_ref, dtm_ref, xb_ref, dtb_ref, bias_ref, alog_ref, h0f_ref, h0b_ref,
                y1_ref, y2_ref, hf_out_ref, hb_out_ref, hf_ref, hb_ref, *, nc):
    i = pl.program_id(1)

    @pl.when(i == 0)
    def _():
        hf_ref[...] = h0f_ref[...]
        hb_ref[...] = h0b_ref[...]

    ii = lax.broadcasted_iota(jnp.int32, (Q, Q), 0)
    jj = lax.broadcasted_iota(jnp.int32, (Q, Q), 1)
    lower = ii >= jj
    diag = ii == jj
    tril =jnp.where(lower, 1.0, 0.0).astype(BF16)
    triu = jnp.where(ii <= jj, 1.0, 0.0).astype(BF16)
    first_half = lax.broadcasted_iota(jnp.int32, (Q, LANES), 1) < SSM_HEAD_DIM
    a_row = -jnp.exp(alog_ref[...])
    gw = HPG * SSM_HEAD_DIM

    def group_operands(xbc, g):
        bg = xbc[:, SSM_INNER + g * SSM_STATE:SSM_INNER + (g + 1) * SSM_STATE]
        cg = xbc[:, SSM_INNER + GS + g * SSM_STATE:SSM_INNER + GS + (g + 1) * SSM_STATE]
        return bg, cg.astype(BF16), xbc[:, g * gw:(g + 1) * gw]

    nbs = range(SSD_BATCH_PER_STEP)
    groups = range(SSM_GROUPS)
    bias = bias_ref[...]
    zero = jnp.zeros((), F32)


    xm = [xm_ref[bi] for bi in nbs]
    xb = [xb_ref[bi] for bi in nbs]
    dtm = [_softplus(dtm_ref[bi] + bias) for bi in nbs]
    dtb = [_softplus(dtb_ref[bi] + bias) for bi in nbs]
    cumf =[_cumsum_mat(tril, d * a_row) for d in dtm]
    cumr = [_cumsum_mat(triu, d * a_row) for d in dtm]
    cumb = [_cumsum_mat(triu, d * a_row) for d in dtb]

    opm = [[group_operands(xm[bi], g) for g in groups] for bi in nbs]
    opb = [[group_operands(xb[bi], g) for g in groups] for bi in nbs]
    gmat = [[_bdot_nt(opm[bi][g][1], opm[bi][g][0]) for g in groups] for bi in nbs]
    hf_prev = [[hf_ref[bi, g] for g in groups] for bi in nbs]
    hb_prev = [[hb_ref[bi, g] for g in groups] for bi in nbs]
    inter_f = [[jnp.dot(opm[bi][g][1], hf_prev[bi][g].astype(BF16), preferred_element_type=F32)
                for g in groups] for bi in nbs]
    inter_b = [[jnp.dot(opb[bi][g][1], hb_prev[bi][g].astype(BF16), preferred_element_type=F32)
                for g in groups] for bi in nbs]
    bt_m = [[opm[bi][g][0].T for g in groups] for bi in nbs]
    bt_b = [[opb[bi][g][0].T for g in groups] for bi in nbs]

    cumf_t = [c.T for c in cumf]
    cumr_t = [c.T for c in cumr]
    dtm_t = [d.T for d in dtm]
    cumb_t = [c.T for c in cumb]
    dtb_t = [d.T for d in dtb]

    sel_r = lax.broadcasted_iota(jnp.int32, (2 * LANES, SSM_INNER), 0) % LANES
    sel_c = lax.broadcasted_iota(jnp.int32, (2 * LANES, SSM_INNER), 1) // SSM_HEAD_DIM
    sel_f = jnp.where(sel_r == sel_c, 1.0, 0.0).astype(BF16)
    sel_b = jnp.where(sel_r == sel_c + SSM_HEADS, 1.0, 0.0).astype(BF16)

    def spread(e, sel):
        hi = e.astype(BF16)
        lo = (e - hi.astype(F32)).astype(BF16)
        return jnp.dot(jnp.concatenate([hi, lo], axis=1), sel, preferred_element_type=F32)

    sl_in_f = [spread(jnp.exp(c), sel_f) for c in cumf]
    sl_in_b = [spread(jnp.exp(c), sel_b) for c in cumb]

    heads = [(bi, h) for bi in nbs for h in range(SSM_HEADS)]
    hb_ = lambda h: SSM_HEADS + h
    cf, cr, wm, wst_f, wst_b = {}, {}, {}, {}, {}
    for t in range(len(heads) + 1):
        if t < len(heads):
            bi, h = k = heads[t]
            cf[k] = jnp.broadcast_to(cumf[bi][:, h:h + 1], (Q, Q))
            cr[k] = jnp.broadcast_to(cumr[bi][:, hb_(h):hb_(h) + 1], (Q, Q))
        if t >= 1:
            bi, h = k = heads[t - 1]
            rf = cumf_t[bi][h:h + 1, :]
            rb = cumb_t[bi][hb_(h):hb_(h) + 1, :]
            dtf_row = dtm_t[bi][h:h + 1, :]
            dtb_row = dtm_t[bi][hb_(h):hb_(h) + 1, :]
            arg = jnp.where(lower, cf[k] - rf, cr[k] - cumr_t[bi][hb_(h):hb_(h) + 1, :])
            wm[k] = (gmat[bi][h // HPG] * (jnp.exp(arg) * jnp.where(lower, dtf_row, dtb_row)
                                           + jnp.where(diag, dtb_row, zero))).astype(BF16)
            w_f = jnp.exp(rf[:, Q - 1:Q] - rf) * dtf_row
            w_b = jnp.exp(rb[:, 0:1] - rb) * dtb_t[bi][hb_(h):hb_(h) + 1, :]
            wst_f[k] = (bt_m[bi][h // HPG] * w_f).astype(BF16)
            wst_b[k] = (bt_b[bi][h // HPG] * w_b).astype(BF16)

    def pair_rhs(x, pr):
        xp = x[:, pr * LANES:(pr + 1) * LANES]
        return jnp.concatenate([jnp.where(first_half, xp, zero), jnp.where(first_half, zero, xp)],
                               axis=0).astype(BF16)

    def pair_dot(mats, bi, pr, rhs):
        lhs = jnp.concatenate([mats[bi, 2 * pr], mats[bi, 2 * pr + 1]], axis=1)
        return jnp.dot(lhs, rhs, preferred_element_type=F32)

    pairs = [(bi, pr) for bi in nbs for pr in range(SSM_HEADS // 2)]
    rhs_m = {k: pair_rhs(xm[k[0]], k[1]) for k in pairs}
    rhs_b = {k: pair_rhs(xb[k[0]], k[1]) for k in pairs}
    y_intra = {k: pair_dot(wm, k[0], k[1], rhs_m[k]) for k in pairs}
    st_f = {k: pair_dot(wst_f, k[0], k[1], rhs_m[k]) for k in pairs}
    st_b = {k: pair_dot(wst_b, k[0], k[1], rhs_b[k]) for k in pairs}

    ppg = HPG // 2
    for bi, pr in pairs:
        g, k = pr // ppg, pr % ppg
        ls = slice(k * LANES, (k + 1) * LANES)
        sin_f = sl_in_f[bi][:, pr * LANES:(pr + 1) * LANES]
        sin_b = sl_in_b[bi][:, pr * LANES:(pr + 1) * LANES]
        y1_ref[bi, :, pr * LANES:(pr + 1) * LANES] = y_intra[bi, pr] + sin_f * inter_f[bi][g][:, ls]
        y2_ref[bi, :, pr * LANES:(pr + 1) * LANES] = sin_b * inter_b[bi][g][:, ls]
        hf_ref[bi, g, :, ls] = hf_prev[bi][g][:, ls] * sin_f[Q - 1:Q, :] + st_f[bi, pr]
        hb_ref[bi, g, :, ls] = hb_prev[bi][g][:, ls] * sin_b[0:1, :] + st_b[bi, pr]

    @pl.when(i == nc - 1)
    def _():
        hf_out_ref[...] = hf_ref[...]
        hb_out_ref[...] = hb_ref[...]


def _ssd(xbc, dt_raw, dt_bias, a_log, h0f, h0b):
    b, t, _ = xbc.shape
    nc = t // Q
    nb = SSD_BATCH_PER_STEP
    st_shape = (b,) + SSD_STATE_SHAPE
    st_spec = pl.BlockSpec((nb,) + SSD_STATE_SHAPE, lambda i, j: (i, 0, 0, 0))
    fwd = lambda i, j: (i, j, 0)
    bwd = lambda i, j: (i, nc - 1 - j, 0)
    return pl.pallas_call(
        functools.partial(_ssd_kernel, nc=nc),
        out_shape=[jax.ShapeDtypeStruct((b, t, SSM_INNER), F32)] * 2
        + [jax.ShapeDtypeStruct(st_shape, F32)] * 2,
        grid=(b // nb, nc),
        in_specs=[
            pl.BlockSpec((nb, Q, XBC_COLS), fwd),
            pl.BlockSpec((nb, Q, DT_PAD), fwd),
            pl.BlockSpec((nb, Q, XBC_COLS), bwd),
            pl.BlockSpec((nb, Q, DT_PAD), bwd),
            pl.BlockSpec((1, DT_PAD), lambda i, j: (0, 0)),
            pl.BlockSpec((1, DT_PAD), lambda i, j: (0, 0)),
            st_spec, st_spec,
        ],
        out_specs=[pl.BlockSpec((nb, Q, SSM_INNER), fwd), pl.BlockSpec((nb, Q, SSM_INNER), bwd),
                   st_spec, st_spec],
        scratch_shapes=[pltpu.VMEM((nb,) + SSD_STATE_SHAPE, F32)] * 2,
        compiler_params=_cparams(("parallel", "arbitrary")),
        name="ssd_scan",
    )(xbc, dt_raw, xbc, dt_raw, dt_bias, a_log, h0f, h0b)


MXU_TILE = 256
FFN_SPLITS = (0, 6 * MXU_TILE, D_FF)


def _mix_ffn_kernel(x_ref, mod_ref, oa_ref, ob_ref, y1_ref, y2_ref, xs_ref, z_ref, d_ref, sg_ref,
                    wo_ref, g_ref, wfi_ref, wfo_ref, gfin_ref, o_ref, *, final):
    y = y1_ref[0] + y2_ref[0] + d_ref[...] * xs_ref[0]
    y = y * _silu(z_ref[0])
    var = jnp.mean(y * y, axis=-1, keepdims=True)
    oc = (y * lax.rsqrt(var + EPS) * sg_ref[...]).astype(BF16)
    wa = 2 * LANES
    mix = (jnp.dot(oa_ref[0].astype(BF16), wo_ref[0, 0:wa, :], preferred_element_type=F32)
           + jnp.dot(ob_ref[0].astype(BF16), wo_ref[0, wa:2 * wa, :], preferred_element_type=F32)
           + jnp.dot(oc, wo_ref[0, 2 * wa:, :], preferred_element_type=F32))
    xn = x_ref[0] + mod_ref[0, 2:3, :] * mix
    h = _norm_mod(xn, g_ref[...], mod_ref[0, 3:4, :], mod_ref[0, 4:5, :]).astype(BF16)
    acc = None
    for lo, hi in zip(FFN_SPLITS[:-1], FFN_SPLITS[1:]):
        gate = jnp.dot(h, wfi_ref[0, :, lo:hi], preferred_element_type=F32)
        up = jnp.dot(h, wfi_ref[0, :, D_FF + lo:D_FF + hi], preferred_element_type=F32)
        act = (_silu(gate) * up).astype(BF16)
        part = jnp.dot(act, wfo_ref[0, lo:hi, :], preferred_element_type=F32)
        acc = part if acc is None else acc + part
    out = xn + mod_ref[0, 5:6, :] * acc
    if final:
        var = jnp.mean(out * out, axis=-1, keepdims=True)
        out = out * lax.rsqrt(var + EPS) * gfin_ref[...]
    o_ref[0] = out


def _mix_ffn(x, mods, o_a, o_b, y1, y2, xbc, xz, d_exp, ssm_g, w_out, g_ffn, w_ffn_in, w_ffn_out,
             g_final, layer, tm, final):
    b, t, _ = x.shape
    per_batch = mods.shape[0] > 1
    row = lambda n: pl.BlockSpec((1, tm, n), lambda i, j: (i, j, 0))
    const = lambda shape: pl.BlockSpec(shape, lambda i, j: (0, 0), pipeline_mode=pl.Buffered(1))
    weight = lambda r, c: pl.BlockSpec((1, r, c), lambda i, j: (layer, 0, 0), pipeline_mode=pl.Buffered(1))
    return pl.pallas_call(
        functools.partial(_mix_ffn_kernel, final=final),
        out_shape=jax.ShapeDtypeStruct((b, t, D_MODEL), F32),
        grid=(b, t // tm),
        in_specs=[
            row(D_MODEL),
            pl.BlockSpec((1, SUBLANES, D_MODEL), (lambda i, j: (i, 0, 0)) if per_batch else (lambda i, j: (0, 0, 0))),
            row(2 * LANES), row(2 * LANES), row(SSM_INNER), row(SSM_INNER),
            pl.BlockSpec((1, tm, SSM_INNER), lambda i, j: (i, j, 0)),
            pl.BlockSpec((1, tm, SSM_INNER), lambda i, j: (i, j, 2)),
            const((1, SSM_INNER)), const((1, SSM_INNER)),
            weight(D_MODEL, D_MODEL),
            const((1, D_MODEL)),
            weight(D_MODEL, 2 * D_FF),
            weight(D_FF, D_MODEL),
            const((1, D_MODEL)),
        ],
        out_specs=row(D_MODEL),
        compiler_params=_cparams(("parallel", "parallel")),
        name="mix_ffn_final" if final else "mix_ffn",
    )(x, mods, o_a, o_b, y1, y2, xbc, xz, d_exp, ssm_g, w_out, g_ffn, w_ffn_in, w_ffn_out, g_final)


def _rope_tables():
    t = np.arange(SEQ)
    pos = np.stack([t // GRID_W, t % GRID_W], axis=1).astype(np.float64)
    quarter = HEAD_DIM // 4
    inv_freq = ROPE_BASE ** (-np.arange(quarter, dtype=np.float64) / quarter)
    lane = np.arange(LANES) % HEAD_DIM
    half = lane // (HEAD_DIM // 2)
    idx = lane % (HEAD_DIM // 2)
    ang = pos[:, half] * inv_freq[idx % quarter][None, :]
    cos, sin = np.cos(ang), np.sin(ang)
    first = (idx < quarter)[None, :]
    tabs = (cos, np.where(first, -sin, 0.0), np.where(first, 0.0, sin))
    return tuple(jnp.asarray(v, F32) for v in tabs)


def _pad_lanes(v, n=LANES):
    v = v.reshape(1, -1)
    return jnp.pad(v, ((0, 0), (0, n - v.shape[1])))


def kernel(x, c, ctx, c_ctx, w_mod, b_mod, g_mix, w_in, wa_sink, na_rpb, ssm_conv_w, ssm_conv_b,
           ssm_dt_bias, ssm_a_log, ssm_d, ssm_norm_g, w_out, g_ffn, w_ffn_in, w_ffn_out, g_final):
    cin = jnp.concatenate([c, c_ctx[None, :], jnp.zeros((SUBLANES - BATCH - 1, D_MODEL), F32)], axis=0)
    mod_all = _modulation(cin, w_mod, b_mod)
    rope_tabs = _rope_tables()
    bias_tabs = _na_bias_tables(na_rpb)
    zeros_state = jnp.zeros((BATCH,) + SSD_STATE_SHAPE, F32)
    gfin = g_final.reshape(1, D_MODEL)
    w_proj = w_in
    wo = w_out.astype(BF16)
    wfi = w_ffn_in.astype(BF16)
    wfo = w_ffn_out.astype(BF16)

    xl, xc = x, ctx
    for l in range(DEPTH):
        last = l == DEPTH - 1
        m6 = mod_all[l].reshape(SUBLANES, 6, D_MODEL)
        mods_l = jnp.pad(m6[:BATCH], ((0, 0), (0, 2), (0, 0)))
        mods_c = jnp.pad(m6[BATCH:BATCH + 1], ((0, 0), (0, 2), (0, 0)))
        w_dt = jnp.pad(w_in[l][:, W_DT:], ((0, 0), (0, DT_PAD - DT_COLS))).astype(BF16)
        g1 = g_mix[l].reshape(1, D_MODEL)
        g2 = g_ffn[l].reshape(1, D_MODEL)
        conv_w = jnp.pad(ssm_conv_w[l], ((0, SUBLANES - SSM_CONV), (0, 0)))
        conv_b = ssm_conv_b[l].reshape(1, XBC_COLS)
        dt_bias = _pad_lanes(ssm_dt_bias[l])
        a_log = _pad_lanes(ssm_a_log[l])
        d_exp = jnp.repeat(ssm_d[l], SSM_HEAD_DIM).reshape(1, SSM_INNER)
        sg = ssm_norm_g[l].reshape(1, SSM_INNER)

        a_c, b_c, xz_c, dt_c = _inproj(xc, mods_c, g1, w_proj, l, w_dt, None, CTX_LEN)
        a_l, b_l, xz_l, dt_l = _inproj(xl, mods_l, g1, w_proj, l, w_dt, rope_tabs, TM_INPROJ)

        o_a = _attn_a(wa_sink[l], a_l, a_c)
        o_b = _attn_b(b_l, b_c, bias_tabs[l])

        xbc_c = _conv_silu(xz_c, conv_w, conv_b, CTX_LEN)
        xbc_l = _conv_silu(xz_l, conv_w, conv_b, TM_CONV)
        y1_c, y2_c, h_f, h_b = _ssd(xbc_c, dt_c, dt_bias, a_log, zeros_state, zeros_state)
        y1_l, y2_l, _, _ = _ssd(xbc_l, dt_l, dt_bias, a_log, h_f, h_b)

        xl = _mix_ffn(xl, mods_l, o_a, o_b, y1_l, y2_l, xbc_l, xz_l, d_exp, sg, wo, g2, wfi, wfo, gfin,
                      l, TM_MIX, last)
        if not last:
            o_ac, o_bc = _ctx_attn(wa_sink[l], a_c, b_c)
            xc = _mix_ffn(xc, mods_c, o_ac, o_bc, y1_c, y2_c, xbc_c, xz_c, d_exp, sg, wo, g2, wfi, wfo,
                          gfin, l, CTX_LEN, False)
    return xl
```
